```python
import jax, jax.numpy as jnp
from jax import lax
import numpy as np

D_MODEL = 1024
BATCH = 16
SEQ = 2048
DEPTH = 1
DEC_BATCH = 32
DEC_SEQ = 64
PAST_LEN = 4096

CHUNK = 64
D_MIX = D_MODEL
WIDTH_A = D_MIX // 2
HEAD_DIM_A = 64
N_HEADS_A = WIDTH_A // HEAD_DIM_A
N_KV_A = 2
N_IDX_HEADS = 4
IDX_DIM = 64
IDX_SCALE = IDX_DIM ** -0.5 * N_IDX_HEADS ** -0.5
TOPK_MAX = 256
Q_BLOCK = 128
WIDTH_B = D_MIX - WIDTH_A
HEAD_DIM_B = 128
N_HEADS_B = WIDTH_B // HEAD_DIM_B
N_EXPERTS = 32
TOP_K = 4
D_EXPERT = D_MODEL
SWIGLU_LIMIT = 7.0
SWIGLU_ALPHA = 1.702
MOE_BLOCK = 256
D_PLE = 256
LN_EPS = 1e-5
RMS_EPS = 1e-6
DN_ALPHA = (2 * DEPTH) ** 0.25
DN_BETA = (8 * DEPTH) ** -0.25
IN_SPLITS = (WIDTH_A, N_KV_A * HEAD_DIM_A, N_KV_A * HEAD_DIM_A, N_IDX_HEADS * IDX_DIM, IDX_DIM, N_IDX_HEADS,
             WIDTH_B, WIDTH_B, WIDTH_B, WIDTH_B)
D_IN = sum(IN_SPLITS)

kernel_name = 'hybrid_dsa_hgrn2_moe_stream_step'


def layer_norm(x, w, b):
    xf = x.astype(jnp.float32)
    mu = jnp.mean(xf, axis=-1, keepdims=True)
    var = jnp.mean(jnp.square(xf - mu), axis=-1, keepdims=True)
    return ((xf - mu) * lax.rsqrt(var + LN_EPS) * w + b).astype(x.dtype)


def rms_norm(x, w):
    xf = x.astype(jnp.float32)
    return (xf * lax.rsqrt(jnp.mean(jnp.square(xf), axis=-1, keepdims=True) + RMS_EPS) * w).astype(x.dtype)


def dsa_attention(q, q_idx, w_idx, k, v, k_idx, q_pos, k_pos, topk):
    B, T = q.shape[0], q.shape[1]
    qb_len = min(Q_BLOCK, T)
    nb = T // qb_len
    rep = N_HEADS_A // N_KV_A

    def split(a):
        return a.reshape((B, nb, qb_len) + a.shape[2:]).swapaxes(0, 1)

    def block(args):
        qb, qib, wb, qpb = args
        logits = jnp.einsum('bqhd,bsd->bqsh', qib, k_idx)
        score = jnp.einsum('bqsh,bqh->bqs', jax.nn.relu(logits), wb).astype(jnp.float32) * IDX_SCALE
        admissible = (k_pos[None, :] // CHUNK) <= (qpb[:, None] // CHUNK)
        score = jnp.where(admissible[None], score, -jnp.inf)
        _, idx = lax.top_k(score, topk)
        ok = (k_pos[idx] // CHUNK) <= (qpb[None, :, None] // CHUNK)
        kg = jax.vmap(lambda kk, ii: kk[ii])(k, idx)
        vg = jax.vmap(lambda vv, ii: vv[ii])(v, idx)
        qg = qb.reshape(B, qb_len, N_KV_A, rep, HEAD_DIM_A)
        s = jnp.einsum('bqgrd,bqkgd->bqgrk', qg, kg).astype(jnp.float32) * HEAD_DIM_A ** -0.5
        s = jnp.where(ok[:, :, None, None, :], s, -jnp.inf)
        pr = jax.nn.softmax(s, axis=-1).astype(vg.dtype)
        o = jnp.einsum('bqgrk,bqkgd->bqgrd', pr, vg)
        return o.reshape(B, qb_len, N_HEADS_A * HEAD_DIM_A)

    out = lax.map(block, (split(q), split(q_idx), split(w_idx), q_pos.reshape(nb, qb_len)))
    return out.swapaxes(0, 1).reshape(B, T, N_HEADS_A * HEAD_DIM_A)


def hgrn2_recurrence(q, k, v, log_f, S0):
    B, L, H, DK = q.shape
    C = min(CHUNK, L)
    n = L // C

    def to_chunks(a):
        return a.reshape(B, n, C, H, a.shape[-1]).transpose(1, 0, 3, 2, 4)

    causal = jnp.tril(jnp.ones((C, C), dtype=bool))

    def step(S, inp):
        qc, kc, vc, gc = inp
        b = jnp.cumsum(gc, axis=2)
        diff = b[:, :, :, None, :] - b[:, :, None, :, :]
        decay = jnp.exp(jnp.where(causal[:, :, None], diff, -jnp.inf))
        A = jnp.einsum('bhtsk,bhsk->bhts', qc[:, :, :, None, :] * decay, kc)
        o = jnp.einsum('bhts,bhsv->bhtv', A, vc) + jnp.einsum('bhtk,bhkv->bhtv', qc * jnp.exp(b), S)
        b_last = b[:, :, -1:, :]
        S_new = jnp.exp(b_last[:, :, 0, :])[..., None] * S + jnp.einsum('bhsk,bhsv->bhkv', kc * jnp.exp(b_last - b), vc)
        return S_new, o

    S_fin, o = lax.scan(step, S0, (to_chunks(q), to_chunks(k), to_chunks(v), to_chunks(log_f)))
    o = o.transpose(1, 0, 3, 2, 4).reshape(B, L, H, v.shape[-1])
    return o, S_fin


def token_mixers(x, hist_k, hist_v, hist_kidx, S0, pos0, w_in, w_out, idx_k_norm_w, idx_k_norm_b, lb, hgrn_norm_w):
    B, T, _ = x.shape
    h = x @ w_in
    q_a, k_a, v_a, q_i, k_i, w_i, q_b, f_b, i_b, g_b = jnp.split(h, np.cumsum(IN_SPLITS)[:-1], axis=-1)
    q_a = q_a.reshape(B, T, N_HEADS_A, HEAD_DIM_A)
    k_a = k_a.reshape(B, T, N_KV_A, HEAD_DIM_A)
    v_a = v_a.reshape(B, T, N_KV_A, HEAD_DIM_A)
    q_i = q_i.reshape(B, T, N_IDX_HEADS, IDX_DIM)
    k_i = layer_norm(k_i, idx_k_norm_w, idx_k_norm_b)
    if hist_k is None:
        k_all, v_all, ki_all = k_a, v_a, k_i
    else:
        k_all = jnp.concatenate([hist_k.astype(k_a.dtype), k_a], axis=1)
        v_all = jnp.concatenate([hist_v.astype(v_a.dtype), v_a], axis=1)
        ki_all = jnp.concatenate([hist_kidx.astype(k_i.dtype), k_i], axis=1)
    n_keys = k_all.shape[1]
    topk = min(TOPK_MAX, n_keys // 4)
    q_pos = pos0 + jnp.arange(T)
    k_pos = jnp.arange(n_keys)
    o_a = dsa_attention(q_a, q_i, w_i, k_all, v_all, ki_all, q_pos, k_pos, topk)
    f = lb + (1.0 - lb) * jax.nn.sigmoid(f_b.astype(jnp.float32))
    shp = (B, T, N_HEADS_B, HEAD_DIM_B)
    o_b, S_fin = hgrn2_recurrence(q_b.astype(jnp.float32).reshape(shp), (1.0 - f).reshape(shp),
                                  i_b.astype(jnp.float32).reshape(shp), jnp.log(f).reshape(shp),
                                  S0.astype(jnp.float32))
    o_b = rms_norm(o_b, hgrn_norm_w) * jax.nn.silu(g_b.astype(jnp.float32).reshape(shp))
    o_b = o_b.reshape(B, T, WIDTH_B).astype(x.dtype)
    y = jnp.concatenate([o_a, o_b], axis=-1) @ w_out
    return y, k_a, v_a, k_i, S_fin.astype(x.dtype)


def moe(x, w_router, b_router, w_gu, b_gu, w_down, b_down):
    shp = x.shape
    x2 = x.reshape(-1, shp[-1])
    N = x2.shape[0]
    NK = N * TOP_K
    logits = (x2 @ w_router + b_router).astype(jnp.float32)
    top_val, top_idx = lax.top_k(logits, TOP_K)
    gates = jax.nn.softmax(top_val, axis=-1)
    flat_e = top_idx.reshape(-1)
    order = jnp.argsort(flat_e)
    sorted_e = flat_e[order]
    token_of_slot = order // TOP_K
    gate_of_slot = gates.reshape(-1)[order]
    counts = jnp.bincount(flat_e, length=N_EXPERTS)
    starts = jnp.cumsum(counts) - counts
    padded = (counts + MOE_BLOCK - 1) // MOE_BLOCK * MOE_BLOCK
    padded_ends = jnp.cumsum(padded)
    padded_starts = padded_ends - padded
    dest = padded_starts[sorted_e] + jnp.arange(NK) - starts[sorted_e]
    nb = -(-(NK + N_EXPERTS * (MOE_BLOCK - 1)) // MOE_BLOCK)
    buf = jnp.zeros((nb * MOE_BLOCK, shp[-1]), x.dtype).at[dest].set(x2[token_of_slot])
    block_e = jnp.minimum(jnp.searchsorted(padded_ends, jnp.arange(nb) * MOE_BLOCK, side='right'), N_EXPERTS - 1)

    def run(args):
        xb, e = args
        gu = xb @ w_gu[e] + b_gu[e]
        gate = jnp.minimum(gu[:, :D_EXPERT], SWIGLU_LIMIT)
        up = jnp.clip(gu[:, D_EXPERT:], -SWIGLU_LIMIT, SWIGLU_LIMIT)
        hdn = (up + 1.0) * gate * jax.nn.sigmoid(SWIGLU_ALPHA * gate)
        return hdn @ w_down[e] + b_down[e]

    out = lax.map(run, (buf.reshape(nb, MOE_BLOCK, shp[-1]), block_e)).reshape(nb * MOE_BLOCK, shp[-1])
    y = jnp.zeros_like(x2).at[token_of_slot].add(out[dest] * gate_of_slot[:, None].astype(x.dtype))
    return y.reshape(shp)


def encoder_layer(x, p, hist_k, hist_v, hist_kidx, S0, pos0, w_in, w_out, idx_k_norm_w, idx_k_norm_b, lb,
                  hgrn_norm_w, ln1_w, ln1_b, w_router, b_router, w_gu, b_gu, w_down, b_down, ln2_w, ln2_b,
                  w_ple, w_ple_gate, b_ple_gate):
    mix, k_new, v_new, kidx_new, S_new = token_mixers(x, hist_k, hist_v, hist_kidx, S0, pos0, w_in, w_out,
                                                      idx_k_norm_w, idx_k_norm_b, lb, hgrn_norm_w)
    x = layer_norm(DN_ALPHA * x + mix, ln1_w, ln1_b)
    x = layer_norm(DN_ALPHA * x + moe(x, w_router, b_router, w_gu, b_gu, w_down, b_down), ln2_w, ln2_b)
    x = x + jax.nn.sigmoid(x @ w_ple_gate + b_ple_gate) * (p @ w_ple)
    return x, k_new, v_new, kidx_new, S_new


def setup_inputs(seed: int = 0) -> dict:
    key = jax.random.key(seed)
    ks = jax.random.split(key, 32)

    def nrm(k, shape, scale):
        return jax.random.normal(k, shape, jnp.float32) * scale

    return {
        'x_prompt': nrm(ks[0], (BATCH, SEQ, D_MODEL), 1.0),
        'x_sample': nrm(ks[1], (DEC_BATCH, DEC_SEQ, D_MODEL), 1.0),
        'cache_k': nrm(ks[2], (DEPTH, DEC_BATCH, PAST_LEN, N_KV_A, HEAD_DIM_A), 1.0),
        'cache_v': nrm(ks[3], (DEPTH, DEC_BATCH, PAST_LEN, N_KV_A, HEAD_DIM_A), 1.0),
        'cache_kidx': nrm(ks[4], (DEPTH, DEC_BATCH, PAST_LEN, IDX_DIM), 1.0),
        'state_hgrn': nrm(ks[5], (DEPTH, DEC_BATCH, N_HEADS_B, HEAD_DIM_B, HEAD_DIM_B), 0.5),
        'p_prompt': nrm(ks[6], (DEPTH, BATCH, SEQ, D_PLE), 1.0),
        'p_sample': nrm(ks[7], (DEPTH, DEC_BATCH, DEC_SEQ, D_PLE), 1.0),
        'w_in': nrm(ks[8], (DEPTH, D_MODEL, D_IN), D_MODEL ** -0.5),
        'w_out': nrm(ks[9], (DEPTH, D_MIX, D_MODEL), D_MIX ** -0.5 * DN_BETA),
        'idx_k_norm_w': 1.0 + nrm(ks[10], (DEPTH, IDX_DIM), 0.02),
        'idx_k_norm_b': nrm(ks[11], (DEPTH, IDX_DIM), 0.02),
        'hgrn_lb_logits': nrm(ks[12], (DEPTH + 1, WIDTH_B), 0.1),
        'hgrn_norm_w': 1.0 + nrm(ks[13], (DEPTH, HEAD_DIM_B), 0.02),
        'ln1_w': 1.0 + nrm(ks[14], (DEPTH, D_MODEL), 0.02),
        'ln1_b': nrm(ks[15], (DEPTH, D_MODEL), 0.02),
        'w_router': nrm(ks[16], (DEPTH, D_MODEL, N_EXPERTS), D_MODEL ** -0.5),
        'b_router': nrm(ks[17], (DEPTH, N_EXPERTS), 0.01),
        'w_gu': nrm(ks[18], (DEPTH, N_EXPERTS, D_MODEL, 2 * D_EXPERT), D_MODEL ** -0.5),
        'b_gu': nrm(ks[19], (DEPTH, N_EXPERTS, 2 * D_EXPERT), 0.01),
        'w_down': nrm(ks[20], (DEPTH, N_EXPERTS, D_EXPERT, D_MODEL), D_EXPERT ** -0.5 * DN_BETA),
        'b_down': nrm(ks[21], (DEPTH, N_EXPERTS, D_MODEL), 0.01),
        'ln2_w': 1.0 + nrm(ks[22], (DEPTH, D_MODEL), 0.02),
        'ln2_b': nrm(ks[23], (DEPTH, D_MODEL), 0.02),
        'w_ple': nrm(ks[24], (DEPTH, D_PLE, D_MODEL), D_PLE ** -0.5),
        'w_ple_gate': nrm(ks[25], (DEPTH, D_MODEL, D_MODEL), D_MODEL ** -0.5),
        'b_ple_gate': nrm(ks[26], (DEPTH, D_MODEL), 0.01),
    }


def reference(x_prompt, x_sample, cache_k, cache_v, cache_kidx, state_hgrn, p_prompt, p_sample,
              w_in, w_out, idx_k_norm_w, idx_k_norm_b, hgrn_lb_logits, hgrn_norm_w, ln1_w, ln1_b,
              w_router, b_router, w_gu, b_gu, w_down, b_down, ln2_w, ln2_b, w_ple, w_ple_gate, b_ple_gate):
    lb_all = jnp.cumsum(jax.nn.softmax(hgrn_lb_logits.astype(jnp.float32), axis=0), axis=0)
    xp, xs = x_prompt, x_sample
    kp_l, vp_l, kip_l, sp_l, ks_l, vs_l, kis_l, ss_l = [], [], [], [], [], [], [], []
    for i in range(DEPTH):
        lw = (w_in[i], w_out[i], idx_k_norm_w[i], idx_k_norm_b[i], lb_all[i], hgrn_norm_w[i], ln1_w[i], ln1_b[i],
              w_router[i], b_router[i], w_gu[i], b_gu[i], w_down[i], b_down[i], ln2_w[i], ln2_b[i],
              w_ple[i], w_ple_gate[i], b_ple_gate[i])
        S0p = jnp.zeros((xp.shape[0], N_HEADS_B, HEAD_DIM_B, HEAD_DIM_B), xp.dtype)
        xp, kp, vp, kip, sp = encoder_layer(xp, p_prompt[i], None, None, None, S0p, 0, *lw)
        xs, kn, vn, kin, sn = encoder_layer(xs, p_sample[i], cache_k[i], cache_v[i], cache_kidx[i],
                                            state_hgrn[i], PAST_LEN, *lw)
        kp_l.append(kp); vp_l.append(vp); kip_l.append(kip); sp_l.append(sp)
        ks_l.append(kn); vs_l.append(vn); kis_l.append(kin); ss_l.append(sn)
    return (xp, xs, jnp.stack(kp_l), jnp.stack(vp_l), jnp.stack(kip_l), jnp.stack(sp_l),
            jnp.stack(ks_l), jnp.stack(vs_l), jnp.stack(kis_l), jnp.stack(ss_l))
```

```python
import functools

import jax
import jax.numpy as jnp
from jax import lax
from jax.experimental import pallas as pl
from jax.experimental.pallas import tpu as pltpu

F32 = jnp.float32
BF16 = jnp.bfloat16
I32 = jnp.int32

D_MODEL = 1024
CHUNK = 64
CHUNK_SHIFT = 6
WIDTH_A = 512
HEAD_DIM_A = 64
N_HEADS_A = 8
N_KV_A = 2
N_IDX_HEADS = 4
IDX_DIM = 64
IDX_SCALE = IDX_DIM ** -0.5 * N_IDX_HEADS ** -0.5
TOPK_MAX = 256
WIDTH_B = 512
HEAD_DIM_B = 128
N_HEADS_B = 4
N_EXPERTS = 32
TOP_K = 4
D_EXPERT = 1024
SWIGLU_LIMIT = 7.0
SWIGLU_ALPHA = 1.702
D_PLE = 256
LN_EPS = 1e-5
RMS_EPS = 1e-6
DEPTH = 1
DN_ALPHA = (2 * DEPTH) ** 0.25

COL_QA, COL_KA, COL_VA, COL_QI, COL_KW, COL_HB, COL_END = 0, 512, 640, 768, 1024, 1152, 3200

VMEM_LIMIT = 56 * 1024 * 1024
KEY_BLOCK = 256
HGRN_SUB = 16
EXPERT_ROWS = 512
NEG_INF = float("-inf")
INT_MIN = -(2 ** 31)


def _cparams(*sem):
    return pltpu.CompilerParams(dimension_semantics=sem, vmem_limit_bytes=VMEM_LIMIT)


def _dot(a, b):
    return jnp.dot(a, b, preferred_element_type=F32)


def _dot_nt(a, b):
    return lax.dot_general(a, b, (((1,), (1,)), ((), ())), preferred_element_type=F32)


def _dot_tn(a, b):
    return lax.dot_general(a, b, (((0,), (0,)), ((), ())), preferred_element_type=F32)


def _layer_norm(z, w, b):
    mu = jnp.mean(z, axis=-1, keepdims=True)
    d = z - mu
    var = jnp.mean(d * d, axis=-1, keepdims=True)
    return d * lax.rsqrt(var + LN_EPS) * w + b


def _inproj_body(x_ref, w_ref, lnw_ref, lnb_ref, qa_ref, ka_ref, va_ref, qi_ref, ki_ref, wi_ref, hb_ref):
    xb = x_ref[...].astype(BF16)

    def mm(c0, c1):
        return _dot(xb, w_ref[:, c0:c1])

    qa_ref[...] = (mm(COL_QA, COL_KA) * (HEAD_DIM_A ** -0.5)).astype(BF16)
    ka_ref[...] = mm(COL_KA, COL_VA)
    va_ref[...] = mm(COL_VA, COL_QI)
    qi_ref[...] = mm(COL_QI, COL_KW).astype(BF16)
    kw = mm(COL_KW, COL_HB)
    ki_ref[...] = _layer_norm(kw[:, :IDX_DIM], lnw_ref[...], lnb_ref[...])
    wi_ref[...] = kw[:, IDX_DIM:IDX_DIM + N_IDX_HEADS]
    hb_ref[...] = mm(COL_HB, COL_END)


def _inproj(x2d, w_pad, lnw, lnb):
    n = x2d.shape[0]
    tm = min(512, n)
    row = lambda width: pl.BlockSpec((tm, width), lambda i: (i, 0))
    full = lambda a: pl.BlockSpec(a.shape, lambda i: (0,) * a.ndim)
    widths = (WIDTH_A, 128, 128, 256, IDX_DIM, N_IDX_HEADS, 4 * WIDTH_B)
    dtypes = (BF16, F32, F32, BF16, F32, F32, F32)
    return pl.pallas_call(
        _inproj_body,
        grid=(n // tm,),
        in_specs=[row(D_MODEL), full(w_pad), full(lnw), full(lnb)],
        out_specs=[row(w) for w in widths],
        out_shape=[jax.ShapeDtypeStruct((n, w), d) for w, d in zip(widths, dtypes)],
        compiler_params=_cparams("parallel"),
        name="inproj",
    )(x2d, w_pad, lnw, lnb)


def _order_bits_to_f32(u):
    key = u ^ INT_MIN
    bits = key ^ ((key >> 31) & 0x7FFFFFFF)
    f = lax.bitcast_convert_type(bits, F32)
    return jnp.where(u >= 0, jnp.where(u <= 0x007FFFFF, NEG_INF, f), f)


def _dsa_body(qa_ref, qi_ref, wi_ref, kidx_ref, k_ref, vt_ref, o_ref, sc_scr, bias_scr, s_scr,
              *, s_pad, s_real, qb, pos0, topk):
    kb_rows = KEY_BLOCK
    nkb = s_pad // kb_rows
    j = pl.program_id(1)
    q_chunk = (pos0 + j * qb + lax.broadcasted_iota(I32, (1, qb), 1)) >> CHUNK_SHIFT

    def rows_of(kb):
        return pl.ds(pl.multiple_of(kb * kb_rows, kb_rows), kb_rows)

    def fold8(a):
        return a.reshape(kb_rows // 8, 8, qb)

    def score_blk(kb, _):
        kidx = kidx_ref[0, rows_of(kb), :]
        acc = jnp.zeros((kb_rows, qb), F32)
        for h in range(N_IDX_HEADS):
            lg = _dot_nt(kidx, qi_ref[0, :, h * IDX_DIM:(h + 1) * IDX_DIM])
            acc = acc + jnp.maximum(lg, 0.0) * wi_ref[0, h:h + 1, :]
        spos = kb * kb_rows + lax.broadcasted_iota(I32, (kb_rows, qb), 0)
        sc = jnp.where((spos >> CHUNK_SHIFT) <= q_chunk, acc * IDX_SCALE + 0.0, NEG_INF)
        if s_real < s_pad:
            sc = jnp.where(spos < s_real, sc, NEG_INF)
        sc_scr[rows_of(kb), :] = sc
        return 0

    lax.fori_loop(0, nkb, score_blk, 0)

    def count(thr, strict):
        def body(kb, acc):
            blk = sc_scr[rows_of(kb), :]
            hit = (blk > thr) if strict else (blk >= thr)
            return acc + jnp.sum(fold8(jnp.where(hit, 1, 0).astype(I32)), axis=0)
        acc = lax.fori_loop(0, nkb, body, jnp.zeros((8, qb), I32))
        return jnp.sum(acc, axis=0, keepdims=True)

    def bit_step(i, carry):
        prefix, cnt = carry
        cand = prefix | lax.shift_left(jnp.int32(1), 31 - i)
        c = count(_order_bits_to_f32(cand), False)
        take = c >= topk
        return jnp.where(take, cand, prefix), jnp.where(take, c, cnt)

    prefix, cnt_ge = lax.fori_loop(0, 32, bit_step,
                                   (jnp.zeros((1, qb), I32), jnp.full((1, qb), s_pad, I32)))
    tau = _order_bits_to_f32(prefix)
    finite_tau = tau > NEG_INF
    tau_floor = jnp.maximum(tau, jnp.finfo(F32).min)
    tie_lanes = jnp.where(finite_tau, jnp.where(cnt_ge > topk, 1, 0), 0)
    has_ties = jnp.max(tie_lanes) > 0

    @pl.when(jnp.logical_not(has_ties))
    def _():
        def body(kb, _):
            blk = sc_scr[rows_of(kb), :]
            bias_scr[rows_of(kb), :] = jnp.where(blk >= tau_floor, 0.0, NEG_INF)
            return 0
        lax.fori_loop(0, nkb, body, 0)

    @pl.when(has_ties)
    def _():
        need = jnp.where(finite_tau, (topk - count(tau, True)).astype(F32), 0.0)
        r = lax.broadcasted_iota(I32, (kb_rows, kb_rows), 0)
        c = lax.broadcasted_iota(I32, (kb_rows, kb_rows), 1)
        tril = jnp.where(r >= c, 1.0, 0.0).astype(BF16)

        def body(kb, seen):
            blk = sc_scr[rows_of(kb), :]
            eq = blk == tau
            rank = _dot(tril, jnp.where(eq, 1.0, 0.0).astype(BF16)) + seen
            tie_bias = jnp.where(eq, jnp.where(rank <= need, 0.0, NEG_INF), NEG_INF)
            bias_scr[rows_of(kb), :] = jnp.where(blk > tau, 0.0, tie_bias)
            return rank[kb_rows - 1:kb_rows, :]
        lax.fori_loop(0, nkb, body, jnp.zeros((1, qb), F32))

    for h in range(N_HEADS_A):
        g = h // (N_HEADS_A // N_KV_A)
        lanes = slice(g * HEAD_DIM_A, (g + 1) * HEAD_DIM_A)
        qh = qa_ref[0, :, h * HEAD_DIM_A:(h + 1) * HEAD_DIM_A]

        def pass1(kb, m8):
            s = _dot_nt(k_ref[0, rows_of(kb), lanes], qh) + bias_scr[rows_of(kb), :]
            s_scr[rows_of(kb), :] = s
            return jnp.maximum(m8, jnp.max(fold8(s), axis=0))

        m8 = lax.fori_loop(0, nkb, pass1, jnp.full((8, qb), NEG_INF, F32))
        m = jnp.max(m8, axis=0, keepdims=True)

        def pass2(kb, carry):
            l8, acc = carry
            p = jnp.exp(s_scr[rows_of(kb), :] - m)
            cols = pl.ds(pl.multiple_of(kb * kb_rows, kb_rows), kb_rows)
            acc = acc + _dot(vt_ref[0, lanes, cols], p.astype(BF16))
            return l8 + jnp.sum(fold8(p), axis=0), acc

        l8, acc = lax.fori_loop(0, nkb, pass2,
                                (jnp.zeros((8, qb), F32), jnp.zeros((HEAD_DIM_A, qb), F32)))
        inv_l = 1.0 / jnp.sum(l8, axis=0, keepdims=True)
        o_ref[0, h * HEAD_DIM_A:(h + 1) * HEAD_DIM_A, :] = (acc * inv_l).astype(BF16)


def _dsa(qa, qi, wi_t, kidx, k_all, vt_all, *, s_real, pos0, topk):
    b, t, _ = qa.shape
    s_pad = kidx.shape[1]
    qb = min(256, t)
    body = functools.partial(_dsa_body, s_pad=s_pad, s_real=s_real, qb=qb, pos0=pos0, topk=topk)
    return pl.pallas_call(
        body,
        grid=(b, t // qb),
        in_specs=[
            pl.BlockSpec((1, qb, WIDTH_A), lambda i, j: (i, j, 0)),
            pl.BlockSpec((1, qb, N_IDX_HEADS * IDX_DIM), lambda i, j: (i, j, 0)),
            pl.BlockSpec((1, N_IDX_HEADS, qb), lambda i, j: (i, 0, j)),
            pl.BlockSpec((1, s_pad, IDX_DIM), lambda i, j: (i, 0, 0)),
            pl.BlockSpec((1, s_pad, N_KV_A * HEAD_DIM_A), lambda i, j: (i, 0, 0)),
            pl.BlockSpec((1, N_KV_A * HEAD_DIM_A, s_pad), lambda i, j: (i, 0, 0)),
        ],
        out_specs=pl.BlockSpec((1, WIDTH_A, qb), lambda i, j: (i, 0, j)),
        out_shape=jax.ShapeDtypeStruct((b, WIDTH_A, t), BF16),
        scratch_shapes=[pltpu.VMEM((s_pad, qb), F32)] * 3,
        compiler_params=_cparams("parallel", "parallel"),
        name="dsa",
    )(qa, qi, wi_t, kidx, k_all, vt_all)


def _split3(a):
    hi = a.astype(BF16)
    r1 = a - hi.astype(F32)
    mid = r1.astype(BF16)
    lo = (r1 - mid.astype(F32)).astype(BF16)
    return hi, mid, lo


def _hgrn_body(hb_ref, s0_ref, lb_ref, nw_ref, ob_ref, sfin_ref, state_scr, *, tb):
    t = pl.program_id(1)

    @pl.when(t == 0)
    def _():
        state_scr[...] = s0_ref[0]

    lb = lb_ref[...]
    r = lax.broadcasted_iota(I32, (CHUNK, CHUNK), 0)
    c = lax.broadcasted_iota(I32, (CHUNK, CHUNK), 1)
    tril = jnp.where(r >= c, 1.0, 0.0).astype(BF16)
    ones = jnp.ones((CHUNK, HEAD_DIM_B), BF16)

    for ci in range(tb // CHUNK):
        rows = slice(ci * CHUNK, (ci + 1) * CHUNK)
        f = lb + (1.0 - lb) * jax.nn.sigmoid(hb_ref[0, rows, WIDTH_B:2 * WIDTH_B])
        log_f = jnp.log(f)
        parts = _split3(log_f)
        bcum = _dot(tril, parts[0]) + _dot(tril, parts[1]) + _dot(tril, parts[2])
        for h in range(N_HEADS_B):
            lanes = slice(h * HEAD_DIM_B, (h + 1) * HEAD_DIM_B)
            q = hb_ref[0, rows, lanes]
            k = 1.0 - f[:, lanes]
            v = hb_ref[0, rows, 2 * WIDTH_B + h * HEAD_DIM_B:2 * WIDTH_B + (h + 1) * HEAD_DIM_B]
            gate = hb_ref[0, rows, 3 * WIDTH_B + h * HEAD_DIM_B:3 * WIDTH_B + (h + 1) * HEAD_DIM_B]
            bh = bcum[:, lanes]
            b_last = bh[CHUNK - 1:CHUNK, :]
            vb = v.astype(BF16)
            state = state_scr[h]
            o_inter = _dot((q * jnp.exp(bh)).astype(BF16), state.astype(BF16))
            o_rows = []
            for i in range(CHUNK // HGRN_SUB):
                lo, hi = i * HGRN_SUB, (i + 1) * HGRN_SUB
                ref = bh[lo - 1:lo, :] if i else jnp.zeros((1, HEAD_DIM_B), F32)
                qs = (q[lo:hi] * jnp.exp(bh[lo:hi] - ref)).astype(BF16)
                ks = (k[:hi] * jnp.exp(ref - bh[:hi])).astype(BF16)
                a = _dot_nt(qs, ks)
                tq = lo + lax.broadcasted_iota(I32, (HGRN_SUB, hi), 0)
                tk = lax.broadcasted_iota(I32, (HGRN_SUB, hi), 1)
                a = jnp.where(tk <= tq, a, 0.0)
                o_rows.append(_dot(a.astype(BF16), vb[:hi]))
            o = o_inter + jnp.concatenate(o_rows, axis=0)
            kdec = (k * jnp.exp(b_last - bh)).astype(BF16)
            col_sum = (_dot_tn(parts[0][:, lanes], ones) + _dot_tn(parts[1][:, lanes], ones)
                       + _dot_tn(parts[2][:, lanes], ones))
            state_scr[h] = jnp.exp(col_sum) * state + _dot_tn(kdec, vb)
            ms = jnp.mean(o * o, axis=-1, keepdims=True)
            y = o * lax.rsqrt(ms + RMS_EPS) * nw_ref[...] * (gate * jax.nn.sigmoid(gate))
            ob_ref[0, rows, lanes] = y.astype(BF16)

    @pl.when(t == pl.num_programs(1) - 1)
    def _():
        sfin_ref[0] = state_scr[...]


def _hgrn(hb, s0, lb, nw):
    b, t, _ = hb.shape
    tb = min(256, t)
    return pl.pallas_call(
        functools.partial(_hgrn_body, tb=tb),
        grid=(b, t // tb),
        in_specs=[
            pl.BlockSpec((1, tb, 4 * WIDTH_B), lambda i, j: (i, j, 0)),
            pl.BlockSpec((1, N_HEADS_B, HEAD_DIM_B, HEAD_DIM_B), lambda i, j: (i, 0, 0, 0)),
            pl.BlockSpec((1, WIDTH_B), lambda i, j: (0, 0)),
            pl.BlockSpec((1, HEAD_DIM_B), lambda i, j: (0, 0)),
        ],
        out_specs=[
            pl.BlockSpec((1, tb, WIDTH_B), lambda i, j: (i, j, 0)),
            pl.BlockSpec((1, N_HEADS_B, HEAD_DIM_B, HEAD_DIM_B), lambda i, j: (i, 0, 0, 0)),
        ],
        out_shape=[jax.ShapeDtypeStruct((b, t, WIDTH_B), BF16),
                   jax.ShapeDtypeStruct((b, N_HEADS_B, HEAD_DIM_B, HEAD_DIM_B), F32)],
        scratch_shapes=[pltpu.VMEM((N_HEADS_B, HEAD_DIM_B, HEAD_DIM_B), F32)],
        compiler_params=_cparams("parallel", "arbitrary"),
        name="hgrn",
    )(hb, s0, lb, nw)


def _outproj_body(oa_ref, ob_ref, x_ref, w_ref, lnw_ref, lnb_ref, wrh_ref, wrl_ref, br_ref, x1_ref, lg_ref):
    y = _dot(oa_ref[...], w_ref[:WIDTH_A, :]) + _dot(ob_ref[...], w_ref[WIDTH_A:, :])
    x1 = _layer_norm(DN_ALPHA * x_ref[...] + y, lnw_ref[...], lnb_ref[...])
    x1_ref[...] = x1
    hi = x1.astype(BF16)
    lo = (x1 - hi.astype(F32)).astype(BF16)
    lg_ref[...] = (_dot_nt(wrh_ref[...], hi) + _dot_nt(wrh_ref[...], lo) + _dot_nt(wrl_ref[...], hi)
                   + br_ref[...])


def _outproj(oa, ob, x2d, w_out, lnw, lnb, wr_hi, wr_lo, br):
    n = x2d.shape[0]
    tm = min(512, n)
    row = lambda width: pl.BlockSpec((tm, width), lambda i: (i, 0))
    full = lambda a: pl.BlockSpec(a.shape, lambda i: (0,) * a.ndim)
    return pl.pallas_call(
        _outproj_body,
        grid=(n // tm,),
        in_specs=[row(WIDTH_A), row(WIDTH_B), row(D_MODEL), full(w_out), full(lnw), full(lnb),
                  full(wr_hi), full(wr_lo), full(br)],
        out_specs=[row(D_MODEL), pl.BlockSpec((N_EXPERTS, tm), lambda i: (0, i))],
        out_shape=[jax.ShapeDtypeStruct((n, D_MODEL), F32), jax.ShapeDtypeStruct((N_EXPERTS, n), F32)],
        compiler_params=_cparams("parallel"),
        name="outproj",
    )(oa, ob, x2d, w_out, lnw, lnb, wr_hi, wr_lo, br)


def _route_body(lg_ref, idx_ref, gate_ref, rank_ref, cnt_ref, cnt_scr, *, tr):
    @pl.when(pl.program_id(0) == 0)
    def _():
        cnt_scr[...] = jnp.zeros_like(cnt_scr)

    l = lg_ref[...]
    rows = lax.broadcasted_iota(I32, (N_EXPERTS, tr), 0)
    vals, hots = [], []
    for j in range(TOP_K):
        m = jnp.max(l, axis=0, keepdims=True)
        idx = jnp.min(jnp.where(l == m, rows, N_EXPERTS), axis=0, keepdims=True)
        hot = rows == idx
        idx_ref[j:j + 1, :] = idx
        vals.append(m)
        hots.append(hot)
        l = jnp.where(hot, NEG_INF, l)
    es = [jnp.exp(v - vals[0]) for v in vals]
    inv = 1.0 / (es[0] + es[1] + es[2] + es[3])
    for j in range(TOP_K):
        gate_ref[j:j + 1, :] = es[j] * inv
    chosen = jnp.zeros((N_EXPERTS, tr), F32)
    for hot in hots:
        chosen = chosen + jnp.where(hot, 1.0, 0.0)
    r = lax.broadcasted_iota(I32, (tr, tr), 0)
    c = lax.broadcasted_iota(I32, (tr, tr), 1)
    before = jnp.where(r < c, 1.0, 0.0).astype(BF16)
    prior = _dot(chosen.astype(BF16), before) + cnt_scr[:, 0:1]
    for j in range(TOP_K):
        rank_ref[j:j + 1, :] = jnp.sum(jnp.where(hots[j], prior, 0.0), axis=0, keepdims=True).astype(I32)
    cnt_scr[...] = cnt_scr[...] + jnp.sum(chosen, axis=1, keepdims=True)
    cnt_ref[...] = cnt_scr[...].astype(I32)


def _route(logits_t):
    n = logits_t.shape[1]
    tr = min(512, n)
    tok = lambda rows: pl.BlockSpec((rows, tr), lambda i: (0, i))
    return pl.pallas_call(
        functools.partial(_route_body, tr=tr),
        grid=(n // tr,),
        in_specs=[tok(N_EXPERTS)],
        out_specs=[tok(TOP_K), tok(TOP_K), tok(TOP_K), pl.BlockSpec((N_EXPERTS, 128), lambda i: (0, 0))],
        out_shape=[jax.ShapeDtypeStruct((TOP_K, n), I32), jax.ShapeDtypeStruct((TOP_K, n), F32),
                   jax.ShapeDtypeStruct((TOP_K, n), I32), jax.ShapeDtypeStruct((N_EXPERTS, 128), I32)],
        scratch_shapes=[pltpu.VMEM((N_EXPERTS, 128), F32)],
        compiler_params=_cparams("arbitrary"),
        name="route",
    )(logits_t)


def _dest_body(idx_ref, rank_ref, start_ref, dest_ref, *, tr):
    rows = lax.broadcasted_iota(I32, (N_EXPERTS, tr), 0)
    for j in range(TOP_K):
        base = jnp.sum(jnp.where(rows == idx_ref[j:j + 1, :], start_ref[...], 0), axis=0, keepdims=True)
        dest_ref[j:j + 1, :] = base + rank_ref[j:j + 1, :]


def _dest(idx_t, rank_t, seg_start):
    n = idx_t.shape[1]
    tr = min(512, n)
    tok = pl.BlockSpec((TOP_K, tr), lambda i: (0, i))
    return pl.pallas_call(
        functools.partial(_dest_body, tr=tr),
        grid=(n // tr,),
        in_specs=[tok, tok, pl.BlockSpec((N_EXPERTS, 1), lambda i: (0, 0))],
        out_specs=tok,
        out_shape=jax.ShapeDtypeStruct((TOP_K, n), I32),
        compiler_params=_cparams("parallel"),
        name="slot_dest",
    )(idx_t, rank_t, seg_start)


def _dispatch_body(seg_ref, dest_ref, x_ref, buf_ref, zero_scr, zsem, rsem, *, td, n_blocks):
    def row_copy(n, j):
        return pltpu.make_async_copy(x_ref.at[pl.ds(n, 1), :], buf_ref.at[pl.ds(dest_ref[j, n], 1), :], rsem)

    @pl.when(pl.program_id(0) == 0)
    def _():
        zero_scr[...] = jnp.zeros_like(zero_scr)
        n_used = seg_ref[1, N_EXPERTS - 1] // EXPERT_ROWS

        def block_copy(start):
            return pltpu.make_async_copy(
                zero_scr, buf_ref.at[pl.ds(pl.multiple_of(start, EXPERT_ROWS), EXPERT_ROWS), :], zsem)

        def clears(action):
            for e in range(N_EXPERTS):
                @pl.when(seg_ref[1, e] > seg_ref[0, e])
                def _():
                    action(block_copy(seg_ref[1, e] - EXPERT_ROWS))

                @pl.when(n_used + e < n_blocks)
                def _():
                    action(block_copy((n_used + e) * EXPERT_ROWS))

        clears(lambda cp: cp.start())
        clears(lambda cp: cp.wait())

    def issue(n, _):
        for j in range(TOP_K):
            row_copy(n, j).start()
        return 0

    lax.fori_loop(0, td, issue, 0)

    def drain(n, _):
        for j in range(TOP_K):
            row_copy(n, j).wait()
        return 0

    lax.fori_loop(0, td, drain, 0)


def _dispatch(x1, dest_t, seg, n_rows):
    n = x1.shape[0]
    td = min(128, n)
    return pl.pallas_call(
        functools.partial(_dispatch_body, td=td, n_blocks=n_rows // EXPERT_ROWS),
        grid_spec=pltpu.PrefetchScalarGridSpec(
            num_scalar_prefetch=1,
            grid=(n // td,),
            in_specs=[pl.BlockSpec((TOP_K, td), lambda i, seg: (0, i), memory_space=pltpu.SMEM),
                      pl.BlockSpec((td, D_MODEL), lambda i, seg: (i, 0))],
            out_specs=pl.BlockSpec(memory_space=pl.ANY),
            scratch_shapes=[pltpu.VMEM((EXPERT_ROWS, D_MODEL), F32),
                            pltpu.SemaphoreType.DMA(()), pltpu.SemaphoreType.DMA(())],
        ),
        out_shape=jax.ShapeDtypeStruct((n_rows, D_MODEL), F32),
        compiler_params=_cparams("arbitrary"),
        name="dispatch",
    )(seg, dest_t, x1)


def _experts_body(be_ref, nu_ref, x_ref, wgu_ref, bgu_ref, wd_ref, bd_ref, o_ref):
    @pl.when(pl.program_id(0) < nu_ref[0])
    def _():
        gu = _dot(x_ref[...].astype(BF16), wgu_ref[0]) + bgu_ref[0]
        gate = jnp.minimum(gu[:, :D_EXPERT], SWIGLU_LIMIT)
        up = jnp.clip(gu[:, D_EXPERT:], -SWIGLU_LIMIT, SWIGLU_LIMIT)
        hdn = (up + 1.0) * gate * jax.nn.sigmoid(SWIGLU_ALPHA * gate)
        o_ref[...] = _dot(hdn.astype(BF16), wd_ref[0]) + bd_ref[0]

    @pl.when(pl.program_id(0) >= nu_ref[0])
    def _():
        o_ref[...] = jnp.zeros_like(o_ref)


def _experts(buf, block_e, n_used, w_gu, b_gu, w_down, b_down):
    n_blocks = buf.shape[0] // EXPERT_ROWS
    rows = pl.BlockSpec((EXPERT_ROWS, D_MODEL), lambda i, be, nu: (jnp.minimum(i, nu[0] - 1), 0))
    per_e = lambda a: pl.BlockSpec((1,) + a.shape[1:], lambda i, be, nu: (be[i],) + (0,) * (a.ndim - 1))
    return pl.pallas_call(
        _experts_body,
        grid_spec=pltpu.PrefetchScalarGridSpec(
            num_scalar_prefetch=2,
            grid=(n_blocks,),
            in_specs=[rows, per_e(w_gu), per_e(b_gu), per_e(w_down), per_e(b_down)],
            out_specs=pl.BlockSpec((EXPERT_ROWS, D_MODEL), lambda i, be, nu: (i, 0)),
        ),
        out_shape=jax.ShapeDtypeStruct(buf.shape, F32),
        compiler_params=_cparams("arbitrary"),
        name="experts",
    )(block_e, n_used, buf, w_gu, b_gu, w_down, b_down)


def _final_body(dest_ref, gate_ref, x1_ref, p_ref, lnw_ref, lnb_ref, wg_ref, bg_ref, wp_ref, eo_ref, y_ref,
                rows_scr, sem, *, tf):
    def row_copy(j, n):
        return pltpu.make_async_copy(eo_ref.at[pl.ds(dest_ref[j, n], 1), :], rows_scr.at[j, pl.ds(n, 1), :], sem)

    def issue(n, _):
        for j in range(TOP_K):
            row_copy(j, n).start()
        return 0

    lax.fori_loop(0, tf, issue, 0)

    def drain(n, _):
        for j in range(TOP_K):
            row_copy(j, n).wait()
        return 0

    lax.fori_loop(0, tf, drain, 0)

    moe = jnp.zeros((tf, D_MODEL), F32)
    for j in range(TOP_K):
        moe = moe + rows_scr[j] * gate_ref[:, j:j + 1]
    x2 = _layer_norm(DN_ALPHA * x1_ref[...] + moe, lnw_ref[...], lnb_ref[...])
    ple_gate = jax.nn.sigmoid(_dot(x2.astype(BF16), wg_ref[...]) + bg_ref[...])
    y_ref[...] = x2 + ple_gate * _dot(p_ref[...].astype(BF16), wp_ref[...])


def _final(dest_t, gate, x1, p2d, lnw, lnb, w_gate, b_gate, w_ple, expert_out):
    n = x1.shape[0]
    tf = min(128, n)
    row = lambda width: pl.BlockSpec((tf, width), lambda i: (i, 0))
    full = lambda a: pl.BlockSpec(a.shape, lambda i: (0,) * a.ndim)
    return pl.pallas_call(
        functools.partial(_final_body, tf=tf),
        grid=(n // tf,),
        in_specs=[pl.BlockSpec((TOP_K, tf), lambda i: (0, i), memory_space=pltpu.SMEM),
                  row(TOP_K), row(D_MODEL), row(D_PLE), full(lnw), full(lnb), full(w_gate), full(b_gate),
                  full(w_ple), pl.BlockSpec(memory_space=pl.ANY)],
        out_specs=row(D_MODEL),
        out_shape=jax.ShapeDtypeStruct((n, D_MODEL), F32),
        scratch_shapes=[pltpu.VMEM((TOP_K, tf, D_MODEL), F32), pltpu.SemaphoreType.DMA(())],
        compiler_params=_cparams("arbitrary"),
        name="combine_final",
    )(dest_t, gate, x1, p2d, lnw, lnb, w_gate, b_gate, w_ple, expert_out)


def _round_up(a, m):
    return -(-a // m) * m


def _encoder_layer(x, p, hist_k, hist_v, hist_kidx, s0, pos0, wts):
    b, t, _ = x.shape
    n = b * t
    x2d = x.reshape(n, D_MODEL)

    qa, ka, va, qi, ki, wi, hb = _inproj(x2d, wts["w_in"], wts["idx_lnw"], wts["idx_lnb"])

    k_new = ka.reshape(b, t, N_KV_A * HEAD_DIM_A)
    v_new = va.reshape(b, t, N_KV_A * HEAD_DIM_A)
    ki_new = ki.reshape(b, t, IDX_DIM)
    if hist_k is None:
        k_all, v_all, ki_all = k_new, v_new, ki_new
    else:
        past = hist_k.shape[1]
        k_all = jnp.concatenate([hist_k.reshape(b, past, -1), k_new], axis=1)
        v_all = jnp.concatenate([hist_v.reshape(b, past, -1), v_new], axis=1)
        ki_all = jnp.concatenate([hist_kidx, ki_new], axis=1)
    s_real = k_all.shape[1]
    topk = min(TOPK_MAX, s_real // 4)
    pad = ((0, 0), (0, _round_up(s_real, KEY_BLOCK) - s_real), (0, 0))
    k_bf = jnp.pad(k_all.astype(BF16), pad)
    vt_bf = jnp.pad(v_all.astype(BF16), pad).transpose(0, 2, 1)
    ki_bf = jnp.pad(ki_all.astype(BF16), pad)
    wi_t = wi.reshape(b, t, N_IDX_HEADS).transpose(0, 2, 1)
    oa_t = _dsa(qa.reshape(b, t, WIDTH_A), qi.reshape(b, t, N_IDX_HEADS * IDX_DIM), wi_t, ki_bf, k_bf, vt_bf,
                s_real=s_real, pos0=pos0, topk=topk)
    oa = oa_t.transpose(0, 2, 1).reshape(n, WIDTH_A)

    ob, s_fin = _hgrn(hb.reshape(b, t, 4 * WIDTH_B), s0, wts["lb"], wts["hgrn_nw"])

    x1, logits_t = _outproj(oa, ob.reshape(n, WIDTH_B), x2d, wts["w_out"], wts["ln1_w"], wts["ln1_b"],
                            wts["wr_hi"], wts["wr_lo"], wts["b_router"])

    idx_t, gate_t, rank_t, counts = _route(logits_t)
    counts = counts[:, 0]
    padded = (counts + EXPERT_ROWS - 1) // EXPERT_ROWS * EXPERT_ROWS
    seg_end = jnp.cumsum(padded)
    seg_start = seg_end - padded
    n_blocks = -(-(n * TOP_K + N_EXPERTS * (EXPERT_ROWS - 1)) // EXPERT_ROWS)
    block_e = jnp.minimum(
        jnp.searchsorted(seg_end, jnp.arange(n_blocks, dtype=I32) * EXPERT_ROWS, side="right"),
        N_EXPERTS - 1).astype(I32)
    n_used = (seg_end[-1:] // EXPERT_ROWS).astype(I32)
    dest_t = _dest(idx_t, rank_t, seg_start.reshape(N_EXPERTS, 1).astype(I32))
    seg = jnp.stack([seg_start, seg_end]).astype(I32)
    buf = _dispatch(x1, dest_t, seg, n_blocks * EXPERT_ROWS)
    expert_out = _experts(buf, block_e, n_used, wts["w_gu"], wts["b_gu"], wts["w_down"], wts["b_down"])
    y = _final(dest_t, gate_t.T, x1, p.reshape(n, D_PLE), wts["ln2_w"], wts["ln2_b"], wts["w_ple_gate"],
               wts["b_ple_gate"], wts["w_ple"], expert_out)

    return (y.reshape(b, t, D_MODEL), ka.reshape(b, t, N_KV_A, HEAD_DIM_A), va.reshape(b, t, N_KV_A, HEAD_DIM_A),
            ki_new, s_fin)


def _prep_weights(w_in, w_out, idx_k_norm_w, idx_k_norm_b, lb, hgrn_norm_w, ln1_w, ln1_b, w_router, b_router,
                  w_gu, b_gu, w_down, b_down, ln2_w, ln2_b, w_ple, w_ple_gate, b_ple_gate):
    n_a = COL_KW + IDX_DIM + N_IDX_HEADS
    w_pad = jnp.concatenate(
        [w_in[:, :n_a], jnp.zeros((D_MODEL, COL_HB - n_a), w_in.dtype), w_in[:, n_a:]], axis=1).astype(BF16)
    wr_t = w_router.T
    wr_hi = wr_t.astype(BF16)
    row = lambda a: a.reshape(1, -1)
    return dict(
        w_in=w_pad, idx_lnw=row(idx_k_norm_w), idx_lnb=row(idx_k_norm_b), lb=row(lb), hgrn_nw=row(hgrn_norm_w),
        w_out=w_out.astype(BF16), ln1_w=row(ln1_w), ln1_b=row(ln1_b),
        wr_hi=wr_hi, wr_lo=(wr_t - wr_hi.astype(F32)).astype(BF16), b_router=b_router.reshape(N_EXPERTS, 1),
        w_gu=w_gu.astype(BF16), b_gu=b_gu.reshape(N_EXPERTS, 1, 2 * D_EXPERT),
        w_down=w_down.astype(BF16), b_down=b_down.reshape(N_EXPERTS, 1, D_MODEL),
        ln2_w=row(ln2_w), ln2_b=row(ln2_b), w_ple=w_ple.astype(BF16), w_ple_gate=w_ple_gate.astype(BF16),
        b_ple_gate=row(b_ple_gate))


def kernel(x_prompt, x_sample, cache_k, cache_v, cache_kidx, state_hgrn, p_prompt, p_sample, w_in, w_out,
           idx_k_norm_w, idx_k_norm_b, hgrn_lb_logits, hgrn_norm_w, ln1_w, ln1_b, w_router, b_router, w_gu, b_gu,
           w_down, b_down, ln2_w, ln2_b, w_ple, w_ple_gate, b_ple_gate):
    lb_all = jnp.cumsum(jax.nn.softmax(hgrn_lb_logits.astype(F32), axis=0), axis=0)
    xp, xs = x_prompt, x_sample
    outs = [[] for _ in range(8)]
    for i in range(DEPTH):
        wts = _prep_weights(w_in[i], w_out[i], idx_k_norm_w[i], idx_k_norm_b[i], lb_all[i], hgrn_norm_w[i],
                            ln1_w[i], ln1_b[i], w_router[i], b_router[i], w_gu[i], b_gu[i], w_down[i], b_down[i],
                            ln2_w[i], ln2_b[i], w_ple[i], w_ple_gate[i], b_ple_gate[i])
        s0p = jnp.zeros((xp.shape[0], N_HEADS_B, HEAD_DIM_B, HEAD_DIM_B), F32)
        xp, kp, vp, kip, sp = _encoder_layer(xp, p_prompt[i], None, None, None, s0p, 0, wts)
        xs, kn, vn, kin, sn = _encoder_layer(xs, p_sample[i], cache_k[i], cache_v[i], cache_kidx[i],
                                             state_hgrn[i], cache_k.shape[2], wts)
        for lst, val in zip(outs, (kp, vp, kip, sp, kn, vn, kin, sn)):
            lst.append(val)
    return (xp, xs) + tuple(jnp.stack(l) for l in outs)
```

```python
import functools

import jax
import jax.numpy as jnp
from jax import lax
from jax.experimental import pallas as pl
from jax.experimental.pallas import tpu as pltpu

F32 = jnp.float32
BF16 = jnp.bfloat16
I32 = jnp.int32

D_MODEL = 1024
CHUNK = 64
CHUNK_SHIFT = 6
WIDTH_A = 512
HEAD_DIM_A = 64
N_HEADS_A = 8
N_KV_A = 2
N_IDX_HEADS = 4
IDX_DIM = 64
IDX_SCALE = IDX_DIM ** -0.5 * N_IDX_HEADS ** -0.5
TOPK_MAX = 256
WIDTH_B = 512
HEAD_DIM_B = 128
N_HEADS_B = 4
N_EXPERTS = 32
TOP_K = 4
D_EXPERT = 1024
SWIGLU_LIMIT = 7.0
SWIGLU_ALPHA = 1.702
D_PLE = 256
LN_EPS = 1e-5
RMS_EPS = 1e-6
DEPTH = 1
DN_ALPHA = (2 * DEPTH) ** 0.25

COL_QA, COL_KA, COL_VA, COL_QI, COL_KW, COL_HB, COL_END = 0, 512, 640, 768, 1024, 1152, 3200

VMEM_LIMIT = 56 * 1024 * 1024
KEY_BLOCK = 256
V_ROWS = 80
LOG2_E = 1.4426950408889634
HGRN_SUB = 16
EXPERT_ROWS = 512
NEG_INF = float("-inf")
INT_MIN = -(2 ** 31)


def _cparams(*sem):
    return pltpu.CompilerParams(dimension_semantics=sem, vmem_limit_bytes=VMEM_LIMIT)


def _dot(a, b):
    return jnp.dot(a, b, preferred_element_type=F32)


def _dot_nt(a, b):
    return lax.dot_general(a, b, (((1,), (1,)), ((), ())), preferred_element_type=F32)


def _dot_tn(a, b):
    return lax.dot_general(a, b, (((0,), (0,)), ((), ())), preferred_element_type=F32)


def _layer_norm(z, w, b):
    mu = jnp.mean(z, axis=-1, keepdims=True)
    d = z - mu
    var = jnp.mean(d * d, axis=-1, keepdims=True)
    return d * lax.rsqrt(var + LN_EPS) * w + b


def _inproj_body(x_ref, w_ref, lnw_ref, lnb_ref, qa_ref, ka_ref, va_ref, qi_ref, ki_ref, wi_ref, hb_ref):
    xb = x_ref[...].astype(BF16)

    def mm(c0, c1):
        return _dot(xb, w_ref[:, c0:c1])

    qa_ref[...] = (mm(COL_QA, COL_KA) * (HEAD_DIM_A ** -0.5 * LOG2_E)).astype(BF16)
    ka_ref[...] = mm(COL_KA, COL_VA)
    va_ref[...] = mm(COL_VA, COL_QI)
    qi_ref[...] = mm(COL_QI, COL_KW).astype(BF16)
    kw = mm(COL_KW, COL_HB)
    ki_ref[...] = _layer_norm(kw[:, :IDX_DIM], lnw_ref[...], lnb_ref[...])
    wi_ref[...] = kw[:, IDX_DIM:IDX_DIM + N_IDX_HEADS]
    hb_ref[...] = mm(COL_HB, COL_END)


def _inproj(x2d, w_pad, lnw, lnb):
    n = x2d.shape[0]
    tm = min(512, n)
    row = lambda width: pl.BlockSpec((tm, width), lambda i: (i, 0))
    full = lambda a: pl.BlockSpec(a.shape, lambda i: (0,) * a.ndim)
    widths = (WIDTH_A, 128, 128, 256, IDX_DIM, N_IDX_HEADS, 4 * WIDTH_B)
    dtypes = (BF16, F32, F32, BF16, F32, F32, F32)
    return pl.pallas_call(
        _inproj_body,
        grid=(n // tm,),
        in_specs=[row(D_MODEL), full(w_pad), full(lnw), full(lnb)],
        out_specs=[row(w) for w in widths],
        out_shape=[jax.ShapeDtypeStruct((n, w), d) for w, d in zip(widths, dtypes)],
        compiler_params=_cparams("parallel"),
        name="inproj",
    )(x2d, w_pad, lnw, lnb)


def _order_bits_to_f32(u):
    key = u ^ INT_MIN
    bits = key ^ ((key >> 31) & 0x7FFFFFFF)
    f = lax.bitcast_convert_type(bits, F32)
    return jnp.where(u >= 0, jnp.where(u <= 0x007FFFFF, NEG_INF, f), f)


def _dsa_body(qa_ref, qi_ref, wi_ref, kidx_ref, k_ref, vt_ref, o_ref, sc_scr, bias_scr, s_scr, acc_scr,
              *, s_pad, s_real, qb, pos0, topk):
    kb_rows = KEY_BLOCK
    heads_per_kv = N_HEADS_A // N_KV_A
    j = pl.program_id(1)
    q_lo = pos0 + j * qb
    q_chunk = (q_lo + lax.broadcasted_iota(I32, (1, qb), 1)) >> CHUNK_SHIFT
    k_lim = (((q_lo + qb - 1) >> CHUNK_SHIFT) + 1) * CHUNK
    nkb = jnp.minimum(s_pad // kb_rows, (k_lim + kb_rows - 1) // kb_rows)

    def rows_of(kb):
        return pl.ds(pl.multiple_of(kb * kb_rows, kb_rows), kb_rows)

    def fold8(a):
        return a.reshape(kb_rows // 8, 8, qb)

    def score_blk(kb, _):
        rows = rows_of(kb)
        kidx = kidx_ref[0, rows, :]
        for h in range(N_IDX_HEADS):
            lg = _dot_nt(kidx, qi_ref[0, :, h * IDX_DIM:(h + 1) * IDX_DIM])
            term = jnp.maximum(lg, 0.0) * wi_ref[0, h:h + 1, :]
            if h == 0:
                sc_scr[rows, :] = term
            elif h < N_IDX_HEADS - 1:
                sc_scr[rows, :] += term
            else:
                spos = kb * kb_rows + lax.broadcasted_iota(I32, (kb_rows, qb), 0)
                sc = jnp.where((spos >> CHUNK_SHIFT) <= q_chunk,
                               (sc_scr[rows, :] + term) * IDX_SCALE + 0.0, NEG_INF)
                if s_real < s_pad:
                    sc = jnp.where(spos < s_real, sc, NEG_INF)
                sc_scr[rows, :] = sc
        return 0

    lax.fori_loop(0, nkb, score_blk, 0)

    def count(thr, strict):
        def body(kb, acc):
            blk = sc_scr[rows_of(kb), :]
            hit = (blk > thr) if strict else (blk >= thr)
            return acc + jnp.sum(fold8(jnp.where(hit, 1, 0).astype(I32)), axis=0)
        acc = lax.fori_loop(0, nkb, body, jnp.zeros((8, qb), I32))
        return jnp.sum(acc, axis=0, keepdims=True)

    def bit_step(i, carry):
        prefix, cnt = carry
        cand = prefix | lax.shift_left(jnp.int32(1), 31 - i)
        c = count(_order_bits_to_f32(cand), False)
        take = c >= topk
        return jnp.where(take, cand, prefix), jnp.where(take, c, cnt)

    prefix, cnt_ge = lax.fori_loop(0, 32, bit_step,
                                   (jnp.zeros((1, qb), I32), jnp.full((1, qb), nkb * kb_rows, I32)))
    tau = _order_bits_to_f32(prefix)
    finite_tau = tau > NEG_INF
    tau_floor = jnp.maximum(tau, jnp.finfo(F32).min)
    tie_lanes = jnp.where(finite_tau, jnp.where(cnt_ge > topk, 1, 0), 0)
    has_ties = jnp.max(tie_lanes) > 0

    @pl.when(jnp.logical_not(has_ties))
    def _():
        def body(kb, _):
            blk = sc_scr[rows_of(kb), :]
            bias_scr[rows_of(kb), :] = jnp.where(blk >= tau_floor, 0.0, NEG_INF)
            return 0
        lax.fori_loop(0, nkb, body, 0)

    @pl.when(has_ties)
    def _():
        need = jnp.where(finite_tau, (topk - count(tau, True)).astype(F32), 0.0)
        r = lax.broadcasted_iota(I32, (kb_rows, kb_rows), 0)
        c = lax.broadcasted_iota(I32, (kb_rows, kb_rows), 1)
        tril = jnp.where(r >= c, 1.0, 0.0).astype(BF16)

        def body(kb, seen):
            blk = sc_scr[rows_of(kb), :]
            eq = blk == tau
            rank = _dot(tril, jnp.where(eq, 1.0, 0.0).astype(BF16)) + seen
            tie_bias = jnp.where(eq, jnp.where(rank <= need, 0.0, NEG_INF), NEG_INF)
            bias_scr[rows_of(kb), :] = jnp.where(blk > tau, 0.0, tie_bias)
            return rank[kb_rows - 1:kb_rows, :]
        lax.fori_loop(0, nkb, body, jnp.zeros((1, qb), F32))

    def pass1(kb, m8):
        rows = rows_of(kb)
        bias = bias_scr[rows, :]
        out = []
        for h in range(N_HEADS_A):
            g = h // heads_per_kv
            s = _dot_nt(k_ref[0, rows, g * HEAD_DIM_A:(g + 1) * HEAD_DIM_A],
                        qa_ref[0, :, h * HEAD_DIM_A:(h + 1) * HEAD_DIM_A]) + bias
            s_scr[h, rows, :] = s
            out.append(jnp.maximum(m8[h], jnp.max(fold8(s), axis=0)))
        return tuple(out)

    m8 = lax.fori_loop(0, nkb, pass1, tuple(jnp.full((8, qb), NEG_INF, F32) for _ in range(N_HEADS_A)))
    m = [jnp.max(x, axis=0, keepdims=True) for x in m8]

    acc_scr[...] = jnp.zeros_like(acc_scr)

    def pass2(kb, _):
        rows = rows_of(kb)
        for h in range(N_HEADS_A):
            g = h // heads_per_kv
            p = jnp.exp2(s_scr[h, rows, :] - m[h]).astype(BF16)
            acc_scr[h] += _dot(vt_ref[0, g * V_ROWS:(g + 1) * V_ROWS, rows], p)
        return 0

    lax.fori_loop(0, nkb, pass2, 0)
    for h in range(N_HEADS_A):
        a = acc_scr[h]
        o_ref[0, h * HEAD_DIM_A:(h + 1) * HEAD_DIM_A, :] = (
            a[:HEAD_DIM_A] * (1.0 / a[HEAD_DIM_A:HEAD_DIM_A + 1])).astype(BF16)


def _dsa(qa, qi, wi_t, kidx, k_all, vt_all, *, s_real, pos0, topk):
    b, t, _ = qa.shape
    s_pad = kidx.shape[1]
    qb = min(256, t)
    body = functools.partial(_dsa_body, s_pad=s_pad, s_real=s_real, qb=qb, pos0=pos0, topk=topk)
    return pl.pallas_call(
        body,
        grid=(b, t // qb),
        in_specs=[
            pl.BlockSpec((1, qb, WIDTH_A), lambda i, j: (i, j, 0)),
            pl.BlockSpec((1, qb, N_IDX_HEADS * IDX_DIM), lambda i, j: (i, j, 0)),
            pl.BlockSpec((1, N_IDX_HEADS, qb), lambda i, j: (i, 0, j)),
            pl.BlockSpec((1, s_pad, IDX_DIM), lambda i, j: (i, 0, 0)),
            pl.BlockSpec((1, s_pad, N_KV_A * HEAD_DIM_A), lambda i, j: (i, 0, 0)),
            pl.BlockSpec((1, N_KV_A * V_ROWS, s_pad), lambda i, j: (i, 0, 0)),
        ],
        out_specs=pl.BlockSpec((1, WIDTH_A, qb), lambda i, j: (i, 0, j)),
        out_shape=jax.ShapeDtypeStruct((b, WIDTH_A, t), BF16),
        scratch_shapes=[pltpu.VMEM((s_pad, qb), F32), pltpu.VMEM((s_pad, qb), F32),
                        pltpu.VMEM((N_HEADS_A, s_pad, qb), F32), pltpu.VMEM((N_HEADS_A, V_ROWS, qb), F32)],
        compiler_params=_cparams("parallel", "parallel"),
        name="dsa",
    )(qa, qi, wi_t, kidx, k_all, vt_all)


def _split3(a):
    hi = a.astype(BF16)
    r1 = a - hi.astype(F32)
    mid = r1.astype(BF16)
    lo = (r1 - mid.astype(F32)).astype(BF16)
    return hi, mid, lo


def _hgrn_body(hb_ref, s0_ref, lb_ref, nw_ref, ob_ref, sfin_ref, state_scr, *, tb):
    t = pl.program_id(1)

    @pl.when(t == 0)
    def _():
        state_scr[...] = s0_ref[0]

    lb = lb_ref[...]
    r = lax.broadcasted_iota(I32, (CHUNK, CHUNK), 0)
    c = lax.broadcasted_iota(I32, (CHUNK, CHUNK), 1)
    tril = jnp.where(r >= c, 1.0, 0.0).astype(BF16)
    ones = jnp.ones((CHUNK, HEAD_DIM_B), BF16)

    for ci in range(tb // CHUNK):
        rows = slice(ci * CHUNK, (ci + 1) * CHUNK)
        f = lb + (1.0 - lb) * jax.nn.sigmoid(hb_ref[0, rows, WIDTH_B:2 * WIDTH_B])
        log_f = jnp.log(f)
        parts = _split3(log_f)
        bcum = _dot(tril, parts[0]) + _dot(tril, parts[1]) + _dot(tril, parts[2])
        for h in range(N_HEADS_B):
            lanes = slice(h * HEAD_DIM_B, (h + 1) * HEAD_DIM_B)
            q = hb_ref[0, rows, lanes]
            k = 1.0 - f[:, lanes]
            v = hb_ref[0, rows, 2 * WIDTH_B + h * HEAD_DIM_B:2 * WIDTH_B + (h + 1) * HEAD_DIM_B]
            gate = hb_ref[0, rows, 3 * WIDTH_B + h * HEAD_DIM_B:3 * WIDTH_B + (h + 1) * HEAD_DIM_B]
            bh = bcum[:, lanes]
            b_last = bh[CHUNK - 1:CHUNK, :]
            vb = v.astype(BF16)
            state = state_scr[h]
            o_inter = _dot((q * jnp.exp(bh)).astype(BF16), state.astype(BF16))
            o_rows = []
            for i in range(CHUNK // HGRN_SUB):
                lo, hi = i * HGRN_SUB, (i + 1) * HGRN_SUB
                ref = bh[lo - 1:lo, :] if i else jnp.zeros((1, HEAD_DIM_B), F32)
                qs = (q[lo:hi] * jnp.exp(bh[lo:hi] - ref)).astype(BF16)
                ks = (k[:hi] * jnp.exp(ref - bh[:hi])).astype(BF16)
                a = _dot_nt(qs, ks)
                tq = lo + lax.broadcasted_iota(I32, (HGRN_SUB, hi), 0)
                tk = lax.broadcasted_iota(I32, (HGRN_SUB, hi), 1)
                a = jnp.where(tk <= tq, a, 0.0)
                o_rows.append(_dot(a.astype(BF16), vb[:hi]))
            o = o_inter + jnp.concatenate(o_rows, axis=0)
            kdec = (k * jnp.exp(b_last - bh)).astype(BF16)
            col_sum = (_dot_tn(parts[0][:, lanes], ones) + _dot_tn(parts[1][:, lanes], ones)
                       + _dot_tn(parts[2][:, lanes], ones))
            state_scr[h] = jnp.exp(col_sum) * state + _dot_tn(kdec, vb)
            ms = jnp.mean(o * o, axis=-1, keepdims=True)
            y = o * lax.rsqrt(ms + RMS_EPS) * nw_ref[...] * (gate * jax.nn.sigmoid(gate))
            ob_ref[0, rows, lanes] = y.astype(BF16)

    @pl.when(t == pl.num_programs(1) - 1)
    def _():
        sfin_ref[0] = state_scr[...]


def _hgrn(hb, s0, lb, nw):
    b, t, _ = hb.shape
    tb = min(256, t)
    return pl.pallas_call(
        functools.partial(_hgrn_body, tb=tb),
        grid=(b, t // tb),
        in_specs=[
            pl.BlockSpec((1, tb, 4 * WIDTH_B), lambda i, j: (i, j, 0)),
            pl.BlockSpec((1, N_HEADS_B, HEAD_DIM_B, HEAD_DIM_B), lambda i, j: (i, 0, 0, 0)),
            pl.BlockSpec((1, WIDTH_B), lambda i, j: (0, 0)),
            pl.BlockSpec((1, HEAD_DIM_B), lambda i, j: (0, 0)),
        ],
        out_specs=[
            pl.BlockSpec((1, tb, WIDTH_B), lambda i, j: (i, j, 0)),
            pl.BlockSpec((1, N_HEADS_B, HEAD_DIM_B, HEAD_DIM_B), lambda i, j: (i, 0, 0, 0)),
        ],
        out_shape=[jax.ShapeDtypeStruct((b, t, WIDTH_B), BF16),
                   jax.ShapeDtypeStruct((b, N_HEADS_B, HEAD_DIM_B, HEAD_DIM_B), F32)],
        scratch_shapes=[pltpu.VMEM((N_HEADS_B, HEAD_DIM_B, HEAD_DIM_B), F32)],
        compiler_params=_cparams("parallel", "arbitrary"),
        name="hgrn",
    )(hb, s0, lb, nw)


def _outproj_body(oa_ref, ob_ref, x_ref, w_ref, lnw_ref, lnb_ref, wrh_ref, wrl_ref, br_ref, x1_ref, lg_ref):
    y = _dot(oa_ref[...], w_ref[:WIDTH_A, :]) + _dot(ob_ref[...], w_ref[WIDTH_A:, :])
    x1 = _layer_norm(DN_ALPHA * x_ref[...] + y, lnw_ref[...], lnb_ref[...])
    x1_ref[...] = x1
    hi = x1.astype(BF16)
    lo = (x1 - hi.astype(F32)).astype(BF16)
    lg_ref[...] = (_dot_nt(wrh_ref[...], hi) + _dot_nt(wrh_ref[...], lo) + _dot_nt(wrl_ref[...], hi)
                   + br_ref[...])


def _outproj(oa, ob, x2d, w_out, lnw, lnb, wr_hi, wr_lo, br):
    n = x2d.shape[0]
    tm = min(512, n)
    row = lambda width: pl.BlockSpec((tm, width), lambda i: (i, 0))
    full = lambda a: pl.BlockSpec(a.shape, lambda i: (0,) * a.ndim)
    return pl.pallas_call(
        _outproj_body,
        grid=(n // tm,),
        in_specs=[row(WIDTH_A), row(WIDTH_B), row(D_MODEL), full(w_out), full(lnw), full(lnb),
                  full(wr_hi), full(wr_lo), full(br)],
        out_specs=[row(D_MODEL), pl.BlockSpec((N_EXPERTS, tm), lambda i: (0, i))],
        out_shape=[jax.ShapeDtypeStruct((n, D_MODEL), F32), jax.ShapeDtypeStruct((N_EXPERTS, n), F32)],
        compiler_params=_cparams("parallel"),
        name="outproj",
    )(oa, ob, x2d, w_out, lnw, lnb, wr_hi, wr_lo, br)


def _route_body(lg_ref, idx_ref, gate_ref, rank_ref, cnt_ref, cnt_scr, *, tr):
    @pl.when(pl.program_id(0) == 0)
    def _():
        cnt_scr[...] = jnp.zeros_like(cnt_scr)

    l = lg_ref[...]
    rows = lax.broadcasted_iota(I32, (N_EXPERTS, tr), 0)
    vals, hots = [], []
    for j in range(TOP_K):
        m = jnp.max(l, axis=0, keepdims=True)
        idx = jnp.min(jnp.where(l == m, rows, N_EXPERTS), axis=0, keepdims=True)
        hot = rows == idx
        idx_ref[j:j + 1, :] = idx
        vals.append(m)
        hots.append(hot)
        l = jnp.where(hot, NEG_INF, l)
    es = [jnp.exp(v - vals[0]) for v in vals]
    inv = 1.0 / (es[0] + es[1] + es[2] + es[3])
    for j in range(TOP_K):
        gate_ref[j:j + 1, :] = es[j] * inv
    chosen = jnp.zeros((N_EXPERTS, tr), F32)
    for hot in hots:
        chosen = chosen + jnp.where(hot, 1.0, 0.0)
    r = lax.broadcasted_iota(I32, (tr, tr), 0)
    c = lax.broadcasted_iota(I32, (tr, tr), 1)
    before = jnp.where(r < c, 1.0, 0.0).astype(BF16)
    prior = _dot(chosen.astype(BF16), before) + cnt_scr[:, 0:1]
    for j in range(TOP_K):
        rank_ref[j:j + 1, :] = jnp.sum(jnp.where(hots[j], prior, 0.0), axis=0, keepdims=True).astype(I32)
    cnt_scr[...] = cnt_scr[...] + jnp.sum(chosen, axis=1, keepdims=True)
    cnt_ref[...] = cnt_scr[...].astype(I32)


def _route(logits_t):
    n = logits_t.shape[1]
    tr = min(512, n)
    tok = lambda rows: pl.BlockSpec((rows, tr), lambda i: (0, i))
    return pl.pallas_call(
        functools.partial(_route_body, tr=tr),
        grid=(n // tr,),
        in_specs=[tok(N_EXPERTS)],
        out_specs=[tok(TOP_K), tok(TOP_K), tok(TOP_K), pl.BlockSpec((N_EXPERTS, 128), lambda i: (0, 0))],
        out_shape=[jax.ShapeDtypeStruct((TOP_K, n), I32), jax.ShapeDtypeStruct((TOP_K, n), F32),
                   jax.ShapeDtypeStruct((TOP_K, n), I32), jax.ShapeDtypeStruct((N_EXPERTS, 128), I32)],
        scratch_shapes=[pltpu.VMEM((N_EXPERTS, 128), F32)],
        compiler_params=_cparams("arbitrary"),
        name="route",
    )(logits_t)


def _dest_body(idx_ref, rank_ref, start_ref, dest_ref, *, tr):
    rows = lax.broadcasted_iota(I32, (N_EXPERTS, tr), 0)
    for j in range(TOP_K):
        base = jnp.sum(jnp.where(rows == idx_ref[j:j + 1, :], start_ref[...], 0), axis=0, keepdims=True)
        dest_ref[j:j + 1, :] = base + rank_ref[j:j + 1, :]


def _dest(idx_t, rank_t, seg_start):
    n = idx_t.shape[1]
    tr = min(512, n)
    tok = pl.BlockSpec((TOP_K, tr), lambda i: (0, i))
    return pl.pallas_call(
        functools.partial(_dest_body, tr=tr),
        grid=(n // tr,),
        in_specs=[tok, tok, pl.BlockSpec((N_EXPERTS, 1), lambda i: (0, 0))],
        out_specs=tok,
        out_shape=jax.ShapeDtypeStruct((TOP_K, n), I32),
        compiler_params=_cparams("parallel"),
        name="slot_dest",
    )(idx_t, rank_t, seg_start)


def _dispatch_body(seg_ref, dest_ref, x_ref, buf_ref, zero_scr, zsem, rsem, *, td, n_blocks):
    def row_copy(n, j):
        return pltpu.make_async_copy(x_ref.at[pl.ds(n, 1), :], buf_ref.at[pl.ds(dest_ref[j, n], 1), :], rsem)

    @pl.when(pl.program_id(0) == 0)
    def _():
        zero_scr[...] = jnp.zeros_like(zero_scr)
        n_used = seg_ref[1, N_EXPERTS - 1] // EXPERT_ROWS

        def block_copy(start):
            return pltpu.make_async_copy(
                zero_scr, buf_ref.at[pl.ds(pl.multiple_of(start, EXPERT_ROWS), EXPERT_ROWS), :], zsem)

        def clears(action):
            for e in range(N_EXPERTS):
                @pl.when(seg_ref[1, e] > seg_ref[0, e])
                def _():
                    action(block_copy(seg_ref[1, e] - EXPERT_ROWS))

                @pl.when(n_used + e < n_blocks)
                def _():
                    action(block_copy((n_used + e) * EXPERT_ROWS))

        clears(lambda cp: cp.start())
        clears(lambda cp: cp.wait())

    def issue(n, _):
        for j in range(TOP_K):
            row_copy(n, j).start()
        return 0

    lax.fori_loop(0, td, issue, 0)

    def drain(n, _):
        for j in range(TOP_K):
            row_copy(n, j).wait()
        return 0

    lax.fori_loop(0, td, drain, 0)


def _dispatch(x1, dest_t, seg, n_rows):
    n = x1.shape[0]
    td = min(128, n)
    return pl.pallas_call(
        functools.partial(_dispatch_body, td=td, n_blocks=n_rows // EXPERT_ROWS),
        grid_spec=pltpu.PrefetchScalarGridSpec(
            num_scalar_prefetch=1,
            grid=(n // td,),
            in_specs=[pl.BlockSpec((TOP_K, td), lambda i, seg: (0, i), memory_space=pltpu.SMEM),
                      pl.BlockSpec((td, D_MODEL), lambda i, seg: (i, 0))],
            out_specs=pl.BlockSpec(memory_space=pl.ANY),
            scratch_shapes=[pltpu.VMEM((EXPERT_ROWS, D_MODEL), F32),
                            pltpu.SemaphoreType.DMA(()), pltpu.SemaphoreType.DMA(())],
        ),
        out_shape=jax.ShapeDtypeStruct((n_rows, D_MODEL), F32),
        compiler_params=_cparams("arbitrary"),
        name="dispatch",
    )(seg, dest_t, x1)


def _experts_body(be_ref, nu_ref, x_ref, wgu_ref, bgu_ref, wd_ref, bd_ref, o_ref):
    @pl.when(pl.program_id(0) < nu_ref[0])
    def _():
        gu = _dot(x_ref[...].astype(BF16), wgu_ref[0]) + bgu_ref[0]
        gate = jnp.minimum(gu[:, :D_EXPERT], SWIGLU_LIMIT)
        up = jnp.clip(gu[:, D_EXPERT:], -SWIGLU_LIMIT, SWIGLU_LIMIT)
        hdn = (up + 1.0) * gate * jax.nn.sigmoid(SWIGLU_ALPHA * gate)
        o_ref[...] = _dot(hdn.astype(BF16), wd_ref[0]) + bd_ref[0]

    @pl.when(pl.program_id(0) >= nu_ref[0])
    def _():
        o_ref[...] = jnp.zeros_like(o_ref)


def _experts(buf, block_e, n_used, w_gu, b_gu, w_down, b_down):
    n_blocks = buf.shape[0] // EXPERT_ROWS
    rows = pl.BlockSpec((EXPERT_ROWS, D_MODEL), lambda i, be, nu: (jnp.minimum(i, nu[0] - 1), 0))
    per_e = lambda a: pl.BlockSpec((1,) + a.shape[1:], lambda i, be, nu: (be[i],) + (0,) * (a.ndim - 1))
    return pl.pallas_call(
        _experts_body,
        grid_spec=pltpu.PrefetchScalarGridSpec(
            num_scalar_prefetch=2,
            grid=(n_blocks,),
            in_specs=[rows, per_e(w_gu), per_e(b_gu), per_e(w_down), per_e(b_down)],
            out_specs=pl.BlockSpec((EXPERT_ROWS, D_MODEL), lambda i, be, nu: (i, 0)),
        ),
        out_shape=jax.ShapeDtypeStruct(buf.shape, F32),
        compiler_params=_cparams("arbitrary"),
        name="experts",
    )(block_e, n_used, buf, w_gu, b_gu, w_down, b_down)


def _final_body(dest_ref, gate_ref, x1_ref, p_ref, lnw_ref, lnb_ref, wg_ref, bg_ref, wp_ref, eo_ref, y_ref,
                rows_scr, sem, *, tf):
    def row_copy(j, n):
        return pltpu.make_async_copy(eo_ref.at[pl.ds(dest_ref[j, n], 1), :], rows_scr.at[j, pl.ds(n, 1), :], sem)

    def issue(n, _):
        for j in range(TOP_K):
            row_copy(j, n).start()
        return 0

    lax.fori_loop(0, tf, issue, 0)

    def drain(n, _):
        for j in range(TOP_K):
            row_copy(j, n).wait()
        return 0

    lax.fori_loop(0, tf, drain, 0)

    moe = jnp.zeros((tf, D_MODEL), F32)
    for j in range(TOP_K):
        moe = moe + rows_scr[j] * gate_ref[:, j:j + 1]
    x2 = _layer_norm(DN_ALPHA * x1_ref[...] + moe, lnw_ref[...], lnb_ref[...])
    ple_gate = jax.nn.sigmoid(_dot(x2.astype(BF16), wg_ref[...]) + bg_ref[...])
    y_ref[...] = x2 + ple_gate * _dot(p_ref[...].astype(BF16), wp_ref[...])


def _final(dest_t, gate, x1, p2d, lnw, lnb, w_gate, b_gate, w_ple, expert_out):
    n = x1.shape[0]
    tf = min(128, n)
    row = lambda width: pl.BlockSpec((tf, width), lambda i: (i, 0))
    full = lambda a: pl.BlockSpec(a.shape, lambda i: (0,) * a.ndim)
    return pl.pallas_call(
        functools.partial(_final_body, tf=tf),
        grid=(n // tf,),
        in_specs=[pl.BlockSpec((TOP_K, tf), lambda i: (0, i), memory_space=pltpu.SMEM),
                  row(TOP_K), row(D_MODEL), row(D_PLE), full(lnw), full(lnb), full(w_gate), full(b_gate),
                  full(w_ple), pl.BlockSpec(memory_space=pl.ANY)],
        out_specs=row(D_MODEL),
        out_shape=jax.ShapeDtypeStruct((n, D_MODEL), F32),
        scratch_shapes=[pltpu.VMEM((TOP_K, tf, D_MODEL), F32), pltpu.SemaphoreType.DMA(())],
        compiler_params=_cparams("arbitrary"),
        name="combine_final",
    )(dest_t, gate, x1, p2d, lnw, lnb, w_gate, b_gate, w_ple, expert_out)


def _round_up(a, m):
    return -(-a // m) * m


def _encoder_layer(x, p, hist_k, hist_v, hist_kidx, s0, pos0, wts):
    b, t, _ = x.shape
    n = b * t
    x2d = x.reshape(n, D_MODEL)

    qa, ka, va, qi, ki, wi, hb = _inproj(x2d, wts["w_in"], wts["idx_lnw"], wts["idx_lnb"])

    k_new = ka.reshape(b, t, N_KV_A * HEAD_DIM_A)
    v_new = va.reshape(b, t, N_KV_A * HEAD_DIM_A)
    ki_new = ki.reshape(b, t, IDX_DIM)
    if hist_k is None:
        k_all, v_all, ki_all = k_new, v_new, ki_new
    else:
        past = hist_k.shape[1]
        k_all = jnp.concatenate([hist_k.reshape(b, past, -1), k_new], axis=1)
        v_all = jnp.concatenate([hist_v.reshape(b, past, -1), v_new], axis=1)
        ki_all = jnp.concatenate([hist_kidx, ki_new], axis=1)
    s_real = k_all.shape[1]
    topk = min(TOPK_MAX, s_real // 4)
    pad = ((0, 0), (0, _round_up(s_real, KEY_BLOCK) - s_real), (0, 0))
    k_bf = jnp.pad(k_all.astype(BF16), pad)
    s_pad = k_bf.shape[1]
    vt_bf = jnp.pad(v_all.astype(BF16), pad).transpose(0, 2, 1).reshape(b, N_KV_A, HEAD_DIM_A, s_pad)
    vt_bf = jnp.concatenate([vt_bf, jnp.ones((b, N_KV_A, V_ROWS - HEAD_DIM_A, s_pad), BF16)], axis=2)
    vt_bf = vt_bf.reshape(b, N_KV_A * V_ROWS, s_pad)
    ki_bf = jnp.pad(ki_all.astype(BF16), pad)
    wi_t = wi.reshape(b, t, N_IDX_HEADS).transpose(0, 2, 1)
    oa_t = _dsa(qa.reshape(b, t, WIDTH_A), qi.reshape(b, t, N_IDX_HEADS * IDX_DIM), wi_t, ki_bf, k_bf, vt_bf,
                s_real=s_real, pos0=pos0, topk=topk)
    oa = oa_t.transpose(0, 2, 1).reshape(n, WIDTH_A)

    ob, s_fin = _hgrn(hb.reshape(b, t, 4 * WIDTH_B), s0, wts["lb"], wts["hgrn_nw"])

    x1, logits_t = _outproj(oa, ob.reshape(n, WIDTH_B), x2d, wts["w_out"], wts["ln1_w"], wts["ln1_b"],
                            wts["wr_hi"], wts["wr_lo"], wts["b_router"])

    idx_t, gate_t, rank_t, counts = _route(logits_t)
    counts = counts[:, 0]
    padded = (counts + EXPERT_ROWS - 1) // EXPERT_ROWS * EXPERT_ROWS
    seg_end = jnp.cumsum(padded)
    seg_start = seg_end - padded
    n_blocks = -(-(n * TOP_K + N_EXPERTS * (EXPERT_ROWS - 1)) // EXPERT_ROWS)
    block_row0 = jnp.arange(n_blocks, dtype=I32) * EXPERT_ROWS
    block_e = jnp.minimum(jnp.sum((seg_end[None, :] <= block_row0[:, None]).astype(I32), axis=1), N_EXPERTS - 1)
    n_used = (seg_end[-1:] // EXPERT_ROWS).astype(I32)
    dest_t = _dest(idx_t, rank_t, seg_start.reshape(N_EXPERTS, 1).astype(I32))
    seg = jnp.stack([seg_start, seg_end]).astype(I32)
    buf = _dispatch(x1, dest_t, seg, n_blocks * EXPERT_ROWS)
    expert_out = _experts(buf, block_e, n_used, wts["w_gu"], wts["b_gu"], wts["w_down"], wts["b_down"])
    y = _final(dest_t, gate_t.T, x1, p.reshape(n, D_PLE), wts["ln2_w"], wts["ln2_b"], wts["w_ple_gate"],
               wts["b_ple_gate"], wts["w_ple"], expert_out)

    return (y.reshape(b, t, D_MODEL), ka.reshape(b, t, N_KV_A, HEAD_DIM_A), va.reshape(b, t, N_KV_A, HEAD_DIM_A),
            ki_new, s_fin)


def _prep_weights(w_in, w_out, idx_k_norm_w, idx_k_norm_b, lb, hgrn_norm_w, ln1_w, ln1_b, w_router, b_router,
                  w_gu, b_gu, w_down, b_down, ln2_w, ln2_b, w_ple, w_ple_gate, b_ple_gate):
    n_a = COL_KW + IDX_DIM + N_IDX_HEADS
    w_pad = jnp.concatenate(
        [w_in[:, :n_a], jnp.zeros((D_MODEL, COL_HB - n_a), w_in.dtype), w_in[:, n_a:]], axis=1).astype(BF16)
    wr_t = w_router.T
    wr_hi = wr_t.astype(BF16)
    row = lambda a: a.reshape(1, -1)
    return dict(
        w_in=w_pad, idx_lnw=row(idx_k_norm_w), idx_lnb=row(idx_k_norm_b), lb=row(lb), hgrn_nw=row(hgrn_norm_w),
        w_out=w_out.astype(BF16), ln1_w=row(ln1_w), ln1_b=row(ln1_b),
        wr_hi=wr_hi, wr_lo=(wr_t - wr_hi.astype(F32)).astype(BF16), b_router=b_router.reshape(N_EXPERTS, 1),
        w_gu=w_gu.astype(BF16), b_gu=b_gu.reshape(N_EXPERTS, 1, 2 * D_EXPERT),
        w_down=w_down.astype(BF16), b_down=b_down.reshape(N_EXPERTS, 1, D_MODEL),
        ln2_w=row(ln2_w), ln2_b=row(ln2_b), w_ple=w_ple.astype(BF16), w_ple_gate=w_ple_gate.astype(BF16),
        b_ple_gate=row(b_ple_gate))


def kernel(x_prompt, x_sample, cache_k, cache_v, cache_kidx, state_hgrn, p_prompt, p_sample, w_in, w_out,
           idx_k_norm_w, idx_k_norm_b, hgrn_lb_logits, hgrn_norm_w, ln1_w, ln1_b, w_router, b_router, w_gu, b_gu,
           w_down, b_down, ln2_w, ln2_b, w_ple, w_ple_gate, b_ple_gate):
    lb_all = jnp.cumsum(jax.nn.softmax(hgrn_lb_logits.astype(F32), axis=0), axis=0)
    xp, xs = x_prompt, x_sample
    outs = [[] for _ in range(8)]
    for i in range(DEPTH):
        wts = _prep_weights(w_in[i], w_out[i], idx_k_norm_w[i], idx_k_norm_b[i], lb_all[i], hgrn_norm_w[i],
                            ln1_w[i], ln1_b[i], w_router[i], b_router[i], w_gu[i], b_gu[i], w_down[i], b_down[i],
                            ln2_w[i], ln2_b[i], w_ple[i], w_ple_gate[i], b_ple_gate[i])
        s0p = jnp.zeros((xp.shape[0], N_HEADS_B, HEAD_DIM_B, HEAD_DIM_B), F32)
        xp, kp, vp, kip, sp = _encoder_layer(xp, p_prompt[i], None, None, None, s0p, 0, wts)
        xs, kn, vn, kin, sn = _encoder_layer(xs, p_sample[i], cache_k[i], cache_v[i], cache_kidx[i],
                                             state_hgrn[i], cache_k.shape[2], wts)
        for lst, val in zip(outs, (kp, vp, kip, sp, kn, vn, kin, sn)):
            lst.append(val)
    return (xp, xs) + tuple(jnp.stack(l) for l in outs)
```

```python
import functools

import jax
import jax.numpy as jnp
from jax import lax
from jax.experimental import pallas as pl
from jax.experimental.pallas import tpu as pltpu

F32 = jnp.float32
BF16 = jnp.bfloat16
I32 = jnp.int32

D_MODEL = 1024
CHUNK = 64
CHUNK_SHIFT = 6
WIDTH_A = 512
HEAD_DIM_A = 64
N_HEADS_A = 8
N_KV_A = 2
N_IDX_HEADS = 4
IDX_DIM = 64
IDX_SCALE = IDX_DIM ** -0.5 * N_IDX_HEADS ** -0.5
TOPK_MAX = 256
WIDTH_B = 512
HEAD_DIM_B = 128
N_HEADS_B = 4
N_EXPERTS = 32
TOP_K = 4
D_EXPERT = 1024
SWIGLU_LIMIT = 7.0
SWIGLU_ALPHA = 1.702
D_PLE = 256
LN_EPS = 1e-5
RMS_EPS = 1e-6
DEPTH = 1
DN_ALPHA = (2 * DEPTH) ** 0.25

COL_QA, COL_KA, COL_VA, COL_QI, COL_KW, COL_HB, COL_END = 0, 512, 640, 768, 1024, 1152, 3200

VMEM_LIMIT = 56 * 1024 * 1024
KEY_BLOCK = 256
V_ROWS = 80
LOG2_E = 1.4426950408889634
HGRN_SUB = 16
EXPERT_ROWS = 512
NEG_INF = float("-inf")
INT_MIN = -(2 ** 31)


def _cparams(*sem):
    return pltpu.CompilerParams(dimension_semantics=sem, vmem_limit_bytes=VMEM_LIMIT)


def _dot(a, b):
    return jnp.dot(a, b, preferred_element_type=F32)


def _dot_nt(a, b):
    return lax.dot_general(a, b, (((1,), (1,)), ((), ())), preferred_element_type=F32)


def _dot_tn(a, b):
    return lax.dot_general(a, b, (((0,), (0,)), ((), ())), preferred_element_type=F32)


def _layer_norm(z, w, b):
    mu = jnp.mean(z, axis=-1, keepdims=True)
    d = z - mu
    var = jnp.mean(d * d, axis=-1, keepdims=True)
    return d * lax.rsqrt(var + LN_EPS) * w + b


def _inproj_body(x_ref, w_ref, lnw_ref, lnb_ref, qa_ref, ka_ref, va_ref, qi_ref, ki_ref, wi_ref, hb_ref):
    xb = x_ref[...].astype(BF16)

    def mm(c0, c1):
        return _dot(xb, w_ref[:, c0:c1])

    qa = (mm(COL_QA, COL_KA) * (HEAD_DIM_A ** -0.5 * LOG2_E)).astype(BF16)
    for h in range(N_HEADS_A):
        qa_ref[h] = qa[:, h * HEAD_DIM_A:(h + 1) * HEAD_DIM_A]
    ka_ref[...] = mm(COL_KA, COL_VA)
    va_ref[...] = mm(COL_VA, COL_QI)
    qi = mm(COL_QI, COL_KW).astype(BF16)
    for h in range(N_IDX_HEADS):
        qi_ref[h] = qi[:, h * IDX_DIM:(h + 1) * IDX_DIM]
    kw = mm(COL_KW, COL_HB)
    ki_ref[...] = _layer_norm(kw[:, :IDX_DIM], lnw_ref[...], lnb_ref[...])
    wi_ref[...] = kw[:, IDX_DIM:IDX_DIM + N_IDX_HEADS]
    hb_ref[...] = mm(COL_HB, COL_END)


def _inproj(x2d, w_pad, lnw, lnb):
    n = x2d.shape[0]
    tm = min(512, n)
    row = lambda width: pl.BlockSpec((tm, width), lambda i: (i, 0))
    full = lambda a: pl.BlockSpec(a.shape, lambda i: (0,) * a.ndim)
    heads = lambda nh, d: pl.BlockSpec((nh, tm, d), lambda i: (0, i, 0))
    sds = jax.ShapeDtypeStruct
    return pl.pallas_call(
        _inproj_body,
        grid=(n // tm,),
        in_specs=[row(D_MODEL), full(w_pad), full(lnw), full(lnb)],
        out_specs=[heads(N_HEADS_A, HEAD_DIM_A), row(128), row(128), heads(N_IDX_HEADS, IDX_DIM),
                   row(IDX_DIM), row(N_IDX_HEADS), row(4 * WIDTH_B)],
        out_shape=[sds((N_HEADS_A, n, HEAD_DIM_A), BF16), sds((n, 128), F32), sds((n, 128), F32),
                   sds((N_IDX_HEADS, n, IDX_DIM), BF16), sds((n, IDX_DIM), F32), sds((n, N_IDX_HEADS), F32),
                   sds((n, 4 * WIDTH_B), F32)],
        compiler_params=_cparams("parallel"),
        name="inproj",
    )(x2d, w_pad, lnw, lnb)


def _order_bits_to_f32(u):
    key = u ^ INT_MIN
    bits = key ^ ((key >> 31) & 0x7FFFFFFF)
    f = lax.bitcast_convert_type(bits, F32)
    return jnp.where(u >= 0, jnp.where(u <= 0x007FFFFF, NEG_INF, f), f)


def _dsa_body(qa_ref, qi_ref, wi_ref, kidx_ref, k_ref, vt_ref, o_ref, sc_scr, bias_scr, s_scr, acc_scr,
              *, s_pad, s_real, q_rows, pair, pos0, topk):
    kb_rows = KEY_BLOCK
    heads_per_kv = N_HEADS_A // N_KV_A
    qb = pair * q_rows
    cdim = pair * HEAD_DIM_A
    v_rows = pair * V_ROWS
    j = pl.program_id(1)
    q_lo = pos0 + j * q_rows
    lane = lax.broadcasted_iota(I32, (1, qb), 1)
    q_chunk = (q_lo + (lane & (q_rows - 1))) >> CHUNK_SHIFT
    k_lim = (((q_lo + q_rows - 1) >> CHUNK_SHIFT) + 1) * CHUNK
    nkb = jnp.minimum(s_pad // kb_rows, (k_lim + kb_rows - 1) // kb_rows)

    def rows_of(kb):
        return pl.ds(pl.multiple_of(kb * kb_rows, kb_rows), kb_rows)

    def fold8(a):
        return a.reshape(kb_rows // 8, 8, qb)

    def score_blk(kb, _):
        rows = rows_of(kb)
        kidx = kidx_ref[0, rows, :]
        for h in range(N_IDX_HEADS):
            lg = _dot_nt(kidx, qi_ref[h, 0])
            term = jnp.maximum(lg, 0.0) * wi_ref[0, h:h + 1, :]
            if h == 0:
                sc_scr[rows, :] = term
            elif h < N_IDX_HEADS - 1:
                sc_scr[rows, :] += term
            else:
                spos = kb * kb_rows + lax.broadcasted_iota(I32, (kb_rows, qb), 0)
                sc = jnp.where((spos >> CHUNK_SHIFT) <= q_chunk,
                               (sc_scr[rows, :] + term) * IDX_SCALE + 0.0, NEG_INF)
                if s_real < s_pad:
                    sc = jnp.where(spos < s_real, sc, NEG_INF)
                sc_scr[rows, :] = sc
        return 0

    lax.fori_loop(0, nkb, score_blk, 0)

    def count(thr, strict):
        def body(kb, acc):
            blk = sc_scr[rows_of(kb), :]
            hit = (blk > thr) if strict else (blk >= thr)
            return acc + jnp.sum(fold8(jnp.where(hit, 1, 0).astype(I32)), axis=0)
        acc = lax.fori_loop(0, nkb, body, jnp.zeros((8, qb), I32))
        return jnp.sum(acc, axis=0, keepdims=True)

    def bit_step(i, carry):
        prefix, cnt = carry
        cand = prefix | lax.shift_left(jnp.int32(1), 31 - i)
        c = count(_order_bits_to_f32(cand), False)
        take = c >= topk
        return jnp.where(take, cand, prefix), jnp.where(take, c, cnt)

    prefix, cnt_ge = lax.fori_loop(0, 32, bit_step,
                                   (jnp.zeros((1, qb), I32), jnp.full((1, qb), nkb * kb_rows, I32)))
    tau = _order_bits_to_f32(prefix)
    finite_tau = tau > NEG_INF
    tau_floor = jnp.maximum(tau, jnp.finfo(F32).min)
    tie_lanes = jnp.where(finite_tau, jnp.where(cnt_ge > topk, 1, 0), 0)
    has_ties = jnp.max(tie_lanes) > 0

    @pl.when(jnp.logical_not(has_ties))
    def _():
        def body(kb, _):
            blk = sc_scr[rows_of(kb), :]
            bias_scr[rows_of(kb), :] = jnp.where(blk >= tau_floor, 0.0, NEG_INF)
            return 0
        lax.fori_loop(0, nkb, body, 0)

    @pl.when(has_ties)
    def _():
        need = jnp.where(finite_tau, (topk - count(tau, True)).astype(F32), 0.0)
        r = lax.broadcasted_iota(I32, (kb_rows, kb_rows), 0)
        c = lax.broadcasted_iota(I32, (kb_rows, kb_rows), 1)
        tril = jnp.where(r >= c, 1.0, 0.0).astype(BF16)

        def body(kb, seen):
            blk = sc_scr[rows_of(kb), :]
            eq = blk == tau
            rank = _dot(tril, jnp.where(eq, 1.0, 0.0).astype(BF16)) + seen
            tie_bias = jnp.where(eq, jnp.where(rank <= need, 0.0, NEG_INF), NEG_INF)
            bias_scr[rows_of(kb), :] = jnp.where(blk > tau, 0.0, tie_bias)
            return rank[kb_rows - 1:kb_rows, :]
        lax.fori_loop(0, nkb, body, jnp.zeros((1, qb), F32))

    def pass1(kb, m8):
        rows = rows_of(kb)
        bias = bias_scr[rows, :]
        out = []
        for h in range(N_HEADS_A):
            g = h // heads_per_kv
            s = _dot_nt(k_ref[0, rows, g * cdim:(g + 1) * cdim], qa_ref[h, 0]) + bias
            s_scr[h, rows, :] = s
            out.append(jnp.maximum(m8[h], jnp.max(fold8(s), axis=0)))
        return tuple(out)

    m8 = lax.fori_loop(0, nkb, pass1, tuple(jnp.full((8, qb), NEG_INF, F32) for _ in range(N_HEADS_A)))
    m = [jnp.max(x, axis=0, keepdims=True) for x in m8]

    acc_scr[...] = jnp.zeros_like(acc_scr)

    def pass2(kb, _):
        rows = rows_of(kb)
        for h in range(N_HEADS_A):
            g = h // heads_per_kv
            p = jnp.exp2(s_scr[h, rows, :] - m[h]).astype(BF16)
            acc_scr[h] += _dot(vt_ref[0, g * v_rows:(g + 1) * v_rows, rows], p)
        return 0

    lax.fori_loop(0, nkb, pass2, 0)
    for h in range(N_HEADS_A):
        a = acc_scr[h]
        num, den = a[:HEAD_DIM_A], a[HEAD_DIM_A:HEAD_DIM_A + 1]
        for mb in range(1, pair):
            mine = lane >= mb * q_rows
            r0 = mb * V_ROWS
            num = jnp.where(mine, a[r0:r0 + HEAD_DIM_A], num)
            den = jnp.where(mine, a[r0 + HEAD_DIM_A:r0 + HEAD_DIM_A + 1], den)
        o_ref[0, h * HEAD_DIM_A:(h + 1) * HEAD_DIM_A, :] = (num * (1.0 / den)).astype(BF16)


def _dsa(qa_h, qi_h, wi, ki_all, k_all, v_all, *, s_real, pos0, topk):
    _, b, t, _ = qa_h.shape
    s_pad = ki_all.shape[1]
    q_rows = min(256, t)
    pair = 2 if (q_rows * 2 <= 128 and q_rows == t and b % 2 == 0) else 1
    bp = b // pair
    lanes, cdim, v_rows = pair * q_rows, pair * HEAD_DIM_A, pair * V_ROWS

    def block_diag(q):
        if pair == 1:
            return q
        nh, d = q.shape[0], q.shape[3]
        eye = jnp.eye(pair, dtype=q.dtype)
        q6 = q.reshape(nh, bp, pair, t, 1, d) * eye.reshape(1, 1, pair, 1, pair, 1)
        return q6.reshape(nh, bp, pair * t, pair * d)

    wi_op = wi.reshape(bp, pair, t, N_IDX_HEADS).transpose(0, 3, 1, 2).reshape(bp, N_IDX_HEADS, pair * t)
    ki_op = ki_all.reshape(bp, pair, s_pad, IDX_DIM).transpose(0, 2, 1, 3).reshape(bp, s_pad, cdim)
    k_op = (k_all.reshape(bp, pair, s_pad, N_KV_A, HEAD_DIM_A).transpose(0, 2, 3, 1, 4)
            .reshape(bp, s_pad, N_KV_A * cdim))
    vt = v_all.reshape(b, s_pad, N_KV_A, HEAD_DIM_A).transpose(0, 2, 3, 1)
    vt = jnp.concatenate([vt, jnp.ones((b, N_KV_A, V_ROWS - HEAD_DIM_A, s_pad), vt.dtype)], axis=2)
    vt_op = (vt.reshape(bp, pair, N_KV_A, V_ROWS, s_pad).transpose(0, 2, 1, 3, 4)
             .reshape(bp, N_KV_A * v_rows, s_pad))

    body = functools.partial(_dsa_body, s_pad=s_pad, s_real=s_real, q_rows=q_rows, pair=pair, pos0=pos0,
                             topk=topk)
    o_t = pl.pallas_call(
        body,
        grid=(bp, t // q_rows),
        in_specs=[
            pl.BlockSpec((N_HEADS_A, 1, lanes, cdim), lambda i, j: (0, i, j, 0)),
            pl.BlockSpec((N_IDX_HEADS, 1, lanes, cdim), lambda i, j: (0, i, j, 0)),
            pl.BlockSpec((1, N_IDX_HEADS, lanes), lambda i, j: (i, 0, j)),
            pl.BlockSpec((1, s_pad, cdim), lambda i, j: (i, 0, 0)),
            pl.BlockSpec((1, s_pad, N_KV_A * cdim), lambda i, j: (i, 0, 0)),
            pl.BlockSpec((1, N_KV_A * v_rows, s_pad), lambda i, j: (i, 0, 0)),
        ],
        out_specs=pl.BlockSpec((1, WIDTH_A, lanes), lambda i, j: (i, 0, j)),
        out_shape=jax.ShapeDtypeStruct((bp, WIDTH_A, pair * t), BF16),
        scratch_shapes=[pltpu.VMEM((s_pad, lanes), F32), pltpu.VMEM((s_pad, lanes), F32),
                        pltpu.VMEM((N_HEADS_A, s_pad, lanes), F32), pltpu.VMEM((N_HEADS_A, v_rows, lanes), F32)],
        compiler_params=_cparams("parallel", "parallel"),
        name="dsa",
    )(block_diag(qa_h), block_diag(qi_h), wi_op, ki_op, k_op, vt_op)
    return o_t.reshape(bp, WIDTH_A, pair, t).transpose(0, 2, 3, 1).reshape(b * t, WIDTH_A)


def _split3(a):
    hi = a.astype(BF16)
    r1 = a - hi.astype(F32)
    mid = r1.astype(BF16)
    lo = (r1 - mid.astype(F32)).astype(BF16)
    return hi, mid, lo


def _hgrn_body(hb_ref, s0_ref, lb_ref, nw_ref, ob_ref, sfin_ref, state_scr, *, tb):
    t = pl.program_id(1)

    @pl.when(t == 0)
    def _():
        state_scr[...] = s0_ref[0]

    lb = lb_ref[...]
    r = lax.broadcasted_iota(I32, (CHUNK, CHUNK), 0)
    c = lax.broadcasted_iota(I32, (CHUNK, CHUNK), 1)
    tril = jnp.where(r >= c, 1.0, 0.0).astype(BF16)
    ones = jnp.ones((CHUNK, HEAD_DIM_B), BF16)

    for ci in range(tb // CHUNK):
        rows = slice(ci * CHUNK, (ci + 1) * CHUNK)
        f = lb + (1.0 - lb) * jax.nn.sigmoid(hb_ref[0, rows, WIDTH_B:2 * WIDTH_B])
        log_f = jnp.log(f)
        parts = _split3(log_f)
        bcum = _dot(tril, parts[0]) + _dot(tril, parts[1]) + _dot(tril, parts[2])
        for h in range(N_HEADS_B):
            lanes = slice(h * HEAD_DIM_B, (h + 1) * HEAD_DIM_B)
            q = hb_ref[0, rows, lanes]
            k = 1.0 - f[:, lanes]
            v = hb_ref[0, rows, 2 * WIDTH_B + h * HEAD_DIM_B:2 * WIDTH_B + (h + 1) * HEAD_DIM_B]
            gate = hb_ref[0, rows, 3 * WIDTH_B + h * HEAD_DIM_B:3 * WIDTH_B + (h + 1) * HEAD_DIM_B]
            bh = bcum[:, lanes]
            b_last = bh[CHUNK - 1:CHUNK, :]
            vb = v.astype(BF16)
            state = state_scr[h]
            o_inter = _dot((q * jnp.exp(bh)).astype(BF16), state.astype(BF16))
            o_rows = []
            for i in range(CHUNK // HGRN_SUB):
                lo, hi = i * HGRN_SUB, (i + 1) * HGRN_SUB
                ref = bh[lo - 1:lo, :] if i else jnp.zeros((1, HEAD_DIM_B), F32)
                qs = (q[lo:hi] * jnp.exp(bh[lo:hi] - ref)).astype(BF16)
                ks = (k[:hi] * jnp.exp(ref - bh[:hi])).astype(BF16)
                a = _dot_nt(qs, ks)
                tq = lo + lax.broadcasted_iota(I32, (HGRN_SUB, hi), 0)
                tk = lax.broadcasted_iota(I32, (HGRN_SUB, hi), 1)
                a = jnp.where(tk <= tq, a, 0.0)
                o_rows.append(_dot(a.astype(BF16), vb[:hi]))
            o = o_inter + jnp.concatenate(o_rows, axis=0)
            kdec = (k * jnp.exp(b_last - bh)).astype(BF16)
            col_sum = (_dot_tn(parts[0][:, lanes], ones) + _dot_tn(parts[1][:, lanes], ones)
                       + _dot_tn(parts[2][:, lanes], ones))
            state_scr[h] = jnp.exp(col_sum) * state + _dot_tn(kdec, vb)
            ms = jnp.mean(o * o, axis=-1, keepdims=True)
            y = o * lax.rsqrt(ms + RMS_EPS) * nw_ref[...] * (gate * jax.nn.sigmoid(gate))
            ob_ref[0, rows, lanes] = y.astype(BF16)

    @pl.when(t == pl.num_programs(1) - 1)
    def _():
        sfin_ref[0] = state_scr[...]


def _hgrn(hb, s0, lb, nw):
    b, t, _ = hb.shape
    tb = min(256, t)
    return pl.pallas_call(
        functools.partial(_hgrn_body, tb=tb),
        grid=(b, t // tb),
        in_specs=[
            pl.BlockSpec((1, tb, 4 * WIDTH_B), lambda i, j: (i, j, 0)),
            pl.BlockSpec((1, N_HEADS_B, HEAD_DIM_B, HEAD_DIM_B), lambda i, j: (i, 0, 0, 0)),
            pl.BlockSpec((1, WIDTH_B), lambda i, j: (0, 0)),
            pl.BlockSpec((1, HEAD_DIM_B), lambda i, j: (0, 0)),
        ],
        out_specs=[
            pl.BlockSpec((1, tb, WIDTH_B), lambda i, j: (i, j, 0)),
            pl.BlockSpec((1, N_HEADS_B, HEAD_DIM_B, HEAD_DIM_B), lambda i, j: (i, 0, 0, 0)),
        ],
        out_shape=[jax.ShapeDtypeStruct((b, t, WIDTH_B), BF16),
                   jax.ShapeDtypeStruct((b, N_HEADS_B, HEAD_DIM_B, HEAD_DIM_B), F32)],
        scratch_shapes=[pltpu.VMEM((N_HEADS_B, HEAD_DIM_B, HEAD_DIM_B), F32)],
        compiler_params=_cparams("parallel", "arbitrary"),
        name="hgrn",
    )(hb, s0, lb, nw)


def _outproj_body(oa_ref, ob_ref, x_ref, w_ref, lnw_ref, lnb_ref, wrh_ref, wrl_ref, br_ref, x1_ref, lg_ref):
    y = _dot(oa_ref[...], w_ref[:WIDTH_A, :]) + _dot(ob_ref[...], w_ref[WIDTH_A:, :])
    x1 = _layer_norm(DN_ALPHA * x_ref[...] + y, lnw_ref[...], lnb_ref[...])
    x1_ref[...] = x1
    hi = x1.astype(BF16)
    lo = (x1 - hi.astype(F32)).astype(BF16)
    lg_ref[...] = (_dot_nt(wrh_ref[...], hi) + _dot_nt(wrh_ref[...], lo) + _dot_nt(wrl_ref[...], hi)
                   + br_ref[...])


def _outproj(oa, ob, x2d, w_out, lnw, lnb, wr_hi, wr_lo, br):
    n = x2d.shape[0]
    tm = min(512, n)
    row = lambda width: pl.BlockSpec((tm, width), lambda i: (i, 0))
    full = lambda a: pl.BlockSpec(a.shape, lambda i: (0,) * a.ndim)
    return pl.pallas_call(
        _outproj_body,
        grid=(n // tm,),
        in_specs=[row(WIDTH_A), row(WIDTH_B), row(D_MODEL), full(w_out), full(lnw), full(lnb),
                  full(wr_hi), full(wr_lo), full(br)],
        out_specs=[row(D_MODEL), pl.BlockSpec((N_EXPERTS, tm), lambda i: (0, i))],
        out_shape=[jax.ShapeDtypeStruct((n, D_MODEL), F32), jax.ShapeDtypeStruct((N_EXPERTS, n), F32)],
        compiler_params=_cparams("parallel"),
        name="outproj",
    )(oa, ob, x2d, w_out, lnw, lnb, wr_hi, wr_lo, br)


def _route_body(lg_ref, idx_ref, gate_ref, rank_ref, cnt_ref, cnt_scr, *, tr):
    @pl.when(pl.program_id(0) == 0)
    def _():
        cnt_scr[...] = jnp.zeros_like(cnt_scr)

    l = lg_ref[...]
    rows = lax.broadcasted_iota(I32, (N_EXPERTS, tr), 0)
    vals, hots = [], []
    for j in range(TOP_K):
        m = jnp.max(l, axis=0, keepdims=True)
        idx = jnp.min(jnp.where(l == m, rows, N_EXPERTS), axis=0, keepdims=True)
        hot = rows == idx
        idx_ref[j:j + 1, :] = idx
        vals.append(m)
        hots.append(hot)
        l = jnp.where(hot, NEG_INF, l)
    es = [jnp.exp(v - vals[0]) for v in vals]
    inv = 1.0 / (es[0] + es[1] + es[2] + es[3])
    for j in range(TOP_K):
        gate_ref[j:j + 1, :] = es[j] * inv
    chosen = jnp.zeros((N_EXPERTS, tr), F32)
    for hot in hots:
        chosen = chosen + jnp.where(hot, 1.0, 0.0)
    r = lax.broadcasted_iota(I32, (tr, tr), 0)
    c = lax.broadcasted_iota(I32, (tr, tr), 1)
    before = jnp.where(r < c, 1.0, 0.0).astype(BF16)
    prior = _dot(chosen.astype(BF16), before) + cnt_scr[:, 0:1]
    for j in range(TOP_K):
        rank_ref[j:j + 1, :] = jnp.sum(jnp.where(hots[j], prior, 0.0), axis=0, keepdims=True).astype(I32)
    cnt_scr[...] = cnt_scr[...] + jnp.sum(chosen, axis=1, keepdims=True)
    cnt_ref[...] = cnt_scr[...].astype(I32)


def _route(logits_t):
    n = logits_t.shape[1]
    tr = min(512, n)
    tok = lambda rows: pl.BlockSpec((rows, tr), lambda i: (0, i))
    return pl.pallas_call(
        functools.partial(_route_body, tr=tr),
        grid=(n // tr,),
        in_specs=[tok(N_EXPERTS)],
        out_specs=[tok(TOP_K), tok(TOP_K), tok(TOP_K), pl.BlockSpec((N_EXPERTS, 128), lambda i: (0, 0))],
        out_shape=[jax.ShapeDtypeStruct((TOP_K, n), I32), jax.ShapeDtypeStruct((TOP_K, n), F32),
                   jax.ShapeDtypeStruct((TOP_K, n), I32), jax.ShapeDtypeStruct((N_EXPERTS, 128), I32)],
        scratch_shapes=[pltpu.VMEM((N_EXPERTS, 128), F32)],
        compiler_params=_cparams("arbitrary"),
        name="route",
    )(logits_t)


def _dest_body(idx_ref, rank_ref, start_ref, dest_ref, *, tr):
    rows = lax.broadcasted_iota(I32, (N_EXPERTS, tr), 0)
    for j in range(TOP_K):
        base = jnp.sum(jnp.where(rows == idx_ref[j:j + 1, :], start_ref[...], 0), axis=0, keepdims=True)
        dest_ref[j:j + 1, :] = base + rank_ref[j:j + 1, :]


def _dest(idx_t, rank_t, seg_start):
    n = idx_t.shape[1]
    tr = min(512, n)
    tok = pl.BlockSpec((TOP_K, tr), lambda i: (0, i))
    return pl.pallas_call(
        functools.partial(_dest_body, tr=tr),
        grid=(n // tr,),
        in_specs=[tok, tok, pl.BlockSpec((N_EXPERTS, 1), lambda i: (0, 0))],
        out_specs=tok,
        out_shape=jax.ShapeDtypeStruct((TOP_K, n), I32),
        compiler_params=_cparams("parallel"),
        name="slot_dest",
    )(idx_t, rank_t, seg_start)


def _dispatch_body(seg_ref, dest_ref, x_ref, buf_ref, zero_scr, zsem, rsem, *, td, n_blocks):
    def row_copy(n, j):
        return pltpu.make_async_copy(x_ref.at[pl.ds(n, 1), :], buf_ref.at[pl.ds(dest_ref[j, n], 1), :], rsem)

    @pl.when(pl.program_id(0) == 0)
    def _():
        zero_scr[...] = jnp.zeros_like(zero_scr)
        n_used = seg_ref[1, N_EXPERTS - 1] // EXPERT_ROWS

        def block_copy(start):
            return pltpu.make_async_copy(
                zero_scr, buf_ref.at[pl.ds(pl.multiple_of(start, EXPERT_ROWS), EXPERT_ROWS), :], zsem)

        def clears(action):
            for e in range(N_EXPERTS):
                @pl.when(seg_ref[1, e] > seg_ref[0, e])
                def _():
                    action(block_copy(seg_ref[1, e] - EXPERT_ROWS))

                @pl.when(n_used + e < n_blocks)
                def _():
                    action(block_copy((n_used + e) * EXPERT_ROWS))

        clears(lambda cp: cp.start())
        clears(lambda cp: cp.wait())

    def issue(n, _):
        for j in range(TOP_K):
            row_copy(n, j).start()
        return 0

    lax.fori_loop(0, td, issue, 0)

    def drain(n, _):
        for j in range(TOP_K):
            row_copy(n, j).wait()
        return 0

    lax.fori_loop(0, td, drain, 0)


def _dispatch(x1, dest_t, seg, n_rows):
    n = x1.shape[0]
    td = min(128, n)
    return pl.pallas_call(
        functools.partial(_dispatch_body, td=td, n_blocks=n_rows // EXPERT_ROWS),
        grid_spec=pltpu.PrefetchScalarGridSpec(
            num_scalar_prefetch=1,
            grid=(n // td,),
            in_specs=[pl.BlockSpec((TOP_K, td), lambda i, seg: (0, i), memory_space=pltpu.SMEM),
                      pl.BlockSpec((td, D_MODEL), lambda i, seg: (i, 0))],
            out_specs=pl.BlockSpec(memory_space=pl.ANY),
            scratch_shapes=[pltpu.VMEM((EXPERT_ROWS, D_MODEL), F32),
                            pltpu.SemaphoreType.DMA(()), pltpu.SemaphoreType.DMA(())],
        ),
        out_shape=jax.ShapeDtypeStruct((n_rows, D_MODEL), F32),
        compiler_params=_cparams("arbitrary"),
        name="dispatch",
    )(seg, dest_t, x1)


def _experts_body(be_ref, nu_ref, x_ref, wgu_ref, bgu_ref, wd_ref, bd_ref, o_ref, wgu_bf, wd_bf):
    i = pl.program_id(0)

    @pl.when(jnp.logical_or(i == 0, be_ref[i] != be_ref[jnp.maximum(i - 1, 0)]))
    def _():
        for r in range(0, D_MODEL, 256):
            wgu_bf[r:r + 256, :] = wgu_ref[0, r:r + 256, :].astype(BF16)
            wd_bf[r:r + 256, :] = wd_ref[0, r:r + 256, :].astype(BF16)

    @pl.when(i < nu_ref[0])
    def _():
        gu = _dot(x_ref[...].astype(BF16), wgu_bf[...]) + bgu_ref[0]
        gate = jnp.minimum(gu[:, :D_EXPERT], SWIGLU_LIMIT)
        up = jnp.clip(gu[:, D_EXPERT:], -SWIGLU_LIMIT, SWIGLU_LIMIT)
        hdn = (up + 1.0) * gate * jax.nn.sigmoid(SWIGLU_ALPHA * gate)
        o_ref[...] = _dot(hdn.astype(BF16), wd_bf[...]) + bd_ref[0]

    @pl.when(i >= nu_ref[0])
    def _():
        o_ref[...] = jnp.zeros_like(o_ref)


def _experts(buf, block_e, n_used, w_gu, b_gu, w_down, b_down):
    n_blocks = buf.shape[0] // EXPERT_ROWS
    rows = pl.BlockSpec((EXPERT_ROWS, D_MODEL), lambda i, be, nu: (jnp.minimum(i, nu[0] - 1), 0))
    per_e = lambda a: pl.BlockSpec((1,) + a.shape[1:], lambda i, be, nu: (be[i],) + (0,) * (a.ndim - 1))
    return pl.pallas_call(
        _experts_body,
        grid_spec=pltpu.PrefetchScalarGridSpec(
            num_scalar_prefetch=2,
            grid=(n_blocks,),
            in_specs=[rows, per_e(w_gu), per_e(b_gu), per_e(w_down), per_e(b_down)],
            out_specs=pl.BlockSpec((EXPERT_ROWS, D_MODEL), lambda i, be, nu: (i, 0)),
            scratch_shapes=[pltpu.VMEM((D_MODEL, 2 * D_EXPERT), BF16), pltpu.VMEM((D_EXPERT, D_MODEL), BF16)],
        ),
        out_shape=jax.ShapeDtypeStruct(buf.shape, F32),
        compiler_params=_cparams("arbitrary"),
        name="experts",
    )(block_e, n_used, buf, w_gu, b_gu, w_down, b_down)


def _final_body(dest_ref, gate_ref, x1_ref, p_ref, lnw_ref, lnb_ref, wg_ref, bg_ref, wp_ref, eo_ref, y_ref,
                rows_scr, sem, *, tf):
    def row_copy(j, n):
        return pltpu.make_async_copy(eo_ref.at[pl.ds(dest_ref[j, n], 1), :], rows_scr.at[j, pl.ds(n, 1), :], sem)

    def issue(n, _):
        for j in range(TOP_K):
            row_copy(j, n).start()
        return 0

    lax.fori_loop(0, tf, issue, 0)

    def drain(n, _):
        for j in range(TOP_K):
            row_copy(j, n).wait()
        return 0

    lax.fori_loop(0, tf, drain, 0)

    moe = jnp.zeros((tf, D_MODEL), F32)
    for j in range(TOP_K):
        moe = moe + rows_scr[j] * gate_ref[:, j:j + 1]
    x2 = _layer_norm(DN_ALPHA * x1_ref[...] + moe, lnw_ref[...], lnb_ref[...])
    ple_gate = jax.nn.sigmoid(_dot(x2.astype(BF16), wg_ref[...]) + bg_ref[...])
    y_ref[...] = x2 + ple_gate * _dot(p_ref[...].astype(BF16), wp_ref[...])


def _final(dest_t, gate, x1, p2d, lnw, lnb, w_gate, b_gate, w_ple, expert_out):
    n = x1.shape[0]
    tf = min(128, n)
    row = lambda width: pl.BlockSpec((tf, width), lambda i: (i, 0))
    full = lambda a: pl.BlockSpec(a.shape, lambda i: (0,) * a.ndim)
    return pl.pallas_call(
        functools.partial(_final_body, tf=tf),
        grid=(n // tf,),
        in_specs=[pl.BlockSpec((TOP_K, tf), lambda i: (0, i), memory_space=pltpu.SMEM),
                  row(TOP_K), row(D_MODEL), row(D_PLE), full(lnw), full(lnb), full(w_gate), full(b_gate),
                  full(w_ple), pl.BlockSpec(memory_space=pl.ANY)],
        out_specs=row(D_MODEL),
        out_shape=jax.ShapeDtypeStruct((n, D_MODEL), F32),
        scratch_shapes=[pltpu.VMEM((TOP_K, tf, D_MODEL), F32), pltpu.SemaphoreType.DMA(())],
        compiler_params=_cparams("arbitrary"),
        name="combine_final",
    )(dest_t, gate, x1, p2d, lnw, lnb, w_gate, b_gate, w_ple, expert_out)


def _round_up(a, m):
    return -(-a // m) * m


def _encoder_layer(x, p, hist_k, hist_v, hist_kidx, s0, pos0, wts):
    b, t, _ = x.shape
    n = b * t
    x2d = x.reshape(n, D_MODEL)

    qa, ka, va, qi, ki, wi, hb = _inproj(x2d, wts["w_in"], wts["idx_lnw"], wts["idx_lnb"])

    k_new = ka.reshape(b, t, N_KV_A * HEAD_DIM_A)
    v_new = va.reshape(b, t, N_KV_A * HEAD_DIM_A)
    ki_new = ki.reshape(b, t, IDX_DIM)
    if hist_k is None:
        k_all, v_all, ki_all = k_new, v_new, ki_new
    else:
        past = hist_k.shape[1]
        k_all = jnp.concatenate([hist_k.reshape(b, past, -1), k_new], axis=1)
        v_all = jnp.concatenate([hist_v.reshape(b, past, -1), v_new], axis=1)
        ki_all = jnp.concatenate([hist_kidx, ki_new], axis=1)
    s_real = k_all.shape[1]
    topk = min(TOPK_MAX, s_real // 4)
    pad = ((0, 0), (0, _round_up(s_real, KEY_BLOCK) - s_real), (0, 0))
    oa = _dsa(qa.reshape(N_HEADS_A, b, t, HEAD_DIM_A), qi.reshape(N_IDX_HEADS, b, t, IDX_DIM),
              wi.reshape(b, t, N_IDX_HEADS), jnp.pad(ki_all.astype(BF16), pad), jnp.pad(k_all.astype(BF16), pad),
              jnp.pad(v_all.astype(BF16), pad), s_real=s_real, pos0=pos0, topk=topk)

    ob, s_fin = _hgrn(hb.reshape(b, t, 4 * WIDTH_B), s0, wts["lb"], wts["hgrn_nw"])

    x1, logits_t = _outproj(oa, ob.reshape(n, WIDTH_B), x2d, wts["w_out"], wts["ln1_w"], wts["ln1_b"],
                            wts["wr_hi"], wts["wr_lo"], wts["b_router"])

    idx_t, gate_t, rank_t, counts = _route(logits_t)
    counts = counts[:, 0]
    padded = (counts + EXPERT_ROWS - 1) // EXPERT_ROWS * EXPERT_ROWS
    seg_end = jnp.cumsum(padded)
    seg_start = seg_end - padded
    n_blocks = -(-(n * TOP_K + N_EXPERTS * (EXPERT_ROWS - 1)) // EXPERT_ROWS)
    block_row0 = jnp.arange(n_blocks, dtype=I32) * EXPERT_ROWS
    block_e = jnp.minimum(jnp.sum((seg_end[None, :] <= block_row0[:, None]).astype(I32), axis=1), N_EXPERTS - 1)
    n_used = (seg_end[-1:] // EXPERT_ROWS).astype(I32)
    dest_t = _dest(idx_t, rank_t, seg_start.reshape(N_EXPERTS, 1).astype(I32))
    seg = jnp.stack([seg_start, seg_end]).astype(I32)
    buf = _dispatch(x1, dest_t, seg, n_blocks * EXPERT_ROWS)
    expert_out = _experts(buf, block_e, n_used, wts["w_gu"], wts["b_gu"], wts["w_down"], wts["b_down"])
    y = _final(dest_t, gate_t.T, x1, p.reshape(n, D_PLE), wts["ln2_w"], wts["ln2_b"], wts["w_ple_gate"],
               wts["b_ple_gate"], wts["w_ple"], expert_out)

    return (y.reshape(b, t, D_MODEL), ka.reshape(b, t, N_KV_A, HEAD_DIM_A), va.reshape(b, t, N_KV_A, HEAD_DIM_A),
            ki_new, s_fin)


def _prep_weights(w_in, w_out, idx_k_norm_w, idx_k_norm_b, lb, hgrn_norm_w, ln1_w, ln1_b, w_router, b_router,
                  w_gu, b_gu, w_down, b_down, ln2_w, ln2_b, w_ple, w_ple_gate, b_ple_gate):
    n_a = COL_KW + IDX_DIM + N_IDX_HEADS
    w_pad = jnp.concatenate(
        [w_in[:, :n_a], jnp.zeros((D_MODEL, COL_HB - n_a), w_in.dtype), w_in[:, n_a:]], axis=1).astype(BF16)
    wr_t = w_router.T
    wr_hi = wr_t.astype(BF16)
    row = lambda a: a.reshape(1, -1)
    return dict(
        w_in=w_pad, idx_lnw=row(idx_k_norm_w), idx_lnb=row(idx_k_norm_b), lb=row(lb), hgrn_nw=row(hgrn_norm_w),
        w_out=w_out.astype(BF16), ln1_w=row(ln1_w), ln1_b=row(ln1_b),
        wr_hi=wr_hi, wr_lo=(wr_t - wr_hi.astype(F32)).astype(BF16), b_router=b_router.reshape(N_EXPERTS, 1),
        w_gu=w_gu, b_gu=b_gu.reshape(N_EXPERTS, 1, 2 * D_EXPERT),
        w_down=w_down, b_down=b_down.reshape(N_EXPERTS, 1, D_MODEL),
        ln2_w=row(ln2_w), ln2_b=row(ln2_b), w_ple=w_ple.astype(BF16), w_ple_gate=w_ple_gate.astype(BF16),
        b_ple_gate=row(b_ple_gate))


def kernel(x_prompt, x_sample, cache_k, cache_v, cache_kidx, state_hgrn, p_prompt, p_sample, w_in, w_out,
           idx_k_norm_w, idx_k_norm_b, hgrn_lb_logits, hgrn_norm_w, ln1_w, ln1_b, w_router, b_router, w_gu, b_gu,
           w_down, b_down, ln2_w, ln2_b, w_ple, w_ple_gate, b_ple_gate):
    lb_all = jnp.cumsum(jax.nn.softmax(hgrn_lb_logits.astype(F32), axis=0), axis=0)
    xp, xs = x_prompt, x_sample
    outs = [[] for _ in range(8)]
    for i in range(DEPTH):
        wts = _prep_weights(w_in[i], w_out[i], idx_k_norm_w[i], idx_k_norm_b[i], lb_all[i], hgrn_norm_w[i],
                            ln1_w[i], ln1_b[i], w_router[i], b_router[i], w_gu[i], b_gu[i], w_down[i], b_down[i],
                            ln2_w[i], ln2_b[i], w_ple[i], w_ple_gate[i], b_ple_gate[i])
        s0p = jnp.zeros((xp.shape[0], N_HEADS_B, HEAD_DIM_B, HEAD_DIM_B), F32)
        xp, kp, vp, kip, sp = _encoder_layer(xp, p_prompt[i], None, None, None, s0p, 0, wts)
        xs, kn, vn, kin, sn = _encoder_layer(xs, p_sample[i], cache_k[i], cache_v[i], cache_kidx[i],
                                             state_hgrn[i], cache_k.shape[2], wts)
        for lst, val in zip(outs, (kp, vp, kip, sp, kn, vn, kin, sn)):
            lst.append(val)
    return (xp, xs) + tuple(jnp.stack(l) for l in outs)
```

```python
import functools

import jax
import jax.numpy as jnp
from jax import lax
from jax.experimental import pallas as pl
from jax.experimental.pallas import tpu as pltpu

F32 = jnp.float32
BF16 = jnp.bfloat16
I32 = jnp.int32

D_MODEL = 1024
CHUNK = 64
CHUNK_SHIFT = 6
WIDTH_A = 512
HEAD_DIM_A = 64
N_HEADS_A = 8
N_KV_A = 2
N_IDX_HEADS = 4
IDX_DIM = 64
IDX_SCALE = IDX_DIM ** -0.5 * N_IDX_HEADS ** -0.5
TOPK_MAX = 256
WIDTH_B = 512
HEAD_DIM_B = 128
N_HEADS_B = 4
N_EXPERTS = 32
TOP_K = 4
D_EXPERT = 1024
SWIGLU_LIMIT = 7.0
SWIGLU_ALPHA = 1.702
D_PLE = 256
LN_EPS = 1e-5
RMS_EPS = 1e-6
DEPTH = 1
DN_ALPHA = (2 * DEPTH) ** 0.25

COL_QA, COL_KA, COL_VA, COL_QI, COL_KW, COL_HB, COL_END = 0, 512, 640, 768, 1024, 1152, 3200

VMEM_LIMIT = 56 * 1024 * 1024
KEY_BLOCK = 256
V_ROWS = 80
LOG2_E = 1.4426950408889634
HGRN_SUB = 16
EXPERT_ROWS = 512
MOE_TILE = 512
ROW_ALIGN = 8
RUN_PIECES = (512, 256, 128, 64, 32, 16, 8)
MOE_SLOTS = -(-(TOP_K * MOE_TILE + N_EXPERTS * (ROW_ALIGN - 1)) // 256) * 256
NEG_INF = float("-inf")
INT_MIN = -(2 ** 31)


def _cparams(*sem):
    return pltpu.CompilerParams(dimension_semantics=sem, vmem_limit_bytes=VMEM_LIMIT)


def _dot(a, b):
    return jnp.dot(a, b, preferred_element_type=F32)


def _dot_nt(a, b):
    return lax.dot_general(a, b, (((1,), (1,)), ((), ())), preferred_element_type=F32)


def _dot_tn(a, b):
    return lax.dot_general(a, b, (((0,), (0,)), ((), ())), preferred_element_type=F32)


def _layer_norm(z, w, b):
    mu = jnp.mean(z, axis=-1, keepdims=True)
    d = z - mu
    var = jnp.mean(d * d, axis=-1, keepdims=True)
    return d * lax.rsqrt(var + LN_EPS) * w + b


def _inproj_body(x_ref, w_ref, lnw_ref, lnb_ref, qa_ref, ka_ref, va_ref, qi_ref, ki_ref, wi_ref, hb_ref):
    xb = x_ref[...].astype(BF16)

    def mm(c0, c1):
        return _dot(xb, w_ref[:, c0:c1])

    qa = (mm(COL_QA, COL_KA) * (HEAD_DIM_A ** -0.5 * LOG2_E)).astype(BF16)
    for h in range(N_HEADS_A):
        qa_ref[h] = qa[:, h * HEAD_DIM_A:(h + 1) * HEAD_DIM_A]
    ka_ref[...] = mm(COL_KA, COL_VA)
    va_ref[...] = mm(COL_VA, COL_QI)
    qi = mm(COL_QI, COL_KW).astype(BF16)
    for h in range(N_IDX_HEADS):
        qi_ref[h] = qi[:, h * IDX_DIM:(h + 1) * IDX_DIM]
    kw = mm(COL_KW, COL_HB)
    ki_ref[...] = _layer_norm(kw[:, :IDX_DIM], lnw_ref[...], lnb_ref[...])
    wi_ref[...] = kw[:, IDX_DIM:IDX_DIM + N_IDX_HEADS]
    hb_ref[...] = mm(COL_HB, COL_END)


def _inproj(x2d, w_pad, lnw, lnb):
    n = x2d.shape[0]
    tm = min(512, n)
    row = lambda width: pl.BlockSpec((tm, width), lambda i: (i, 0))
    full = lambda a: pl.BlockSpec(a.shape, lambda i: (0,) * a.ndim)
    heads = lambda nh, d: pl.BlockSpec((nh, tm, d), lambda i: (0, i, 0))
    sds = jax.ShapeDtypeStruct
    return pl.pallas_call(
        _inproj_body,
        grid=(n // tm,),
        in_specs=[row(D_MODEL), full(w_pad), full(lnw), full(lnb)],
        out_specs=[heads(N_HEADS_A, HEAD_DIM_A), row(128), row(128), heads(N_IDX_HEADS, IDX_DIM),
                   row(IDX_DIM), row(N_IDX_HEADS), row(4 * WIDTH_B)],
        out_shape=[sds((N_HEADS_A, n, HEAD_DIM_A), BF16), sds((n, 128), F32), sds((n, 128), F32),
                   sds((N_IDX_HEADS, n, IDX_DIM), BF16), sds((n, IDX_DIM), F32), sds((n, N_IDX_HEADS), F32),
                   sds((n, 4 * WIDTH_B), F32)],
        compiler_params=_cparams("parallel"),
        name="inproj",
    )(x2d, w_pad, lnw, lnb)


def _order_bits_to_f32(u):
    key = u ^ INT_MIN
    bits = key ^ ((key >> 31) & 0x7FFFFFFF)
    f = lax.bitcast_convert_type(bits, F32)
    return jnp.where(u >= 0, jnp.where(u <= 0x007FFFFF, NEG_INF, f), f)


def _dsa_body(qa_ref, qi_ref, wi_ref, kidx_ref, k_ref, vt_ref, o_ref, sc_scr, bias_scr, s_scr, acc_scr,
              *, s_pad, s_real, q_rows, pair, pos0, topk):
    kb_rows = KEY_BLOCK
    heads_per_kv = N_HEADS_A // N_KV_A
    qb = pair * q_rows
    cdim = pair * HEAD_DIM_A
    v_rows = pair * V_ROWS
    j = pl.program_id(1)
    q_lo = pos0 + j * q_rows
    lane = lax.broadcasted_iota(I32, (1, qb), 1)
    q_chunk = (q_lo + (lane & (q_rows - 1))) >> CHUNK_SHIFT
    k_lim = (((q_lo + q_rows - 1) >> CHUNK_SHIFT) + 1) * CHUNK
    nkb = jnp.minimum(s_pad // kb_rows, (k_lim + kb_rows - 1) // kb_rows)

    def rows_of(kb):
        return pl.ds(pl.multiple_of(kb * kb_rows, kb_rows), kb_rows)

    def fold8(a):
        return a.reshape(kb_rows // 8, 8, qb)

    def score_blk(kb, _):
        rows = rows_of(kb)
        kidx = kidx_ref[0, rows, :]
        for h in range(N_IDX_HEADS):
            lg = _dot_nt(kidx, qi_ref[h, 0])
            term = jnp.maximum(lg, 0.0) * wi_ref[0, h:h + 1, :]
            if h == 0:
                sc_scr[rows, :] = term
            elif h < N_IDX_HEADS - 1:
                sc_scr[rows, :] += term
            else:
                spos = kb * kb_rows + lax.broadcasted_iota(I32, (kb_rows, qb), 0)
                sc = jnp.where((spos >> CHUNK_SHIFT) <= q_chunk,
                               (sc_scr[rows, :] + term) * IDX_SCALE + 0.0, NEG_INF)
                if s_real < s_pad:
                    sc = jnp.where(spos < s_real, sc, NEG_INF)
                sc_scr[rows, :] = sc
        return 0

    lax.fori_loop(0, nkb, score_blk, 0)

    def count(thr, strict):
        def body(kb, acc):
            blk = sc_scr[rows_of(kb), :]
            hit = (blk > thr) if strict else (blk >= thr)
            return acc + jnp.sum(fold8(jnp.where(hit, 1, 0).astype(I32)), axis=0)
        acc = lax.fori_loop(0, nkb, body, jnp.zeros((8, qb), I32))
        return jnp.sum(acc, axis=0, keepdims=True)

    def bit_step(i, carry):
        prefix, cnt = carry
        cand = prefix | lax.shift_left(jnp.int32(1), 31 - i)
        c = count(_order_bits_to_f32(cand), False)
        take = c >= topk
        return jnp.where(take, cand, prefix), jnp.where(take, c, cnt)

    prefix, cnt_ge = lax.fori_loop(0, 32, bit_step,
                                   (jnp.zeros((1, qb), I32), jnp.full((1, qb), nkb * kb_rows, I32)))
    tau = _order_bits_to_f32(prefix)
    finite_tau = tau > NEG_INF
    tau_floor = jnp.maximum(tau, jnp.finfo(F32).min)
    tie_lanes = jnp.where(finite_tau, jnp.where(cnt_ge > topk, 1, 0), 0)
    has_ties = jnp.max(tie_lanes) > 0

    @pl.when(jnp.logical_not(has_ties))
    def _():
        def body(kb, _):
            blk = sc_scr[rows_of(kb), :]
            bias_scr[rows_of(kb), :] = jnp.where(blk >= tau_floor, 0.0, NEG_INF)
            return 0
        lax.fori_loop(0, nkb, body, 0)

    @pl.when(has_ties)
    def _():
        need = jnp.where(finite_tau, (topk - count(tau, True)).astype(F32), 0.0)
        r = lax.broadcasted_iota(I32, (kb_rows, kb_rows), 0)
        c = lax.broadcasted_iota(I32, (kb_rows, kb_rows), 1)
        tril = jnp.where(r >= c, 1.0, 0.0).astype(BF16)

        def body(kb, seen):
            blk = sc_scr[rows_of(kb), :]
            eq = blk == tau
            rank = _dot(tril, jnp.where(eq, 1.0, 0.0).astype(BF16)) + seen
            tie_bias = jnp.where(eq, jnp.where(rank <= need, 0.0, NEG_INF), NEG_INF)
            bias_scr[rows_of(kb), :] = jnp.where(blk > tau, 0.0, tie_bias)
            return rank[kb_rows - 1:kb_rows, :]
        lax.fori_loop(0, nkb, body, jnp.zeros((1, qb), F32))

    def pass1(kb, m8):
        rows = rows_of(kb)
        bias = bias_scr[rows, :]
        out = []
        for h in range(N_HEADS_A):
            g = h // heads_per_kv
            s = _dot_nt(k_ref[0, rows, g * cdim:(g + 1) * cdim], qa_ref[h, 0]) + bias
            s_scr[h, rows, :] = s
            out.append(jnp.maximum(m8[h], jnp.max(fold8(s), axis=0)))
        return tuple(out)

    m8 = lax.fori_loop(0, nkb, pass1, tuple(jnp.full((8, qb), NEG_INF, F32) for _ in range(N_HEADS_A)))
    m = [jnp.max(x, axis=0, keepdims=True) for x in m8]

    acc_scr[...] = jnp.zeros_like(acc_scr)

    def pass2(kb, _):
        rows = rows_of(kb)
        for h in range(N_HEADS_A):
            g = h // heads_per_kv
            p = jnp.exp2(s_scr[h, rows, :] - m[h]).astype(BF16)
            acc_scr[h] += _dot(vt_ref[0, g * v_rows:(g + 1) * v_rows, rows], p)
        return 0

    lax.fori_loop(0, nkb, pass2, 0)
    for h in range(N_HEADS_A):
        a = acc_scr[h]
        num, den = a[:HEAD_DIM_A], a[HEAD_DIM_A:HEAD_DIM_A + 1]
        for mb in range(1, pair):
            mine = lane >= mb * q_rows
            r0 = mb * V_ROWS
            num = jnp.where(mine, a[r0:r0 + HEAD_DIM_A], num)
            den = jnp.where(mine, a[r0 + HEAD_DIM_A:r0 + HEAD_DIM_A + 1], den)
        o_ref[0, h * HEAD_DIM_A:(h + 1) * HEAD_DIM_A, :] = (num * (1.0 / den)).astype(BF16)


def _dsa(qa_h, qi_h, wi, ki_all, k_all, v_all, *, s_real, pos0, topk):
    _, b, t, _ = qa_h.shape
    s_pad = ki_all.shape[1]
    q_rows = min(256, t)
    pair = 2 if (q_rows * 2 <= 128 and q_rows == t and b % 2 == 0) else 1
    bp = b // pair
    lanes, cdim, v_rows = pair * q_rows, pair * HEAD_DIM_A, pair * V_ROWS

    def block_diag(q):
        if pair == 1:
            return q
        nh, d = q.shape[0], q.shape[3]
        eye = jnp.eye(pair, dtype=q.dtype)
        q6 = q.reshape(nh, bp, pair, t, 1, d) * eye.reshape(1, 1, pair, 1, pair, 1)
        return q6.reshape(nh, bp, pair * t, pair * d)

    wi_op = wi.reshape(bp, pair, t, N_IDX_HEADS).transpose(0, 3, 1, 2).reshape(bp, N_IDX_HEADS, pair * t)
    ki_op = ki_all.reshape(bp, pair, s_pad, IDX_DIM).transpose(0, 2, 1, 3).reshape(bp, s_pad, cdim)
    k_op = (k_all.reshape(bp, pair, s_pad, N_KV_A, HEAD_DIM_A).transpose(0, 2, 3, 1, 4)
            .reshape(bp, s_pad, N_KV_A * cdim))
    vt = v_all.reshape(b, s_pad, N_KV_A, HEAD_DIM_A).transpose(0, 2, 3, 1)
    vt = jnp.concatenate([vt, jnp.ones((b, N_KV_A, V_ROWS - HEAD_DIM_A, s_pad), vt.dtype)], axis=2)
    vt_op = (vt.reshape(bp, pair, N_KV_A, V_ROWS, s_pad).transpose(0, 2, 1, 3, 4)
             .reshape(bp, N_KV_A * v_rows, s_pad))

    body = functools.partial(_dsa_body, s_pad=s_pad, s_real=s_real, q_rows=q_rows, pair=pair, pos0=pos0,
                             topk=topk)
    o_t = pl.pallas_call(
        body,
        grid=(bp, t // q_rows),
        in_specs=[
            pl.BlockSpec((N_HEADS_A, 1, lanes, cdim), lambda i, j: (0, i, j, 0)),
            pl.BlockSpec((N_IDX_HEADS, 1, lanes, cdim), lambda i, j: (0, i, j, 0)),
            pl.BlockSpec((1, N_IDX_HEADS, lanes), lambda i, j: (i, 0, j)),
            pl.BlockSpec((1, s_pad, cdim), lambda i, j: (i, 0, 0)),
            pl.BlockSpec((1, s_pad, N_KV_A * cdim), lambda i, j: (i, 0, 0)),
            pl.BlockSpec((1, N_KV_A * v_rows, s_pad), lambda i, j: (i, 0, 0)),
        ],
        out_specs=pl.BlockSpec((1, WIDTH_A, lanes), lambda i, j: (i, 0, j)),
        out_shape=jax.ShapeDtypeStruct((bp, WIDTH_A, pair * t), BF16),
        scratch_shapes=[pltpu.VMEM((s_pad, lanes), F32), pltpu.VMEM((s_pad, lanes), F32),
                        pltpu.VMEM((N_HEADS_A, s_pad, lanes), F32), pltpu.VMEM((N_HEADS_A, v_rows, lanes), F32)],
        compiler_params=_cparams("parallel", "parallel"),
        name="dsa",
    )(block_diag(qa_h), block_diag(qi_h), wi_op, ki_op, k_op, vt_op)
    return o_t.reshape(bp, WIDTH_A, pair, t).transpose(0, 2, 3, 1).reshape(b * t, WIDTH_A)


def _split3(a):
    hi = a.astype(BF16)
    r1 = a - hi.astype(F32)
    mid = r1.astype(BF16)
    lo = (r1 - mid.astype(F32)).astype(BF16)
    return hi, mid, lo


def _hgrn_body(hb_ref, s0_ref, lb_ref, nw_ref, ob_ref, sfin_ref, state_scr, *, tb):
    t = pl.program_id(1)

    @pl.when(t == 0)
    def _():
        state_scr[...] = s0_ref[0]

    lb = lb_ref[...]
    r = lax.broadcasted_iota(I32, (CHUNK, CHUNK), 0)
    c = lax.broadcasted_iota(I32, (CHUNK, CHUNK), 1)
    tril = jnp.where(r >= c, 1.0, 0.0).astype(BF16)
    ones = jnp.ones((CHUNK, HEAD_DIM_B), BF16)

    for ci in range(tb // CHUNK):
        rows = slice(ci * CHUNK, (ci + 1) * CHUNK)
        f = lb + (1.0 - lb) * jax.nn.sigmoid(hb_ref[0, rows, WIDTH_B:2 * WIDTH_B])
        log_f = jnp.log(f)
        parts = _split3(log_f)
        bcum = _dot(tril, parts[0]) + _dot(tril, parts[1]) + _dot(tril, parts[2])
        for h in range(N_HEADS_B):
            lanes = slice(h * HEAD_DIM_B, (h + 1) * HEAD_DIM_B)
            q = hb_ref[0, rows, lanes]
            k = 1.0 - f[:, lanes]
            v = hb_ref[0, rows, 2 * WIDTH_B + h * HEAD_DIM_B:2 * WIDTH_B + (h + 1) * HEAD_DIM_B]
            gate = hb_ref[0, rows, 3 * WIDTH_B + h * HEAD_DIM_B:3 * WIDTH_B + (h + 1) * HEAD_DIM_B]
            bh = bcum[:, lanes]
            b_last = bh[CHUNK - 1:CHUNK, :]
            vb = v.astype(BF16)
            state = state_scr[h]
            o_inter = _dot((q * jnp.exp(bh)).astype(BF16), state.astype(BF16))
            o_rows = []
            for i in range(CHUNK // HGRN_SUB):
                lo, hi = i * HGRN_SUB, (i + 1) * HGRN_SUB
                ref = bh[lo - 1:lo, :] if i else jnp.zeros((1, HEAD_DIM_B), F32)
                qs = (q[lo:hi] * jnp.exp(bh[lo:hi] - ref)).astype(BF16)
                ks = (k[:hi] * jnp.exp(ref - bh[:hi])).astype(BF16)
                a = _dot_nt(qs, ks)
                tq = lo + lax.broadcasted_iota(I32, (HGRN_SUB, hi), 0)
                tk = lax.broadcasted_iota(I32, (HGRN_SUB, hi), 1)
                a = jnp.where(tk <= tq, a, 0.0)
                o_rows.append(_dot(a.astype(BF16), vb[:hi]))
            o = o_inter + jnp.concatenate(o_rows, axis=0)
            kdec = (k * jnp.exp(b_last - bh)).astype(BF16)
            col_sum = (_dot_tn(parts[0][:, lanes], ones) + _dot_tn(parts[1][:, lanes], ones)
                       + _dot_tn(parts[2][:, lanes], ones))
            state_scr[h] = jnp.exp(col_sum) * state + _dot_tn(kdec, vb)
            ms = jnp.mean(o * o, axis=-1, keepdims=True)
            y = o * lax.rsqrt(ms + RMS_EPS) * nw_ref[...] * (gate * jax.nn.sigmoid(gate))
            ob_ref[0, rows, lanes] = y.astype(BF16)

    @pl.when(t == pl.num_programs(1) - 1)
    def _():
        sfin_ref[0] = state_scr[...]


def _hgrn(hb, s0, lb, nw):
    b, t, _ = hb.shape
    tb = min(256, t)
    return pl.pallas_call(
        functools.partial(_hgrn_body, tb=tb),
        grid=(b, t // tb),
        in_specs=[
            pl.BlockSpec((1, tb, 4 * WIDTH_B), lambda i, j: (i, j, 0)),
            pl.BlockSpec((1, N_HEADS_B, HEAD_DIM_B, HEAD_DIM_B), lambda i, j: (i, 0, 0, 0)),
            pl.BlockSpec((1, WIDTH_B), lambda i, j: (0, 0)),
            pl.BlockSpec((1, HEAD_DIM_B), lambda i, j: (0, 0)),
        ],
        out_specs=[
            pl.BlockSpec((1, tb, WIDTH_B), lambda i, j: (i, j, 0)),
            pl.BlockSpec((1, N_HEADS_B, HEAD_DIM_B, HEAD_DIM_B), lambda i, j: (i, 0, 0, 0)),
        ],
        out_shape=[jax.ShapeDtypeStruct((b, t, WIDTH_B), BF16),
                   jax.ShapeDtypeStruct((b, N_HEADS_B, HEAD_DIM_B, HEAD_DIM_B), F32)],
        scratch_shapes=[pltpu.VMEM((N_HEADS_B, HEAD_DIM_B, HEAD_DIM_B), F32)],
        compiler_params=_cparams("parallel", "arbitrary"),
        name="hgrn",
    )(hb, s0, lb, nw)


def _outproj_body(oa_ref, ob_ref, x_ref, w_ref, lnw_ref, lnb_ref, wrh_ref, wrl_ref, br_ref, x1_ref, lg_ref):
    y = _dot(oa_ref[...], w_ref[:WIDTH_A, :]) + _dot(ob_ref[...], w_ref[WIDTH_A:, :])
    x1 = _layer_norm(DN_ALPHA * x_ref[...] + y, lnw_ref[...], lnb_ref[...])
    x1_ref[...] = x1
    hi = x1.astype(BF16)
    lo = (x1 - hi.astype(F32)).astype(BF16)
    lg_ref[...] = (_dot_nt(wrh_ref[...], hi) + _dot_nt(wrh_ref[...], lo) + _dot_nt(wrl_ref[...], hi)
                   + br_ref[...])


def _outproj(oa, ob, x2d, w_out, lnw, lnb, wr_hi, wr_lo, br):
    n = x2d.shape[0]
    tm = min(512, n)
    row = lambda width: pl.BlockSpec((tm, width), lambda i: (i, 0))
    full = lambda a: pl.BlockSpec(a.shape, lambda i: (0,) * a.ndim)
    return pl.pallas_call(
        _outproj_body,
        grid=(n // tm,),
        in_specs=[row(WIDTH_A), row(WIDTH_B), row(D_MODEL), full(w_out), full(lnw), full(lnb),
                  full(wr_hi), full(wr_lo), full(br)],
        out_specs=[row(D_MODEL), pl.BlockSpec((N_EXPERTS, tm), lambda i: (0, i))],
        out_shape=[jax.ShapeDtypeStruct((n, D_MODEL), F32), jax.ShapeDtypeStruct((N_EXPERTS, n), F32)],
        compiler_params=_cparams("parallel"),
        name="outproj",
    )(oa, ob, x2d, w_out, lnw, lnb, wr_hi, wr_lo, br)


def _route_body(lg_ref, lpos_ref, gate_ref, tc_ref, base_ref, cnt_ref, cnt_scr, *, tr):
    @pl.when(pl.program_id(0) == 0)
    def _():
        cnt_scr[...] = jnp.zeros_like(cnt_scr)

    l = lg_ref[...]
    rows = lax.broadcasted_iota(I32, (N_EXPERTS, tr), 0)
    vals, hots = [], []
    for j in range(TOP_K):
        m = jnp.max(l, axis=0, keepdims=True)
        idx = jnp.min(jnp.where(l == m, rows, N_EXPERTS), axis=0, keepdims=True)
        hot = rows == idx
        vals.append(m)
        hots.append(hot)
        l = jnp.where(hot, NEG_INF, l)
    es = [jnp.exp(v - vals[0]) for v in vals]
    inv = 1.0 / (es[0] + es[1] + es[2] + es[3])
    for j in range(TOP_K):
        gate_ref[j:j + 1, :] = es[j] * inv
    chosen = jnp.zeros((N_EXPERTS, tr), F32)
    for hot in hots:
        chosen = chosen + jnp.where(hot, 1.0, 0.0)
    r = lax.broadcasted_iota(I32, (tr, tr), 0)
    c = lax.broadcasted_iota(I32, (tr, tr), 1)
    before = jnp.where(r < c, 1.0, 0.0).astype(BF16)
    prior = _dot(chosen.astype(BF16), before)
    units = jnp.ceil(jnp.sum(chosen, axis=1, keepdims=True) * (1.0 / ROW_ALIGN))
    er = lax.broadcasted_iota(I32, (N_EXPERTS, N_EXPERTS), 0)
    ec = lax.broadcasted_iota(I32, (N_EXPERTS, N_EXPERTS), 1)
    earlier = jnp.where(ec < er, 1.0, 0.0).astype(BF16)
    run_len = jnp.broadcast_to(units, (N_EXPERTS, 128)) * ROW_ALIGN
    run_start = _dot(earlier, jnp.broadcast_to(units, (N_EXPERTS, 128)).astype(BF16)) * ROW_ALIGN
    where_in_tile = prior + run_start[:, 0:1]
    for j in range(TOP_K):
        lpos_ref[j:j + 1, :] = jnp.sum(jnp.where(hots[j], where_in_tile, 0.0), axis=0,
                                       keepdims=True).astype(I32)
    tc_ref[0] = run_len.astype(I32)
    base_ref[0] = cnt_scr[...].astype(I32)
    cnt_scr[...] = cnt_scr[...] + run_len
    cnt_ref[...] = cnt_scr[...].astype(I32)


def _route(logits_t):
    n = logits_t.shape[1]
    tr = MOE_TILE
    tiles = n // tr
    tok = lambda rows: pl.BlockSpec((rows, tr), lambda i: (0, i))
    per_tile = pl.BlockSpec((1, N_EXPERTS, 128), lambda i: (i, 0, 0))
    return pl.pallas_call(
        functools.partial(_route_body, tr=tr),
        grid=(tiles,),
        in_specs=[tok(N_EXPERTS)],
        out_specs=[tok(TOP_K), tok(TOP_K), per_tile, per_tile, pl.BlockSpec((N_EXPERTS, 128), lambda i: (0, 0))],
        out_shape=[jax.ShapeDtypeStruct((TOP_K, n), I32), jax.ShapeDtypeStruct((TOP_K, n), F32),
                   jax.ShapeDtypeStruct((tiles, N_EXPERTS, 128), I32),
                   jax.ShapeDtypeStruct((tiles, N_EXPERTS, 128), I32),
                   jax.ShapeDtypeStruct((N_EXPERTS, 128), I32)],
        scratch_shapes=[pltpu.VMEM((N_EXPERTS, 128), F32)],
        compiler_params=_cparams("arbitrary"),
        name="route",
    )(logits_t)


def _pack_pairs(a):
    half = a.shape[1] // 2
    hi = lax.bitcast_convert_type(a[:, :half], I32)
    lo = lax.shift_right_logical(lax.bitcast_convert_type(a[:, half:], I32), 16)
    return hi | lo


def _unpack_pairs(pk):
    hi = lax.bitcast_convert_type(pk & jnp.int32(-65536), F32).astype(BF16)
    lo = lax.bitcast_convert_type(pk << 16, F32).astype(BF16)
    return hi, lo


def _for_each_run_piece(tc_ref, g0_ref, t, fn):
    def per_expert(e, local):
        c = tc_ref[t, e]
        glob = g0_ref[t, e]
        lo = local
        for p in RUN_PIECES:
            take = (c & p) != 0

            @pl.when(take)
            def _():
                fn(pl.multiple_of(lo, ROW_ALIGN), pl.multiple_of(glob, ROW_ALIGN), p)

            step = jnp.where(take, p, 0)
            lo = lo + step
            glob = glob + step
        return local + c

    lax.fori_loop(0, N_EXPERTS, per_expert, 0)


def _dispatch_body(tc_ref, g0_ref, seg_ref, lpos_ref, x_ref, buf_ref, stage, zero_scr, zsem, sems, *, n_blocks):
    i = pl.program_id(0)
    slot = i % 2

    def run_copy(slot_, lo, glob, rows):
        return pltpu.make_async_copy(stage.at[slot_, pl.ds(lo, rows), :], buf_ref.at[pl.ds(glob, rows), :],
                                     sems.at[slot_])

    @pl.when(i == 0)
    def _():
        zero_scr[...] = jnp.zeros_like(zero_scr)
        n_used = seg_ref[1, N_EXPERTS - 1] // EXPERT_ROWS

        def block_copy(start):
            return pltpu.make_async_copy(
                zero_scr, buf_ref.at[pl.ds(pl.multiple_of(start, EXPERT_ROWS), EXPERT_ROWS), :], zsem)

        def clears(action):
            for e in range(N_EXPERTS):
                @pl.when(seg_ref[1, e] > seg_ref[0, e])
                def _():
                    action(block_copy(seg_ref[1, e] - EXPERT_ROWS))

                @pl.when(n_used + e < n_blocks)
                def _():
                    action(block_copy((n_used + e) * EXPERT_ROWS))

        clears(lambda cp: cp.start())
        clears(lambda cp: cp.wait())

    r = lax.broadcasted_iota(I32, (MOE_SLOTS, MOE_TILE), 0)
    onehot = jnp.zeros((MOE_SLOTS, MOE_TILE), F32)
    for j in range(TOP_K):
        onehot = jnp.where(r == lpos_ref[j:j + 1, :], 1.0, onehot)
    stage[slot] = _pack_pairs(_dot(onehot.astype(BF16), x_ref[...].astype(BF16)))

    @pl.when(i > 0)
    def _():
        _for_each_run_piece(tc_ref, g0_ref, i - 1,
                            lambda lo, glob, rows: run_copy(1 - slot, lo, glob, rows).wait())

    _for_each_run_piece(tc_ref, g0_ref, i, lambda lo, glob, rows: run_copy(slot, lo, glob, rows).start())

    @pl.when(i == pl.num_programs(0) - 1)
    def _():
        _for_each_run_piece(tc_ref, g0_ref, i, lambda lo, glob, rows: run_copy(slot, lo, glob, rows).wait())


def _dispatch(x1, lpos_t, tc, g0, seg, n_rows):
    n = x1.shape[0]
    return pl.pallas_call(
        functools.partial(_dispatch_body, n_blocks=n_rows // EXPERT_ROWS),
        grid_spec=pltpu.PrefetchScalarGridSpec(
            num_scalar_prefetch=3,
            grid=(n // MOE_TILE,),
            in_specs=[pl.BlockSpec((TOP_K, MOE_TILE), lambda i, *_: (0, i)),
                      pl.BlockSpec((MOE_TILE, D_MODEL), lambda i, *_: (i, 0))],
            out_specs=pl.BlockSpec(memory_space=pl.ANY),
            scratch_shapes=[pltpu.VMEM((2, MOE_SLOTS, D_MODEL // 2), I32),
                            pltpu.VMEM((EXPERT_ROWS, D_MODEL // 2), I32),
                            pltpu.SemaphoreType.DMA(()), pltpu.SemaphoreType.DMA((2,))],
        ),
        out_shape=jax.ShapeDtypeStruct((n_rows, D_MODEL // 2), I32),
        compiler_params=_cparams("arbitrary"),
        name="dispatch",
    )(tc, g0, seg, lpos_t, x1)


def _experts_body(be_ref, nu_ref, x_ref, wgu_ref, bgu_ref, wd_ref, bd_ref, o_ref, wgu_bf, wd_bf):
    i = pl.program_id(0)

    @pl.when(jnp.logical_or(i == 0, be_ref[i] != be_ref[jnp.maximum(i - 1, 0)]))
    def _():
        for r in range(0, D_MODEL, 256):
            wgu_bf[r:r + 256, :] = wgu_ref[0, r:r + 256, :].astype(BF16)
            wd_bf[r:r + 256, :] = wd_ref[0, r:r + 256, :].astype(BF16)

    @pl.when(i < nu_ref[0])
    def _():
        x_lo, x_hi = _unpack_pairs(x_ref[...])
        half = D_MODEL // 2
        gu = _dot(x_lo, wgu_bf[:half, :]) + _dot(x_hi, wgu_bf[half:, :]) + bgu_ref[0]
        gate = jnp.minimum(gu[:, :D_EXPERT], SWIGLU_LIMIT)
        up = jnp.clip(gu[:, D_EXPERT:], -SWIGLU_LIMIT, SWIGLU_LIMIT)
        hdn = (up + 1.0) * gate * jax.nn.sigmoid(SWIGLU_ALPHA * gate)
        out = _dot(hdn.astype(BF16), wd_bf[...]) + bd_ref[0]
        o_ref[...] = _pack_pairs(out.astype(BF16).astype(F32))

    @pl.when(i >= nu_ref[0])
    def _():
        o_ref[...] = jnp.zeros_like(o_ref)


def _experts(buf, block_e, n_used, w_gu, b_gu, w_down, b_down):
    n_blocks = buf.shape[0] // EXPERT_ROWS
    rows = pl.BlockSpec((EXPERT_ROWS, D_MODEL // 2), lambda i, be, nu: (jnp.minimum(i, nu[0] - 1), 0))
    per_e = lambda a: pl.BlockSpec((1,) + a.shape[1:], lambda i, be, nu: (be[i],) + (0,) * (a.ndim - 1))
    return pl.pallas_call(
        _experts_body,
        grid_spec=pltpu.PrefetchScalarGridSpec(
            num_scalar_prefetch=2,
            grid=(n_blocks,),
            in_specs=[rows, per_e(w_gu), per_e(b_gu), per_e(w_down), per_e(b_down)],
            out_specs=pl.BlockSpec((EXPERT_ROWS, D_MODEL // 2), lambda i, be, nu: (i, 0)),
            scratch_shapes=[pltpu.VMEM((D_MODEL, 2 * D_EXPERT), BF16), pltpu.VMEM((D_EXPERT, D_MODEL), BF16)],
        ),
        out_shape=jax.ShapeDtypeStruct(buf.shape, I32),
        compiler_params=_cparams("arbitrary"),
        name="experts",
    )(block_e, n_used, buf, w_gu, b_gu, w_down, b_down)


def _final_body(tc_ref, g0_ref, lpos_ref, gate_ref, x1_ref, p_ref, lnw_ref, lnb_ref, wg_ref, bg_ref, wp_ref,
                eo_ref, y_ref, stage, sem):
    i = pl.program_id(0)

    def run_copy(lo, glob, rows):
        return pltpu.make_async_copy(eo_ref.at[pl.ds(glob, rows), :], stage.at[pl.ds(lo, rows), :], sem)

    stage[...] = jnp.zeros_like(stage)
    _for_each_run_piece(tc_ref, g0_ref, i, lambda lo, glob, rows: run_copy(lo, glob, rows).start())
    _for_each_run_piece(tc_ref, g0_ref, i, lambda lo, glob, rows: run_copy(lo, glob, rows).wait())

    r = lax.broadcasted_iota(I32, (MOE_TILE, MOE_SLOTS), 1)
    g = jnp.zeros((MOE_TILE, MOE_SLOTS), F32)
    for j in range(TOP_K):
        g = jnp.where(r == lpos_ref[:, j:j + 1], gate_ref[:, j:j + 1], g)
    g_hi = g.astype(BF16)
    g_lo = (g - g_hi.astype(F32)).astype(BF16)
    e_a, e_b = _unpack_pairs(stage[...])
    moe = jnp.concatenate([_dot(g_hi, e_a) + _dot(g_lo, e_a), _dot(g_hi, e_b) + _dot(g_lo, e_b)], axis=1)
    x2 = _layer_norm(DN_ALPHA * x1_ref[...] + moe, lnw_ref[...], lnb_ref[...])
    ple_gate = jax.nn.sigmoid(_dot(x2.astype(BF16), wg_ref[...]) + bg_ref[...])
    y_ref[...] = x2 + ple_gate * _dot(p_ref[...].astype(BF16), wp_ref[...])


def _final(tc, g0, lpos, gate, x1, p2d, lnw, lnb, w_gate, b_gate, w_ple, expert_out):
    n = x1.shape[0]
    row = lambda width: pl.BlockSpec((MOE_TILE, width), lambda i, *_: (i, 0))
    full = lambda a: pl.BlockSpec(a.shape, lambda i, *_: (0,) * a.ndim)
    return pl.pallas_call(
        _final_body,
        grid_spec=pltpu.PrefetchScalarGridSpec(
            num_scalar_prefetch=2,
            grid=(n // MOE_TILE,),
            in_specs=[row(TOP_K), row(TOP_K), row(D_MODEL), row(D_PLE), full(lnw), full(lnb), full(w_gate),
                      full(b_gate), full(w_ple), pl.BlockSpec(memory_space=pl.ANY)],
            out_specs=row(D_MODEL),
            scratch_shapes=[pltpu.VMEM((MOE_SLOTS, D_MODEL // 2), I32), pltpu.SemaphoreType.DMA(())],
        ),
        out_shape=jax.ShapeDtypeStruct((n, D_MODEL), F32),
        compiler_params=_cparams("arbitrary"),
        name="combine_final",
    )(tc, g0, lpos, gate, x1, p2d, lnw, lnb, w_gate, b_gate, w_ple, expert_out)


def _round_up(a, m):
    return -(-a // m) * m


def _encoder_layer(x, p, hist_k, hist_v, hist_kidx, s0, pos0, wts):
    b, t, _ = x.shape
    n = b * t
    x2d = x.reshape(n, D_MODEL)

    qa, ka, va, qi, ki, wi, hb = _inproj(x2d, wts["w_in"], wts["idx_lnw"], wts["idx_lnb"])

    k_new = ka.reshape(b, t, N_KV_A * HEAD_DIM_A)
    v_new = va.reshape(b, t, N_KV_A * HEAD_DIM_A)
    ki_new = ki.reshape(b, t, IDX_DIM)
    if hist_k is None:
        k_all, v_all, ki_all = k_new, v_new, ki_new
    else:
        past = hist_k.shape[1]
        k_all = jnp.concatenate([hist_k.reshape(b, past, -1), k_new], axis=1)
        v_all = jnp.concatenate([hist_v.reshape(b, past, -1), v_new], axis=1)
        ki_all = jnp.concatenate([hist_kidx, ki_new], axis=1)
    s_real = k_all.shape[1]
    topk = min(TOPK_MAX, s_real // 4)
    pad = ((0, 0), (0, _round_up(s_real, KEY_BLOCK) - s_real), (0, 0))
    oa = _dsa(qa.reshape(N_HEADS_A, b, t, HEAD_DIM_A), qi.reshape(N_IDX_HEADS, b, t, IDX_DIM),
              wi.reshape(b, t, N_IDX_HEADS), jnp.pad(ki_all.astype(BF16), pad), jnp.pad(k_all.astype(BF16), pad),
              jnp.pad(v_all.astype(BF16), pad), s_real=s_real, pos0=pos0, topk=topk)

    ob, s_fin = _hgrn(hb.reshape(b, t, 4 * WIDTH_B), s0, wts["lb"], wts["hgrn_nw"])

    x1, logits_t = _outproj(oa, ob.reshape(n, WIDTH_B), x2d, wts["w_out"], wts["ln1_w"], wts["ln1_b"],
                            wts["wr_hi"], wts["wr_lo"], wts["b_router"])

    assert n % MOE_TILE == 0
    tiles = n // MOE_TILE
    lpos_t, gate_t, tile_runs, tile_base, counts = _route(logits_t)
    tc, counts = tile_runs[:, :, 0], counts[:, 0]
    padded = (counts + EXPERT_ROWS - 1) // EXPERT_ROWS * EXPERT_ROWS
    seg_end = jnp.cumsum(padded)
    seg_start = seg_end - padded
    g0 = (seg_start[None, :] + tile_base[:, :, 0]).astype(I32)
    n_blocks = -(-(n * TOP_K + N_EXPERTS * tiles * (ROW_ALIGN - 1) + N_EXPERTS * (EXPERT_ROWS - 1))
                 // EXPERT_ROWS)
    block_row0 = jnp.arange(n_blocks, dtype=I32) * EXPERT_ROWS
    block_e = jnp.minimum(jnp.sum((seg_end[None, :] <= block_row0[:, None]).astype(I32), axis=1), N_EXPERTS - 1)
    n_used = (seg_end[-1:] // EXPERT_ROWS).astype(I32)
    seg = jnp.stack([seg_start, seg_end]).astype(I32)
    buf = _dispatch(x1, lpos_t, tc, g0, seg, n_blocks * EXPERT_ROWS)
    expert_out = _experts(buf, block_e, n_used, wts["w_gu"], wts["b_gu"], wts["w_down"], wts["b_down"])
    y = _final(tc, g0, lpos_t.T, gate_t.T, x1, p.reshape(n, D_PLE), wts["ln2_w"], wts["ln2_b"],
               wts["w_ple_gate"], wts["b_ple_gate"], wts["w_ple"], expert_out)

    return (y.reshape(b, t, D_MODEL), ka.reshape(b, t, N_KV_A, HEAD_DIM_A), va.reshape(b, t, N_KV_A, HEAD_DIM_A),
            ki_new, s_fin)


def _prep_weights(w_in, w_out, idx_k_norm_w, idx_k_norm_b, lb, hgrn_norm_w, ln1_w, ln1_b, w_router, b_router,
                  w_gu, b_gu, w_down, b_down, ln2_w, ln2_b, w_ple, w_ple_gate, b_ple_gate):
    n_a = COL_KW + IDX_DIM + N_IDX_HEADS
    w_pad = jnp.concatenate(
        [w_in[:, :n_a], jnp.zeros((D_MODEL, COL_HB - n_a), w_in.dtype), w_in[:, n_a:]], axis=1).astype(BF16)
    wr_t = w_router.T
    wr_hi = wr_t.astype(BF16)
    row = lambda a: a.reshape(1, -1)
    return dict(
        w_in=w_pad, idx_lnw=row(idx_k_norm_w), idx_lnb=row(idx_k_norm_b), lb=row(lb), hgrn_nw=row(hgrn_norm_w),
        w_out=w_out.astype(BF16), ln1_w=row(ln1_w), ln1_b=row(ln1_b),
        wr_hi=wr_hi, wr_lo=(wr_t - wr_hi.astype(F32)).astype(BF16), b_router=b_router.reshape(N_EXPERTS, 1),
        w_gu=w_gu, b_gu=b_gu.reshape(N_EXPERTS, 1, 2 * D_EXPERT),
        w_down=w_down, b_down=b_down.reshape(N_EXPERTS, 1, D_MODEL),
        ln2_w=row(ln2_w), ln2_b=row(ln2_b), w_ple=w_ple.astype(BF16), w_ple_gate=w_ple_gate.astype(BF16),
        b_ple_gate=row(b_ple_gate))


def kernel(x_prompt, x_sample, cache_k, cache_v, cache_kidx, state_hgrn, p_prompt, p_sample, w_in, w_out,
           idx_k_norm_w, idx_k_norm_b, hgrn_lb_logits, hgrn_norm_w, ln1_w, ln1_b, w_router, b_router, w_gu, b_gu,
           w_down, b_down, ln2_w, ln2_b, w_ple, w_ple_gate, b_ple_gate):
    lb_all = jnp.cumsum(jax.nn.softmax(hgrn_lb_logits.astype(F32), axis=0), axis=0)
    xp, xs = x_prompt, x_sample
    outs = [[] for _ in range(8)]
    for i in range(DEPTH):
        wts = _prep_weights(w_in[i], w_out[i], idx_k_norm_w[i], idx_k_norm_b[i], lb_all[i], hgrn_norm_w[i],
                            ln1_w[i], ln1_b[i], w_router[i], b_router[i], w_gu[i], b_gu[i], w_down[i], b_down[i],
                            ln2_w[i], ln2_b[i], w_ple[i], w_ple_gate[i], b_ple_gate[i])
        s0p = jnp.zeros((xp.shape[0], N_HEADS_B, HEAD_DIM_B, HEAD_DIM_B), F32)
        xp, kp, vp, kip, sp = _encoder_layer(xp, p_prompt[i], None, None, None, s0p, 0, wts)
        xs, kn, vn, kin, sn = _encoder_layer(xs, p_sample[i], cache_k[i], cache_v[i], cache_kidx[i],
                                             state_hgrn[i], cache_k.shape[2], wts)
        for lst, val in zip(outs, (kp, vp, kip, sp, kn, vn, kin, sn)):
            lst.append(val)
    return (xp, xs) + tuple(jnp.stack(l) for l in outs)
```

```python
import functools

import jax
import jax.numpy as jnp
from jax import lax
from jax.experimental import pallas as pl
from jax.experimental.pallas import tpu as pltpu

F32 = jnp.float32
BF16 = jnp.bfloat16
I32 = jnp.int32

D_MODEL = 1024
CHUNK = 64
CHUNK_SHIFT = 6
WIDTH_A = 512
HEAD_DIM_A = 64
N_HEADS_A = 8
N_KV_A = 2
N_IDX_HEADS = 4
IDX_DIM = 64
IDX_SCALE = IDX_DIM ** -0.5 * N_IDX_HEADS ** -0.5
TOPK_MAX = 256
WIDTH_B = 512
HEAD_DIM_B = 128
N_HEADS_B = 4
N_EXPERTS = 32
TOP_K = 4
D_EXPERT = 1024
SWIGLU_LIMIT = 7.0
SWIGLU_ALPHA = 1.702
D_PLE = 256
LN_EPS = 1e-5
RMS_EPS = 1e-6
DEPTH = 1
DN_ALPHA = (2 * DEPTH) ** 0.25

COL_QA, COL_KA, COL_VA, COL_QI, COL_KW, COL_HB, COL_END = 0, 512, 640, 768, 1024, 1152, 3200

VMEM_LIMIT = 56 * 1024 * 1024
KEY_BLOCK = 256
V_ROWS = 80
LOG2_E = 1.4426950408889634
HGRN_SUB = 16
EXPERT_ROWS = 512
MOE_TILE = 512
ROW_ALIGN = 8
RUN_PIECES = (512, 256, 128, 64, 32, 16, 8)
MOE_SLOTS = -(-(TOP_K * MOE_TILE + N_EXPERTS * (ROW_ALIGN - 1)) // 256) * 256
NEG_INF = float("-inf")
INT_MIN = -(2 ** 31)


def _cparams(*sem):
    return pltpu.CompilerParams(dimension_semantics=sem, vmem_limit_bytes=VMEM_LIMIT)


def _dot(a, b):
    return jnp.dot(a, b, preferred_element_type=F32)


def _dot_nt(a, b):
    return lax.dot_general(a, b, (((1,), (1,)), ((), ())), preferred_element_type=F32)


def _dot_tn(a, b):
    return lax.dot_general(a, b, (((0,), (0,)), ((), ())), preferred_element_type=F32)


def _layer_norm(z, w, b):
    mu = jnp.mean(z, axis=-1, keepdims=True)
    d = z - mu
    var = jnp.mean(d * d, axis=-1, keepdims=True)
    return d * lax.rsqrt(var + LN_EPS) * w + b


def _inproj_body(x_ref, w_ref, lnw_ref, lnb_ref, qa_ref, ka_ref, va_ref, qi_ref, ki_ref, wi_ref, hb_ref):
    xb = x_ref[...].astype(BF16)

    def mm(c0, c1):
        return _dot(xb, w_ref[:, c0:c1])

    qa = (mm(COL_QA, COL_KA) * (HEAD_DIM_A ** -0.5 * LOG2_E)).astype(BF16)
    for h in range(N_HEADS_A):
        qa_ref[h] = qa[:, h * HEAD_DIM_A:(h + 1) * HEAD_DIM_A]
    ka_ref[...] = mm(COL_KA, COL_VA)
    va_ref[...] = mm(COL_VA, COL_QI)
    qi = mm(COL_QI, COL_KW).astype(BF16)
    for h in range(N_IDX_HEADS):
        qi_ref[h] = qi[:, h * IDX_DIM:(h + 1) * IDX_DIM]
    kw = mm(COL_KW, COL_HB)
    ki_ref[...] = _layer_norm(kw[:, :IDX_DIM], lnw_ref[...], lnb_ref[...])
    wi_ref[...] = kw[:, IDX_DIM:IDX_DIM + N_IDX_HEADS]
    hb_ref[...] = mm(COL_HB, COL_END)


def _inproj(x2d, w_pad, lnw, lnb):
    n = x2d.shape[0]
    tm = min(512, n)
    row = lambda width: pl.BlockSpec((tm, width), lambda i: (i, 0))
    full = lambda a: pl.BlockSpec(a.shape, lambda i: (0,) * a.ndim)
    heads = lambda nh, d: pl.BlockSpec((nh, tm, d), lambda i: (0, i, 0))
    sds = jax.ShapeDtypeStruct
    return pl.pallas_call(
        _inproj_body,
        grid=(n // tm,),
        in_specs=[row(D_MODEL), full(w_pad), full(lnw), full(lnb)],
        out_specs=[heads(N_HEADS_A, HEAD_DIM_A), row(128), row(128), heads(N_IDX_HEADS, IDX_DIM),
                   row(IDX_DIM), row(N_IDX_HEADS), row(4 * WIDTH_B)],
        out_shape=[sds((N_HEADS_A, n, HEAD_DIM_A), BF16), sds((n, 128), F32), sds((n, 128), F32),
                   sds((N_IDX_HEADS, n, IDX_DIM), BF16), sds((n, IDX_DIM), F32), sds((n, N_IDX_HEADS), F32),
                   sds((n, 4 * WIDTH_B), F32)],
        compiler_params=_cparams("parallel"),
        name="inproj",
    )(x2d, w_pad, lnw, lnb)


def _order_bits_to_f32(u):
    key = u ^ INT_MIN
    bits = key ^ ((key >> 31) & 0x7FFFFFFF)
    f = lax.bitcast_convert_type(bits, F32)
    return jnp.where(u >= 0, jnp.where(u <= 0x007FFFFF, NEG_INF, f), f)


def _dsa_body(qa_ref, qi_ref, wi_ref, kidx_ref, k_ref, vt_ref, o_ref, sc_scr, bias_scr, s_scr, acc_scr,
              *, s_pad, s_real, q_rows, pair, pos0, topk):
    kb_rows = KEY_BLOCK
    heads_per_kv = N_HEADS_A // N_KV_A
    qb = pair * q_rows
    cdim = pair * HEAD_DIM_A
    v_rows = pair * V_ROWS
    j = pl.program_id(1)
    q_lo = pos0 + j * q_rows
    lane = lax.broadcasted_iota(I32, (1, qb), 1)
    q_chunk = (q_lo + (lane & (q_rows - 1))) >> CHUNK_SHIFT
    k_lim = (((q_lo + q_rows - 1) >> CHUNK_SHIFT) + 1) * CHUNK
    nkb = jnp.minimum(s_pad // kb_rows, (k_lim + kb_rows - 1) // kb_rows)

    def rows_of(kb):
        return pl.ds(pl.multiple_of(kb * kb_rows, kb_rows), kb_rows)

    def fold8(a):
        return a.reshape(kb_rows // 8, 8, qb)

    def score_blk(kb, _):
        rows = rows_of(kb)
        kidx = kidx_ref[0, rows, :]
        for h in range(N_IDX_HEADS):
            lg = _dot_nt(kidx, qi_ref[h, 0])
            term = jnp.maximum(lg, 0.0) * wi_ref[0, h:h + 1, :]
            if h == 0:
                sc_scr[rows, :] = term
            elif h < N_IDX_HEADS - 1:
                sc_scr[rows, :] += term
            else:
                spos = kb * kb_rows + lax.broadcasted_iota(I32, (kb_rows, qb), 0)
                sc = jnp.where((spos >> CHUNK_SHIFT) <= q_chunk,
                               (sc_scr[rows, :] + term) * IDX_SCALE + 0.0, NEG_INF)
                if s_real < s_pad:
                    sc = jnp.where(spos < s_real, sc, NEG_INF)
                sc_scr[rows, :] = sc
        return 0

    lax.fori_loop(0, nkb, score_blk, 0)

    def count(thr, strict):
        def body(kb, acc):
            blk = sc_scr[rows_of(kb), :]
            hit = (blk > thr) if strict else (blk >= thr)
            return acc + jnp.sum(fold8(jnp.where(hit, 1, 0).astype(I32)), axis=0)
        acc = lax.fori_loop(0, nkb, body, jnp.zeros((8, qb), I32))
        return jnp.sum(acc, axis=0, keepdims=True)

    def bit_step(i, carry):
        prefix, cnt = carry
        cand = prefix | lax.shift_left(jnp.int32(1), 31 - i)
        c = count(_order_bits_to_f32(cand), False)
        take = c >= topk
        return jnp.where(take, cand, prefix), jnp.where(take, c, cnt)

    prefix, cnt_ge = lax.fori_loop(0, 32, bit_step,
                                   (jnp.zeros((1, qb), I32), jnp.full((1, qb), nkb * kb_rows, I32)))
    tau = _order_bits_to_f32(prefix)
    finite_tau = tau > NEG_INF
    tau_floor = jnp.maximum(tau, jnp.finfo(F32).min)
    tie_lanes = jnp.where(finite_tau, jnp.where(cnt_ge > topk, 1, 0), 0)
    has_ties = jnp.max(tie_lanes) > 0

    @pl.when(jnp.logical_not(has_ties))
    def _():
        def body(kb, _):
            blk = sc_scr[rows_of(kb), :]
            bias_scr[rows_of(kb), :] = jnp.where(blk >= tau_floor, 0.0, NEG_INF)
            return 0
        lax.fori_loop(0, nkb, body, 0)

    @pl.when(has_ties)
    def _():
        need = jnp.where(finite_tau, (topk - count(tau, True)).astype(F32), 0.0)
        r = lax.broadcasted_iota(I32, (kb_rows, kb_rows), 0)
        c = lax.broadcasted_iota(I32, (kb_rows, kb_rows), 1)
        tril = jnp.where(r >= c, 1.0, 0.0).astype(BF16)

        def body(kb, seen):
            blk = sc_scr[rows_of(kb), :]
            eq = blk == tau
            rank = _dot(tril, jnp.where(eq, 1.0, 0.0).astype(BF16)) + seen
            tie_bias = jnp.where(eq, jnp.where(rank <= need, 0.0, NEG_INF), NEG_INF)
            bias_scr[rows_of(kb), :] = jnp.where(blk > tau, 0.0, tie_bias)
            return rank[kb_rows - 1:kb_rows, :]
        lax.fori_loop(0, nkb, body, jnp.zeros((1, qb), F32))

    def pass1(kb, m8):
        rows = rows_of(kb)
        bias = bias_scr[rows, :]
        out = []
        for h in range(N_HEADS_A):
            g = h // heads_per_kv
            s = _dot_nt(k_ref[0, rows, g * cdim:(g + 1) * cdim], qa_ref[h, 0]) + bias
            s_scr[h, rows, :] = s
            out.append(jnp.maximum(m8[h], jnp.max(fold8(s), axis=0)))
        return tuple(out)

    m8 = lax.fori_loop(0, nkb, pass1, tuple(jnp.full((8, qb), NEG_INF, F32) for _ in range(N_HEADS_A)))
    m = [jnp.max(x, axis=0, keepdims=True) for x in m8]

    acc_scr[...] = jnp.zeros_like(acc_scr)

    def pass2(kb, _):
        rows = rows_of(kb)
        for h in range(N_HEADS_A):
            g = h // heads_per_kv
            p = jnp.exp2(s_scr[h, rows, :] - m[h]).astype(BF16)
            acc_scr[h] += _dot(vt_ref[0, g * v_rows:(g + 1) * v_rows, rows], p)
        return 0

    lax.fori_loop(0, nkb, pass2, 0)
    for h in range(N_HEADS_A):
        a = acc_scr[h]
        num, den = a[:HEAD_DIM_A], a[HEAD_DIM_A:HEAD_DIM_A + 1]
        for mb in range(1, pair):
            mine = lane >= mb * q_rows
            r0 = mb * V_ROWS
            num = jnp.where(mine, a[r0:r0 + HEAD_DIM_A], num)
            den = jnp.where(mine, a[r0 + HEAD_DIM_A:r0 + HEAD_DIM_A + 1], den)
        o_ref[0, h * HEAD_DIM_A:(h + 1) * HEAD_DIM_A, :] = (num * (1.0 / den)).astype(BF16)


def _dsa(qa_h, qi_h, wi, ki_all, k_all, v_all, *, s_real, pos0, topk):
    _, b, t, _ = qa_h.shape
    s_pad = ki_all.shape[1]
    q_rows = min(256, t)
    pair = 2 if (q_rows * 2 <= 128 and q_rows == t and b % 2 == 0) else 1
    bp = b // pair
    lanes, cdim, v_rows = pair * q_rows, pair * HEAD_DIM_A, pair * V_ROWS

    def block_diag(q):
        if pair == 1:
            return q
        nh, d = q.shape[0], q.shape[3]
        eye = jnp.eye(pair, dtype=q.dtype)
        q6 = q.reshape(nh, bp, pair, t, 1, d) * eye.reshape(1, 1, pair, 1, pair, 1)
        return q6.reshape(nh, bp, pair * t, pair * d)

    wi_op = wi.reshape(bp, pair, t, N_IDX_HEADS).transpose(0, 3, 1, 2).reshape(bp, N_IDX_HEADS, pair * t)
    ki_op = ki_all.reshape(bp, pair, s_pad, IDX_DIM).transpose(0, 2, 1, 3).reshape(bp, s_pad, cdim)
    k_op = (k_all.reshape(bp, pair, s_pad, N_KV_A, HEAD_DIM_A).transpose(0, 2, 3, 1, 4)
            .reshape(bp, s_pad, N_KV_A * cdim))
    vt = v_all.reshape(b, s_pad, N_KV_A, HEAD_DIM_A).transpose(0, 2, 3, 1)
    vt = jnp.concatenate([vt, jnp.ones((b, N_KV_A, V_ROWS - HEAD_DIM_A, s_pad), vt.dtype)], axis=2)
    vt_op = (vt.reshape(bp, pair, N_KV_A, V_ROWS, s_pad).transpose(0, 2, 1, 3, 4)
             .reshape(bp, N_KV_A * v_rows, s_pad))

    body = functools.partial(_dsa_body, s_pad=s_pad, s_real=s_real, q_rows=q_rows, pair=pair, pos0=pos0,
                             topk=topk)
    o_t = pl.pallas_call(
        body,
        grid=(bp, t // q_rows),
        in_specs=[
            pl.BlockSpec((N_HEADS_A, 1, lanes, cdim), lambda i, j: (0, i, j, 0)),
            pl.BlockSpec((N_IDX_HEADS, 1, lanes, cdim), lambda i, j: (0, i, j, 0)),
            pl.BlockSpec((1, N_IDX_HEADS, lanes), lambda i, j: (i, 0, j)),
            pl.BlockSpec((1, s_pad, cdim), lambda i, j: (i, 0, 0)),
            pl.BlockSpec((1, s_pad, N_KV_A * cdim), lambda i, j: (i, 0, 0)),
            pl.BlockSpec((1, N_KV_A * v_rows, s_pad), lambda i, j: (i, 0, 0)),
        ],
        out_specs=pl.BlockSpec((1, WIDTH_A, lanes), lambda i, j: (i, 0, j)),
        out_shape=jax.ShapeDtypeStruct((bp, WIDTH_A, pair * t), BF16),
        scratch_shapes=[pltpu.VMEM((s_pad, lanes), F32), pltpu.VMEM((s_pad, lanes), F32),
                        pltpu.VMEM((N_HEADS_A, s_pad, lanes), F32), pltpu.VMEM((N_HEADS_A, v_rows, lanes), F32)],
        compiler_params=_cparams("parallel", "parallel"),
        name="dsa",
    )(block_diag(qa_h), block_diag(qi_h), wi_op, ki_op, k_op, vt_op)
    return o_t.reshape(bp, WIDTH_A, pair, t).transpose(0, 2, 3, 1).reshape(b * t, WIDTH_A)


def _split3(a):
    hi = a.astype(BF16)
    r1 = a - hi.astype(F32)
    mid = r1.astype(BF16)
    lo = (r1 - mid.astype(F32)).astype(BF16)
    return hi, mid, lo


def _hgrn_body(hb_ref, s0_ref, lb_ref, nw_ref, ob_ref, sfin_ref, state_scr, *, tb):
    t = pl.program_id(1)

    @pl.when(t == 0)
    def _():
        for h in range(N_HEADS_B):
            state_scr[h] = s0_ref[0, h].T

    lb = lb_ref[...]
    r = lax.broadcasted_iota(I32, (CHUNK, CHUNK), 0)
    c = lax.broadcasted_iota(I32, (CHUNK, CHUNK), 1)
    causal = r >= c
    tril = jnp.where(causal, 1.0, 0.0).astype(BF16)

    for ci in range(tb // CHUNK):
        rows = slice(ci * CHUNK, (ci + 1) * CHUNK)
        f = lb + (1.0 - lb) * jax.nn.sigmoid(hb_ref[0, rows, WIDTH_B:2 * WIDTH_B])
        parts = _split3(jnp.log(f))
        bcum = _dot(tril, parts[0]) + _dot(tril, parts[1]) + _dot(tril, parts[2])
        for h in range(N_HEADS_B):
            lanes = slice(h * HEAD_DIM_B, (h + 1) * HEAD_DIM_B)
            q = hb_ref[0, rows, lanes]
            k = 1.0 - f[:, lanes]
            v = hb_ref[0, rows, 2 * WIDTH_B + h * HEAD_DIM_B:2 * WIDTH_B + (h + 1) * HEAD_DIM_B]
            gate = hb_ref[0, rows, 3 * WIDTH_B + h * HEAD_DIM_B:3 * WIDTH_B + (h + 1) * HEAD_DIM_B]
            bh = bcum[:, lanes]
            b_last = bh[CHUNK - 1:CHUNK, :]
            vb = v.astype(BF16)
            state_t = state_scr[h]
            o_inter = _dot_nt((q * jnp.exp(bh)).astype(BF16), state_t.astype(BF16))
            a_rows = []
            for i in range(CHUNK // HGRN_SUB):
                lo, hi = i * HGRN_SUB, (i + 1) * HGRN_SUB
                ref = bh[lo - 1:lo, :] if i else jnp.zeros((1, HEAD_DIM_B), F32)
                qs = (q[lo:hi] * jnp.exp(bh[lo:hi] - ref)).astype(BF16)
                ks = (k * jnp.exp(ref - bh)).astype(BF16)
                a_rows.append(_dot_nt(qs, ks))
            a = jnp.where(causal, jnp.concatenate(a_rows, axis=0), 0.0)
            o = o_inter + _dot(a.astype(BF16), vb)
            kdec = (k * jnp.exp(b_last - bh)).astype(BF16)
            state_scr[h] = jnp.exp(b_last) * state_t + _dot_tn(vb, kdec)
            ms = jnp.mean(o * o, axis=-1, keepdims=True)
            y = o * lax.rsqrt(ms + RMS_EPS) * nw_ref[...] * (gate * jax.nn.sigmoid(gate))
            ob_ref[0, rows, lanes] = y.astype(BF16)

    @pl.when(t == pl.num_programs(1) - 1)
    def _():
        for h in range(N_HEADS_B):
            sfin_ref[0, h] = state_scr[h].T


def _hgrn(hb, s0, lb, nw):
    b, t, _ = hb.shape
    tb = min(256, t)
    return pl.pallas_call(
        functools.partial(_hgrn_body, tb=tb),
        grid=(b, t // tb),
        in_specs=[
            pl.BlockSpec((1, tb, 4 * WIDTH_B), lambda i, j: (i, j, 0)),
            pl.BlockSpec((1, N_HEADS_B, HEAD_DIM_B, HEAD_DIM_B), lambda i, j: (i, 0, 0, 0)),
            pl.BlockSpec((1, WIDTH_B), lambda i, j: (0, 0)),
            pl.BlockSpec((1, HEAD_DIM_B), lambda i, j: (0, 0)),
        ],
        out_specs=[
            pl.BlockSpec((1, tb, WIDTH_B), lambda i, j: (i, j, 0)),
            pl.BlockSpec((1, N_HEADS_B, HEAD_DIM_B, HEAD_DIM_B), lambda i, j: (i, 0, 0, 0)),
        ],
        out_shape=[jax.ShapeDtypeStruct((b, t, WIDTH_B), BF16),
                   jax.ShapeDtypeStruct((b, N_HEADS_B, HEAD_DIM_B, HEAD_DIM_B), F32)],
        scratch_shapes=[pltpu.VMEM((N_HEADS_B, HEAD_DIM_B, HEAD_DIM_B), F32)],
        compiler_params=_cparams("parallel", "arbitrary"),
        name="hgrn",
    )(hb, s0, lb, nw)


def _outproj_body(oa_ref, ob_ref, x_ref, w_ref, lnw_ref, lnb_ref, wrh_ref, wrl_ref, br_ref, x1_ref, lg_ref):
    y = _dot(oa_ref[...], w_ref[:WIDTH_A, :]) + _dot(ob_ref[...], w_ref[WIDTH_A:, :])
    x1 = _layer_norm(DN_ALPHA * x_ref[...] + y, lnw_ref[...], lnb_ref[...])
    x1_ref[...] = x1
    hi = x1.astype(BF16)
    lo = (x1 - hi.astype(F32)).astype(BF16)
    lg_ref[...] = (_dot_nt(wrh_ref[...], hi) + _dot_nt(wrh_ref[...], lo) + _dot_nt(wrl_ref[...], hi)
                   + br_ref[...])


def _outproj(oa, ob, x2d, w_out, lnw, lnb, wr_hi, wr_lo, br):
    n = x2d.shape[0]
    tm = min(512, n)
    row = lambda width: pl.BlockSpec((tm, width), lambda i: (i, 0))
    full = lambda a: pl.BlockSpec(a.shape, lambda i: (0,) * a.ndim)
    return pl.pallas_call(
        _outproj_body,
        grid=(n // tm,),
        in_specs=[row(WIDTH_A), row(WIDTH_B), row(D_MODEL), full(w_out), full(lnw), full(lnb),
                  full(wr_hi), full(wr_lo), full(br)],
        out_specs=[row(D_MODEL), pl.BlockSpec((N_EXPERTS, tm), lambda i: (0, i))],
        out_shape=[jax.ShapeDtypeStruct((n, D_MODEL), F32), jax.ShapeDtypeStruct((N_EXPERTS, n), F32)],
        compiler_params=_cparams("parallel"),
        name="outproj",
    )(oa, ob, x2d, w_out, lnw, lnb, wr_hi, wr_lo, br)


def _route_body(lg_ref, lpos_ref, gate_ref, tc_ref, base_ref, cnt_ref, cnt_scr, *, tr):
    @pl.when(pl.program_id(0) == 0)
    def _():
        cnt_scr[...] = jnp.zeros_like(cnt_scr)

    l = lg_ref[...]
    rows = lax.broadcasted_iota(I32, (N_EXPERTS, tr), 0)
    vals, hots = [], []
    for j in range(TOP_K):
        m = jnp.max(l, axis=0, keepdims=True)
        idx = jnp.min(jnp.where(l == m, rows, N_EXPERTS), axis=0, keepdims=True)
        hot = rows == idx
        vals.append(m)
        hots.append(hot)
        l = jnp.where(hot, NEG_INF, l)
    es = [jnp.exp(v - vals[0]) for v in vals]
    inv = 1.0 / (es[0] + es[1] + es[2] + es[3])
    for j in range(TOP_K):
        gate_ref[j:j + 1, :] = es[j] * inv
    chosen = jnp.zeros((N_EXPERTS, tr), F32)
    for hot in hots:
        chosen = chosen + jnp.where(hot, 1.0, 0.0)
    r = lax.broadcasted_iota(I32, (tr, tr), 0)
    c = lax.broadcasted_iota(I32, (tr, tr), 1)
    before = jnp.where(r < c, 1.0, 0.0).astype(BF16)
    prior = _dot(chosen.astype(BF16), before)
    units = jnp.ceil(jnp.sum(chosen, axis=1, keepdims=True) * (1.0 / ROW_ALIGN))
    er = lax.broadcasted_iota(I32, (N_EXPERTS, N_EXPERTS), 0)
    ec = lax.broadcasted_iota(I32, (N_EXPERTS, N_EXPERTS), 1)
    earlier = jnp.where(ec < er, 1.0, 0.0).astype(BF16)
    run_len = jnp.broadcast_to(units, (N_EXPERTS, 128)) * ROW_ALIGN
    run_start = _dot(earlier, jnp.broadcast_to(units, (N_EXPERTS, 128)).astype(BF16)) * ROW_ALIGN
    where_in_tile = prior + run_start[:, 0:1]
    for j in range(TOP_K):
        lpos_ref[j:j + 1, :] = jnp.sum(jnp.where(hots[j], where_in_tile, 0.0), axis=0,
                                       keepdims=True).astype(I32)
    tc_ref[0] = run_len.astype(I32)
    base_ref[0] = cnt_scr[...].astype(I32)
    cnt_scr[...] = cnt_scr[...] + run_len
    cnt_ref[...] = cnt_scr[...].astype(I32)


def _route(logits_t):
    n = logits_t.shape[1]
    tr = MOE_TILE
    tiles = n // tr
    tok = lambda rows: pl.BlockSpec((rows, tr), lambda i: (0, i))
    per_tile = pl.BlockSpec((1, N_EXPERTS, 128), lambda i: (i, 0, 0))
    return pl.pallas_call(
        functools.partial(_route_body, tr=tr),
        grid=(tiles,),
        in_specs=[tok(N_EXPERTS)],
        out_specs=[tok(TOP_K), tok(TOP_K), per_tile, per_tile, pl.BlockSpec((N_EXPERTS, 128), lambda i: (0, 0))],
        out_shape=[jax.ShapeDtypeStruct((TOP_K, n), I32), jax.ShapeDtypeStruct((TOP_K, n), F32),
                   jax.ShapeDtypeStruct((tiles, N_EXPERTS, 128), I32),
                   jax.ShapeDtypeStruct((tiles, N_EXPERTS, 128), I32),
                   jax.ShapeDtypeStruct((N_EXPERTS, 128), I32)],
        scratch_shapes=[pltpu.VMEM((N_EXPERTS, 128), F32)],
        compiler_params=_cparams("arbitrary"),
        name="route",
    )(logits_t)


def _pack_pairs(a):
    half = a.shape[1] // 2
    hi = lax.bitcast_convert_type(a[:, :half], I32)
    lo = lax.shift_right_logical(lax.bitcast_convert_type(a[:, half:], I32), 16)
    return hi | lo


def _unpack_pairs(pk):
    hi = lax.bitcast_convert_type(pk & jnp.int32(-65536), F32).astype(BF16)
    lo = lax.bitcast_convert_type(pk << 16, F32).astype(BF16)
    return hi, lo


def _for_each_run_piece(tc_ref, g0_ref, t, fn):
    def per_expert(e, local):
        c = tc_ref[t, e]
        glob = g0_ref[t, e]
        lo = local
        for p in RUN_PIECES:
            take = (c & p) != 0

            @pl.when(take)
            def _():
                fn(pl.multiple_of(lo, ROW_ALIGN), pl.multiple_of(glob, ROW_ALIGN), p)

            step = jnp.where(take, p, 0)
            lo = lo + step
            glob = glob + step
        return local + c

    lax.fori_loop(0, N_EXPERTS, per_expert, 0)


def _dispatch_body(tc_ref, g0_ref, seg_ref, lpos_ref, x_ref, buf_ref, stage, zero_scr, zsem, sems, *, n_blocks):
    i = pl.program_id(0)
    slot = i % 2

    def run_copy(slot_, lo, glob, rows):
        return pltpu.make_async_copy(stage.at[slot_, pl.ds(lo, rows), :], buf_ref.at[pl.ds(glob, rows), :],
                                     sems.at[slot_])

    @pl.when(i == 0)
    def _():
        zero_scr[...] = jnp.zeros_like(zero_scr)
        n_used = seg_ref[1, N_EXPERTS - 1] // EXPERT_ROWS

        def block_copy(start):
            return pltpu.make_async_copy(
                zero_scr, buf_ref.at[pl.ds(pl.multiple_of(start, EXPERT_ROWS), EXPERT_ROWS), :], zsem)

        def clears(action):
            for e in range(N_EXPERTS):
                @pl.when(seg_ref[1, e] > seg_ref[0, e])
                def _():
                    action(block_copy(seg_ref[1, e] - EXPERT_ROWS))

                @pl.when(n_used + e < n_blocks)
                def _():
                    action(block_copy((n_used + e) * EXPERT_ROWS))

        clears(lambda cp: cp.start())
        clears(lambda cp: cp.wait())

    r = lax.broadcasted_iota(I32, (MOE_SLOTS, MOE_TILE), 0)
    onehot = jnp.zeros((MOE_SLOTS, MOE_TILE), F32)
    for j in range(TOP_K):
        onehot = jnp.where(r == lpos_ref[j:j + 1, :], 1.0, onehot)
    stage[slot] = _pack_pairs(_dot(onehot.astype(BF16), x_ref[...].astype(BF16)))

    @pl.when(i > 0)
    def _():
        _for_each_run_piece(tc_ref, g0_ref, i - 1,
                            lambda lo, glob, rows: run_copy(1 - slot, lo, glob, rows).wait())

    _for_each_run_piece(tc_ref, g0_ref, i, lambda lo, glob, rows: run_copy(slot, lo, glob, rows).start())

    @pl.when(i == pl.num_programs(0) - 1)
    def _():
        _for_each_run_piece(tc_ref, g0_ref, i, lambda lo, glob, rows: run_copy(slot, lo, glob, rows).wait())


def _dispatch(x1, lpos_t, tc, g0, seg, n_rows):
    n = x1.shape[0]
    return pl.pallas_call(
        functools.partial(_dispatch_body, n_blocks=n_rows // EXPERT_ROWS),
        grid_spec=pltpu.PrefetchScalarGridSpec(
            num_scalar_prefetch=3,
            grid=(n // MOE_TILE,),
            in_specs=[pl.BlockSpec((TOP_K, MOE_TILE), lambda i, *_: (0, i)),
                      pl.BlockSpec((MOE_TILE, D_MODEL), lambda i, *_: (i, 0))],
            out_specs=pl.BlockSpec(memory_space=pl.ANY),
            scratch_shapes=[pltpu.VMEM((2, MOE_SLOTS, D_MODEL // 2), I32),
                            pltpu.VMEM((EXPERT_ROWS, D_MODEL // 2), I32),
                            pltpu.SemaphoreType.DMA(()), pltpu.SemaphoreType.DMA((2,))],
        ),
        out_shape=jax.ShapeDtypeStruct((n_rows, D_MODEL // 2), I32),
        compiler_params=_cparams("arbitrary"),
        name="dispatch",
    )(tc, g0, seg, lpos_t, x1)


def _experts_body(be_ref, nu_ref, x_ref, wgu_ref, bgu_ref, wd_ref, bd_ref, o_ref, wgu_bf, wd_bf):
    i = pl.program_id(0)

    @pl.when(jnp.logical_or(i == 0, be_ref[i] != be_ref[jnp.maximum(i - 1, 0)]))
    def _():
        for r in range(0, D_MODEL, 256):
            wgu_bf[r:r + 256, :] = wgu_ref[0, r:r + 256, :].astype(BF16)
            wd_bf[r:r + 256, :] = wd_ref[0, r:r + 256, :].astype(BF16)

    @pl.when(i < nu_ref[0])
    def _():
        x_lo, x_hi = _unpack_pairs(x_ref[...])
        half = D_MODEL // 2
        gu = _dot(x_lo, wgu_bf[:half, :]) + _dot(x_hi, wgu_bf[half:, :]) + bgu_ref[0]
        gate = jnp.minimum(gu[:, :D_EXPERT], SWIGLU_LIMIT)
        up = jnp.clip(gu[:, D_EXPERT:], -SWIGLU_LIMIT, SWIGLU_LIMIT)
        hdn = (up + 1.0) * gate * jax.nn.sigmoid(SWIGLU_ALPHA * gate)
        out = _dot(hdn.astype(BF16), wd_bf[...]) + bd_ref[0]
        o_ref[...] = _pack_pairs(out.astype(BF16).astype(F32))

    @pl.when(i >= nu_ref[0])
    def _():
        o_ref[...] = jnp.zeros_like(o_ref)


def _experts(buf, block_e, n_used, w_gu, b_gu, w_down, b_down):
    n_blocks = buf.shape[0] // EXPERT_ROWS
    rows = pl.BlockSpec((EXPERT_ROWS, D_MODEL // 2), lambda i, be, nu: (jnp.minimum(i, nu[0] - 1), 0))
    per_e = lambda a: pl.BlockSpec((1,) + a.shape[1:], lambda i, be, nu: (be[i],) + (0,) * (a.ndim - 1))
    return pl.pallas_call(
        _experts_body,
        grid_spec=pltpu.PrefetchScalarGridSpec(
            num_scalar_prefetch=2,
            grid=(n_blocks,),
            in_specs=[rows, per_e(w_gu), per_e(b_gu), per_e(w_down), per_e(b_down)],
            out_specs=pl.BlockSpec((EXPERT_ROWS, D_MODEL // 2), lambda i, be, nu: (i, 0)),
            scratch_shapes=[pltpu.VMEM((D_MODEL, 2 * D_EXPERT), BF16), pltpu.VMEM((D_EXPERT, D_MODEL), BF16)],
        ),
        out_shape=jax.ShapeDtypeStruct(buf.shape, I32),
        compiler_params=_cparams("arbitrary"),
        name="experts",
    )(block_e, n_used, buf, w_gu, b_gu, w_down, b_down)


def _final_body(tc_ref, g0_ref, lpos_ref, gate_ref, x1_ref, p_ref, lnw_ref, lnb_ref, wg_ref, bg_ref, wp_ref,
                eo_ref, y_ref, stage, sems):
    i = pl.program_id(0)
    slot = i % 2

    def run_copy(slot_, lo, glob, rows):
        return pltpu.make_async_copy(eo_ref.at[pl.ds(glob, rows), :], stage.at[slot_, pl.ds(lo, rows), :],
                                     sems.at[slot_])

    def fetch(tile, slot_):
        stage[slot_] = jnp.zeros(stage.shape[1:], stage.dtype)
        _for_each_run_piece(tc_ref, g0_ref, tile, lambda lo, glob, rows: run_copy(slot_, lo, glob, rows).start())

    @pl.when(i == 0)
    def _():
        fetch(0, 0)

    @pl.when(i + 1 < pl.num_programs(0))
    def _():
        fetch(i + 1, 1 - slot)

    _for_each_run_piece(tc_ref, g0_ref, i, lambda lo, glob, rows: run_copy(slot, lo, glob, rows).wait())

    r = lax.broadcasted_iota(I32, (MOE_TILE, MOE_SLOTS), 1)
    g = jnp.zeros((MOE_TILE, MOE_SLOTS), F32)
    for j in range(TOP_K):
        g = jnp.where(r == lpos_ref[:, j:j + 1], gate_ref[:, j:j + 1], g)
    g_hi = g.astype(BF16)
    g_lo = (g - g_hi.astype(F32)).astype(BF16)
    e_a, e_b = _unpack_pairs(stage[slot])
    moe = jnp.concatenate([_dot(g_hi, e_a) + _dot(g_lo, e_a), _dot(g_hi, e_b) + _dot(g_lo, e_b)], axis=1)
    x2 = _layer_norm(DN_ALPHA * x1_ref[...] + moe, lnw_ref[...], lnb_ref[...])
    ple_gate = jax.nn.sigmoid(_dot(x2.astype(BF16), wg_ref[...]) + bg_ref[...])
    y_ref[...] = x2 + ple_gate * _dot(p_ref[...].astype(BF16), wp_ref[...])


def _final(tc, g0, lpos, gate, x1, p2d, lnw, lnb, w_gate, b_gate, w_ple, expert_out):
    n = x1.shape[0]
    row = lambda width: pl.BlockSpec((MOE_TILE, width), lambda i, *_: (i, 0))
    full = lambda a: pl.BlockSpec(a.shape, lambda i, *_: (0,) * a.ndim)
    return pl.pallas_call(
        _final_body,
        grid_spec=pltpu.PrefetchScalarGridSpec(
            num_scalar_prefetch=2,
            grid=(n // MOE_TILE,),
            in_specs=[row(TOP_K), row(TOP_K), row(D_MODEL), row(D_PLE), full(lnw), full(lnb), full(w_gate),
                      full(b_gate), full(w_ple), pl.BlockSpec(memory_space=pl.ANY)],
            out_specs=row(D_MODEL),
            scratch_shapes=[pltpu.VMEM((2, MOE_SLOTS, D_MODEL // 2), I32), pltpu.SemaphoreType.DMA((2,))],
        ),
        out_shape=jax.ShapeDtypeStruct((n, D_MODEL), F32),
        compiler_params=_cparams("arbitrary"),
        name="combine_final",
    )(tc, g0, lpos, gate, x1, p2d, lnw, lnb, w_gate, b_gate, w_ple, expert_out)


def _round_up(a, m):
    return -(-a // m) * m


def _encoder_layer(x, p, hist_k, hist_v, hist_kidx, s0, pos0, wts):
    b, t, _ = x.shape
    n = b * t
    x2d = x.reshape(n, D_MODEL)

    qa, ka, va, qi, ki, wi, hb = _inproj(x2d, wts["w_in"], wts["idx_lnw"], wts["idx_lnb"])

    k_new = ka.reshape(b, t, N_KV_A * HEAD_DIM_A)
    v_new = va.reshape(b, t, N_KV_A * HEAD_DIM_A)
    ki_new = ki.reshape(b, t, IDX_DIM)
    if hist_k is None:
        k_all, v_all, ki_all = k_new, v_new, ki_new
    else:
        past = hist_k.shape[1]
        k_all = jnp.concatenate([hist_k.reshape(b, past, -1), k_new], axis=1)
        v_all = jnp.concatenate([hist_v.reshape(b, past, -1), v_new], axis=1)
        ki_all = jnp.concatenate([hist_kidx, ki_new], axis=1)
    s_real = k_all.shape[1]
    topk = min(TOPK_MAX, s_real // 4)
    pad = ((0, 0), (0, _round_up(s_real, KEY_BLOCK) - s_real), (0, 0))
    oa = _dsa(qa.reshape(N_HEADS_A, b, t, HEAD_DIM_A), qi.reshape(N_IDX_HEADS, b, t, IDX_DIM),
              wi.reshape(b, t, N_IDX_HEADS), jnp.pad(ki_all.astype(BF16), pad), jnp.pad(k_all.astype(BF16), pad),
              jnp.pad(v_all.astype(BF16), pad), s_real=s_real, pos0=pos0, topk=topk)

    ob, s_fin = _hgrn(hb.reshape(b, t, 4 * WIDTH_B), s0, wts["lb"], wts["hgrn_nw"])

    x1, logits_t = _outproj(oa, ob.reshape(n, WIDTH_B), x2d, wts["w_out"], wts["ln1_w"], wts["ln1_b"],
                            wts["wr_hi"], wts["wr_lo"], wts["b_router"])

    assert n % MOE_TILE == 0
    tiles = n // MOE_TILE
    lpos_t, gate_t, tile_runs, tile_base, counts = _route(logits_t)
    tc, counts = tile_runs[:, :, 0], counts[:, 0]
    padded = (counts + EXPERT_ROWS - 1) // EXPERT_ROWS * EXPERT_ROWS
    seg_end = jnp.cumsum(padded)
    seg_start = seg_end - padded
    g0 = (seg_start[None, :] + tile_base[:, :, 0]).astype(I32)
    n_blocks = -(-(n * TOP_K + N_EXPERTS * tiles * (ROW_ALIGN - 1) + N_EXPERTS * (EXPERT_ROWS - 1))
                 // EXPERT_ROWS)
    block_row0 = jnp.arange(n_blocks, dtype=I32) * EXPERT_ROWS
    block_e = jnp.minimum(jnp.sum((seg_end[None, :] <= block_row0[:, None]).astype(I32), axis=1), N_EXPERTS - 1)
    n_used = (seg_end[-1:] // EXPERT_ROWS).astype(I32)
    seg = jnp.stack([seg_start, seg_end]).astype(I32)
    buf = _dispatch(x1, lpos_t, tc, g0, seg, n_blocks * EXPERT_ROWS)
    expert_out = _experts(buf, block_e, n_used, wts["w_gu"], wts["b_gu"], wts["w_down"], wts["b_down"])
    y = _final(tc, g0, lpos_t.T, gate_t.T, x1, p.reshape(n, D_PLE), wts["ln2_w"], wts["ln2_b"],
               wts["w_ple_gate"], wts["b_ple_gate"], wts["w_ple"], expert_out)

    return (y.reshape(b, t, D_MODEL), ka.reshape(b, t, N_KV_A, HEAD_DIM_A), va.reshape(b, t, N_KV_A, HEAD_DIM_A),
            ki_new, s_fin)


def _prep_weights(w_in, w_out, idx_k_norm_w, idx_k_norm_b, lb, hgrn_norm_w, ln1_w, ln1_b, w_router, b_router,
                  w_gu, b_gu, w_down, b_down, ln2_w, ln2_b, w_ple, w_ple_gate, b_ple_gate):
    n_a = COL_KW + IDX_DIM + N_IDX_HEADS
    w_pad = jnp.concatenate(
        [w_in[:, :n_a], jnp.zeros((D_MODEL, COL_HB - n_a), w_in.dtype), w_in[:, n_a:]], axis=1).astype(BF16)
    wr_t = w_router.T
    wr_hi = wr_t.astype(BF16)
    row = lambda a: a.reshape(1, -1)
    return dict(
        w_in=w_pad, idx_lnw=row(idx_k_norm_w), idx_lnb=row(idx_k_norm_b), lb=row(lb), hgrn_nw=row(hgrn_norm_w),
        w_out=w_out.astype(BF16), ln1_w=row(ln1_w), ln1_b=row(ln1_b),
        wr_hi=wr_hi, wr_lo=(wr_t - wr_hi.astype(F32)).astype(BF16), b_router=b_router.reshape(N_EXPERTS, 1),
        w_gu=w_gu, b_gu=b_gu.reshape(N_EXPERTS, 1, 2 * D_EXPERT),
        w_down=w_down, b_down=b_down.reshape(N_EXPERTS, 1, D_MODEL),
        ln2_w=row(ln2_w), ln2_b=row(ln2_b), w_ple=w_ple.astype(BF16), w_ple_gate=w_ple_gate.astype(BF16),
        b_ple_gate=row(b_ple_gate))


def kernel(x_prompt, x_sample, cache_k, cache_v, cache_kidx, state_hgrn, p_prompt, p_sample, w_in, w_out,
           idx_k_norm_w, idx_k_norm_b, hgrn_lb_logits, hgrn_norm_w, ln1_w, ln1_b, w_router, b_router, w_gu, b_gu,
           w_down, b_down, ln2_w, ln2_b, w_ple, w_ple_gate, b_ple_gate):
    lb_all = jnp.cumsum(jax.nn.softmax(hgrn_lb_logits.astype(F32), axis=0), axis=0)
    xp, xs = x_prompt, x_sample
    outs = [[] for _ in range(8)]
    for i in range(DEPTH):
        wts = _prep_weights(w_in[i], w_out[i], idx_k_norm_w[i], idx_k_norm_b[i], lb_all[i], hgrn_norm_w[i],
                            ln1_w[i], ln1_b[i], w_router[i], b_router[i], w_gu[i], b_gu[i], w_down[i], b_down[i],
                            ln2_w[i], ln2_b[i], w_ple[i], w_ple_gate[i], b_ple_gate[i])
        s0p = jnp.zeros((xp.shape[0], N_HEADS_B, HEAD_DIM_B, HEAD_DIM_B), F32)
        xp, kp, vp, kip, sp = _encoder_layer(xp, p_prompt[i], None, None, None, s0p, 0, wts)
        xs, kn, vn, kin, sn = _encoder_layer(xs, p_sample[i], cache_k[i], cache_v[i], cache_kidx[i],
                                             state_hgrn[i], cache_k.shape[2], wts)
        for lst, val in zip(outs, (kp, vp, kip, sp, kn, vn, kin, sn)):
            lst.append(val)
    return (xp, xs) + tuple(jnp.stack(l) for l in outs)
```

```python
import functools

import jax
import jax.numpy as jnp
from jax import lax
from jax.experimental import pallas as pl
from jax.experimental.pallas import tpu as pltpu

F32 = jnp.float32
BF16 = jnp.bfloat16
I32 = jnp.int32

D_MODEL = 1024
CHUNK = 64
CHUNK_SHIFT = 6
WIDTH_A = 512
HEAD_DIM_A = 64
N_HEADS_A = 8
N_KV_A = 2
N_IDX_HEADS = 4
IDX_DIM = 64
IDX_SCALE = IDX_DIM ** -0.5 * N_IDX_HEADS ** -0.5
TOPK_MAX = 256
WIDTH_B = 512
HEAD_DIM_B = 128
N_HEADS_B = 4
N_EXPERTS = 32
TOP_K = 4
D_EXPERT = 1024
SWIGLU_LIMIT = 7.0
SWIGLU_ALPHA = 1.702
D_PLE = 256
LN_EPS = 1e-5
RMS_EPS = 1e-6
DEPTH = 1
DN_ALPHA = (2 * DEPTH) ** 0.25

COL_QA, COL_KA, COL_VA, COL_QI, COL_KW, COL_HB, COL_END = 0, 512, 640, 768, 1024, 1152, 3200

VMEM_LIMIT = 56 * 1024 * 1024
KEY_BLOCK = 256
V_ROWS = 80
LOG2_E = 1.4426950408889634
HGRN_SUB = 16
EXPERT_ROWS = 512
MOE_TILE = 512
ROW_ALIGN = 8
RUN_PIECES = (512, 256, 128, 64, 32, 16, 8)
MOE_SLOTS = -(-(TOP_K * MOE_TILE + N_EXPERTS * (ROW_ALIGN - 1)) // 256) * 256
NEG_INF = float("-inf")
INT_MIN = -(2 ** 31)


def _cparams(*sem):
    return pltpu.CompilerParams(dimension_semantics=sem, vmem_limit_bytes=VMEM_LIMIT)


def _dot(a, b):
    return jnp.dot(a, b, preferred_element_type=F32)


def _dot_nt(a, b):
    return lax.dot_general(a, b, (((1,), (1,)), ((), ())), preferred_element_type=F32)


def _dot_tn(a, b):
    return lax.dot_general(a, b, (((0,), (0,)), ((), ())), preferred_element_type=F32)


def _layer_norm(z, w, b):
    mu = jnp.mean(z, axis=-1, keepdims=True)
    d = z - mu
    var = jnp.mean(d * d, axis=-1, keepdims=True)
    return d * lax.rsqrt(var + LN_EPS) * w + b


def _inproj_body(x_ref, w_ref, lnw_ref, lnb_ref, qa_ref, ka_ref, va_ref, qi_ref, ki_ref, wi_ref, hb_ref):
    xb = x_ref[...].astype(BF16)

    def mm(c0, c1):
        return _dot(xb, w_ref[:, c0:c1])

    qa = (mm(COL_QA, COL_KA) * (HEAD_DIM_A ** -0.5 * LOG2_E)).astype(BF16)
    for h in range(N_HEADS_A):
        qa_ref[h] = qa[:, h * HEAD_DIM_A:(h + 1) * HEAD_DIM_A]
    ka_ref[...] = mm(COL_KA, COL_VA)
    va_ref[...] = mm(COL_VA, COL_QI)
    qi = mm(COL_QI, COL_KW).astype(BF16)
    for h in range(N_IDX_HEADS):
        qi_ref[h] = qi[:, h * IDX_DIM:(h + 1) * IDX_DIM]
    kw = mm(COL_KW, COL_HB)
    ki_ref[...] = _layer_norm(kw[:, :IDX_DIM], lnw_ref[...], lnb_ref[...])
    wi_ref[...] = kw[:, IDX_DIM:IDX_DIM + N_IDX_HEADS]
    hb_ref[...] = mm(COL_HB, COL_END)


def _inproj(x2d, w_pad, lnw, lnb):
    n = x2d.shape[0]
    tm = min(512, n)
    row = lambda width: pl.BlockSpec((tm, width), lambda i: (i, 0))
    full = lambda a: pl.BlockSpec(a.shape, lambda i: (0,) * a.ndim)
    heads = lambda nh, d: pl.BlockSpec((nh, tm, d), lambda i: (0, i, 0))
    sds = jax.ShapeDtypeStruct
    return pl.pallas_call(
        _inproj_body,
        grid=(n // tm,),
        in_specs=[row(D_MODEL), full(w_pad), full(lnw), full(lnb)],
        out_specs=[heads(N_HEADS_A, HEAD_DIM_A), row(128), row(128), heads(N_IDX_HEADS, IDX_DIM),
                   row(IDX_DIM), row(N_IDX_HEADS), row(4 * WIDTH_B)],
        out_shape=[sds((N_HEADS_A, n, HEAD_DIM_A), BF16), sds((n, 128), F32), sds((n, 128), F32),
                   sds((N_IDX_HEADS, n, IDX_DIM), BF16), sds((n, IDX_DIM), F32), sds((n, N_IDX_HEADS), F32),
                   sds((n, 4 * WIDTH_B), F32)],
        compiler_params=_cparams("parallel"),
        name="inproj",
    )(x2d, w_pad, lnw, lnb)


def _order_bits_to_f32(u):
    key = u ^ INT_MIN
    bits = key ^ ((key >> 31) & 0x7FFFFFFF)
    f = lax.bitcast_convert_type(bits, F32)
    return jnp.where(u >= 0, jnp.where(u <= 0x007FFFFF, NEG_INF, f), f)


def _dsa_body(qa_ref, qi_ref, wi_ref, kidx_ref, k_ref, vt_ref, o_ref, sc_scr, bias_scr, term_scr, s_scr, acc_scr,
              *, s_pad, s_real, q_rows, pair, pos0, topk):
    kb_rows = KEY_BLOCK
    heads_per_kv = N_HEADS_A // N_KV_A
    qb = pair * q_rows
    cdim = pair * HEAD_DIM_A
    v_rows = pair * V_ROWS
    j = pl.program_id(1)
    q_lo = pos0 + j * q_rows
    lane = lax.broadcasted_iota(I32, (1, qb), 1)
    q_chunk = (q_lo + (lane & (q_rows - 1))) >> CHUNK_SHIFT
    k_lim = (((q_lo + q_rows - 1) >> CHUNK_SHIFT) + 1) * CHUNK
    nkb = jnp.minimum(s_pad // kb_rows, (k_lim + kb_rows - 1) // kb_rows)

    def rows_of(kb):
        return pl.ds(pl.multiple_of(kb * kb_rows, kb_rows), kb_rows)

    def fold8(a):
        return a.reshape(kb_rows // 8, 8, qb)

    def score_blk(kb, _):
        rows = rows_of(kb)
        kidx = kidx_ref[0, rows, :]
        for h in range(N_IDX_HEADS):
            term_scr[h] = jnp.maximum(_dot_nt(kidx, qi_ref[h, 0]), 0.0) * wi_ref[0, h:h + 1, :]
        total = (term_scr[0] + term_scr[1]) + (term_scr[2] + term_scr[3])
        spos = kb * kb_rows + lax.broadcasted_iota(I32, (kb_rows, qb), 0)
        sc = jnp.where((spos >> CHUNK_SHIFT) <= q_chunk, total * IDX_SCALE + 0.0, NEG_INF)
        if s_real < s_pad:
            sc = jnp.where(spos < s_real, sc, NEG_INF)
        sc_scr[rows, :] = sc
        return 0

    lax.fori_loop(0, nkb, score_blk, 0)

    def count(thr, strict):
        def body(kb, acc):
            blk = sc_scr[rows_of(kb), :]
            hit = (blk > thr) if strict else (blk >= thr)
            return acc + jnp.sum(fold8(jnp.where(hit, 1, 0).astype(I32)), axis=0)
        acc = lax.fori_loop(0, nkb, body, jnp.zeros((8, qb), I32))
        return jnp.sum(acc, axis=0, keepdims=True)

    def bit_step(i, carry):
        prefix, cnt = carry
        cand = prefix | lax.shift_left(jnp.int32(1), 31 - i)
        c = count(_order_bits_to_f32(cand), False)
        take = c >= topk
        return jnp.where(take, cand, prefix), jnp.where(take, c, cnt)

    prefix, cnt_ge = lax.fori_loop(0, 32, bit_step,
                                   (jnp.zeros((1, qb), I32), jnp.full((1, qb), nkb * kb_rows, I32)))
    tau = _order_bits_to_f32(prefix)
    finite_tau = tau > NEG_INF
    tau_floor = jnp.maximum(tau, jnp.finfo(F32).min)
    tie_lanes = jnp.where(finite_tau, jnp.where(cnt_ge > topk, 1, 0), 0)
    has_ties = jnp.max(tie_lanes) > 0

    @pl.when(jnp.logical_not(has_ties))
    def _():
        def body(kb, _):
            blk = sc_scr[rows_of(kb), :]
            bias_scr[rows_of(kb), :] = jnp.where(blk >= tau_floor, 0.0, NEG_INF)
            return 0
        lax.fori_loop(0, nkb, body, 0)

    @pl.when(has_ties)
    def _():
        need = jnp.where(finite_tau, (topk - count(tau, True)).astype(F32), 0.0)
        r = lax.broadcasted_iota(I32, (kb_rows, kb_rows), 0)
        c = lax.broadcasted_iota(I32, (kb_rows, kb_rows), 1)
        tril = jnp.where(r >= c, 1.0, 0.0).astype(BF16)

        def body(kb, seen):
            blk = sc_scr[rows_of(kb), :]
            eq = blk == tau
            rank = _dot(tril, jnp.where(eq, 1.0, 0.0).astype(BF16)) + seen
            tie_bias = jnp.where(eq, jnp.where(rank <= need, 0.0, NEG_INF), NEG_INF)
            bias_scr[rows_of(kb), :] = jnp.where(blk > tau, 0.0, tie_bias)
            return rank[kb_rows - 1:kb_rows, :]
        lax.fori_loop(0, nkb, body, jnp.zeros((1, qb), F32))

    def pass1(kb, m8):
        rows = rows_of(kb)
        bias = bias_scr[rows, :]
        out = []
        for h in range(N_HEADS_A):
            g = h // heads_per_kv
            s = _dot_nt(k_ref[0, rows, g * cdim:(g + 1) * cdim], qa_ref[h, 0]) + bias
            s_scr[h, rows, :] = s
            out.append(jnp.maximum(m8[h], jnp.max(fold8(s), axis=0)))
        return tuple(out)

    m8 = lax.fori_loop(0, nkb, pass1, tuple(jnp.full((8, qb), NEG_INF, F32) for _ in range(N_HEADS_A)))
    m = [jnp.max(x, axis=0, keepdims=True) for x in m8]

    acc_scr[...] = jnp.zeros_like(acc_scr)

    def pass2(kb, _):
        rows = rows_of(kb)
        for h in range(N_HEADS_A):
            g = h // heads_per_kv
            p = jnp.exp2(s_scr[h, rows, :] - m[h]).astype(BF16)
            acc_scr[h] += _dot(vt_ref[0, g * v_rows:(g + 1) * v_rows, rows], p)
        return 0

    lax.fori_loop(0, nkb, pass2, 0)
    for h in range(N_HEADS_A):
        a = acc_scr[h]
        num, den = a[:HEAD_DIM_A], a[HEAD_DIM_A:HEAD_DIM_A + 1]
        for mb in range(1, pair):
            mine = lane >= mb * q_rows
            r0 = mb * V_ROWS
            num = jnp.where(mine, a[r0:r0 + HEAD_DIM_A], num)
            den = jnp.where(mine, a[r0 + HEAD_DIM_A:r0 + HEAD_DIM_A + 1], den)
        o_ref[0, h * HEAD_DIM_A:(h + 1) * HEAD_DIM_A, :] = (num * (1.0 / den)).astype(BF16)


def _dsa(qa_h, qi_h, wi, ki_all, k_all, v_all, *, s_real, pos0, topk):
    _, b, t, _ = qa_h.shape
    s_pad = ki_all.shape[1]
    q_rows = min(256, t)
    pair = 2 if (q_rows * 2 <= 128 and q_rows == t and b % 2 == 0) else 1
    bp = b // pair
    lanes, cdim, v_rows = pair * q_rows, pair * HEAD_DIM_A, pair * V_ROWS

    def block_diag(q):
        if pair == 1:
            return q
        nh, d = q.shape[0], q.shape[3]
        eye = jnp.eye(pair, dtype=q.dtype)
        q6 = q.reshape(nh, bp, pair, t, 1, d) * eye.reshape(1, 1, pair, 1, pair, 1)
        return q6.reshape(nh, bp, pair * t, pair * d)

    wi_op = wi.reshape(bp, pair, t, N_IDX_HEADS).transpose(0, 3, 1, 2).reshape(bp, N_IDX_HEADS, pair * t)
    ki_op = ki_all.reshape(bp, pair, s_pad, IDX_DIM).transpose(0, 2, 1, 3).reshape(bp, s_pad, cdim)
    k_op = (k_all.reshape(bp, pair, s_pad, N_KV_A, HEAD_DIM_A).transpose(0, 2, 3, 1, 4)
            .reshape(bp, s_pad, N_KV_A * cdim))
    vt = v_all.reshape(b, s_pad, N_KV_A, HEAD_DIM_A).transpose(0, 2, 3, 1)
    vt = jnp.concatenate([vt, jnp.ones((b, N_KV_A, V_ROWS - HEAD_DIM_A, s_pad), vt.dtype)], axis=2)
    vt_op = (vt.reshape(bp, pair, N_KV_A, V_ROWS, s_pad).transpose(0, 2, 1, 3, 4)
             .reshape(bp, N_KV_A * v_rows, s_pad))

    body = functools.partial(_dsa_body, s_pad=s_pad, s_real=s_real, q_rows=q_rows, pair=pair, pos0=pos0,
                             topk=topk)
    o_t = pl.pallas_call(
        body,
        grid=(bp, t // q_rows),
        in_specs=[
            pl.BlockSpec((N_HEADS_A, 1, lanes, cdim), lambda i, j: (0, i, j, 0)),
            pl.BlockSpec((N_IDX_HEADS, 1, lanes, cdim), lambda i, j: (0, i, j, 0)),
            pl.BlockSpec((1, N_IDX_HEADS, lanes), lambda i, j: (i, 0, j)),
            pl.BlockSpec((1, s_pad, cdim), lambda i, j: (i, 0, 0)),
            pl.BlockSpec((1, s_pad, N_KV_A * cdim), lambda i, j: (i, 0, 0)),
            pl.BlockSpec((1, N_KV_A * v_rows, s_pad), lambda i, j: (i, 0, 0)),
        ],
        out_specs=pl.BlockSpec((1, WIDTH_A, lanes), lambda i, j: (i, 0, j)),
        out_shape=jax.ShapeDtypeStruct((bp, WIDTH_A, pair * t), BF16),
        scratch_shapes=[pltpu.VMEM((s_pad, lanes), F32), pltpu.VMEM((s_pad, lanes), F32),
                        pltpu.VMEM((N_IDX_HEADS, KEY_BLOCK, lanes), F32),
                        pltpu.VMEM((N_HEADS_A, s_pad, lanes), F32), pltpu.VMEM((N_HEADS_A, v_rows, lanes), F32)],
        compiler_params=_cparams("parallel", "parallel"),
        name="dsa",
    )(block_diag(qa_h), block_diag(qi_h), wi_op, ki_op, k_op, vt_op)
    return o_t.reshape(bp, WIDTH_A, pair, t).transpose(0, 2, 3, 1).reshape(b * t, WIDTH_A)


def _split3(a):
    hi = a.astype(BF16)
    r1 = a - hi.astype(F32)
    mid = r1.astype(BF16)
    lo = (r1 - mid.astype(F32)).astype(BF16)
    return hi, mid, lo


def _hgrn_body(hb_ref, s0_ref, lb_ref, nw_ref, ob_ref, sfin_ref, state_scr, *, tb):
    t = pl.program_id(1)

    @pl.when(t == 0)
    def _():
        for h in range(N_HEADS_B):
            state_scr[h] = s0_ref[0, h].T

    lb = lb_ref[...]
    r = lax.broadcasted_iota(I32, (CHUNK, CHUNK), 0)
    c = lax.broadcasted_iota(I32, (CHUNK, CHUNK), 1)
    causal = r >= c
    tril = jnp.where(causal, 1.0, 0.0).astype(BF16)

    for ci in range(tb // CHUNK):
        rows = slice(ci * CHUNK, (ci + 1) * CHUNK)
        f = lb + (1.0 - lb) * jax.nn.sigmoid(hb_ref[0, rows, WIDTH_B:2 * WIDTH_B])
        parts = _split3(jnp.log(f))
        bcum = _dot(tril, parts[0]) + _dot(tril, parts[1]) + _dot(tril, parts[2])
        for h in range(N_HEADS_B):
            lanes = slice(h * HEAD_DIM_B, (h + 1) * HEAD_DIM_B)
            q = hb_ref[0, rows, lanes]
            k = 1.0 - f[:, lanes]
            v = hb_ref[0, rows, 2 * WIDTH_B + h * HEAD_DIM_B:2 * WIDTH_B + (h + 1) * HEAD_DIM_B]
            gate = hb_ref[0, rows, 3 * WIDTH_B + h * HEAD_DIM_B:3 * WIDTH_B + (h + 1) * HEAD_DIM_B]
            bh = bcum[:, lanes]
            b_last = bh[CHUNK - 1:CHUNK, :]
            vb = v.astype(BF16)
            state_t = state_scr[h]
            o_inter = _dot_nt((q * jnp.exp(bh)).astype(BF16), state_t.astype(BF16))
            a_rows = []
            for i in range(CHUNK // HGRN_SUB):
                lo, hi = i * HGRN_SUB, (i + 1) * HGRN_SUB
                ref = bh[lo - 1:lo, :] if i else jnp.zeros((1, HEAD_DIM_B), F32)
                qs = (q[lo:hi] * jnp.exp(bh[lo:hi] - ref)).astype(BF16)
                ks = (k * jnp.exp(ref - bh)).astype(BF16)
                a_rows.append(_dot_nt(qs, ks))
            a = jnp.where(causal, jnp.concatenate(a_rows, axis=0), 0.0)
            o = o_inter + _dot(a.astype(BF16), vb)
            kdec = (k * jnp.exp(b_last - bh)).astype(BF16)
            state_scr[h] = jnp.exp(b_last) * state_t + _dot_tn(vb, kdec)
            ms = jnp.mean(o * o, axis=-1, keepdims=True)
            y = o * lax.rsqrt(ms + RMS_EPS) * nw_ref[...] * (gate * jax.nn.sigmoid(gate))
            ob_ref[0, rows, lanes] = y.astype(BF16)

    @pl.when(t == pl.num_programs(1) - 1)
    def _():
        for h in range(N_HEADS_B):
            sfin_ref[0, h] = state_scr[h].T


def _hgrn(hb, s0, lb, nw):
    b, t, _ = hb.shape
    tb = min(256, t)
    return pl.pallas_call(
        functools.partial(_hgrn_body, tb=tb),
        grid=(b, t // tb),
        in_specs=[
            pl.BlockSpec((1, tb, 4 * WIDTH_B), lambda i, j: (i, j, 0)),
            pl.BlockSpec((1, N_HEADS_B, HEAD_DIM_B, HEAD_DIM_B), lambda i, j: (i, 0, 0, 0)),
            pl.BlockSpec((1, WIDTH_B), lambda i, j: (0, 0)),
            pl.BlockSpec((1, HEAD_DIM_B), lambda i, j: (0, 0)),
        ],
        out_specs=[
            pl.BlockSpec((1, tb, WIDTH_B), lambda i, j: (i, j, 0)),
            pl.BlockSpec((1, N_HEADS_B, HEAD_DIM_B, HEAD_DIM_B), lambda i, j: (i, 0, 0, 0)),
        ],
        out_shape=[jax.ShapeDtypeStruct((b, t, WIDTH_B), BF16),
                   jax.ShapeDtypeStruct((b, N_HEADS_B, HEAD_DIM_B, HEAD_DIM_B), F32)],
        scratch_shapes=[pltpu.VMEM((N_HEADS_B, HEAD_DIM_B, HEAD_DIM_B), F32)],
        compiler_params=_cparams("parallel", "arbitrary"),
        name="hgrn",
    )(hb, s0, lb, nw)


def _outproj_body(oa_ref, ob_ref, x_ref, w_ref, lnw_ref, lnb_ref, wrh_ref, wrl_ref, br_ref, x1_ref, lg_ref):
    y = _dot(oa_ref[...], w_ref[:WIDTH_A, :]) + _dot(ob_ref[...], w_ref[WIDTH_A:, :])
    x1 = _layer_norm(DN_ALPHA * x_ref[...] + y, lnw_ref[...], lnb_ref[...])
    x1_ref[...] = x1
    hi = x1.astype(BF16)
    lo = (x1 - hi.astype(F32)).astype(BF16)
    lg_ref[...] = (_dot_nt(wrh_ref[...], hi) + _dot_nt(wrh_ref[...], lo) + _dot_nt(wrl_ref[...], hi)
                   + br_ref[...])


def _outproj(oa, ob, x2d, w_out, lnw, lnb, wr_hi, wr_lo, br):
    n = x2d.shape[0]
    tm = min(512, n)
    row = lambda width: pl.BlockSpec((tm, width), lambda i: (i, 0))
    full = lambda a: pl.BlockSpec(a.shape, lambda i: (0,) * a.ndim)
    return pl.pallas_call(
        _outproj_body,
        grid=(n // tm,),
        in_specs=[row(WIDTH_A), row(WIDTH_B), row(D_MODEL), full(w_out), full(lnw), full(lnb),
                  full(wr_hi), full(wr_lo), full(br)],
        out_specs=[row(D_MODEL), pl.BlockSpec((N_EXPERTS, tm), lambda i: (0, i))],
        out_shape=[jax.ShapeDtypeStruct((n, D_MODEL), F32), jax.ShapeDtypeStruct((N_EXPERTS, n), F32)],
        compiler_params=_cparams("parallel"),
        name="outproj",
    )(oa, ob, x2d, w_out, lnw, lnb, wr_hi, wr_lo, br)


def _route_body(lg_ref, lpos_ref, gate_ref, tc_ref, base_ref, cnt_ref, cnt_scr, *, tr):
    @pl.when(pl.program_id(0) == 0)
    def _():
        cnt_scr[...] = jnp.zeros_like(cnt_scr)

    l = lg_ref[...]
    rows = lax.broadcasted_iota(I32, (N_EXPERTS, tr), 0)
    vals, hots = [], []
    for j in range(TOP_K):
        m = jnp.max(l, axis=0, keepdims=True)
        idx = jnp.min(jnp.where(l == m, rows, N_EXPERTS), axis=0, keepdims=True)
        hot = rows == idx
        vals.append(m)
        hots.append(hot)
        l = jnp.where(hot, NEG_INF, l)
    es = [jnp.exp(v - vals[0]) for v in vals]
    inv = 1.0 / (es[0] + es[1] + es[2] + es[3])
    for j in range(TOP_K):
        gate_ref[j:j + 1, :] = es[j] * inv
    chosen = jnp.zeros((N_EXPERTS, tr), F32)
    for hot in hots:
        chosen = chosen + jnp.where(hot, 1.0, 0.0)
    r = lax.broadcasted_iota(I32, (tr, tr), 0)
    c = lax.broadcasted_iota(I32, (tr, tr), 1)
    before = jnp.where(r < c, 1.0, 0.0).astype(BF16)
    prior = _dot(chosen.astype(BF16), before)
    units = jnp.ceil(jnp.sum(chosen, axis=1, keepdims=True) * (1.0 / ROW_ALIGN))
    er = lax.broadcasted_iota(I32, (N_EXPERTS, N_EXPERTS), 0)
    ec = lax.broadcasted_iota(I32, (N_EXPERTS, N_EXPERTS), 1)
    earlier = jnp.where(ec < er, 1.0, 0.0).astype(BF16)
    run_len = jnp.broadcast_to(units, (N_EXPERTS, 128)) * ROW_ALIGN
    run_start = _dot(earlier, jnp.broadcast_to(units, (N_EXPERTS, 128)).astype(BF16)) * ROW_ALIGN
    where_in_tile = prior + run_start[:, 0:1]
    for j in range(TOP_K):
        lpos_ref[j:j + 1, :] = jnp.sum(jnp.where(hots[j], where_in_tile, 0.0), axis=0,
                                       keepdims=True).astype(I32)
    tc_ref[0] = run_len.astype(I32)
    base_ref[0] = cnt_scr[...].astype(I32)
    cnt_scr[...] = cnt_scr[...] + run_len
    cnt_ref[...] = cnt_scr[...].astype(I32)


def _route(logits_t):
    n = logits_t.shape[1]
    tr = MOE_TILE
    tiles = n // tr
    tok = lambda rows: pl.BlockSpec((rows, tr), lambda i: (0, i))
    per_tile = pl.BlockSpec((1, N_EXPERTS, 128), lambda i: (i, 0, 0))
    return pl.pallas_call(
        functools.partial(_route_body, tr=tr),
        grid=(tiles,),
        in_specs=[tok(N_EXPERTS)],
        out_specs=[tok(TOP_K), tok(TOP_K), per_tile, per_tile, pl.BlockSpec((N_EXPERTS, 128), lambda i: (0, 0))],
        out_shape=[jax.ShapeDtypeStruct((TOP_K, n), I32), jax.ShapeDtypeStruct((TOP_K, n), F32),
                   jax.ShapeDtypeStruct((tiles, N_EXPERTS, 128), I32),
                   jax.ShapeDtypeStruct((tiles, N_EXPERTS, 128), I32),
                   jax.ShapeDtypeStruct((N_EXPERTS, 128), I32)],
        scratch_shapes=[pltpu.VMEM((N_EXPERTS, 128), F32)],
        compiler_params=_cparams("arbitrary"),
        name="route",
    )(logits_t)


def _pack_pairs(a):
    half = a.shape[1] // 2
    hi = lax.bitcast_convert_type(a[:, :half], I32)
    lo = lax.shift_right_logical(lax.bitcast_convert_type(a[:, half:], I32), 16)
    return hi | lo


def _unpack_pairs(pk):
    hi = lax.bitcast_convert_type(pk & jnp.int32(-65536), F32).astype(BF16)
    lo = lax.bitcast_convert_type(pk << 16, F32).astype(BF16)
    return hi, lo


def _for_each_run_piece(tc_ref, g0_ref, t, fn):
    def per_expert(e, local):
        c = tc_ref[t, e]
        glob = g0_ref[t, e]
        lo = local
        for p in RUN_PIECES:
            take = (c & p) != 0

            @pl.when(take)
            def _():
                fn(pl.multiple_of(lo, ROW_ALIGN), pl.multiple_of(glob, ROW_ALIGN), p)

            step = jnp.where(take, p, 0)
            lo = lo + step
            glob = glob + step
        return local + c

    lax.fori_loop(0, N_EXPERTS, per_expert, 0)


def _dispatch_body(tc_ref, g0_ref, seg_ref, lpos_ref, x_ref, buf_ref, stage, zero_scr, zsem, sems, *, n_blocks):
    i = pl.program_id(0)
    slot = i % 2

    def run_copy(slot_, lo, glob, rows):
        return pltpu.make_async_copy(stage.at[slot_, pl.ds(lo, rows), :], buf_ref.at[pl.ds(glob, rows), :],
                                     sems.at[slot_])

    @pl.when(i == 0)
    def _():
        zero_scr[...] = jnp.zeros_like(zero_scr)
        n_used = seg_ref[1, N_EXPERTS - 1] // EXPERT_ROWS

        def block_copy(start):
            return pltpu.make_async_copy(
                zero_scr, buf_ref.at[pl.ds(pl.multiple_of(start, EXPERT_ROWS), EXPERT_ROWS), :], zsem)

        def clears(action):
            for e in range(N_EXPERTS):
                @pl.when(seg_ref[1, e] > seg_ref[0, e])
                def _():
                    action(block_copy(seg_ref[1, e] - EXPERT_ROWS))

                @pl.when(n_used + e < n_blocks)
                def _():
                    action(block_copy((n_used + e) * EXPERT_ROWS))

        clears(lambda cp: cp.start())
        clears(lambda cp: cp.wait())

    r = lax.broadcasted_iota(I32, (MOE_SLOTS, MOE_TILE), 0)
    onehot = jnp.zeros((MOE_SLOTS, MOE_TILE), F32)
    for j in range(TOP_K):
        onehot = jnp.where(r == lpos_ref[j:j + 1, :], 1.0, onehot)
    stage[slot] = _pack_pairs(_dot(onehot.astype(BF16), x_ref[...].astype(BF16)))

    @pl.when(i > 0)
    def _():
        _for_each_run_piece(tc_ref, g0_ref, i - 1,
                            lambda lo, glob, rows: run_copy(1 - slot, lo, glob, rows).wait())

    _for_each_run_piece(tc_ref, g0_ref, i, lambda lo, glob, rows: run_copy(slot, lo, glob, rows).start())

    @pl.when(i == pl.num_programs(0) - 1)
    def _():
        _for_each_run_piece(tc_ref, g0_ref, i, lambda lo, glob, rows: run_copy(slot, lo, glob, rows).wait())


def _dispatch(x1, lpos_t, tc, g0, seg, n_rows):
    n = x1.shape[0]
    return pl.pallas_call(
        functools.partial(_dispatch_body, n_blocks=n_rows // EXPERT_ROWS),
        grid_spec=pltpu.PrefetchScalarGridSpec(
            num_scalar_prefetch=3,
            grid=(n // MOE_TILE,),
            in_specs=[pl.BlockSpec((TOP_K, MOE_TILE), lambda i, *_: (0, i)),
                      pl.BlockSpec((MOE_TILE, D_MODEL), lambda i, *_: (i, 0))],
            out_specs=pl.BlockSpec(memory_space=pl.ANY),
            scratch_shapes=[pltpu.VMEM((2, MOE_SLOTS, D_MODEL // 2), I32),
                            pltpu.VMEM((EXPERT_ROWS, D_MODEL // 2), I32),
                            pltpu.SemaphoreType.DMA(()), pltpu.SemaphoreType.DMA((2,))],
        ),
        out_shape=jax.ShapeDtypeStruct((n_rows, D_MODEL // 2), I32),
        compiler_params=_cparams("arbitrary"),
        name="dispatch",
    )(tc, g0, seg, lpos_t, x1)


def _experts_body(be_ref, nu_ref, x_ref, wgu_ref, bgu_ref, wd_ref, bd_ref, o_ref, wgu_bf, wd_bf):
    i = pl.program_id(0)

    @pl.when(jnp.logical_or(i == 0, be_ref[i] != be_ref[jnp.maximum(i - 1, 0)]))
    def _():
        for r in range(0, D_MODEL, 256):
            wgu_bf[r:r + 256, :] = wgu_ref[0, r:r + 256, :].astype(BF16)
            wd_bf[r:r + 256, :] = wd_ref[0, r:r + 256, :].astype(BF16)

    @pl.when(i < nu_ref[0])
    def _():
        half = D_MODEL // 2
        x_a, x_b = _unpack_pairs(x_ref[...])
        gu = _dot(x_a, wgu_bf[:half, :]) + _dot(x_b, wgu_bf[half:, :]) + bgu_ref[0]
        gate = jnp.minimum(gu[:, :D_EXPERT], SWIGLU_LIMIT)
        up = jnp.clip(gu[:, D_EXPERT:], -SWIGLU_LIMIT, SWIGLU_LIMIT)
        hdn = (up + 1.0) * gate * jax.nn.sigmoid(SWIGLU_ALPHA * gate)
        out = _dot(hdn.astype(BF16), wd_bf[...]) + bd_ref[0]
        o_ref[...] = _pack_pairs(out.astype(BF16).astype(F32))

    @pl.when(i >= nu_ref[0])
    def _():
        o_ref[...] = jnp.zeros_like(o_ref)


def _experts(buf, block_e, n_used, w_gu, b_gu, w_down, b_down):
    n_blocks = buf.shape[0] // EXPERT_ROWS
    rows = pl.BlockSpec((EXPERT_ROWS, D_MODEL // 2), lambda i, be, nu: (jnp.minimum(i, nu[0] - 1), 0))
    per_e = lambda a: pl.BlockSpec((1,) + a.shape[1:], lambda i, be, nu: (be[i],) + (0,) * (a.ndim - 1))
    return pl.pallas_call(
        _experts_body,
        grid_spec=pltpu.PrefetchScalarGridSpec(
            num_scalar_prefetch=2,
            grid=(n_blocks,),
            in_specs=[rows, per_e(w_gu), per_e(b_gu), per_e(w_down), per_e(b_down)],
            out_specs=pl.BlockSpec((EXPERT_ROWS, D_MODEL // 2), lambda i, be, nu: (i, 0)),
            scratch_shapes=[pltpu.VMEM((D_MODEL, 2 * D_EXPERT), BF16), pltpu.VMEM((D_EXPERT, D_MODEL), BF16)],
        ),
        out_shape=jax.ShapeDtypeStruct(buf.shape, I32),
        compiler_params=_cparams("arbitrary"),
        name="experts",
    )(block_e, n_used, buf, w_gu, b_gu, w_down, b_down)


def _final_body(tc_ref, g0_ref, lpos_ref, gate_ref, x1_ref, p_ref, lnw_ref, lnb_ref, wg_ref, bg_ref, wp_ref,
                eo_ref, y_ref, stage, sems):
    i = pl.program_id(0)
    slot = i % 2

    def run_copy(slot_, lo, glob, rows):
        return pltpu.make_async_copy(eo_ref.at[pl.ds(glob, rows), :], stage.at[slot_, pl.ds(lo, rows), :],
                                     sems.at[slot_])

    def fetch(tile, slot_):
        stage[slot_] = jnp.zeros(stage.shape[1:], stage.dtype)
        _for_each_run_piece(tc_ref, g0_ref, tile, lambda lo, glob, rows: run_copy(slot_, lo, glob, rows).start())

    @pl.when(i == 0)
    def _():
        fetch(0, 0)

    @pl.when(i + 1 < pl.num_programs(0))
    def _():
        fetch(i + 1, 1 - slot)

    _for_each_run_piece(tc_ref, g0_ref, i, lambda lo, glob, rows: run_copy(slot, lo, glob, rows).wait())

    r = lax.broadcasted_iota(I32, (MOE_TILE, MOE_SLOTS), 1)
    g = jnp.zeros((MOE_TILE, MOE_SLOTS), F32)
    for j in range(TOP_K):
        g = jnp.where(r == lpos_ref[:, j:j + 1], gate_ref[:, j:j + 1], g)
    g_hi = g.astype(BF16)
    g_lo = (g - g_hi.astype(F32)).astype(BF16)
    g2 = jnp.concatenate([g_hi, g_lo], axis=0)
    e_a, e_b = _unpack_pairs(stage[slot])
    r_a, r_b = _dot(g2, e_a), _dot(g2, e_b)
    moe = jnp.concatenate([r_a[:MOE_TILE] + r_a[MOE_TILE:], r_b[:MOE_TILE] + r_b[MOE_TILE:]], axis=1)
    x2 = _layer_norm(DN_ALPHA * x1_ref[...] + moe, lnw_ref[...], lnb_ref[...])
    ple_gate = jax.nn.sigmoid(_dot(x2.astype(BF16), wg_ref[...]) + bg_ref[...])
    y_ref[...] = x2 + ple_gate * _dot(p_ref[...].astype(BF16), wp_ref[...])


def _final(tc, g0, lpos, gate, x1, p2d, lnw, lnb, w_gate, b_gate, w_ple, expert_out):
    n = x1.shape[0]
    row = lambda width: pl.BlockSpec((MOE_TILE, width), lambda i, *_: (i, 0))
    full = lambda a: pl.BlockSpec(a.shape, lambda i, *_: (0,) * a.ndim)
    return pl.pallas_call(
        _final_body,
        grid_spec=pltpu.PrefetchScalarGridSpec(
            num_scalar_prefetch=2,
            grid=(n // MOE_TILE,),
            in_specs=[row(TOP_K), row(TOP_K), row(D_MODEL), row(D_PLE), full(lnw), full(lnb), full(w_gate),
                      full(b_gate), full(w_ple), pl.BlockSpec(memory_space=pl.ANY)],
            out_specs=row(D_MODEL),
            scratch_shapes=[pltpu.VMEM((2, MOE_SLOTS, D_MODEL // 2), I32), pltpu.SemaphoreType.DMA((2,))],
        ),
        out_shape=jax.ShapeDtypeStruct((n, D_MODEL), F32),
        compiler_params=_cparams("arbitrary"),
        name="combine_final",
    )(tc, g0, lpos, gate, x1, p2d, lnw, lnb, w_gate, b_gate, w_ple, expert_out)


def _round_up(a, m):
    return -(-a // m) * m


def _encoder_layer(x, p, hist_k, hist_v, hist_kidx, s0, pos0, wts):
    b, t, _ = x.shape
    n = b * t
    x2d = x.reshape(n, D_MODEL)

    qa, ka, va, qi, ki, wi, hb = _inproj(x2d, wts["w_in"], wts["idx_lnw"], wts["idx_lnb"])

    k_new = ka.reshape(b, t, N_KV_A * HEAD_DIM_A)
    v_new = va.reshape(b, t, N_KV_A * HEAD_DIM_A)
    ki_new = ki.reshape(b, t, IDX_DIM)
    if hist_k is None:
        k_all, v_all, ki_all = k_new, v_new, ki_new
    else:
        past = hist_k.shape[1]
        k_all = jnp.concatenate([hist_k.reshape(b, past, -1), k_new], axis=1)
        v_all = jnp.concatenate([hist_v.reshape(b, past, -1), v_new], axis=1)
        ki_all = jnp.concatenate([hist_kidx, ki_new], axis=1)
    s_real = k_all.shape[1]
    topk = min(TOPK_MAX, s_real // 4)
    pad = ((0, 0), (0, _round_up(s_real, KEY_BLOCK) - s_real), (0, 0))
    oa = _dsa(qa.reshape(N_HEADS_A, b, t, HEAD_DIM_A), qi.reshape(N_IDX_HEADS, b, t, IDX_DIM),
              wi.reshape(b, t, N_IDX_HEADS), jnp.pad(ki_all.astype(BF16), pad), jnp.pad(k_all.astype(BF16), pad),
              jnp.pad(v_all.astype(BF16), pad), s_real=s_real, pos0=pos0, topk=topk)

    ob, s_fin = _hgrn(hb.reshape(b, t, 4 * WIDTH_B), s0, wts["lb"], wts["hgrn_nw"])

    x1, logits_t = _outproj(oa, ob.reshape(n, WIDTH_B), x2d, wts["w_out"], wts["ln1_w"], wts["ln1_b"],
                            wts["wr_hi"], wts["wr_lo"], wts["b_router"])

    assert n % MOE_TILE == 0
    tiles = n // MOE_TILE
    lpos_t, gate_t, tile_runs, tile_base, counts = _route(logits_t)
    tc, counts = tile_runs[:, :, 0], counts[:, 0]
    padded = (counts + EXPERT_ROWS - 1) // EXPERT_ROWS * EXPERT_ROWS
    seg_end = jnp.cumsum(padded)
    seg_start = seg_end - padded
    g0 = (seg_start[None, :] + tile_base[:, :, 0]).astype(I32)
    n_blocks = -(-(n * TOP_K + N_EXPERTS * tiles * (ROW_ALIGN - 1) + N_EXPERTS * (EXPERT_ROWS - 1))
                 // EXPERT_ROWS)
    block_row0 = jnp.arange(n_blocks, dtype=I32) * EXPERT_ROWS
    block_e = jnp.minimum(jnp.sum((seg_end[None, :] <= block_row0[:, None]).astype(I32), axis=1), N_EXPERTS - 1)
    n_used = (seg_end[-1:] // EXPERT_ROWS).astype(I32)
    seg = jnp.stack([seg_start, seg_end]).astype(I32)
    buf = _dispatch(x1, lpos_t, tc, g0, seg, n_blocks * EXPERT_ROWS)
    expert_out = _experts(buf, block_e, n_used, wts["w_gu"], wts["b_gu"], wts["w_down"], wts["b_down"])
    y = _final(tc, g0, lpos_t.T, gate_t.T, x1, p.reshape(n, D_PLE), wts["ln2_w"], wts["ln2_b"],
               wts["w_ple_gate"], wts["b_ple_gate"], wts["w_ple"], expert_out)

    return (y.reshape(b, t, D_MODEL), ka.reshape(b, t, N_KV_A, HEAD_DIM_A), va.reshape(b, t, N_KV_A, HEAD_DIM_A),
            ki_new, s_fin)


def _prep_weights(w_in, w_out, idx_k_norm_w, idx_k_norm_b, lb, hgrn_norm_w, ln1_w, ln1_b, w_router, b_router,
                  w_gu, b_gu, w_down, b_down, ln2_w, ln2_b, w_ple, w_ple_gate, b_ple_gate):
    n_a = COL_KW + IDX_DIM + N_IDX_HEADS
    w_pad = jnp.concatenate(
        [w_in[:, :n_a], jnp.zeros((D_MODEL, COL_HB - n_a), w_in.dtype), w_in[:, n_a:]], axis=1).astype(BF16)
    wr_t = w_router.T
    wr_hi = wr_t.astype(BF16)
    row = lambda a: a.reshape(1, -1)
    return dict(
        w_in=w_pad, idx_lnw=row(idx_k_norm_w), idx_lnb=row(idx_k_norm_b), lb=row(lb), hgrn_nw=row(hgrn_norm_w),
        w_out=w_out.astype(BF16), ln1_w=row(ln1_w), ln1_b=row(ln1_b),
        wr_hi=wr_hi, wr_lo=(wr_t - wr_hi.astype(F32)).astype(BF16), b_router=b_router.reshape(N_EXPERTS, 1),
        w_gu=w_gu, b_gu=b_gu.reshape(N_EXPERTS, 1, 2 * D_EXPERT),
        w_down=w_down, b_down=b_down.reshape(N_EXPERTS, 1, D_MODEL),
        ln2_w=row(ln2_w), ln2_b=row(ln2_b), w_ple=w_ple.astype(BF16), w_ple_gate=w_ple_gate.astype(BF16),
        b_ple_gate=row(b_ple_gate))


def kernel(x_prompt, x_sample, cache_k, cache_v, cache_kidx, state_hgrn, p_prompt, p_sample, w_in, w_out,
           idx_k_norm_w, idx_k_norm_b, hgrn_lb_logits, hgrn_norm_w, ln1_w, ln1_b, w_router, b_router, w_gu, b_gu,
           w_down, b_down, ln2_w, ln2_b, w_ple, w_ple_gate, b_ple_gate):
    lb_all = jnp.cumsum(jax.nn.softmax(hgrn_lb_logits.astype(F32), axis=0), axis=0)
    xp, xs = x_prompt, x_sample
    outs = [[] for _ in range(8)]
    for i in range(DEPTH):
        wts = _prep_weights(w_in[i], w_out[i], idx_k_norm_w[i], idx_k_norm_b[i], lb_all[i], hgrn_norm_w[i],
                            ln1_w[i], ln1_b[i], w_router[i], b_router[i], w_gu[i], b_gu[i], w_down[i], b_down[i],
                            ln2_w[i], ln2_b[i], w_ple[i], w_ple_gate[i], b_ple_gate[i])
        s0p = jnp.zeros((xp.shape[0], N_HEADS_B, HEAD_DIM_B, HEAD_DIM_B), F32)
        xp, kp, vp, kip, sp = _encoder_layer(xp, p_prompt[i], None, None, None, s0p, 0, wts)
        xs, kn, vn, kin, sn = _encoder_layer(xs, p_sample[i], cache_k[i], cache_v[i], cache_kidx[i],
                                             state_hgrn[i], cache_k.shape[2], wts)
        for lst, val in zip(outs, (kp, vp, kip, sp, kn, vn, kin, sn)):
            lst.append(val)
    return (xp, xs) + tuple(jnp.stack(l) for l in outs)
```

```python
import functools
import itertools

import jax
import jax.numpy as jnp
from jax import lax
from jax.experimental import pallas as pl
from jax.experimental.pallas import tpu as pltpu

F32 = jnp.float32
BF16 = jnp.bfloat16
I32 = jnp.int32

D_MODEL = 1024
CHUNK = 64
CHUNK_SHIFT = 6
WIDTH_A = 512
HEAD_DIM_A = 64
N_HEADS_A = 8
N_KV_A = 2
N_IDX_HEADS = 4
IDX_DIM = 64
IDX_SCALE = IDX_DIM ** -0.5 * N_IDX_HEADS ** -0.5
TOPK_MAX = 256
WIDTH_B = 512
HEAD_DIM_B = 128
N_HEADS_B = 4
N_EXPERTS = 32
TOP_K = 4
D_EXPERT = 1024
SWIGLU_LIMIT = 7.0
SWIGLU_ALPHA = 1.702
D_PLE = 256
LN_EPS = 1e-5
RMS_EPS = 1e-6
DEPTH = 1
DN_ALPHA = (2 * DEPTH) ** 0.25

COL_QA, COL_KA, COL_VA, COL_QI, COL_KW, COL_HB, COL_END = 0, 512, 640, 768, 1024, 1152, 3200

VMEM_LIMIT = 56 * 1024 * 1024
KEY_BLOCK = 256
V_ROWS = 80
LOG2_E = 1.4426950408889634
HGRN_SUB = 16
EXPERT_ROWS = 512
MOE_TILE = 512
ROW_ALIGN = 8
RUN_PIECES = (512, 256, 128, 64, 32, 16, 8)
MOE_SLOTS = -(-(TOP_K * MOE_TILE + N_EXPERTS * (ROW_ALIGN - 1)) // 256) * 256
NEG_INF = float("-inf")
INT_MIN = -(2 ** 31)


def _cparams(*sem):
    return pltpu.CompilerParams(dimension_semantics=sem, vmem_limit_bytes=VMEM_LIMIT)


def _dot(a, b):
    return jnp.dot(a, b, preferred_element_type=F32)


def _dot_nt(a, b):
    return lax.dot_general(a, b, (((1,), (1,)), ((), ())), preferred_element_type=F32)


def _dot_tn(a, b):
    return lax.dot_general(a, b, (((0,), (0,)), ((), ())), preferred_element_type=F32)


def _layer_norm(z, w, b):
    mu = jnp.mean(z, axis=-1, keepdims=True)
    d = z - mu
    var = jnp.mean(d * d, axis=-1, keepdims=True)
    return d * lax.rsqrt(var + LN_EPS) * w + b


def _inproj_body(x_ref, w_ref, lnw_ref, lnb_ref, qa_ref, ka_ref, va_ref, qi_ref, ki_ref, wi_ref, hb_ref):
    xb = x_ref[...].astype(BF16)

    def mm(c0, c1):
        return _dot(xb, w_ref[:, c0:c1])

    qa = (mm(COL_QA, COL_KA) * (HEAD_DIM_A ** -0.5 * LOG2_E)).astype(BF16)
    for h in range(N_HEADS_A):
        qa_ref[h] = qa[:, h * HEAD_DIM_A:(h + 1) * HEAD_DIM_A]
    ka_ref[...] = mm(COL_KA, COL_VA)
    va_ref[...] = mm(COL_VA, COL_QI)
    qi = mm(COL_QI, COL_KW).astype(BF16)
    for h in range(N_IDX_HEADS):
        qi_ref[h] = qi[:, h * IDX_DIM:(h + 1) * IDX_DIM]
    kw = mm(COL_KW, COL_HB)
    ki_ref[...] = _layer_norm(kw[:, :IDX_DIM], lnw_ref[...], lnb_ref[...])
    wi_ref[...] = kw[:, IDX_DIM:IDX_DIM + N_IDX_HEADS]
    hb_ref[...] = mm(COL_HB, COL_END)


def _inproj(x2d, w_pad, lnw, lnb):
    n = x2d.shape[0]
    tm = min(512, n)
    row = lambda width: pl.BlockSpec((tm, width), lambda i: (i, 0))
    full = lambda a: pl.BlockSpec(a.shape, lambda i: (0,) * a.ndim)
    heads = lambda nh, d: pl.BlockSpec((nh, tm, d), lambda i: (0, i, 0))
    sds = jax.ShapeDtypeStruct
    return pl.pallas_call(
        _inproj_body,
        grid=(n // tm,),
        in_specs=[row(D_MODEL), full(w_pad), full(lnw), full(lnb)],
        out_specs=[heads(N_HEADS_A, HEAD_DIM_A), row(128), row(128), heads(N_IDX_HEADS, IDX_DIM),
                   row(IDX_DIM), row(N_IDX_HEADS), row(4 * WIDTH_B)],
        out_shape=[sds((N_HEADS_A, n, HEAD_DIM_A), BF16), sds((n, 128), F32), sds((n, 128), F32),
                   sds((N_IDX_HEADS, n, IDX_DIM), BF16), sds((n, IDX_DIM), F32), sds((n, N_IDX_HEADS), F32),
                   sds((n, 4 * WIDTH_B), F32)],
        compiler_params=_cparams("parallel"),
        name="inproj",
    )(x2d, w_pad, lnw, lnb)


def _order_bits_to_f32(u):
    key = u ^ INT_MIN
    bits = key ^ ((key >> 31) & 0x7FFFFFFF)
    f = lax.bitcast_convert_type(bits, F32)
    return jnp.where(u >= 0, jnp.where(u <= 0x007FFFFF, NEG_INF, f), f)


def _dsa_body(qa_ref, qi_ref, wi_ref, kidx_ref, k_ref, vt_ref, o_ref, sc_scr, bias_scr, term_scr, s_scr, acc_scr,
              *, s_pad, s_real, q_rows, pair, pos0, topk):
    kb_rows = KEY_BLOCK
    heads_per_kv = N_HEADS_A // N_KV_A
    qb = pair * q_rows
    cdim = pair * HEAD_DIM_A
    v_rows = pair * V_ROWS
    j = pl.program_id(1)
    q_lo = pos0 + j * q_rows
    lane = lax.broadcasted_iota(I32, (1, qb), 1)
    q_chunk = (q_lo + (lane & (q_rows - 1))) >> CHUNK_SHIFT
    k_lim = (((q_lo + q_rows - 1) >> CHUNK_SHIFT) + 1) * CHUNK
    nkb = jnp.minimum(s_pad // kb_rows, (k_lim + kb_rows - 1) // kb_rows)

    def rows_of(kb):
        return pl.ds(pl.multiple_of(kb * kb_rows, kb_rows), kb_rows)

    def fold8(a):
        return a.reshape(kb_rows // 8, 8, qb)

    def score_blk(kb, _):
        rows = rows_of(kb)
        kidx = kidx_ref[0, rows, :]
        for h in range(N_IDX_HEADS):
            term_scr[h] = jnp.maximum(_dot_nt(kidx, qi_ref[h, 0]), 0.0) * wi_ref[0, h:h + 1, :]
        total = (term_scr[0] + term_scr[1]) + (term_scr[2] + term_scr[3])
        spos = kb * kb_rows + lax.broadcasted_iota(I32, (kb_rows, qb), 0)
        sc = jnp.where((spos >> CHUNK_SHIFT) <= q_chunk, total * IDX_SCALE + 0.0, NEG_INF)
        if s_real < s_pad:
            sc = jnp.where(spos < s_real, sc, NEG_INF)
        sc_scr[rows, :] = sc
        return 0

    lax.fori_loop(0, nkb, score_blk, 0)

    def count(thr, strict):
        def body(kb, acc):
            blk = sc_scr[rows_of(kb), :]
            hit = (blk > thr) if strict else (blk >= thr)
            return acc + jnp.sum(fold8(jnp.where(hit, 1, 0).astype(I32)), axis=0)
        acc = lax.fori_loop(0, nkb, body, jnp.zeros((8, qb), I32))
        return jnp.sum(acc, axis=0, keepdims=True)

    def bit_step(i, carry):
        prefix, cnt = carry
        cand = prefix | lax.shift_left(jnp.int32(1), 31 - i)
        c = count(_order_bits_to_f32(cand), False)
        take = c >= topk
        return jnp.where(take, cand, prefix), jnp.where(take, c, cnt)

    prefix, cnt_ge = lax.fori_loop(0, 32, bit_step,
                                   (jnp.zeros((1, qb), I32), jnp.full((1, qb), nkb * kb_rows, I32)))
    tau = _order_bits_to_f32(prefix)
    finite_tau = tau > NEG_INF
    tau_floor = jnp.maximum(tau, jnp.finfo(F32).min)
    tie_lanes = jnp.where(finite_tau, jnp.where(cnt_ge > topk, 1, 0), 0)
    has_ties = jnp.max(tie_lanes) > 0

    @pl.when(jnp.logical_not(has_ties))
    def _():
        def body(kb, _):
            blk = sc_scr[rows_of(kb), :]
            bias_scr[rows_of(kb), :] = jnp.where(blk >= tau_floor, 0.0, NEG_INF)
            return 0
        lax.fori_loop(0, nkb, body, 0)

    @pl.when(has_ties)
    def _():
        need = jnp.where(finite_tau, (topk - count(tau, True)).astype(F32), 0.0)
        r = lax.broadcasted_iota(I32, (kb_rows, kb_rows), 0)
        c = lax.broadcasted_iota(I32, (kb_rows, kb_rows), 1)
        tril = jnp.where(r >= c, 1.0, 0.0).astype(BF16)

        def body(kb, seen):
            blk = sc_scr[rows_of(kb), :]
            eq = blk == tau
            rank = _dot(tril, jnp.where(eq, 1.0, 0.0).astype(BF16)) + seen
            tie_bias = jnp.where(eq, jnp.where(rank <= need, 0.0, NEG_INF), NEG_INF)
            bias_scr[rows_of(kb), :] = jnp.where(blk > tau, 0.0, tie_bias)
            return rank[kb_rows - 1:kb_rows, :]
        lax.fori_loop(0, nkb, body, jnp.zeros((1, qb), F32))

    def pass1(kb, m8):
        rows = rows_of(kb)
        bias = bias_scr[rows, :]
        out = []
        for h in range(N_HEADS_A):
            g = h // heads_per_kv
            s = _dot_nt(k_ref[0, rows, g * cdim:(g + 1) * cdim], qa_ref[h, 0]) + bias
            s_scr[h, rows, :] = s
            out.append(jnp.maximum(m8[h], jnp.max(fold8(s), axis=0)))
        return tuple(out)

    m8 = lax.fori_loop(0, nkb, pass1, tuple(jnp.full((8, qb), NEG_INF, F32) for _ in range(N_HEADS_A)))
    m = [jnp.max(x, axis=0, keepdims=True) for x in m8]

    acc_scr[...] = jnp.zeros_like(acc_scr)

    def pass2(kb, _):
        rows = rows_of(kb)
        for h in range(N_HEADS_A):
            g = h // heads_per_kv
            p = jnp.exp2(s_scr[h, rows, :] - m[h]).astype(BF16)
            acc_scr[h] += _dot(vt_ref[0, g * v_rows:(g + 1) * v_rows, rows], p)
        return 0

    lax.fori_loop(0, nkb, pass2, 0)
    for h in range(N_HEADS_A):
        a = acc_scr[h]
        num, den = a[:HEAD_DIM_A], a[HEAD_DIM_A:HEAD_DIM_A + 1]
        for mb in range(1, pair):
            mine = lane >= mb * q_rows
            r0 = mb * V_ROWS
            num = jnp.where(mine, a[r0:r0 + HEAD_DIM_A], num)
            den = jnp.where(mine, a[r0 + HEAD_DIM_A:r0 + HEAD_DIM_A + 1], den)
        o_ref[0, h * HEAD_DIM_A:(h + 1) * HEAD_DIM_A, :] = (num * (1.0 / den)).astype(BF16)


def _keyprep_body(hk_ref, hv_ref, hki_ref, nk_ref, nv_ref, nki_ref, k_ref, ki_ref, vt_ref, *, hist_blocks, t_new):
    j = pl.program_id(1)

    def emit(k2, v2, ki2):
        for mb in range(2):
            ki_ref[0, :, mb * IDX_DIM:(mb + 1) * IDX_DIM] = ki2[mb].astype(BF16)
            vt = v2[mb].T
            for g in range(N_KV_A):
                col = (g * 2 + mb) * HEAD_DIM_A
                k_ref[0, :, col:col + HEAD_DIM_A] = k2[mb][:, g * HEAD_DIM_A:(g + 1) * HEAD_DIM_A].astype(BF16)
                row = (g * 2 + mb) * V_ROWS
                vt_ref[0, row:row + HEAD_DIM_A, :] = vt[g * HEAD_DIM_A:(g + 1) * HEAD_DIM_A, :].astype(BF16)
                vt_ref[0, row + HEAD_DIM_A:row + V_ROWS, :] = jnp.ones((V_ROWS - HEAD_DIM_A, KEY_BLOCK), BF16)

    @pl.when(j < hist_blocks)
    def _():
        emit([hk_ref[0], hk_ref[1]], [hv_ref[0], hv_ref[1]], [hki_ref[0], hki_ref[1]])

    @pl.when(j == hist_blocks)
    def _():
        def padded(ref, mb):
            return jnp.concatenate([ref[mb], jnp.zeros((KEY_BLOCK - t_new, ref.shape[2]), F32)], axis=0)
        emit([padded(nk_ref, 0), padded(nk_ref, 1)], [padded(nv_ref, 0), padded(nv_ref, 1)],
             [padded(nki_ref, 0), padded(nki_ref, 1)])


def _paired_key_operands(hist_k, hist_v, hist_kidx, k_new, v_new, ki_new):
    b, past, _ = hist_k.shape
    t_new = k_new.shape[1]
    hist_blocks = past // KEY_BLOCK
    s_pad = past + KEY_BLOCK
    hist = lambda width: pl.BlockSpec((2, KEY_BLOCK, width), lambda i, j: (i, jnp.minimum(j, hist_blocks - 1), 0))
    new = lambda width: pl.BlockSpec((2, t_new, width), lambda i, j: (i, 0, 0))
    kv = N_KV_A * HEAD_DIM_A
    k_op, ki_op, vt_op = pl.pallas_call(
        functools.partial(_keyprep_body, hist_blocks=hist_blocks, t_new=t_new),
        grid=(b // 2, hist_blocks + 1),
        in_specs=[hist(kv), hist(kv), hist(IDX_DIM), new(kv), new(kv), new(IDX_DIM)],
        out_specs=[pl.BlockSpec((1, KEY_BLOCK, 2 * kv), lambda i, j: (i, j, 0)),
                   pl.BlockSpec((1, KEY_BLOCK, 2 * IDX_DIM), lambda i, j: (i, j, 0)),
                   pl.BlockSpec((1, 2 * N_KV_A * V_ROWS, KEY_BLOCK), lambda i, j: (i, 0, j))],
        out_shape=[jax.ShapeDtypeStruct((b // 2, s_pad, 2 * kv), BF16),
                   jax.ShapeDtypeStruct((b // 2, s_pad, 2 * IDX_DIM), BF16),
                   jax.ShapeDtypeStruct((b // 2, 2 * N_KV_A * V_ROWS, s_pad), BF16)],
        compiler_params=_cparams("parallel", "parallel"),
        name="keyprep",
    )(hist_k, hist_v, hist_kidx, k_new, v_new, ki_new)
    return ki_op, k_op, vt_op


def _dsa_pairing(b, t):
    return 2 if (2 * t <= 128 and b % 2 == 0) else 1


def _dsa(qa_h, qi_h, wi, keys, *, s_real, pos0, topk, keys_arranged=False):
    _, b, t, _ = qa_h.shape
    s_pad = keys[0].shape[1]
    q_rows = min(256, t)
    pair = _dsa_pairing(b, t)
    bp = b // pair
    lanes, cdim, v_rows = pair * q_rows, pair * HEAD_DIM_A, pair * V_ROWS

    def block_diag(q):
        if pair == 1:
            return q
        nh, d = q.shape[0], q.shape[3]
        eye = jnp.eye(pair, dtype=q.dtype)
        q6 = q.reshape(nh, bp, pair, t, 1, d) * eye.reshape(1, 1, pair, 1, pair, 1)
        return q6.reshape(nh, bp, pair * t, pair * d)

    wi_op = wi.reshape(bp, pair, t, N_IDX_HEADS).transpose(0, 3, 1, 2).reshape(bp, N_IDX_HEADS, pair * t)
    if keys_arranged:
        ki_op, k_op, vt_op = keys
    else:
        ki_all, k_all, v_all = keys
        ki_op = ki_all.reshape(bp, pair, s_pad, IDX_DIM).transpose(0, 2, 1, 3).reshape(bp, s_pad, cdim)
        k_op = (k_all.reshape(bp, pair, s_pad, N_KV_A, HEAD_DIM_A).transpose(0, 2, 3, 1, 4)
                .reshape(bp, s_pad, N_KV_A * cdim))
        vt = v_all.reshape(b, s_pad, N_KV_A, HEAD_DIM_A).transpose(0, 2, 3, 1)
        vt = jnp.concatenate([vt, jnp.ones((b, N_KV_A, V_ROWS - HEAD_DIM_A, s_pad), vt.dtype)], axis=2)
        vt_op = (vt.reshape(bp, pair, N_KV_A, V_ROWS, s_pad).transpose(0, 2, 1, 3, 4)
                 .reshape(bp, N_KV_A * v_rows, s_pad))

    body = functools.partial(_dsa_body, s_pad=s_pad, s_real=s_real, q_rows=q_rows, pair=pair, pos0=pos0,
                             topk=topk)
    o_t = pl.pallas_call(
        body,
        grid=(bp, t // q_rows),
        in_specs=[
            pl.BlockSpec((N_HEADS_A, 1, lanes, cdim), lambda i, j: (0, i, j, 0)),
            pl.BlockSpec((N_IDX_HEADS, 1, lanes, cdim), lambda i, j: (0, i, j, 0)),
            pl.BlockSpec((1, N_IDX_HEADS, lanes), lambda i, j: (i, 0, j)),
            pl.BlockSpec((1, s_pad, cdim), lambda i, j: (i, 0, 0)),
            pl.BlockSpec((1, s_pad, N_KV_A * cdim), lambda i, j: (i, 0, 0)),
            pl.BlockSpec((1, N_KV_A * v_rows, s_pad), lambda i, j: (i, 0, 0)),
        ],
        out_specs=pl.BlockSpec((1, WIDTH_A, lanes), lambda i, j: (i, 0, j)),
        out_shape=jax.ShapeDtypeStruct((bp, WIDTH_A, pair * t), BF16),
        scratch_shapes=[pltpu.VMEM((s_pad, lanes), F32), pltpu.VMEM((s_pad, lanes), F32),
                        pltpu.VMEM((N_IDX_HEADS, KEY_BLOCK, lanes), F32),
                        pltpu.VMEM((N_HEADS_A, s_pad, lanes), F32), pltpu.VMEM((N_HEADS_A, v_rows, lanes), F32)],
        compiler_params=_cparams("parallel", "parallel"),
        name="dsa",
    )(block_diag(qa_h), block_diag(qi_h), wi_op, ki_op, k_op, vt_op)
    return o_t.reshape(bp, WIDTH_A, pair, t).transpose(0, 2, 3, 1).reshape(b * t, WIDTH_A)


def _split3(a):
    hi = a.astype(BF16)
    r1 = a - hi.astype(F32)
    mid = r1.astype(BF16)
    lo = (r1 - mid.astype(F32)).astype(BF16)
    return hi, mid, lo


def _hgrn_body(hb_ref, s0_ref, lb_ref, nw_ref, ob_ref, sfin_ref, state_scr, *, tb, nb):
    t = pl.program_id(1)

    @pl.when(t == 0)
    def _():
        for bi, h in itertools.product(range(nb), range(N_HEADS_B)):
            state_scr[bi, h] = s0_ref[bi, h].T

    lb = lb_ref[...]
    r = lax.broadcasted_iota(I32, (CHUNK, CHUNK), 0)
    c = lax.broadcasted_iota(I32, (CHUNK, CHUNK), 1)
    causal = r >= c
    tril = jnp.where(causal, 1.0, 0.0).astype(BF16)

    for ci, bi in itertools.product(range(tb // CHUNK), range(nb)):
        rows = slice(ci * CHUNK, (ci + 1) * CHUNK)
        f = lb + (1.0 - lb) * jax.nn.sigmoid(hb_ref[bi, rows, WIDTH_B:2 * WIDTH_B])
        parts = _split3(jnp.log(f))
        bcum = _dot(tril, parts[0]) + _dot(tril, parts[1]) + _dot(tril, parts[2])
        for h in range(N_HEADS_B):
            lanes = slice(h * HEAD_DIM_B, (h + 1) * HEAD_DIM_B)
            q = hb_ref[bi, rows, lanes]
            k = 1.0 - f[:, lanes]
            v = hb_ref[bi, rows, 2 * WIDTH_B + h * HEAD_DIM_B:2 * WIDTH_B + (h + 1) * HEAD_DIM_B]
            gate = hb_ref[bi, rows, 3 * WIDTH_B + h * HEAD_DIM_B:3 * WIDTH_B + (h + 1) * HEAD_DIM_B]
            bh = bcum[:, lanes]
            b_last = bh[CHUNK - 1:CHUNK, :]
            vb = v.astype(BF16)
            state_t = state_scr[bi, h]
            o_inter = _dot_nt((q * jnp.exp(bh)).astype(BF16), state_t.astype(BF16))
            a_rows = []
            for i in range(CHUNK // HGRN_SUB):
                lo, hi = i * HGRN_SUB, (i + 1) * HGRN_SUB
                ref = bh[lo - 1:lo, :] if i else jnp.zeros((1, HEAD_DIM_B), F32)
                qs = (q[lo:hi] * jnp.exp(bh[lo:hi] - ref)).astype(BF16)
                ks = (k * jnp.exp(ref - bh)).astype(BF16)
                a_rows.append(_dot_nt(qs, ks))
            a = jnp.where(causal, jnp.concatenate(a_rows, axis=0), 0.0)
            o = o_inter + _dot(a.astype(BF16), vb)
            kdec = (k * jnp.exp(b_last - bh)).astype(BF16)
            state_scr[bi, h] = jnp.exp(b_last) * state_t + _dot_tn(vb, kdec)
            ms = jnp.mean(o * o, axis=-1, keepdims=True)
            y = o * lax.rsqrt(ms + RMS_EPS) * nw_ref[...] * (gate * jax.nn.sigmoid(gate))
            ob_ref[bi, rows, lanes] = y.astype(BF16)

    @pl.when(t == pl.num_programs(1) - 1)
    def _():
        for bi, h in itertools.product(range(nb), range(N_HEADS_B)):
            sfin_ref[bi, h] = state_scr[bi, h].T


def _hgrn(hb, s0, lb, nw):
    b, t, _ = hb.shape
    tb = min(256, t)
    nb = 2 if b % 2 == 0 else 1
    return pl.pallas_call(
        functools.partial(_hgrn_body, tb=tb, nb=nb),
        grid=(b // nb, t // tb),
        in_specs=[
            pl.BlockSpec((nb, tb, 4 * WIDTH_B), lambda i, j: (i, j, 0)),
            pl.BlockSpec((nb, N_HEADS_B, HEAD_DIM_B, HEAD_DIM_B), lambda i, j: (i, 0, 0, 0)),
            pl.BlockSpec((1, WIDTH_B), lambda i, j: (0, 0)),
            pl.BlockSpec((1, HEAD_DIM_B), lambda i, j: (0, 0)),
        ],
        out_specs=[
            pl.BlockSpec((nb, tb, WIDTH_B), lambda i, j: (i, j, 0)),
            pl.BlockSpec((nb, N_HEADS_B, HEAD_DIM_B, HEAD_DIM_B), lambda i, j: (i, 0, 0, 0)),
        ],
        out_shape=[jax.ShapeDtypeStruct((b, t, WIDTH_B), BF16),
                   jax.ShapeDtypeStruct((b, N_HEADS_B, HEAD_DIM_B, HEAD_DIM_B), F32)],
        scratch_shapes=[pltpu.VMEM((nb, N_HEADS_B, HEAD_DIM_B, HEAD_DIM_B), F32)],
        compiler_params=_cparams("parallel", "arbitrary"),
        name="hgrn",
    )(hb, s0, lb, nw)


def _outproj_body(oa_ref, ob_ref, x_ref, w_ref, lnw_ref, lnb_ref, wrh_ref, wrl_ref, br_ref, x1_ref, lg_ref):
    y = _dot(oa_ref[...], w_ref[:WIDTH_A, :]) + _dot(ob_ref[...], w_ref[WIDTH_A:, :])
    x1 = _layer_norm(DN_ALPHA * x_ref[...] + y, lnw_ref[...], lnb_ref[...])
    x1_ref[...] = x1
    hi = x1.astype(BF16)
    lo = (x1 - hi.astype(F32)).astype(BF16)
    lg_ref[...] = (_dot_nt(wrh_ref[...], hi) + _dot_nt(wrh_ref[...], lo) + _dot_nt(wrl_ref[...], hi)
                   + br_ref[...])


def _outproj(oa, ob, x2d, w_out, lnw, lnb, wr_hi, wr_lo, br):
    n = x2d.shape[0]
    tm = min(512, n)
    row = lambda width: pl.BlockSpec((tm, width), lambda i: (i, 0))
    full = lambda a: pl.BlockSpec(a.shape, lambda i: (0,) * a.ndim)
    return pl.pallas_call(
        _outproj_body,
        grid=(n // tm,),
        in_specs=[row(WIDTH_A), row(WIDTH_B), row(D_MODEL), full(w_out), full(lnw), full(lnb),
                  full(wr_hi), full(wr_lo), full(br)],
        out_specs=[row(D_MODEL), pl.BlockSpec((N_EXPERTS, tm), lambda i: (0, i))],
        out_shape=[jax.ShapeDtypeStruct((n, D_MODEL), F32), jax.ShapeDtypeStruct((N_EXPERTS, n), F32)],
        compiler_params=_cparams("parallel"),
        name="outproj",
    )(oa, ob, x2d, w_out, lnw, lnb, wr_hi, wr_lo, br)


def _route_body(lg_ref, lpos_ref, gate_ref, tc_ref, base_ref, cnt_ref, cnt_scr, *, tr):
    @pl.when(pl.program_id(0) == 0)
    def _():
        cnt_scr[...] = jnp.zeros_like(cnt_scr)

    l = lg_ref[...]
    rows = lax.broadcasted_iota(I32, (N_EXPERTS, tr), 0)
    vals, hots = [], []
    for j in range(TOP_K):
        m = jnp.max(l, axis=0, keepdims=True)
        idx = jnp.min(jnp.where(l == m, rows, N_EXPERTS), axis=0, keepdims=True)
        hot = rows == idx
        vals.append(m)
        hots.append(hot)
        l = jnp.where(hot, NEG_INF, l)
    es = [jnp.exp(v - vals[0]) for v in vals]
    inv = 1.0 / (es[0] + es[1] + es[2] + es[3])
    for j in range(TOP_K):
        gate_ref[j:j + 1, :] = es[j] * inv
    chosen = jnp.zeros((N_EXPERTS, tr), F32)
    for hot in hots:
        chosen = chosen + jnp.where(hot, 1.0, 0.0)
    r = lax.broadcasted_iota(I32, (tr, tr), 0)
    c = lax.broadcasted_iota(I32, (tr, tr), 1)
    before = jnp.where(r < c, 1.0, 0.0).astype(BF16)
    prior = _dot(chosen.astype(BF16), before)
    units = jnp.ceil(jnp.sum(chosen, axis=1, keepdims=True) * (1.0 / ROW_ALIGN))
    er = lax.broadcasted_iota(I32, (N_EXPERTS, N_EXPERTS), 0)
    ec = lax.broadcasted_iota(I32, (N_EXPERTS, N_EXPERTS), 1)
    earlier = jnp.where(ec < er, 1.0, 0.0).astype(BF16)
    run_len = jnp.broadcast_to(units, (N_EXPERTS, 128)) * ROW_ALIGN
    run_start = _dot(earlier, jnp.broadcast_to(units, (N_EXPERTS, 128)).astype(BF16)) * ROW_ALIGN
    where_in_tile = prior + run_start[:, 0:1]
    for j in range(TOP_K):
        lpos_ref[j:j + 1, :] = jnp.sum(jnp.where(hots[j], where_in_tile, 0.0), axis=0,
                                       keepdims=True).astype(I32)
    tc_ref[0] = run_len.astype(I32)
    base_ref[0] = cnt_scr[...].astype(I32)
    cnt_scr[...] = cnt_scr[...] + run_len
    cnt_ref[...] = cnt_scr[...].astype(I32)


def _route(logits_t):
    n = logits_t.shape[1]
    tr = MOE_TILE
    tiles = n // tr
    tok = lambda rows: pl.BlockSpec((rows, tr), lambda i: (0, i))
    per_tile = pl.BlockSpec((1, N_EXPERTS, 128), lambda i: (i, 0, 0))
    return pl.pallas_call(
        functools.partial(_route_body, tr=tr),
        grid=(tiles,),
        in_specs=[tok(N_EXPERTS)],
        out_specs=[tok(TOP_K), tok(TOP_K), per_tile, per_tile, pl.BlockSpec((N_EXPERTS, 128), lambda i: (0, 0))],
        out_shape=[jax.ShapeDtypeStruct((TOP_K, n), I32), jax.ShapeDtypeStruct((TOP_K, n), F32),
                   jax.ShapeDtypeStruct((tiles, N_EXPERTS, 128), I32),
                   jax.ShapeDtypeStruct((tiles, N_EXPERTS, 128), I32),
                   jax.ShapeDtypeStruct((N_EXPERTS, 128), I32)],
        scratch_shapes=[pltpu.VMEM((N_EXPERTS, 128), F32)],
        compiler_params=_cparams("arbitrary"),
        name="route",
    )(logits_t)


def _pack_pairs(a):
    half = a.shape[1] // 2
    hi = lax.bitcast_convert_type(a[:, :half], I32)
    lo = lax.shift_right_logical(lax.bitcast_convert_type(a[:, half:], I32), 16)
    return hi | lo


def _unpack_pairs(pk):
    hi = lax.bitcast_convert_type(pk & jnp.int32(-65536), F32).astype(BF16)
    lo = lax.bitcast_convert_type(pk << 16, F32).astype(BF16)
    return hi, lo


def _for_each_run_piece(tc_ref, g0_ref, t, fn):
    def per_expert(e, local):
        c = tc_ref[t, e]
        glob = g0_ref[t, e]
        lo = local
        for p in RUN_PIECES:
            take = (c & p) != 0

            @pl.when(take)
            def _():
                fn(pl.multiple_of(lo, ROW_ALIGN), pl.multiple_of(glob, ROW_ALIGN), p)

            step = jnp.where(take, p, 0)
            lo = lo + step
            glob = glob + step
        return local + c

    lax.fori_loop(0, N_EXPERTS, per_expert, 0)


def _dispatch_body(tc_ref, g0_ref, seg_ref, lpos_ref, x_ref, buf_ref, stage, zero_scr, zsem, sems, *, n_blocks):
    i = pl.program_id(0)
    slot = i % 2

    def run_copy(slot_, lo, glob, rows):
        return pltpu.make_async_copy(stage.at[slot_, pl.ds(lo, rows), :], buf_ref.at[pl.ds(glob, rows), :],
                                     sems.at[slot_])

    @pl.when(i == 0)
    def _():
        zero_scr[...] = jnp.zeros_like(zero_scr)
        n_used = seg_ref[1, N_EXPERTS - 1] // EXPERT_ROWS

        def block_copy(start):
            return pltpu.make_async_copy(
                zero_scr, buf_ref.at[pl.ds(pl.multiple_of(start, EXPERT_ROWS), EXPERT_ROWS), :], zsem)

        def clears(action):
            for e in range(N_EXPERTS):
                @pl.when(seg_ref[1, e] > seg_ref[0, e])
                def _():
                    action(block_copy(seg_ref[1, e] - EXPERT_ROWS))

                @pl.when(n_used + e < n_blocks)
                def _():
                    action(block_copy((n_used + e) * EXPERT_ROWS))

        clears(lambda cp: cp.start())
        clears(lambda cp: cp.wait())

    r = lax.broadcasted_iota(I32, (MOE_SLOTS, MOE_TILE), 0)
    onehot = jnp.zeros((MOE_SLOTS, MOE_TILE), F32)
    for j in range(TOP_K):
        onehot = jnp.where(r == lpos_ref[j:j + 1, :], 1.0, onehot)
    stage[slot] = _pack_pairs(_dot(onehot.astype(BF16), x_ref[...].astype(BF16)))

    @pl.when(i > 0)
    def _():
        _for_each_run_piece(tc_ref, g0_ref, i - 1,
                            lambda lo, glob, rows: run_copy(1 - slot, lo, glob, rows).wait())

    _for_each_run_piece(tc_ref, g0_ref, i, lambda lo, glob, rows: run_copy(slot, lo, glob, rows).start())

    @pl.when(i == pl.num_programs(0) - 1)
    def _():
        _for_each_run_piece(tc_ref, g0_ref, i, lambda lo, glob, rows: run_copy(slot, lo, glob, rows).wait())


def _dispatch(x1, lpos_t, tc, g0, seg, n_rows):
    n = x1.shape[0]
    return pl.pallas_call(
        functools.partial(_dispatch_body, n_blocks=n_rows // EXPERT_ROWS),
        grid_spec=pltpu.PrefetchScalarGridSpec(
            num_scalar_prefetch=3,
            grid=(n // MOE_TILE,),
            in_specs=[pl.BlockSpec((TOP_K, MOE_TILE), lambda i, *_: (0, i)),
                      pl.BlockSpec((MOE_TILE, D_MODEL), lambda i, *_: (i, 0))],
            out_specs=pl.BlockSpec(memory_space=pl.ANY),
            scratch_shapes=[pltpu.VMEM((2, MOE_SLOTS, D_MODEL // 2), I32),
                            pltpu.VMEM((EXPERT_ROWS, D_MODEL // 2), I32),
                            pltpu.SemaphoreType.DMA(()), pltpu.SemaphoreType.DMA((2,))],
        ),
        out_shape=jax.ShapeDtypeStruct((n_rows, D_MODEL // 2), I32),
        compiler_params=_cparams("arbitrary"),
        name="dispatch",
    )(tc, g0, seg, lpos_t, x1)


def _experts_body(be_ref, nu_ref, x_ref, wgu_ref, bgu_ref, wd_ref, bd_ref, o_ref, wgu_bf, wd_bf):
    i = pl.program_id(0)

    @pl.when(jnp.logical_or(i == 0, be_ref[i] != be_ref[jnp.maximum(i - 1, 0)]))
    def _():
        for r in range(0, D_MODEL, 256):
            wgu_bf[r:r + 256, :] = wgu_ref[0, r:r + 256, :].astype(BF16)
            wd_bf[r:r + 256, :] = wd_ref[0, r:r + 256, :].astype(BF16)

    @pl.when(i < nu_ref[0])
    def _():
        half = D_MODEL // 2
        x_a, x_b = _unpack_pairs(x_ref[...])
        gu = _dot(x_a, wgu_bf[:half, :]) + _dot(x_b, wgu_bf[half:, :]) + bgu_ref[0]
        gate = jnp.minimum(gu[:, :D_EXPERT], SWIGLU_LIMIT)
        up = jnp.clip(gu[:, D_EXPERT:], -SWIGLU_LIMIT, SWIGLU_LIMIT)
        hdn = (up + 1.0) * gate * jax.nn.sigmoid(SWIGLU_ALPHA * gate)
        out = _dot(hdn.astype(BF16), wd_bf[...]) + bd_ref[0]
        o_ref[...] = _pack_pairs(out.astype(BF16).astype(F32))

    @pl.when(i >= nu_ref[0])
    def _():
        o_ref[...] = jnp.zeros_like(o_ref)


def _experts(buf, block_e, n_used, w_gu, b_gu, w_down, b_down):
    n_blocks = buf.shape[0] // EXPERT_ROWS
    rows = pl.BlockSpec((EXPERT_ROWS, D_MODEL // 2), lambda i, be, nu: (jnp.minimum(i, nu[0] - 1), 0))
    per_e = lambda a: pl.BlockSpec((1,) + a.shape[1:], lambda i, be, nu: (be[i],) + (0,) * (a.ndim - 1))
    return pl.pallas_call(
        _experts_body,
        grid_spec=pltpu.PrefetchScalarGridSpec(
            num_scalar_prefetch=2,
            grid=(n_blocks,),
            in_specs=[rows, per_e(w_gu), per_e(b_gu), per_e(w_down), per_e(b_down)],
            out_specs=pl.BlockSpec((EXPERT_ROWS, D_MODEL // 2), lambda i, be, nu: (i, 0)),
            scratch_shapes=[pltpu.VMEM((D_MODEL, 2 * D_EXPERT), BF16), pltpu.VMEM((D_EXPERT, D_MODEL), BF16)],
        ),
        out_shape=jax.ShapeDtypeStruct(buf.shape, I32),
        compiler_params=_cparams("arbitrary"),
        name="experts",
    )(block_e, n_used, buf, w_gu, b_gu, w_down, b_down)


def _final_body(tc_ref, g0_ref, lpos_ref, gate_ref, x1_ref, p_ref, lnw_ref, lnb_ref, wg_ref, bg_ref, wp_ref,
                eo_ref, y_ref, stage, sems):
    i = pl.program_id(0)
    slot = i % 2

    def run_copy(slot_, lo, glob, rows):
        return pltpu.make_async_copy(eo_ref.at[pl.ds(glob, rows), :], stage.at[slot_, pl.ds(lo, rows), :],
                                     sems.at[slot_])

    def fetch(tile, slot_):
        stage[slot_] = jnp.zeros(stage.shape[1:], stage.dtype)
        _for_each_run_piece(tc_ref, g0_ref, tile, lambda lo, glob, rows: run_copy(slot_, lo, glob, rows).start())

    @pl.when(i == 0)
    def _():
        fetch(0, 0)

    @pl.when(i + 1 < pl.num_programs(0))
    def _():
        fetch(i + 1, 1 - slot)

    _for_each_run_piece(tc_ref, g0_ref, i, lambda lo, glob, rows: run_copy(slot, lo, glob, rows).wait())

    r = lax.broadcasted_iota(I32, (MOE_TILE, MOE_SLOTS), 1)
    g = jnp.zeros((MOE_TILE, MOE_SLOTS), F32)
    for j in range(TOP_K):
        g = jnp.where(r == lpos_ref[:, j:j + 1], gate_ref[:, j:j + 1], g)
    g_hi = g.astype(BF16)
    g_lo = (g - g_hi.astype(F32)).astype(BF16)
    g2 = jnp.concatenate([g_hi, g_lo], axis=0)
    e_a, e_b = _unpack_pairs(stage[slot])
    r_a, r_b = _dot(g2, e_a), _dot(g2, e_b)
    moe = jnp.concatenate([r_a[:MOE_TILE] + r_a[MOE_TILE:], r_b[:MOE_TILE] + r_b[MOE_TILE:]], axis=1)
    x2 = _layer_norm(DN_ALPHA * x1_ref[...] + moe, lnw_ref[...], lnb_ref[...])
    ple_gate = jax.nn.sigmoid(_dot(x2.astype(BF16), wg_ref[...]) + bg_ref[...])
    y_ref[...] = x2 + ple_gate * _dot(p_ref[...].astype(BF16), wp_ref[...])


def _final(tc, g0, lpos, gate, x1, p2d, lnw, lnb, w_gate, b_gate, w_ple, expert_out):
    n = x1.shape[0]
    row = lambda width: pl.BlockSpec((MOE_TILE, width), lambda i, *_: (i, 0))
    full = lambda a: pl.BlockSpec(a.shape, lambda i, *_: (0,) * a.ndim)
    return pl.pallas_call(
        _final_body,
        grid_spec=pltpu.PrefetchScalarGridSpec(
            num_scalar_prefetch=2,
            grid=(n // MOE_TILE,),
            in_specs=[row(TOP_K), row(TOP_K), row(D_MODEL), row(D_PLE), full(lnw), full(lnb), full(w_gate),
                      full(b_gate), full(w_ple), pl.BlockSpec(memory_space=pl.ANY)],
            out_specs=row(D_MODEL),
            scratch_shapes=[pltpu.VMEM((2, MOE_SLOTS, D_MODEL // 2), I32), pltpu.SemaphoreType.DMA((2,))],
        ),
        out_shape=jax.ShapeDtypeStruct((n, D_MODEL), F32),
        compiler_params=_cparams("arbitrary"),
        name="combine_final",
    )(tc, g0, lpos, gate, x1, p2d, lnw, lnb, w_gate, b_gate, w_ple, expert_out)


def _round_up(a, m):
    return -(-a // m) * m


def _encoder_layer(x, p, hist_k, hist_v, hist_kidx, s0, pos0, wts):
    b, t, _ = x.shape
    n = b * t
    x2d = x.reshape(n, D_MODEL)

    qa, ka, va, qi, ki, wi, hb = _inproj(x2d, wts["w_in"], wts["idx_lnw"], wts["idx_lnb"])

    k_new = ka.reshape(b, t, N_KV_A * HEAD_DIM_A)
    v_new = va.reshape(b, t, N_KV_A * HEAD_DIM_A)
    ki_new = ki.reshape(b, t, IDX_DIM)
    q_ops = (qa.reshape(N_HEADS_A, b, t, HEAD_DIM_A), qi.reshape(N_IDX_HEADS, b, t, IDX_DIM),
             wi.reshape(b, t, N_IDX_HEADS))
    past = 0 if hist_k is None else hist_k.shape[1]
    s_real = past + t
    topk = min(TOPK_MAX, s_real // 4)
    if past and past % KEY_BLOCK == 0 and t <= KEY_BLOCK and _dsa_pairing(b, t) == 2:
        keys = _paired_key_operands(hist_k.reshape(b, past, -1), hist_v.reshape(b, past, -1), hist_kidx,
                                    k_new, v_new, ki_new)
        oa = _dsa(*q_ops, keys, s_real=s_real, pos0=pos0, topk=topk, keys_arranged=True)
    else:
        if past:
            k_all = jnp.concatenate([hist_k.reshape(b, past, -1), k_new], axis=1)
            v_all = jnp.concatenate([hist_v.reshape(b, past, -1), v_new], axis=1)
            ki_all = jnp.concatenate([hist_kidx, ki_new], axis=1)
        else:
            k_all, v_all, ki_all = k_new, v_new, ki_new
        pad = ((0, 0), (0, _round_up(s_real, KEY_BLOCK) - s_real), (0, 0))
        keys = tuple(jnp.pad(a.astype(BF16), pad) for a in (ki_all, k_all, v_all))
        oa = _dsa(*q_ops, keys, s_real=s_real, pos0=pos0, topk=topk)

    ob, s_fin = _hgrn(hb.reshape(b, t, 4 * WIDTH_B), s0, wts["lb"], wts["hgrn_nw"])

    x1, logits_t = _outproj(oa, ob.reshape(n, WIDTH_B), x2d, wts["w_out"], wts["ln1_w"], wts["ln1_b"],
                            wts["wr_hi"], wts["wr_lo"], wts["b_router"])

    assert n % MOE_TILE == 0
    tiles = n // MOE_TILE
    lpos_t, gate_t, tile_runs, tile_base, counts = _route(logits_t)
    tc, counts = tile_runs[:, :, 0], counts[:, 0]
    padded = (counts + EXPERT_ROWS - 1) // EXPERT_ROWS * EXPERT_ROWS
    seg_end = jnp.cumsum(padded)
    seg_start = seg_end - padded
    g0 = (seg_start[None, :] + tile_base[:, :, 0]).astype(I32)
    n_blocks = -(-(n * TOP_K + N_EXPERTS * tiles * (ROW_ALIGN - 1) + N_EXPERTS * (EXPERT_ROWS - 1))
                 // EXPERT_ROWS)
    block_row0 = jnp.arange(n_blocks, dtype=I32) * EXPERT_ROWS
    block_e = jnp.minimum(jnp.sum((seg_end[None, :] <= block_row0[:, None]).astype(I32), axis=1), N_EXPERTS - 1)
    n_used = (seg_end[-1:] // EXPERT_ROWS).astype(I32)
    seg = jnp.stack([seg_start, seg_end]).astype(I32)
    buf = _dispatch(x1, lpos_t, tc, g0, seg, n_blocks * EXPERT_ROWS)
    expert_out = _experts(buf, block_e, n_used, wts["w_gu"], wts["b_gu"], wts["w_down"], wts["b_down"])
    y = _final(tc, g0, lpos_t.T, gate_t.T, x1, p.reshape(n, D_PLE), wts["ln2_w"], wts["ln2_b"],
               wts["w_ple_gate"], wts["b_ple_gate"], wts["w_ple"], expert_out)

    return (y.reshape(b, t, D_MODEL), ka.reshape(b, t, N_KV_A, HEAD_DIM_A), va.reshape(b, t, N_KV_A, HEAD_DIM_A),
            ki_new, s_fin)


def _prep_weights(w_in, w_out, idx_k_norm_w, idx_k_norm_b, lb, hgrn_norm_w, ln1_w, ln1_b, w_router, b_router,
                  w_gu, b_gu, w_down, b_down, ln2_w, ln2_b, w_ple, w_ple_gate, b_ple_gate):
    n_a = COL_KW + IDX_DIM + N_IDX_HEADS
    w_pad = jnp.concatenate(
        [w_in[:, :n_a], jnp.zeros((D_MODEL, COL_HB - n_a), w_in.dtype), w_in[:, n_a:]], axis=1).astype(BF16)
    wr_t = w_router.T
    wr_hi = wr_t.astype(BF16)
    row = lambda a: a.reshape(1, -1)
    return dict(
        w_in=w_pad, idx_lnw=row(idx_k_norm_w), idx_lnb=row(idx_k_norm_b), lb=row(lb), hgrn_nw=row(hgrn_norm_w),
        w_out=w_out.astype(BF16), ln1_w=row(ln1_w), ln1_b=row(ln1_b),
        wr_hi=wr_hi, wr_lo=(wr_t - wr_hi.astype(F32)).astype(BF16), b_router=b_router.reshape(N_EXPERTS, 1),
        w_gu=w_gu, b_gu=b_gu.reshape(N_EXPERTS, 1, 2 * D_EXPERT),
        w_down=w_down, b_down=b_down.reshape(N_EXPERTS, 1, D_MODEL),
        ln2_w=row(ln2_w), ln2_b=row(ln2_b), w_ple=w_ple.astype(BF16), w_ple_gate=w_ple_gate.astype(BF16),
        b_ple_gate=row(b_ple_gate))


def kernel(x_prompt, x_sample, cache_k, cache_v, cache_kidx, state_hgrn, p_prompt, p_sample, w_in, w_out,
           idx_k_norm_w, idx_k_norm_b, hgrn_lb_logits, hgrn_norm_w, ln1_w, ln1_b, w_router, b_router, w_gu, b_gu,
           w_down, b_down, ln2_w, ln2_b, w_ple, w_ple_gate, b_ple_gate):
    lb_all = jnp.cumsum(jax.nn.softmax(hgrn_lb_logits.astype(F32), axis=0), axis=0)
    xp, xs = x_prompt, x_sample
    outs = [[] for _ in range(8)]
    for i in range(DEPTH):
        wts = _prep_weights(w_in[i], w_out[i], idx_k_norm_w[i], idx_k_norm_b[i], lb_all[i], hgrn_norm_w[i],
                            ln1_w[i], ln1_b[i], w_router[i], b_router[i], w_gu[i], b_gu[i], w_down[i], b_down[i],
                            ln2_w[i], ln2_b[i], w_ple[i], w_ple_gate[i], b_ple_gate[i])
        s0p = jnp.zeros((xp.shape[0], N_HEADS_B, HEAD_DIM_B, HEAD_DIM_B), F32)
        xp, kp, vp, kip, sp = _encoder_layer(xp, p_prompt[i], None, None, None, s0p, 0, wts)
        xs, kn, vn, kin, sn = _encoder_layer(xs, p_sample[i], cache_k[i], cache_v[i], cache_kidx[i],
                                             state_hgrn[i], cache_k.shape[2], wts)
        for lst, val in zip(outs, (kp, vp, kip, sp, kn, vn, kin, sn)):
            lst.append(val)
    return (xp, xs) + tuple(jnp.stack(l) for l in outs)
```

```python
import functools
import itertools

import jax
import jax.numpy as jnp
from jax import lax
from jax.experimental import pallas as pl
from jax.experimental.pallas import tpu as pltpu

F32 = jnp.float32
BF16 = jnp.bfloat16
I32 = jnp.int32

D_MODEL = 1024
CHUNK = 64
CHUNK_SHIFT = 6
WIDTH_A = 512
HEAD_DIM_A = 64
N_HEADS_A = 8
N_KV_A = 2
N_IDX_HEADS = 4
IDX_DIM = 64
IDX_SCALE = IDX_DIM ** -0.5 * N_IDX_HEADS ** -0.5
TOPK_MAX = 256
WIDTH_B = 512
HEAD_DIM_B = 128
N_HEADS_B = 4
N_EXPERTS = 32
TOP_K = 4
D_EXPERT = 1024
SWIGLU_LIMIT = 7.0
SWIGLU_ALPHA = 1.702
D_PLE = 256
LN_EPS = 1e-5
RMS_EPS = 1e-6
DEPTH = 1
DN_ALPHA = (2 * DEPTH) ** 0.25

COL_QA, COL_KA, COL_VA, COL_QI, COL_KW, COL_HB, COL_END = 0, 512, 640, 768, 1024, 1152, 3200

VMEM_LIMIT = 56 * 1024 * 1024
KEY_BLOCK = 256
V_ROWS = 80
LOG2_E = 1.4426950408889634
HGRN_SUB = 16
EXPERT_ROWS = 512
MOE_TILE = 512
ROW_ALIGN = 8
RUN_PIECES = (512, 256, 128, 64, 32, 16, 8)
MOE_SLOTS = -(-(TOP_K * MOE_TILE + N_EXPERTS * (ROW_ALIGN - 1)) // 256) * 256
NEG_INF = float("-inf")
INT_MIN = -(2 ** 31)


def _cparams(*sem):
    return pltpu.CompilerParams(dimension_semantics=sem, vmem_limit_bytes=VMEM_LIMIT)


def _dot(a, b):
    return jnp.dot(a, b, preferred_element_type=F32)


def _dot_nt(a, b):
    return lax.dot_general(a, b, (((1,), (1,)), ((), ())), preferred_element_type=F32)


def _dot_tn(a, b):
    return lax.dot_general(a, b, (((0,), (0,)), ((), ())), preferred_element_type=F32)


def _layer_norm(z, w, b):
    mu = jnp.mean(z, axis=-1, keepdims=True)
    d = z - mu
    var = jnp.mean(d * d, axis=-1, keepdims=True)
    return d * lax.rsqrt(var + LN_EPS) * w + b


def _inproj_body(x_ref, w_ref, lnw_ref, lnb_ref, qa_ref, ka_ref, va_ref, kab_ref, vab_ref, qi_ref, ki_ref, kib_ref,
                 wi_ref, hb_ref):
    xb = x_ref[...].astype(BF16)

    def mm(c0, c1):
        return _dot(xb, w_ref[:, c0:c1])

    qa = (mm(COL_QA, COL_KA) * (HEAD_DIM_A ** -0.5 * LOG2_E)).astype(BF16)
    for h in range(N_HEADS_A):
        qa_ref[h] = qa[:, h * HEAD_DIM_A:(h + 1) * HEAD_DIM_A]
    for src, out3_ref, bf_ref in ((mm(COL_KA, COL_VA), ka_ref, kab_ref), (mm(COL_VA, COL_QI), va_ref, vab_ref)):
        for g in range(N_KV_A):
            out3_ref[:, g, :] = src[:, g * HEAD_DIM_A:(g + 1) * HEAD_DIM_A]
        bf_ref[...] = src.astype(BF16)
    qi = mm(COL_QI, COL_KW).astype(BF16)
    for h in range(N_IDX_HEADS):
        qi_ref[h] = qi[:, h * IDX_DIM:(h + 1) * IDX_DIM]
    kw = mm(COL_KW, COL_HB)
    ki = _layer_norm(kw[:, :IDX_DIM], lnw_ref[...], lnb_ref[...])
    ki_ref[...] = ki
    kib_ref[...] = ki.astype(BF16)
    wi_ref[...] = kw[:, IDX_DIM:IDX_DIM + N_IDX_HEADS]
    hb_ref[...] = mm(COL_HB, COL_END)


def _inproj(x2d, w_pad, lnw, lnb):
    n = x2d.shape[0]
    tm = min(512, n)
    row = lambda width: pl.BlockSpec((tm, width), lambda i: (i, 0))
    full = lambda a: pl.BlockSpec(a.shape, lambda i: (0,) * a.ndim)
    heads = lambda nh, d: pl.BlockSpec((nh, tm, d), lambda i: (0, i, 0))
    kv = N_KV_A * HEAD_DIM_A
    kv3 = pl.BlockSpec((tm, N_KV_A, HEAD_DIM_A), lambda i: (i, 0, 0))
    sds = jax.ShapeDtypeStruct
    return pl.pallas_call(
        _inproj_body,
        grid=(n // tm,),
        in_specs=[row(D_MODEL), full(w_pad), full(lnw), full(lnb)],
        out_specs=[heads(N_HEADS_A, HEAD_DIM_A), kv3, kv3, row(kv), row(kv), heads(N_IDX_HEADS, IDX_DIM),
                   row(IDX_DIM), row(IDX_DIM), row(N_IDX_HEADS), row(4 * WIDTH_B)],
        out_shape=[sds((N_HEADS_A, n, HEAD_DIM_A), BF16), sds((n, N_KV_A, HEAD_DIM_A), F32),
                   sds((n, N_KV_A, HEAD_DIM_A), F32), sds((n, kv), BF16), sds((n, kv), BF16),
                   sds((N_IDX_HEADS, n, IDX_DIM), BF16), sds((n, IDX_DIM), F32), sds((n, IDX_DIM), BF16),
                   sds((n, N_IDX_HEADS), F32), sds((n, 4 * WIDTH_B), F32)],
        compiler_params=_cparams("parallel"),
        name="inproj",
    )(x2d, w_pad, lnw, lnb)


def _order_bits_to_f32(u):
    key = u ^ INT_MIN
    bits = key ^ ((key >> 31) & 0x7FFFFFFF)
    f = lax.bitcast_convert_type(bits, F32)
    return jnp.where(u >= 0, jnp.where(u <= 0x007FFFFF, NEG_INF, f), f)


def _dsa_body(qa_ref, qi_ref, wi_ref, kidx_ref, k_ref, vt_ref, o_ref, sc_scr, bias_scr, term_scr, s_scr, acc_scr,
              *, s_pad, s_real, q_rows, pair, pos0, topk):
    kb_rows = KEY_BLOCK
    heads_per_kv = N_HEADS_A // N_KV_A
    qb = pair * q_rows
    cdim = pair * HEAD_DIM_A
    v_rows = pair * V_ROWS
    j = pl.program_id(1)
    q_lo = pos0 + j * q_rows
    lane = lax.broadcasted_iota(I32, (1, qb), 1)
    q_chunk = (q_lo + (lane & (q_rows - 1))) >> CHUNK_SHIFT
    k_lim = (((q_lo + q_rows - 1) >> CHUNK_SHIFT) + 1) * CHUNK
    nkb = jnp.minimum(s_pad // kb_rows, (k_lim + kb_rows - 1) // kb_rows)

    def rows_of(kb):
        return pl.ds(pl.multiple_of(kb * kb_rows, kb_rows), kb_rows)

    def fold8(a):
        return a.reshape(kb_rows // 8, 8, qb)

    def score_blk(kb, _):
        rows = rows_of(kb)
        kidx = kidx_ref[0, rows, :]
        for h in range(N_IDX_HEADS):
            term_scr[h] = jnp.maximum(_dot_nt(kidx, qi_ref[h, 0]), 0.0) * wi_ref[0, h:h + 1, :]
        total = (term_scr[0] + term_scr[1]) + (term_scr[2] + term_scr[3])
        spos = kb * kb_rows + lax.broadcasted_iota(I32, (kb_rows, qb), 0)
        sc = jnp.where((spos >> CHUNK_SHIFT) <= q_chunk, total * IDX_SCALE + 0.0, NEG_INF)
        if s_real < s_pad:
            sc = jnp.where(spos < s_real, sc, NEG_INF)
        sc_scr[rows, :] = sc
        return 0

    lax.fori_loop(0, nkb, score_blk, 0)

    def count(thr, strict):
        def body(kb, acc):
            blk = sc_scr[rows_of(kb), :]
            hit = (blk > thr) if strict else (blk >= thr)
            return acc + jnp.sum(fold8(jnp.where(hit, 1, 0).astype(I32)), axis=0)
        acc = lax.fori_loop(0, nkb, body, jnp.zeros((8, qb), I32))
        return jnp.sum(acc, axis=0, keepdims=True)

    def bit_step(i, carry):
        prefix, cnt = carry
        cand = prefix | lax.shift_left(jnp.int32(1), 31 - i)
        c = count(_order_bits_to_f32(cand), False)
        take = c >= topk
        return jnp.where(take, cand, prefix), jnp.where(take, c, cnt)

    prefix, cnt_ge = lax.fori_loop(0, 32, bit_step,
                                   (jnp.zeros((1, qb), I32), jnp.full((1, qb), nkb * kb_rows, I32)))
    tau = _order_bits_to_f32(prefix)
    finite_tau = tau > NEG_INF
    tau_floor = jnp.maximum(tau, jnp.finfo(F32).min)
    tie_lanes = jnp.where(finite_tau, jnp.where(cnt_ge > topk, 1, 0), 0)
    has_ties = jnp.max(tie_lanes) > 0

    @pl.when(jnp.logical_not(has_ties))
    def _():
        def body(kb, _):
            blk = sc_scr[rows_of(kb), :]
            bias_scr[rows_of(kb), :] = jnp.where(blk >= tau_floor, 0.0, NEG_INF)
            return 0
        lax.fori_loop(0, nkb, body, 0)

    @pl.when(has_ties)
    def _():
        need = jnp.where(finite_tau, (topk - count(tau, True)).astype(F32), 0.0)
        r = lax.broadcasted_iota(I32, (kb_rows, kb_rows), 0)
        c = lax.broadcasted_iota(I32, (kb_rows, kb_rows), 1)
        tril = jnp.where(r >= c, 1.0, 0.0).astype(BF16)

        def body(kb, seen):
            blk = sc_scr[rows_of(kb), :]
            eq = blk == tau
            rank = _dot(tril, jnp.where(eq, 1.0, 0.0).astype(BF16)) + seen
            tie_bias = jnp.where(eq, jnp.where(rank <= need, 0.0, NEG_INF), NEG_INF)
            bias_scr[rows_of(kb), :] = jnp.where(blk > tau, 0.0, tie_bias)
            return rank[kb_rows - 1:kb_rows, :]
        lax.fori_loop(0, nkb, body, jnp.zeros((1, qb), F32))

    def pass1(kb, m8):
        rows = rows_of(kb)
        bias = bias_scr[rows, :]
        out = []
        for h in range(N_HEADS_A):
            g = h // heads_per_kv
            s = _dot_nt(k_ref[0, rows, g * cdim:(g + 1) * cdim], qa_ref[h, 0]) + bias
            s_scr[h, rows, :] = s
            out.append(jnp.maximum(m8[h], jnp.max(fold8(s), axis=0)))
        return tuple(out)

    m8 = lax.fori_loop(0, nkb, pass1, tuple(jnp.full((8, qb), NEG_INF, F32) for _ in range(N_HEADS_A)))
    m = [jnp.max(x, axis=0, keepdims=True) for x in m8]

    acc_scr[...] = jnp.zeros_like(acc_scr)

    def pass2(kb, _):
        rows = rows_of(kb)
        for h in range(N_HEADS_A):
            g = h // heads_per_kv
            p = jnp.exp2(s_scr[h, rows, :] - m[h]).astype(BF16)
            acc_scr[h] += _dot(vt_ref[0, g * v_rows:(g + 1) * v_rows, rows], p)
        return 0

    lax.fori_loop(0, nkb, pass2, 0)
    for h in range(N_HEADS_A):
        a = acc_scr[h]
        num, den = a[:HEAD_DIM_A], a[HEAD_DIM_A:HEAD_DIM_A + 1]
        for mb in range(1, pair):
            mine = lane >= mb * q_rows
            r0 = mb * V_ROWS
            num = jnp.where(mine, a[r0:r0 + HEAD_DIM_A], num)
            den = jnp.where(mine, a[r0 + HEAD_DIM_A:r0 + HEAD_DIM_A + 1], den)
        o_ref[0, h * HEAD_DIM_A:(h + 1) * HEAD_DIM_A, :] = (num * (1.0 / den)).astype(BF16)


def _dsa_pairing(b, t):
    return 2 if (2 * t <= 128 and b % 2 == 0) else 1


def _dsa(qa_h, qi_h, wi, keys, *, s_real, pos0, topk):
    _, b, t, _ = qa_h.shape
    s_pad = keys[0].shape[1]
    q_rows = min(256, t)
    pair = _dsa_pairing(b, t)
    bp = b // pair
    lanes, cdim, v_rows = pair * q_rows, pair * HEAD_DIM_A, pair * V_ROWS

    def block_diag(q):
        if pair == 1:
            return q
        nh, d = q.shape[0], q.shape[3]
        eye = jnp.eye(pair, dtype=q.dtype)
        q6 = q.reshape(nh, bp, pair, t, 1, d) * eye.reshape(1, 1, pair, 1, pair, 1)
        return q6.reshape(nh, bp, pair * t, pair * d)

    wi_op = wi.reshape(bp, pair, t, N_IDX_HEADS).transpose(0, 3, 1, 2).reshape(bp, N_IDX_HEADS, pair * t)
    ki_all, k_all, v_all = keys
    ki_op = ki_all.reshape(bp, pair, s_pad, IDX_DIM).transpose(0, 2, 1, 3).reshape(bp, s_pad, cdim)
    k_op = (k_all.reshape(bp, pair, s_pad, N_KV_A, HEAD_DIM_A).transpose(0, 2, 3, 1, 4)
            .reshape(bp, s_pad, N_KV_A * cdim))
    vt = v_all.reshape(b, s_pad, N_KV_A, HEAD_DIM_A).transpose(0, 2, 3, 1)
    vt = jnp.concatenate([vt, jnp.ones((b, N_KV_A, V_ROWS - HEAD_DIM_A, s_pad), vt.dtype)], axis=2)
    vt_op = (vt.reshape(bp, pair, N_KV_A, V_ROWS, s_pad).transpose(0, 2, 1, 3, 4)
             .reshape(bp, N_KV_A * v_rows, s_pad))

    body = functools.partial(_dsa_body, s_pad=s_pad, s_real=s_real, q_rows=q_rows, pair=pair, pos0=pos0,
                             topk=topk)
    o_t = pl.pallas_call(
        body,
        grid=(bp, t // q_rows),
        in_specs=[
            pl.BlockSpec((N_HEADS_A, 1, lanes, cdim), lambda i, j: (0, i, j, 0)),
            pl.BlockSpec((N_IDX_HEADS, 1, lanes, cdim), lambda i, j: (0, i, j, 0)),
            pl.BlockSpec((1, N_IDX_HEADS, lanes), lambda i, j: (i, 0, j)),
            pl.BlockSpec((1, s_pad, cdim), lambda i, j: (i, 0, 0)),
            pl.BlockSpec((1, s_pad, N_KV_A * cdim), lambda i, j: (i, 0, 0)),
            pl.BlockSpec((1, N_KV_A * v_rows, s_pad), lambda i, j: (i, 0, 0)),
        ],
        out_specs=pl.BlockSpec((1, WIDTH_A, lanes), lambda i, j: (i, 0, j)),
        out_shape=jax.ShapeDtypeStruct((bp, WIDTH_A, pair * t), BF16),
        scratch_shapes=[pltpu.VMEM((s_pad, lanes), F32), pltpu.VMEM((s_pad, lanes), F32),
                        pltpu.VMEM((N_IDX_HEADS, KEY_BLOCK, lanes), F32),
                        pltpu.VMEM((N_HEADS_A, s_pad, lanes), F32), pltpu.VMEM((N_HEADS_A, v_rows, lanes), F32)],
        compiler_params=_cparams("parallel", "parallel"),
        name="dsa",
    )(block_diag(qa_h), block_diag(qi_h), wi_op, ki_op, k_op, vt_op)
    return o_t.reshape(bp, WIDTH_A, pair, t).transpose(0, 2, 3, 1).reshape(b * t, WIDTH_A)


def _split3(a):
    hi = a.astype(BF16)
    r1 = a - hi.astype(F32)
    mid = r1.astype(BF16)
    lo = (r1 - mid.astype(F32)).astype(BF16)
    return hi, mid, lo


def _hgrn_body(hb_ref, s0_ref, lb_ref, nw_ref, ob_ref, sfin_ref, state_scr, *, tb, nb):
    t = pl.program_id(1)

    @pl.when(t == 0)
    def _():
        for bi, h in itertools.product(range(nb), range(N_HEADS_B)):
            state_scr[bi, h] = s0_ref[bi, h].T

    lb = lb_ref[...]
    r = lax.broadcasted_iota(I32, (CHUNK, CHUNK), 0)
    c = lax.broadcasted_iota(I32, (CHUNK, CHUNK), 1)
    causal = r >= c
    tril = jnp.where(causal, 1.0, 0.0).astype(BF16)

    for ci, bi in itertools.product(range(tb // CHUNK), range(nb)):
        rows = slice(ci * CHUNK, (ci + 1) * CHUNK)
        f = lb + (1.0 - lb) * jax.nn.sigmoid(hb_ref[bi, rows, WIDTH_B:2 * WIDTH_B])
        parts = _split3(jnp.log(f))
        bcum = _dot(tril, parts[0]) + _dot(tril, parts[1]) + _dot(tril, parts[2])
        for h in range(N_HEADS_B):
            lanes = slice(h * HEAD_DIM_B, (h + 1) * HEAD_DIM_B)
            q = hb_ref[bi, rows, lanes]
            k = 1.0 - f[:, lanes]
            v = hb_ref[bi, rows, 2 * WIDTH_B + h * HEAD_DIM_B:2 * WIDTH_B + (h + 1) * HEAD_DIM_B]
            gate = hb_ref[bi, rows, 3 * WIDTH_B + h * HEAD_DIM_B:3 * WIDTH_B + (h + 1) * HEAD_DIM_B]
            bh = bcum[:, lanes]
            b_last = bh[CHUNK - 1:CHUNK, :]
            vb = v.astype(BF16)
            state_t = state_scr[bi, h]
            o_inter = _dot_nt((q * jnp.exp(bh)).astype(BF16), state_t.astype(BF16))
            a_rows = []
            for i in range(CHUNK // HGRN_SUB):
                lo, hi = i * HGRN_SUB, (i + 1) * HGRN_SUB
                ref = bh[lo - 1:lo, :] if i else jnp.zeros((1, HEAD_DIM_B), F32)
                qs = (q[lo:hi] * jnp.exp(bh[lo:hi] - ref)).astype(BF16)
                ks = (k * jnp.exp(ref - bh)).astype(BF16)
                a_rows.append(_dot_nt(qs, ks))
            a = jnp.where(causal, jnp.concatenate(a_rows, axis=0), 0.0)
            o = o_inter + _dot(a.astype(BF16), vb)
            kdec = (k * jnp.exp(b_last - bh)).astype(BF16)
            state_scr[bi, h] = jnp.exp(b_last) * state_t + _dot_tn(vb, kdec)
            ms = jnp.mean(o * o, axis=-1, keepdims=True)
            y = o * lax.rsqrt(ms + RMS_EPS) * nw_ref[...] * (gate * jax.nn.sigmoid(gate))
            ob_ref[bi, rows, lanes] = y.astype(BF16)

    @pl.when(t == pl.num_programs(1) - 1)
    def _():
        for bi, h in itertools.product(range(nb), range(N_HEADS_B)):
            sfin_ref[bi, h] = state_scr[bi, h].T


def _hgrn(hb, s0, lb, nw):
    b, t, _ = hb.shape
    tb = min(256, t)
    nb = 2 if b % 2 == 0 else 1
    return pl.pallas_call(
        functools.partial(_hgrn_body, tb=tb, nb=nb),
        grid=(b // nb, t // tb),
        in_specs=[
            pl.BlockSpec((nb, tb, 4 * WIDTH_B), lambda i, j: (i, j, 0)),
            pl.BlockSpec((nb, N_HEADS_B, HEAD_DIM_B, HEAD_DIM_B), lambda i, j: (i, 0, 0, 0)),
            pl.BlockSpec((1, WIDTH_B), lambda i, j: (0, 0)),
            pl.BlockSpec((1, HEAD_DIM_B), lambda i, j: (0, 0)),
        ],
        out_specs=[
            pl.BlockSpec((nb, tb, WIDTH_B), lambda i, j: (i, j, 0)),
            pl.BlockSpec((nb, N_HEADS_B, HEAD_DIM_B, HEAD_DIM_B), lambda i, j: (i, 0, 0, 0)),
        ],
        out_shape=[jax.ShapeDtypeStruct((b, t, WIDTH_B), BF16),
                   jax.ShapeDtypeStruct((b, N_HEADS_B, HEAD_DIM_B, HEAD_DIM_B), F32)],
        scratch_shapes=[pltpu.VMEM((nb, N_HEADS_B, HEAD_DIM_B, HEAD_DIM_B), F32)],
        compiler_params=_cparams("parallel", "arbitrary"),
        name="hgrn",
    )(hb, s0, lb, nw)


def _outproj_body(oa_ref, ob_ref, xa_ref, xb_ref, w_ref, lnw_ref, lnb_ref, wrh_ref, wrl_ref, br_ref, x1_ref, lg_ref,
                  *, steps_a):
    x = jnp.where(pl.program_id(0) < steps_a, xa_ref[...], xb_ref[...])
    y = _dot(oa_ref[...], w_ref[:WIDTH_A, :]) + _dot(ob_ref[...], w_ref[WIDTH_A:, :])
    x1 = _layer_norm(DN_ALPHA * x + y, lnw_ref[...], lnb_ref[...])
    x1_ref[...] = x1
    hi = x1.astype(BF16)
    lo = (x1 - hi.astype(F32)).astype(BF16)
    lg_ref[...] = (_dot_nt(wrh_ref[...], hi) + _dot_nt(wrh_ref[...], lo) + _dot_nt(wrl_ref[...], hi)
                   + br_ref[...])


def _two_group_rows(tm, width, steps_a):
    return (pl.BlockSpec((tm, width), lambda i, *_: (jnp.minimum(i, steps_a - 1), 0)),
            pl.BlockSpec((tm, width), lambda i, *_: (jnp.maximum(i - steps_a, 0), 0)))


def _outproj(oa, ob, x_a, x_b, w_out, lnw, lnb, wr_hi, wr_lo, br):
    n = oa.shape[0]
    tm = MOE_TILE
    assert x_a.shape[0] % tm == 0 and x_b.shape[0] % tm == 0 and x_a.shape[0] + x_b.shape[0] == n
    steps_a = x_a.shape[0] // tm
    row = lambda width: pl.BlockSpec((tm, width), lambda i: (i, 0))
    full = lambda a: pl.BlockSpec(a.shape, lambda i: (0,) * a.ndim)
    return pl.pallas_call(
        functools.partial(_outproj_body, steps_a=steps_a),
        grid=(n // tm,),
        in_specs=[row(WIDTH_A), row(WIDTH_B), *_two_group_rows(tm, D_MODEL, steps_a), full(w_out), full(lnw),
                  full(lnb), full(wr_hi), full(wr_lo), full(br)],
        out_specs=[row(D_MODEL), pl.BlockSpec((N_EXPERTS, tm), lambda i: (0, i))],
        out_shape=[jax.ShapeDtypeStruct((n, D_MODEL), F32), jax.ShapeDtypeStruct((N_EXPERTS, n), F32)],
        compiler_params=_cparams("parallel"),
        name="outproj",
    )(oa, ob, x_a, x_b, w_out, lnw, lnb, wr_hi, wr_lo, br)


def _route_body(lg_ref, lpos_ref, gate_ref, tc_ref, base_ref, cnt_ref, cnt_scr, *, tr):
    @pl.when(pl.program_id(0) == 0)
    def _():
        cnt_scr[...] = jnp.zeros_like(cnt_scr)

    l = lg_ref[...]
    rows = lax.broadcasted_iota(I32, (N_EXPERTS, tr), 0)
    vals, hots = [], []
    for j in range(TOP_K):
        m = jnp.max(l, axis=0, keepdims=True)
        idx = jnp.min(jnp.where(l == m, rows, N_EXPERTS), axis=0, keepdims=True)
        hot = rows == idx
        vals.append(m)
        hots.append(hot)
        l = jnp.where(hot, NEG_INF, l)
    es = [jnp.exp(v - vals[0]) for v in vals]
    inv = 1.0 / (es[0] + es[1] + es[2] + es[3])
    for j in range(TOP_K):
        gate_ref[j:j + 1, :] = es[j] * inv
    chosen = jnp.zeros((N_EXPERTS, tr), F32)
    for hot in hots:
        chosen = chosen + jnp.where(hot, 1.0, 0.0)
    r = lax.broadcasted_iota(I32, (tr, tr), 0)
    c = lax.broadcasted_iota(I32, (tr, tr), 1)
    before = jnp.where(r < c, 1.0, 0.0).astype(BF16)
    prior = _dot(chosen.astype(BF16), before)
    units = jnp.ceil(jnp.sum(chosen, axis=1, keepdims=True) * (1.0 / ROW_ALIGN))
    er = lax.broadcasted_iota(I32, (N_EXPERTS, N_EXPERTS), 0)
    ec = lax.broadcasted_iota(I32, (N_EXPERTS, N_EXPERTS), 1)
    earlier = jnp.where(ec < er, 1.0, 0.0).astype(BF16)
    run_len = jnp.broadcast_to(units, (N_EXPERTS, 128)) * ROW_ALIGN
    run_start = _dot(earlier, jnp.broadcast_to(units, (N_EXPERTS, 128)).astype(BF16)) * ROW_ALIGN
    where_in_tile = prior + run_start[:, 0:1]
    for j in range(TOP_K):
        lpos_ref[j:j + 1, :] = jnp.sum(jnp.where(hots[j], where_in_tile, 0.0), axis=0,
                                       keepdims=True).astype(I32)
    tc_ref[0] = run_len.astype(I32)
    base_ref[0] = cnt_scr[...].astype(I32)
    cnt_scr[...] = cnt_scr[...] + run_len
    cnt_ref[...] = cnt_scr[...].astype(I32)


def _route(logits_t):
    n = logits_t.shape[1]
    tr = MOE_TILE
    tiles = n // tr
    tok = lambda rows: pl.BlockSpec((rows, tr), lambda i: (0, i))
    per_tile = pl.BlockSpec((1, N_EXPERTS, 128), lambda i: (i, 0, 0))
    return pl.pallas_call(
        functools.partial(_route_body, tr=tr),
        grid=(tiles,),
        in_specs=[tok(N_EXPERTS)],
        out_specs=[tok(TOP_K), tok(TOP_K), per_tile, per_tile, pl.BlockSpec((N_EXPERTS, 128), lambda i: (0, 0))],
        out_shape=[jax.ShapeDtypeStruct((TOP_K, n), I32), jax.ShapeDtypeStruct((TOP_K, n), F32),
                   jax.ShapeDtypeStruct((tiles, N_EXPERTS, 128), I32),
                   jax.ShapeDtypeStruct((tiles, N_EXPERTS, 128), I32),
                   jax.ShapeDtypeStruct((N_EXPERTS, 128), I32)],
        scratch_shapes=[pltpu.VMEM((N_EXPERTS, 128), F32)],
        compiler_params=_cparams("arbitrary"),
        name="route",
    )(logits_t)


def _pack_pairs(a):
    half = a.shape[1] // 2
    hi = lax.bitcast_convert_type(a[:, :half], I32)
    lo = lax.shift_right_logical(lax.bitcast_convert_type(a[:, half:], I32), 16)
    return hi | lo


def _unpack_pairs(pk):
    hi = lax.bitcast_convert_type(pk & jnp.int32(-65536), F32).astype(BF16)
    lo = lax.bitcast_convert_type(pk << 16, F32).astype(BF16)
    return hi, lo


def _for_each_run_piece(tc_ref, g0_ref, t, fn):
    def per_expert(e, local):
        c = tc_ref[t, e]
        glob = g0_ref[t, e]
        lo = local
        for p in RUN_PIECES:
            take = (c & p) != 0

            @pl.when(take)
            def _():
                fn(pl.multiple_of(lo, ROW_ALIGN), pl.multiple_of(glob, ROW_ALIGN), p)

            step = jnp.where(take, p, 0)
            lo = lo + step
            glob = glob + step
        return local + c

    lax.fori_loop(0, N_EXPERTS, per_expert, 0)


def _dispatch_body(tc_ref, g0_ref, seg_ref, lpos_ref, x_ref, buf_ref, stage, zero_scr, zsem, sems, *, n_blocks):
    i = pl.program_id(0)
    slot = i % 2

    def run_copy(slot_, lo, glob, rows):
        return pltpu.make_async_copy(stage.at[slot_, pl.ds(lo, rows), :], buf_ref.at[pl.ds(glob, rows), :],
                                     sems.at[slot_])

    @pl.when(i == 0)
    def _():
        zero_scr[...] = jnp.zeros_like(zero_scr)
        n_used = seg_ref[1, N_EXPERTS - 1] // EXPERT_ROWS

        def block_copy(start):
            return pltpu.make_async_copy(
                zero_scr, buf_ref.at[pl.ds(pl.multiple_of(start, EXPERT_ROWS), EXPERT_ROWS), :], zsem)

        def clears(action):
            for e in range(N_EXPERTS):
                @pl.when(seg_ref[1, e] > seg_ref[0, e])
                def _():
                    action(block_copy(seg_ref[1, e] - EXPERT_ROWS))

                @pl.when(n_used + e < n_blocks)
                def _():
                    action(block_copy((n_used + e) * EXPERT_ROWS))

        clears(lambda cp: cp.start())
        clears(lambda cp: cp.wait())

    r = lax.broadcasted_iota(I32, (MOE_SLOTS, MOE_TILE), 0)
    onehot = jnp.zeros((MOE_SLOTS, MOE_TILE), F32)
    for j in range(TOP_K):
        onehot = jnp.where(r == lpos_ref[j:j + 1, :], 1.0, onehot)
    stage[slot] = _pack_pairs(_dot(onehot.astype(BF16), x_ref[...].astype(BF16)))

    @pl.when(i > 0)
    def _():
        _for_each_run_piece(tc_ref, g0_ref, i - 1,
                            lambda lo, glob, rows: run_copy(1 - slot, lo, glob, rows).wait())

    _for_each_run_piece(tc_ref, g0_ref, i, lambda lo, glob, rows: run_copy(slot, lo, glob, rows).start())

    @pl.when(i == pl.num_programs(0) - 1)
    def _():
        _for_each_run_piece(tc_ref, g0_ref, i, lambda lo, glob, rows: run_copy(slot, lo, glob, rows).wait())


def _dispatch(x1, lpos_t, tc, g0, seg, n_rows):
    n = x1.shape[0]
    return pl.pallas_call(
        functools.partial(_dispatch_body, n_blocks=n_rows // EXPERT_ROWS),
        grid_spec=pltpu.PrefetchScalarGridSpec(
            num_scalar_prefetch=3,
            grid=(n // MOE_TILE,),
            in_specs=[pl.BlockSpec((TOP_K, MOE_TILE), lambda i, *_: (0, i)),
                      pl.BlockSpec((MOE_TILE, D_MODEL), lambda i, *_: (i, 0))],
            out_specs=pl.BlockSpec(memory_space=pl.ANY),
            scratch_shapes=[pltpu.VMEM((2, MOE_SLOTS, D_MODEL // 2), I32),
                            pltpu.VMEM((EXPERT_ROWS, D_MODEL // 2), I32),
                            pltpu.SemaphoreType.DMA(()), pltpu.SemaphoreType.DMA((2,))],
        ),
        out_shape=jax.ShapeDtypeStruct((n_rows, D_MODEL // 2), I32),
        compiler_params=_cparams("arbitrary"),
        name="dispatch",
    )(tc, g0, seg, lpos_t, x1)


def _experts_body(be_ref, nu_ref, x_ref, wgu_ref, bgu_ref, wd_ref, bd_ref, o_ref, wgu_bf, wd_bf):
    i = pl.program_id(0)

    @pl.when(jnp.logical_or(i == 0, be_ref[i] != be_ref[jnp.maximum(i - 1, 0)]))
    def _():
        for r in range(0, D_MODEL, 256):
            wgu_bf[r:r + 256, :] = wgu_ref[0, r:r + 256, :].astype(BF16)
            wd_bf[r:r + 256, :] = wd_ref[0, r:r + 256, :].astype(BF16)

    @pl.when(i < nu_ref[0])
    def _():
        half = D_MODEL // 2
        x_a, x_b = _unpack_pairs(x_ref[...])
        gu = _dot(x_a, wgu_bf[:half, :]) + _dot(x_b, wgu_bf[half:, :]) + bgu_ref[0]
        gate = jnp.minimum(gu[:, :D_EXPERT], SWIGLU_LIMIT)
        up = jnp.clip(gu[:, D_EXPERT:], -SWIGLU_LIMIT, SWIGLU_LIMIT)
        hdn = (up + 1.0) * gate * jax.nn.sigmoid(SWIGLU_ALPHA * gate)
        out = _dot(hdn.astype(BF16), wd_bf[...]) + bd_ref[0]
        o_ref[...] = _pack_pairs(out.astype(BF16).astype(F32))

    @pl.when(i >= nu_ref[0])
    def _():
        o_ref[...] = jnp.zeros_like(o_ref)


def _experts(buf, block_e, n_used, w_gu, b_gu, w_down, b_down):
    n_blocks = buf.shape[0] // EXPERT_ROWS
    rows = pl.BlockSpec((EXPERT_ROWS, D_MODEL // 2), lambda i, be, nu: (jnp.minimum(i, nu[0] - 1), 0))
    per_e = lambda a: pl.BlockSpec((1,) + a.shape[1:], lambda i, be, nu: (be[i],) + (0,) * (a.ndim - 1))
    return pl.pallas_call(
        _experts_body,
        grid_spec=pltpu.PrefetchScalarGridSpec(
            num_scalar_prefetch=2,
            grid=(n_blocks,),
            in_specs=[rows, per_e(w_gu), per_e(b_gu), per_e(w_down), per_e(b_down)],
            out_specs=pl.BlockSpec((EXPERT_ROWS, D_MODEL // 2), lambda i, be, nu: (i, 0)),
            scratch_shapes=[pltpu.VMEM((D_MODEL, 2 * D_EXPERT), BF16), pltpu.VMEM((D_EXPERT, D_MODEL), BF16)],
        ),
        out_shape=jax.ShapeDtypeStruct(buf.shape, I32),
        compiler_params=_cparams("arbitrary"),
        name="experts",
    )(block_e, n_used, buf, w_gu, b_gu, w_down, b_down)


def _final_body(tc_ref, g0_ref, lpos_ref, gate_ref, x1_ref, pa_ref, pb_ref, lnw_ref, lnb_ref, wg_ref, bg_ref, wp_ref,
                eo_ref, ya_ref, yb_ref, stage, sems, *, steps_a):
    i = pl.program_id(0)
    slot = i % 2

    def run_copy(slot_, lo, glob, rows):
        return pltpu.make_async_copy(eo_ref.at[pl.ds(glob, rows), :], stage.at[slot_, pl.ds(lo, rows), :],
                                     sems.at[slot_])

    def fetch(tile, slot_):
        stage[slot_] = jnp.zeros(stage.shape[1:], stage.dtype)
        _for_each_run_piece(tc_ref, g0_ref, tile, lambda lo, glob, rows: run_copy(slot_, lo, glob, rows).start())

    @pl.when(i == 0)
    def _():
        fetch(0, 0)

    @pl.when(i + 1 < pl.num_programs(0))
    def _():
        fetch(i + 1, 1 - slot)

    _for_each_run_piece(tc_ref, g0_ref, i, lambda lo, glob, rows: run_copy(slot, lo, glob, rows).wait())

    r = lax.broadcasted_iota(I32, (MOE_TILE, MOE_SLOTS), 1)
    g = jnp.zeros((MOE_TILE, MOE_SLOTS), F32)
    for j in range(TOP_K):
        g = jnp.where(r == lpos_ref[:, j:j + 1], gate_ref[:, j:j + 1], g)
    g_hi = g.astype(BF16)
    g_lo = (g - g_hi.astype(F32)).astype(BF16)
    g2 = jnp.concatenate([g_hi, g_lo], axis=0)
    e_a, e_b = _unpack_pairs(stage[slot])
    r_a, r_b = _dot(g2, e_a), _dot(g2, e_b)
    moe = jnp.concatenate([r_a[:MOE_TILE] + r_a[MOE_TILE:], r_b[:MOE_TILE] + r_b[MOE_TILE:]], axis=1)
    x2 = _layer_norm(DN_ALPHA * x1_ref[...] + moe, lnw_ref[...], lnb_ref[...])
    ple_gate = jax.nn.sigmoid(_dot(x2.astype(BF16), wg_ref[...]) + bg_ref[...])
    p = jnp.where(i < steps_a, pa_ref[...], pb_ref[...])
    y = x2 + ple_gate * _dot(p.astype(BF16), wp_ref[...])

    @pl.when(i < steps_a)
    def _():
        ya_ref[...] = y

    @pl.when(i >= steps_a)
    def _():
        yb_ref[...] = y


def _final(tc, g0, lpos, gate, x1, p_a, p_b, lnw, lnb, w_gate, b_gate, w_ple, expert_out):
    n = x1.shape[0]
    steps_a = p_a.shape[0] // MOE_TILE
    row = lambda width: pl.BlockSpec((MOE_TILE, width), lambda i, *_: (i, 0))
    full = lambda a: pl.BlockSpec(a.shape, lambda i, *_: (0,) * a.ndim)
    return pl.pallas_call(
        functools.partial(_final_body, steps_a=steps_a),
        grid_spec=pltpu.PrefetchScalarGridSpec(
            num_scalar_prefetch=2,
            grid=(n // MOE_TILE,),
            in_specs=[row(TOP_K), row(TOP_K), row(D_MODEL), *_two_group_rows(MOE_TILE, D_PLE, steps_a), full(lnw),
                      full(lnb), full(w_gate), full(b_gate), full(w_ple), pl.BlockSpec(memory_space=pl.ANY)],
            out_specs=list(_two_group_rows(MOE_TILE, D_MODEL, steps_a)),
            scratch_shapes=[pltpu.VMEM((2, MOE_SLOTS, D_MODEL // 2), I32), pltpu.SemaphoreType.DMA((2,))],
        ),
        out_shape=[jax.ShapeDtypeStruct((p_a.shape[0], D_MODEL), F32),
                   jax.ShapeDtypeStruct((p_b.shape[0], D_MODEL), F32)],
        compiler_params=_cparams("arbitrary"),
        name="combine_final",
    )(tc, g0, lpos, gate, x1, p_a, p_b, lnw, lnb, w_gate, b_gate, w_ple, expert_out)


def _round_up(a, m):
    return -(-a // m) * m


def _token_mixers(x, hist_k, hist_v, hist_kidx, s0, pos0, wts):
    b, t, _ = x.shape
    n = b * t
    x2d = x.reshape(n, D_MODEL)

    qa, ka, va, ka_bf, va_bf, qi, ki, ki_bf, wi, hb = _inproj(x2d, wts["w_in"], wts["idx_lnw"], wts["idx_lnb"])

    k_new = ka_bf.reshape(b, t, N_KV_A * HEAD_DIM_A)
    v_new = va_bf.reshape(b, t, N_KV_A * HEAD_DIM_A)
    ki_new = ki_bf.reshape(b, t, IDX_DIM)
    q_ops = (qa.reshape(N_HEADS_A, b, t, HEAD_DIM_A), qi.reshape(N_IDX_HEADS, b, t, IDX_DIM),
             wi.reshape(b, t, N_IDX_HEADS))
    past = 0 if hist_k is None else hist_k.shape[1]
    s_real = past + t
    topk = min(TOPK_MAX, s_real // 4)
    if past:
        k_all = jnp.concatenate([hist_k.reshape(b, past, -1).astype(BF16), k_new], axis=1)
        v_all = jnp.concatenate([hist_v.reshape(b, past, -1).astype(BF16), v_new], axis=1)
        ki_all = jnp.concatenate([hist_kidx.astype(BF16), ki_new], axis=1)
    else:
        k_all, v_all, ki_all = k_new, v_new, ki_new
    pad = ((0, 0), (0, _round_up(s_real, KEY_BLOCK) - s_real), (0, 0))
    keys = tuple(jnp.pad(a, pad) for a in (ki_all, k_all, v_all))
    oa = _dsa(*q_ops, keys, s_real=s_real, pos0=pos0, topk=topk)

    ob, s_fin = _hgrn(hb.reshape(b, t, 4 * WIDTH_B), s0, wts["lb"], wts["hgrn_nw"])
    return (oa, ob.reshape(n, WIDTH_B), ka.reshape(b, t, N_KV_A, HEAD_DIM_A), va.reshape(b, t, N_KV_A, HEAD_DIM_A),
            ki.reshape(b, t, IDX_DIM), s_fin)


def _channel_mixer(mix_a, mix_b, x_a, x_b, p_a, p_b, wts):
    shape_a, shape_b = x_a.shape, x_b.shape
    x_a, x_b = x_a.reshape(-1, D_MODEL), x_b.reshape(-1, D_MODEL)
    oa = jnp.concatenate([mix_a[0], mix_b[0]], axis=0)
    ob = jnp.concatenate([mix_a[1], mix_b[1]], axis=0)
    n = oa.shape[0]
    x1, logits_t = _outproj(oa, ob, x_a, x_b, wts["w_out"], wts["ln1_w"], wts["ln1_b"],
                            wts["wr_hi"], wts["wr_lo"], wts["b_router"])

    tiles = n // MOE_TILE
    lpos_t, gate_t, tile_runs, tile_base, counts = _route(logits_t)
    tc, counts = tile_runs[:, :, 0], counts[:, 0]
    padded = (counts + EXPERT_ROWS - 1) // EXPERT_ROWS * EXPERT_ROWS
    seg_end = jnp.cumsum(padded)
    seg_start = seg_end - padded
    g0 = (seg_start[None, :] + tile_base[:, :, 0]).astype(I32)
    n_blocks = -(-(n * TOP_K + N_EXPERTS * tiles * (ROW_ALIGN - 1) + N_EXPERTS * (EXPERT_ROWS - 1))
                 // EXPERT_ROWS)
    block_row0 = jnp.arange(n_blocks, dtype=I32) * EXPERT_ROWS
    block_e = jnp.minimum(jnp.sum((seg_end[None, :] <= block_row0[:, None]).astype(I32), axis=1), N_EXPERTS - 1)
    n_used = (seg_end[-1:] // EXPERT_ROWS).astype(I32)
    seg = jnp.stack([seg_start, seg_end]).astype(I32)
    buf = _dispatch(x1, lpos_t, tc, g0, seg, n_blocks * EXPERT_ROWS)
    expert_out = _experts(buf, block_e, n_used, wts["w_gu"], wts["b_gu"], wts["w_down"], wts["b_down"])
    y_a, y_b = _final(tc, g0, lpos_t.T, gate_t.T, x1, p_a.reshape(-1, D_PLE), p_b.reshape(-1, D_PLE),
                      wts["ln2_w"], wts["ln2_b"], wts["w_ple_gate"], wts["b_ple_gate"], wts["w_ple"], expert_out)
    return y_a.reshape(shape_a), y_b.reshape(shape_b)


def _prep_weights(w_in, w_out, idx_k_norm_w, idx_k_norm_b, lb, hgrn_norm_w, ln1_w, ln1_b, w_router, b_router,
                  w_gu, b_gu, w_down, b_down, ln2_w, ln2_b, w_ple, w_ple_gate, b_ple_gate):
    n_a = COL_KW + IDX_DIM + N_IDX_HEADS
    w_pad = jnp.concatenate(
        [w_in[:, :n_a], jnp.zeros((D_MODEL, COL_HB - n_a), w_in.dtype), w_in[:, n_a:]], axis=1).astype(BF16)
    wr_t = w_router.T
    wr_hi = wr_t.astype(BF16)
    row = lambda a: a.reshape(1, -1)
    return dict(
        w_in=w_pad, idx_lnw=row(idx_k_norm_w), idx_lnb=row(idx_k_norm_b), lb=row(lb), hgrn_nw=row(hgrn_norm_w),
        w_out=w_out.astype(BF16), ln1_w=row(ln1_w), ln1_b=row(ln1_b),
        wr_hi=wr_hi, wr_lo=(wr_t - wr_hi.astype(F32)).astype(BF16), b_router=b_router.reshape(N_EXPERTS, 1),
        w_gu=w_gu, b_gu=b_gu.reshape(N_EXPERTS, 1, 2 * D_EXPERT),
        w_down=w_down, b_down=b_down.reshape(N_EXPERTS, 1, D_MODEL),
        ln2_w=row(ln2_w), ln2_b=row(ln2_b), w_ple=w_ple.astype(BF16), w_ple_gate=w_ple_gate.astype(BF16),
        b_ple_gate=row(b_ple_gate))


def kernel(x_prompt, x_sample, cache_k, cache_v, cache_kidx, state_hgrn, p_prompt, p_sample, w_in, w_out,
           idx_k_norm_w, idx_k_norm_b, hgrn_lb_logits, hgrn_norm_w, ln1_w, ln1_b, w_router, b_router, w_gu, b_gu,
           w_down, b_down, ln2_w, ln2_b, w_ple, w_ple_gate, b_ple_gate):
    lb_all = jnp.cumsum(jax.nn.softmax(hgrn_lb_logits.astype(F32), axis=0), axis=0)
    xp, xs = x_prompt, x_sample
    outs = [[] for _ in range(8)]
    for i in range(DEPTH):
        wts = _prep_weights(w_in[i], w_out[i], idx_k_norm_w[i], idx_k_norm_b[i], lb_all[i], hgrn_norm_w[i],
                            ln1_w[i], ln1_b[i], w_router[i], b_router[i], w_gu[i], b_gu[i], w_down[i], b_down[i],
                            ln2_w[i], ln2_b[i], w_ple[i], w_ple_gate[i], b_ple_gate[i])
        s0p = jnp.zeros((xp.shape[0], N_HEADS_B, HEAD_DIM_B, HEAD_DIM_B), F32)
        mix_p = _token_mixers(xp, None, None, None, s0p, 0, wts)
        mix_s = _token_mixers(xs, cache_k[i], cache_v[i], cache_kidx[i], state_hgrn[i], cache_k.shape[2], wts)
        xp, xs = _channel_mixer(mix_p, mix_s, xp, xs, p_prompt[i], p_sample[i], wts)
        for lst, val in zip(outs, mix_p[2:] + mix_s[2:]):
            lst.append(val)
    return (xp, xs) + tuple(jnp.stack(l) for l in outs)
```

```python
import functools
import itertools

import jax
import jax.numpy as jnp
from jax import lax
from jax.experimental import pallas as pl
from jax.experimental.pallas import tpu as pltpu

F32 = jnp.float32
BF16 = jnp.bfloat16
I32 = jnp.int32

D_MODEL = 1024
CHUNK = 64
CHUNK_SHIFT = 6
WIDTH_A = 512
HEAD_DIM_A = 64
N_HEADS_A = 8
N_KV_A = 2
N_IDX_HEADS = 4
IDX_DIM = 64
IDX_SCALE = IDX_DIM ** -0.5 * N_IDX_HEADS ** -0.5
TOPK_MAX = 256
WIDTH_B = 512
HEAD_DIM_B = 128
N_HEADS_B = 4
N_EXPERTS = 32
TOP_K = 4
D_EXPERT = 1024
SWIGLU_LIMIT = 7.0
SWIGLU_ALPHA = 1.702
D_PLE = 256
LN_EPS = 1e-5
RMS_EPS = 1e-6
DEPTH = 1
DN_ALPHA = (2 * DEPTH) ** 0.25

COL_QA, COL_KA, COL_VA, COL_QI, COL_KW, COL_HB, COL_END = 0, 512, 640, 768, 1024, 1152, 3200

VMEM_LIMIT = 56 * 1024 * 1024
INPROJ_ROWS = 1024
KEY_BLOCK = 256
V_ROWS = 80
LOG2_E = 1.4426950408889634
HGRN_SUB = 16
EXPERT_ROWS = 512
MOE_TILE = 512
ROW_ALIGN = 8
RUN_PIECES = (512, 256, 128, 64, 32, 16, 8)
MOE_SLOTS = -(-(TOP_K * MOE_TILE + N_EXPERTS * (ROW_ALIGN - 1)) // 256) * 256
NEG_INF = float("-inf")
INT_MIN = -(2 ** 31)


def _cparams(*sem):
    return pltpu.CompilerParams(dimension_semantics=sem, vmem_limit_bytes=VMEM_LIMIT)


def _dot(a, b):
    return jnp.dot(a, b, preferred_element_type=F32)


def _dot_nt(a, b):
    return lax.dot_general(a, b, (((1,), (1,)), ((), ())), preferred_element_type=F32)


def _dot_tn(a, b):
    return lax.dot_general(a, b, (((0,), (0,)), ((), ())), preferred_element_type=F32)


def _layer_norm(z, w, b):
    mu = jnp.mean(z, axis=-1, keepdims=True)
    d = z - mu
    var = jnp.mean(d * d, axis=-1, keepdims=True)
    return d * lax.rsqrt(var + LN_EPS) * w + b


def _inproj_body(x_ref, w_ref, lnw_ref, lnb_ref, qa_ref, ka_ref, va_ref, kab_ref, vab_ref, qi_ref, ki_ref, kib_ref,
                 wi_ref, hb_ref):
    xb = x_ref[...].astype(BF16)

    def mm(c0, c1):
        return _dot(xb, w_ref[:, c0:c1])

    qa = (mm(COL_QA, COL_KA) * (HEAD_DIM_A ** -0.5 * LOG2_E)).astype(BF16)
    for h in range(N_HEADS_A):
        qa_ref[h] = qa[:, h * HEAD_DIM_A:(h + 1) * HEAD_DIM_A]
    for src, out3_ref, bf_ref in ((mm(COL_KA, COL_VA), ka_ref, kab_ref), (mm(COL_VA, COL_QI), va_ref, vab_ref)):
        for g in range(N_KV_A):
            out3_ref[:, g, :] = src[:, g * HEAD_DIM_A:(g + 1) * HEAD_DIM_A]
        bf_ref[...] = src.astype(BF16)
    qi = mm(COL_QI, COL_KW).astype(BF16)
    for h in range(N_IDX_HEADS):
        qi_ref[h] = qi[:, h * IDX_DIM:(h + 1) * IDX_DIM]
    kw = mm(COL_KW, COL_HB)
    ki = _layer_norm(kw[:, :IDX_DIM], lnw_ref[...], lnb_ref[...])
    ki_ref[...] = ki
    kib_ref[...] = ki.astype(BF16)
    wi_ref[...] = kw[:, IDX_DIM:IDX_DIM + N_IDX_HEADS]
    hb_ref[...] = mm(COL_HB, COL_END)


def _inproj(x2d, w_pad, lnw, lnb):
    n = x2d.shape[0]
    tm = min(INPROJ_ROWS, n)
    row = lambda width: pl.BlockSpec((tm, width), lambda i: (i, 0))
    full = lambda a: pl.BlockSpec(a.shape, lambda i: (0,) * a.ndim)
    heads = lambda nh, d: pl.BlockSpec((nh, tm, d), lambda i: (0, i, 0))
    kv = N_KV_A * HEAD_DIM_A
    kv3 = pl.BlockSpec((tm, N_KV_A, HEAD_DIM_A), lambda i: (i, 0, 0))
    sds = jax.ShapeDtypeStruct
    return pl.pallas_call(
        _inproj_body,
        grid=(n // tm,),
        in_specs=[row(D_MODEL), full(w_pad), full(lnw), full(lnb)],
        out_specs=[heads(N_HEADS_A, HEAD_DIM_A), kv3, kv3, row(kv), row(kv), heads(N_IDX_HEADS, IDX_DIM),
                   row(IDX_DIM), row(IDX_DIM), row(N_IDX_HEADS), row(4 * WIDTH_B)],
        out_shape=[sds((N_HEADS_A, n, HEAD_DIM_A), BF16), sds((n, N_KV_A, HEAD_DIM_A), F32),
                   sds((n, N_KV_A, HEAD_DIM_A), F32), sds((n, kv), BF16), sds((n, kv), BF16),
                   sds((N_IDX_HEADS, n, IDX_DIM), BF16), sds((n, IDX_DIM), F32), sds((n, IDX_DIM), BF16),
                   sds((n, N_IDX_HEADS), F32), sds((n, 4 * WIDTH_B), F32)],
        compiler_params=_cparams("parallel"),
        name="inproj",
    )(x2d, w_pad, lnw, lnb)


def _order_bits_to_f32(u):
    key = u ^ INT_MIN
    bits = key ^ ((key >> 31) & 0x7FFFFFFF)
    f = lax.bitcast_convert_type(bits, F32)
    return jnp.where(u >= 0, jnp.where(u <= 0x007FFFFF, NEG_INF, f), f)


def _dsa_body(qa_ref, qi_ref, wi_ref, kidx_ref, k_ref, vt_ref, o_ref, sc_scr, bias_scr, term_scr, s_scr, acc_scr,
              *, s_pad, s_real, q_rows, pair, pos0, topk):
    kb_rows = KEY_BLOCK
    heads_per_kv = N_HEADS_A // N_KV_A
    qb = pair * q_rows
    cdim = pair * HEAD_DIM_A
    v_rows = pair * V_ROWS
    j = pl.program_id(1)
    q_lo = pos0 + j * q_rows
    lane = lax.broadcasted_iota(I32, (1, qb), 1)
    q_chunk = (q_lo + (lane & (q_rows - 1))) >> CHUNK_SHIFT
    k_lim = (((q_lo + q_rows - 1) >> CHUNK_SHIFT) + 1) * CHUNK
    nkb = jnp.minimum(s_pad // kb_rows, (k_lim + kb_rows - 1) // kb_rows)

    def rows_of(kb):
        return pl.ds(pl.multiple_of(kb * kb_rows, kb_rows), kb_rows)

    def fold8(a):
        return a.reshape(kb_rows // 8, 8, qb)

    def score_blk(kb, _):
        rows = rows_of(kb)
        kidx = kidx_ref[0, rows, :]
        for h in range(N_IDX_HEADS):
            term_scr[h] = jnp.maximum(_dot_nt(kidx, qi_ref[h, 0]), 0.0) * wi_ref[0, h:h + 1, :]
        total = (term_scr[0] + term_scr[1]) + (term_scr[2] + term_scr[3])
        spos = kb * kb_rows + lax.broadcasted_iota(I32, (kb_rows, qb), 0)
        sc = jnp.where((spos >> CHUNK_SHIFT) <= q_chunk, total * IDX_SCALE + 0.0, NEG_INF)
        if s_real < s_pad:
            sc = jnp.where(spos < s_real, sc, NEG_INF)
        sc_scr[rows, :] = sc
        return 0

    lax.fori_loop(0, nkb, score_blk, 0)

    def count(thr, strict):
        def body(kb, acc):
            blk = sc_scr[rows_of(kb), :]
            hit = (blk > thr) if strict else (blk >= thr)
            return acc + jnp.sum(fold8(jnp.where(hit, 1, 0).astype(I32)), axis=0)
        acc = lax.fori_loop(0, nkb, body, jnp.zeros((8, qb), I32))
        return jnp.sum(acc, axis=0, keepdims=True)

    def bit_step(i, carry):
        prefix, cnt = carry
        cand = prefix | lax.shift_left(jnp.int32(1), 31 - i)
        c = count(_order_bits_to_f32(cand), False)
        take = c >= topk
        return jnp.where(take, cand, prefix), jnp.where(take, c, cnt)

    prefix, cnt_ge = lax.fori_loop(0, 32, bit_step,
                                   (jnp.zeros((1, qb), I32), jnp.full((1, qb), nkb * kb_rows, I32)))
    tau = _order_bits_to_f32(prefix)
    finite_tau = tau > NEG_INF
    tau_floor = jnp.maximum(tau, jnp.finfo(F32).min)
    tie_lanes = jnp.where(finite_tau, jnp.where(cnt_ge > topk, 1, 0), 0)
    has_ties = jnp.max(tie_lanes) > 0

    @pl.when(jnp.logical_not(has_ties))
    def _():
        def body(kb, _):
            blk = sc_scr[rows_of(kb), :]
            bias_scr[rows_of(kb), :] = jnp.where(blk >= tau_floor, 0.0, NEG_INF)
            return 0
        lax.fori_loop(0, nkb, body, 0)

    @pl.when(has_ties)
    def _():
        need = jnp.where(finite_tau, (topk - count(tau, True)).astype(F32), 0.0)
        r = lax.broadcasted_iota(I32, (kb_rows, kb_rows), 0)
        c = lax.broadcasted_iota(I32, (kb_rows, kb_rows), 1)
        tril = jnp.where(r >= c, 1.0, 0.0).astype(BF16)

        def body(kb, seen):
            blk = sc_scr[rows_of(kb), :]
            eq = blk == tau
            rank = _dot(tril, jnp.where(eq, 1.0, 0.0).astype(BF16)) + seen
            tie_bias = jnp.where(eq, jnp.where(rank <= need, 0.0, NEG_INF), NEG_INF)
            bias_scr[rows_of(kb), :] = jnp.where(blk > tau, 0.0, tie_bias)
            return rank[kb_rows - 1:kb_rows, :]
        lax.fori_loop(0, nkb, body, jnp.zeros((1, qb), F32))

    def pass1(kb, m8):
        rows = rows_of(kb)
        bias = bias_scr[rows, :]
        out = []
        for h in range(N_HEADS_A):
            g = h // heads_per_kv
            s = _dot_nt(k_ref[0, rows, g * cdim:(g + 1) * cdim], qa_ref[h, 0]) + bias
            s_scr[h, rows, :] = s
            out.append(jnp.maximum(m8[h], jnp.max(fold8(s), axis=0)))
        return tuple(out)

    m8 = lax.fori_loop(0, nkb, pass1, tuple(jnp.full((8, qb), NEG_INF, F32) for _ in range(N_HEADS_A)))
    m = [jnp.max(x, axis=0, keepdims=True) for x in m8]

    acc_scr[...] = jnp.zeros_like(acc_scr)

    def pass2(kb, _):
        rows = rows_of(kb)
        for h in range(N_HEADS_A):
            g = h // heads_per_kv
            p = jnp.exp2(s_scr[h, rows, :] - m[h]).astype(BF16)
            acc_scr[h] += _dot(vt_ref[0, g * v_rows:(g + 1) * v_rows, rows], p)
        return 0

    lax.fori_loop(0, nkb, pass2, 0)
    for h in range(N_HEADS_A):
        a = acc_scr[h]
        num, den = a[:HEAD_DIM_A], a[HEAD_DIM_A:HEAD_DIM_A + 1]
        for mb in range(1, pair):
            mine = lane >= mb * q_rows
            r0 = mb * V_ROWS
            num = jnp.where(mine, a[r0:r0 + HEAD_DIM_A], num)
            den = jnp.where(mine, a[r0 + HEAD_DIM_A:r0 + HEAD_DIM_A + 1], den)
        o_ref[0, h * HEAD_DIM_A:(h + 1) * HEAD_DIM_A, :] = (num * (1.0 / den)).astype(BF16)


def _dsa_pairing(b, t):
    return 2 if (2 * t <= 128 and b % 2 == 0) else 1


def _dsa(qa_h, qi_h, wi, keys, *, s_real, pos0, topk):
    _, b, t, _ = qa_h.shape
    s_pad = keys[0].shape[1]
    q_rows = min(256, t)
    pair = _dsa_pairing(b, t)
    bp = b // pair
    lanes, cdim, v_rows = pair * q_rows, pair * HEAD_DIM_A, pair * V_ROWS

    def block_diag(q):
        if pair == 1:
            return q
        nh, d = q.shape[0], q.shape[3]
        eye = jnp.eye(pair, dtype=q.dtype)
        q6 = q.reshape(nh, bp, pair, t, 1, d) * eye.reshape(1, 1, pair, 1, pair, 1)
        return q6.reshape(nh, bp, pair * t, pair * d)

    wi_op = wi.reshape(bp, pair, t, N_IDX_HEADS).transpose(0, 3, 1, 2).reshape(bp, N_IDX_HEADS, pair * t)
    ki_all, k_all, v_all = keys
    ki_op = ki_all.reshape(bp, pair, s_pad, IDX_DIM).transpose(0, 2, 1, 3).reshape(bp, s_pad, cdim)
    k_op = (k_all.reshape(bp, pair, s_pad, N_KV_A, HEAD_DIM_A).transpose(0, 2, 3, 1, 4)
            .reshape(bp, s_pad, N_KV_A * cdim))
    vt = v_all.reshape(b, s_pad, N_KV_A, HEAD_DIM_A).transpose(0, 2, 3, 1)
    vt = jnp.concatenate([vt, jnp.ones((b, N_KV_A, V_ROWS - HEAD_DIM_A, s_pad), vt.dtype)], axis=2)
    vt_op = (vt.reshape(bp, pair, N_KV_A, V_ROWS, s_pad).transpose(0, 2, 1, 3, 4)
             .reshape(bp, N_KV_A * v_rows, s_pad))

    body = functools.partial(_dsa_body, s_pad=s_pad, s_real=s_real, q_rows=q_rows, pair=pair, pos0=pos0,
                             topk=topk)
    o_t = pl.pallas_call(
        body,
        grid=(bp, t // q_rows),
        in_specs=[
            pl.BlockSpec((N_HEADS_A, 1, lanes, cdim), lambda i, j: (0, i, j, 0)),
            pl.BlockSpec((N_IDX_HEADS, 1, lanes, cdim), lambda i, j: (0, i, j, 0)),
            pl.BlockSpec((1, N_IDX_HEADS, lanes), lambda i, j: (i, 0, j)),
            pl.BlockSpec((1, s_pad, cdim), lambda i, j: (i, 0, 0)),
            pl.BlockSpec((1, s_pad, N_KV_A * cdim), lambda i, j: (i, 0, 0)),
            pl.BlockSpec((1, N_KV_A * v_rows, s_pad), lambda i, j: (i, 0, 0)),
        ],
        out_specs=pl.BlockSpec((1, WIDTH_A, lanes), lambda i, j: (i, 0, j)),
        out_shape=jax.ShapeDtypeStruct((bp, WIDTH_A, pair * t), BF16),
        scratch_shapes=[pltpu.VMEM((s_pad, lanes), F32), pltpu.VMEM((s_pad, lanes), F32),
                        pltpu.VMEM((N_IDX_HEADS, KEY_BLOCK, lanes), F32),
                        pltpu.VMEM((N_HEADS_A, s_pad, lanes), F32), pltpu.VMEM((N_HEADS_A, v_rows, lanes), F32)],
        compiler_params=_cparams("parallel", "parallel"),
        name="dsa",
    )(block_diag(qa_h), block_diag(qi_h), wi_op, ki_op, k_op, vt_op)
    return o_t.reshape(bp, WIDTH_A, pair, t).transpose(0, 2, 3, 1).reshape(b * t, WIDTH_A)


def _split3(a):
    hi = a.astype(BF16)
    r1 = a - hi.astype(F32)
    mid = r1.astype(BF16)
    lo = (r1 - mid.astype(F32)).astype(BF16)
    return hi, mid, lo


def _hgrn_body(hb_ref, s0_ref, lb_ref, nw_ref, ob_ref, sfin_ref, state_scr, *, tb, nb):
    t = pl.program_id(1)

    @pl.when(t == 0)
    def _():
        for bi, h in itertools.product(range(nb), range(N_HEADS_B)):
            state_scr[bi, h] = s0_ref[bi, h].T

    lb = lb_ref[...]
    r = lax.broadcasted_iota(I32, (CHUNK, CHUNK), 0)
    c = lax.broadcasted_iota(I32, (CHUNK, CHUNK), 1)
    causal = r >= c
    tril = jnp.where(causal, 1.0, 0.0).astype(BF16)

    for ci, bi in itertools.product(range(tb // CHUNK), range(nb)):
        rows = slice(ci * CHUNK, (ci + 1) * CHUNK)
        f = lb + (1.0 - lb) * jax.nn.sigmoid(hb_ref[bi, rows, WIDTH_B:2 * WIDTH_B])
        parts = _split3(jnp.log(f))
        bcum = _dot(tril, parts[0]) + _dot(tril, parts[1]) + _dot(tril, parts[2])
        for h in range(N_HEADS_B):
            lanes = slice(h * HEAD_DIM_B, (h + 1) * HEAD_DIM_B)
            q = hb_ref[bi, rows, lanes]
            k = 1.0 - f[:, lanes]
            v = hb_ref[bi, rows, 2 * WIDTH_B + h * HEAD_DIM_B:2 * WIDTH_B + (h + 1) * HEAD_DIM_B]
            gate = hb_ref[bi, rows, 3 * WIDTH_B + h * HEAD_DIM_B:3 * WIDTH_B + (h + 1) * HEAD_DIM_B]
            bh = bcum[:, lanes]
            b_last = bh[CHUNK - 1:CHUNK, :]
            vb = v.astype(BF16)
            state_t = state_scr[bi, h]
            o_inter = _dot_nt((q * jnp.exp(bh)).astype(BF16), state_t.astype(BF16))
            a_rows = []
            for i in range(CHUNK // HGRN_SUB):
                lo, hi = i * HGRN_SUB, (i + 1) * HGRN_SUB
                ref = bh[lo - 1:lo, :] if i else jnp.zeros((1, HEAD_DIM_B), F32)
                qs = (q[lo:hi] * jnp.exp(bh[lo:hi] - ref)).astype(BF16)
                ks = (k * jnp.exp(ref - bh)).astype(BF16)
                a_rows.append(_dot_nt(qs, ks))
            a = jnp.where(causal, jnp.concatenate(a_rows, axis=0), 0.0)
            o = o_inter + _dot(a.astype(BF16), vb)
            kdec = (k * jnp.exp(b_last - bh)).astype(BF16)
            state_scr[bi, h] = jnp.exp(b_last) * state_t + _dot_tn(vb, kdec)
            ms = jnp.mean(o * o, axis=-1, keepdims=True)
            y = o * lax.rsqrt(ms + RMS_EPS) * nw_ref[...] * (gate * jax.nn.sigmoid(gate))
            ob_ref[bi, rows, lanes] = y.astype(BF16)

    @pl.when(t == pl.num_programs(1) - 1)
    def _():
        for bi, h in itertools.product(range(nb), range(N_HEADS_B)):
            sfin_ref[bi, h] = state_scr[bi, h].T


def _hgrn(hb, s0, lb, nw):
    b, t, _ = hb.shape
    tb = min(256, t)
    nb = 2 if b % 2 == 0 else 1
    return pl.pallas_call(
        functools.partial(_hgrn_body, tb=tb, nb=nb),
        grid=(b // nb, t // tb),
        in_specs=[
            pl.BlockSpec((nb, tb, 4 * WIDTH_B), lambda i, j: (i, j, 0)),
            pl.BlockSpec((nb, N_HEADS_B, HEAD_DIM_B, HEAD_DIM_B), lambda i, j: (i, 0, 0, 0)),
            pl.BlockSpec((1, WIDTH_B), lambda i, j: (0, 0)),
            pl.BlockSpec((1, HEAD_DIM_B), lambda i, j: (0, 0)),
        ],
        out_specs=[
            pl.BlockSpec((nb, tb, WIDTH_B), lambda i, j: (i, j, 0)),
            pl.BlockSpec((nb, N_HEADS_B, HEAD_DIM_B, HEAD_DIM_B), lambda i, j: (i, 0, 0, 0)),
        ],
        out_shape=[jax.ShapeDtypeStruct((b, t, WIDTH_B), BF16),
                   jax.ShapeDtypeStruct((b, N_HEADS_B, HEAD_DIM_B, HEAD_DIM_B), F32)],
        scratch_shapes=[pltpu.VMEM((nb, N_HEADS_B, HEAD_DIM_B, HEAD_DIM_B), F32)],
        compiler_params=_cparams("parallel", "arbitrary"),
        name="hgrn",
    )(hb, s0, lb, nw)


def _outproj_body(oaa_ref, oab_ref, oba_ref, obb_ref, xa_ref, xb_ref, w_ref, lnw_ref, lnb_ref, wrh_ref, wrl_ref, br_ref,
                  x1_ref, lg_ref, *, steps_a):
    first = pl.program_id(0) < steps_a
    pick = lambda a_ref, b_ref: jnp.where(first, a_ref[...], b_ref[...])
    x = pick(xa_ref, xb_ref)
    y = _dot(pick(oaa_ref, oab_ref), w_ref[:WIDTH_A, :]) + _dot(pick(oba_ref, obb_ref), w_ref[WIDTH_A:, :])
    x1 = _layer_norm(DN_ALPHA * x + y, lnw_ref[...], lnb_ref[...])
    x1_ref[...] = x1
    hi = x1.astype(BF16)
    lo = (x1 - hi.astype(F32)).astype(BF16)
    lg_ref[...] = (_dot_nt(wrh_ref[...], hi) + _dot_nt(wrh_ref[...], lo) + _dot_nt(wrl_ref[...], hi)
                   + br_ref[...])


def _two_group_rows(tm, width, steps_a):
    return (pl.BlockSpec((tm, width), lambda i, *_: (jnp.minimum(i, steps_a - 1), 0)),
            pl.BlockSpec((tm, width), lambda i, *_: (jnp.maximum(i - steps_a, 0), 0)))


def _outproj(mix_a, mix_b, x_a, x_b, w_out, lnw, lnb, wr_hi, wr_lo, br):
    tm = MOE_TILE
    assert x_a.shape[0] % tm == 0 and x_b.shape[0] % tm == 0
    n = x_a.shape[0] + x_b.shape[0]
    steps_a = x_a.shape[0] // tm
    row = lambda width: pl.BlockSpec((tm, width), lambda i: (i, 0))
    full = lambda a: pl.BlockSpec(a.shape, lambda i: (0,) * a.ndim)
    return pl.pallas_call(
        functools.partial(_outproj_body, steps_a=steps_a),
        grid=(n // tm,),
        in_specs=[*_two_group_rows(tm, WIDTH_A, steps_a), *_two_group_rows(tm, WIDTH_B, steps_a),
                  *_two_group_rows(tm, D_MODEL, steps_a), full(w_out), full(lnw), full(lnb), full(wr_hi),
                  full(wr_lo), full(br)],
        out_specs=[row(D_MODEL), pl.BlockSpec((N_EXPERTS, tm), lambda i: (0, i))],
        out_shape=[jax.ShapeDtypeStruct((n, D_MODEL), F32), jax.ShapeDtypeStruct((N_EXPERTS, n), F32)],
        compiler_params=_cparams("parallel"),
        name="outproj",
    )(mix_a[0], mix_b[0], mix_a[1], mix_b[1], x_a, x_b, w_out, lnw, lnb, wr_hi, wr_lo, br)


def _route_body(lg_ref, lpos_ref, gate_ref, tc_ref, base_ref, cnt_ref, cnt_scr, *, tr):
    @pl.when(pl.program_id(0) == 0)
    def _():
        cnt_scr[...] = jnp.zeros_like(cnt_scr)

    l = lg_ref[...]
    rows = lax.broadcasted_iota(I32, (N_EXPERTS, tr), 0)
    vals, hots = [], []
    for j in range(TOP_K):
        m = jnp.max(l, axis=0, keepdims=True)
        idx = jnp.min(jnp.where(l == m, rows, N_EXPERTS), axis=0, keepdims=True)
        hot = rows == idx
        vals.append(m)
        hots.append(hot)
        l = jnp.where(hot, NEG_INF, l)
    es = [jnp.exp(v - vals[0]) for v in vals]
    inv = 1.0 / (es[0] + es[1] + es[2] + es[3])
    for j in range(TOP_K):
        gate_ref[j:j + 1, :] = es[j] * inv
    chosen = jnp.zeros((N_EXPERTS, tr), F32)
    for hot in hots:
        chosen = chosen + jnp.where(hot, 1.0, 0.0)
    r = lax.broadcasted_iota(I32, (tr, tr), 0)
    c = lax.broadcasted_iota(I32, (tr, tr), 1)
    before = jnp.where(r < c, 1.0, 0.0).astype(BF16)
    prior = _dot(chosen.astype(BF16), before)
    units = jnp.ceil(jnp.sum(chosen, axis=1, keepdims=True) * (1.0 / ROW_ALIGN))
    er = lax.broadcasted_iota(I32, (N_EXPERTS, N_EXPERTS), 0)
    ec = lax.broadcasted_iota(I32, (N_EXPERTS, N_EXPERTS), 1)
    earlier = jnp.where(ec < er, 1.0, 0.0).astype(BF16)
    run_len = jnp.broadcast_to(units, (N_EXPERTS, 128)) * ROW_ALIGN
    run_start = _dot(earlier, jnp.broadcast_to(units, (N_EXPERTS, 128)).astype(BF16)) * ROW_ALIGN
    where_in_tile = prior + run_start[:, 0:1]
    for j in range(TOP_K):
        lpos_ref[j:j + 1, :] = jnp.sum(jnp.where(hots[j], where_in_tile, 0.0), axis=0,
                                       keepdims=True).astype(I32)
    tc_ref[0] = run_len.astype(I32)
    base_ref[0] = cnt_scr[...].astype(I32)
    cnt_scr[...] = cnt_scr[...] + run_len
    cnt_ref[...] = cnt_scr[...].astype(I32)


def _route(logits_t):
    n = logits_t.shape[1]
    tr = MOE_TILE
    tiles = n // tr
    tok = lambda rows: pl.BlockSpec((rows, tr), lambda i: (0, i))
    per_tile = pl.BlockSpec((1, N_EXPERTS, 128), lambda i: (i, 0, 0))
    return pl.pallas_call(
        functools.partial(_route_body, tr=tr),
        grid=(tiles,),
        in_specs=[tok(N_EXPERTS)],
        out_specs=[tok(TOP_K), tok(TOP_K), per_tile, per_tile, pl.BlockSpec((N_EXPERTS, 128), lambda i: (0, 0))],
        out_shape=[jax.ShapeDtypeStruct((TOP_K, n), I32), jax.ShapeDtypeStruct((TOP_K, n), F32),
                   jax.ShapeDtypeStruct((tiles, N_EXPERTS, 128), I32),
                   jax.ShapeDtypeStruct((tiles, N_EXPERTS, 128), I32),
                   jax.ShapeDtypeStruct((N_EXPERTS, 128), I32)],
        scratch_shapes=[pltpu.VMEM((N_EXPERTS, 128), F32)],
        compiler_params=_cparams("arbitrary"),
        name="route",
    )(logits_t)


def _pack_pairs(a):
    half = a.shape[1] // 2
    hi = lax.bitcast_convert_type(a[:, :half], I32)
    lo = lax.shift_right_logical(lax.bitcast_convert_type(a[:, half:], I32), 16)
    return hi | lo


def _unpack_pairs(pk):
    hi = lax.bitcast_convert_type(pk & jnp.int32(-65536), F32).astype(BF16)
    lo = lax.bitcast_convert_type(pk << 16, F32).astype(BF16)
    return hi, lo


def _for_each_run_piece(tc_ref, g0_ref, t, fn):
    def per_expert(e, local):
        c = tc_ref[t, e]
        glob = g0_ref[t, e]
        lo = local
        for p in RUN_PIECES:
            take = (c & p) != 0

            @pl.when(take)
            def _():
                fn(pl.multiple_of(lo, ROW_ALIGN), pl.multiple_of(glob, ROW_ALIGN), p)

            step = jnp.where(take, p, 0)
            lo = lo + step
            glob = glob + step
        return local + c

    lax.fori_loop(0, N_EXPERTS, per_expert, 0)


def _dispatch_body(tc_ref, g0_ref, seg_ref, lpos_ref, x_ref, buf_ref, stage, zero_scr, zsem, sems, *, n_blocks):
    i = pl.program_id(0)
    slot = i % 2

    def run_copy(slot_, lo, glob, rows):
        return pltpu.make_async_copy(stage.at[slot_, pl.ds(lo, rows), :], buf_ref.at[pl.ds(glob, rows), :],
                                     sems.at[slot_])

    @pl.when(i == 0)
    def _():
        zero_scr[...] = jnp.zeros_like(zero_scr)
        n_used = seg_ref[1, N_EXPERTS - 1] // EXPERT_ROWS

        def block_copy(start):
            return pltpu.make_async_copy(
                zero_scr, buf_ref.at[pl.ds(pl.multiple_of(start, EXPERT_ROWS), EXPERT_ROWS), :], zsem)

        def clears(action):
            for e in range(N_EXPERTS):
                @pl.when(seg_ref[1, e] > seg_ref[0, e])
                def _():
                    action(block_copy(seg_ref[1, e] - EXPERT_ROWS))

                @pl.when(n_used + e < n_blocks)
                def _():
                    action(block_copy((n_used + e) * EXPERT_ROWS))

        clears(lambda cp: cp.start())
        clears(lambda cp: cp.wait())

    r = lax.broadcasted_iota(I32, (MOE_SLOTS, MOE_TILE), 0)
    onehot = jnp.zeros((MOE_SLOTS, MOE_TILE), F32)
    for j in range(TOP_K):
        onehot = jnp.where(r == lpos_ref[j:j + 1, :], 1.0, onehot)
    stage[slot] = _pack_pairs(_dot(onehot.astype(BF16), x_ref[...].astype(BF16)))

    @pl.when(i > 0)
    def _():
        _for_each_run_piece(tc_ref, g0_ref, i - 1,
                            lambda lo, glob, rows: run_copy(1 - slot, lo, glob, rows).wait())

    _for_each_run_piece(tc_ref, g0_ref, i, lambda lo, glob, rows: run_copy(slot, lo, glob, rows).start())

    @pl.when(i == pl.num_programs(0) - 1)
    def _():
        _for_each_run_piece(tc_ref, g0_ref, i, lambda lo, glob, rows: run_copy(slot, lo, glob, rows).wait())


def _dispatch(x1, lpos_t, tc, g0, seg, n_rows):
    n = x1.shape[0]
    return pl.pallas_call(
        functools.partial(_dispatch_body, n_blocks=n_rows // EXPERT_ROWS),
        grid_spec=pltpu.PrefetchScalarGridSpec(
            num_scalar_prefetch=3,
            grid=(n // MOE_TILE,),
            in_specs=[pl.BlockSpec((TOP_K, MOE_TILE), lambda i, *_: (0, i)),
                      pl.BlockSpec((MOE_TILE, D_MODEL), lambda i, *_: (i, 0))],
            out_specs=pl.BlockSpec(memory_space=pl.ANY),
            scratch_shapes=[pltpu.VMEM((2, MOE_SLOTS, D_MODEL // 2), I32),
                            pltpu.VMEM((EXPERT_ROWS, D_MODEL // 2), I32),
                            pltpu.SemaphoreType.DMA(()), pltpu.SemaphoreType.DMA((2,))],
        ),
        out_shape=jax.ShapeDtypeStruct((n_rows, D_MODEL // 2), I32),
        compiler_params=_cparams("arbitrary"),
        name="dispatch",
    )(tc, g0, seg, lpos_t, x1)


def _experts_body(be_ref, nu_ref, x_ref, wgu_ref, bgu_ref, wd_ref, bd_ref, o_ref, wgu_bf, wd_bf):
    i = pl.program_id(0)

    @pl.when(jnp.logical_or(i == 0, be_ref[i] != be_ref[jnp.maximum(i - 1, 0)]))
    def _():
        for r in range(0, D_MODEL, 256):
            wgu_bf[r:r + 256, :] = wgu_ref[0, r:r + 256, :].astype(BF16)
            wd_bf[r:r + 256, :] = wd_ref[0, r:r + 256, :].astype(BF16)

    @pl.when(i < nu_ref[0])
    def _():
        half = D_MODEL // 2
        x_a, x_b = _unpack_pairs(x_ref[...])
        gu = _dot(x_a, wgu_bf[:half, :]) + _dot(x_b, wgu_bf[half:, :]) + bgu_ref[0]
        gate = jnp.minimum(gu[:, :D_EXPERT], SWIGLU_LIMIT)
        up = jnp.clip(gu[:, D_EXPERT:], -SWIGLU_LIMIT, SWIGLU_LIMIT)
        hdn = (up + 1.0) * gate * jax.nn.sigmoid(SWIGLU_ALPHA * gate)
        out = _dot(hdn.astype(BF16), wd_bf[...]) + bd_ref[0]
        o_ref[...] = _pack_pairs(out.astype(BF16).astype(F32))

    @pl.when(i >= nu_ref[0])
    def _():
        o_ref[...] = jnp.zeros_like(o_ref)


def _experts(buf, block_e, n_used, w_gu, b_gu, w_down, b_down):
    n_blocks = buf.shape[0] // EXPERT_ROWS
    rows = pl.BlockSpec((EXPERT_ROWS, D_MODEL // 2), lambda i, be, nu: (jnp.minimum(i, nu[0] - 1), 0))
    per_e = lambda a: pl.BlockSpec((1,) + a.shape[1:], lambda i, be, nu: (be[i],) + (0,) * (a.ndim - 1))
    return pl.pallas_call(
        _experts_body,
        grid_spec=pltpu.PrefetchScalarGridSpec(
            num_scalar_prefetch=2,
            grid=(n_blocks,),
            in_specs=[rows, per_e(w_gu), per_e(b_gu), per_e(w_down), per_e(b_down)],
            out_specs=pl.BlockSpec((EXPERT_ROWS, D_MODEL // 2), lambda i, be, nu: (i, 0)),
            scratch_shapes=[pltpu.VMEM((D_MODEL, 2 * D_EXPERT), BF16), pltpu.VMEM((D_EXPERT, D_MODEL), BF16)],
        ),
        out_shape=jax.ShapeDtypeStruct(buf.shape, I32),
        compiler_params=_cparams("arbitrary"),
        name="experts",
    )(block_e, n_used, buf, w_gu, b_gu, w_down, b_down)


def _final_body(tc_ref, g0_ref, lpos_ref, gate_ref, x1_ref, pa_ref, pb_ref, lnw_ref, lnb_ref, wg_ref, bg_ref, wp_ref,
                eo_ref, ya_ref, yb_ref, stage, sems, *, steps_a):
    i = pl.program_id(0)
    slot = i % 2

    def run_copy(slot_, lo, glob, rows):
        return pltpu.make_async_copy(eo_ref.at[pl.ds(glob, rows), :], stage.at[slot_, pl.ds(lo, rows), :],
                                     sems.at[slot_])

    def fetch(tile, slot_):
        stage[slot_] = jnp.zeros(stage.shape[1:], stage.dtype)
        _for_each_run_piece(tc_ref, g0_ref, tile, lambda lo, glob, rows: run_copy(slot_, lo, glob, rows).start())

    @pl.when(i == 0)
    def _():
        fetch(0, 0)

    @pl.when(i + 1 < pl.num_programs(0))
    def _():
        fetch(i + 1, 1 - slot)

    _for_each_run_piece(tc_ref, g0_ref, i, lambda lo, glob, rows: run_copy(slot, lo, glob, rows).wait())

    r = lax.broadcasted_iota(I32, (MOE_TILE, MOE_SLOTS), 1)
    g = jnp.zeros((MOE_TILE, MOE_SLOTS), F32)
    for j in range(TOP_K):
        g = jnp.where(r == lpos_ref[:, j:j + 1], gate_ref[:, j:j + 1], g)
    g_hi = g.astype(BF16)
    g_lo = (g - g_hi.astype(F32)).astype(BF16)
    g2 = jnp.concatenate([g_hi, g_lo], axis=0)
    e_a, e_b = _unpack_pairs(stage[slot])
    r_a, r_b = _dot(g2, e_a), _dot(g2, e_b)
    moe = jnp.concatenate([r_a[:MOE_TILE] + r_a[MOE_TILE:], r_b[:MOE_TILE] + r_b[MOE_TILE:]], axis=1)
    x2 = _layer_norm(DN_ALPHA * x1_ref[...] + moe, lnw_ref[...], lnb_ref[...])
    ple_gate = jax.nn.sigmoid(_dot(x2.astype(BF16), wg_ref[...]) + bg_ref[...])
    p = jnp.where(i < steps_a, pa_ref[...], pb_ref[...])
    y = x2 + ple_gate * _dot(p.astype(BF16), wp_ref[...])

    @pl.when(i < steps_a)
    def _():
        ya_ref[...] = y

    @pl.when(i >= steps_a)
    def _():
        yb_ref[...] = y


def _final(tc, g0, lpos, gate, x1, p_a, p_b, lnw, lnb, w_gate, b_gate, w_ple, expert_out):
    n = x1.shape[0]
    steps_a = p_a.shape[0] // MOE_TILE
    row = lambda width: pl.BlockSpec((MOE_TILE, width), lambda i, *_: (i, 0))
    full = lambda a: pl.BlockSpec(a.shape, lambda i, *_: (0,) * a.ndim)
    return pl.pallas_call(
        functools.partial(_final_body, steps_a=steps_a),
        grid_spec=pltpu.PrefetchScalarGridSpec(
            num_scalar_prefetch=2,
            grid=(n // MOE_TILE,),
            in_specs=[row(TOP_K), row(TOP_K), row(D_MODEL), *_two_group_rows(MOE_TILE, D_PLE, steps_a), full(lnw),
                      full(lnb), full(w_gate), full(b_gate), full(w_ple), pl.BlockSpec(memory_space=pl.ANY)],
            out_specs=list(_two_group_rows(MOE_TILE, D_MODEL, steps_a)),
            scratch_shapes=[pltpu.VMEM((2, MOE_SLOTS, D_MODEL // 2), I32), pltpu.SemaphoreType.DMA((2,))],
        ),
        out_shape=[jax.ShapeDtypeStruct((p_a.shape[0], D_MODEL), F32),
                   jax.ShapeDtypeStruct((p_b.shape[0], D_MODEL), F32)],
        compiler_params=_cparams("arbitrary"),
        name="combine_final",
    )(tc, g0, lpos, gate, x1, p_a, p_b, lnw, lnb, w_gate, b_gate, w_ple, expert_out)


def _round_up(a, m):
    return -(-a // m) * m


def _token_mixers(x, hist_k, hist_v, hist_kidx, s0, pos0, wts):
    b, t, _ = x.shape
    n = b * t
    x2d = x.reshape(n, D_MODEL)

    qa, ka, va, ka_bf, va_bf, qi, ki, ki_bf, wi, hb = _inproj(x2d, wts["w_in"], wts["idx_lnw"], wts["idx_lnb"])

    k_new = ka_bf.reshape(b, t, N_KV_A * HEAD_DIM_A)
    v_new = va_bf.reshape(b, t, N_KV_A * HEAD_DIM_A)
    ki_new = ki_bf.reshape(b, t, IDX_DIM)
    q_ops = (qa.reshape(N_HEADS_A, b, t, HEAD_DIM_A), qi.reshape(N_IDX_HEADS, b, t, IDX_DIM),
             wi.reshape(b, t, N_IDX_HEADS))
    past = 0 if hist_k is None else hist_k.shape[1]
    s_real = past + t
    topk = min(TOPK_MAX, s_real // 4)
    if past:
        k_all = jnp.concatenate([hist_k.reshape(b, past, -1).astype(BF16), k_new], axis=1)
        v_all = jnp.concatenate([hist_v.reshape(b, past, -1).astype(BF16), v_new], axis=1)
        ki_all = jnp.concatenate([hist_kidx.astype(BF16), ki_new], axis=1)
    else:
        k_all, v_all, ki_all = k_new, v_new, ki_new
    pad = ((0, 0), (0, _round_up(s_real, KEY_BLOCK) - s_real), (0, 0))
    keys = tuple(jnp.pad(a, pad) for a in (ki_all, k_all, v_all))
    oa = _dsa(*q_ops, keys, s_real=s_real, pos0=pos0, topk=topk)

    ob, s_fin = _hgrn(hb.reshape(b, t, 4 * WIDTH_B), s0, wts["lb"], wts["hgrn_nw"])
    return (oa, ob.reshape(n, WIDTH_B), ka.reshape(b, t, N_KV_A, HEAD_DIM_A), va.reshape(b, t, N_KV_A, HEAD_DIM_A),
            ki.reshape(b, t, IDX_DIM), s_fin)


def _channel_mixer(mix_a, mix_b, x_a, x_b, p_a, p_b, wts):
    shape_a, shape_b = x_a.shape, x_b.shape
    x_a, x_b = x_a.reshape(-1, D_MODEL), x_b.reshape(-1, D_MODEL)
    n = x_a.shape[0] + x_b.shape[0]
    x1, logits_t = _outproj(mix_a, mix_b, x_a, x_b, wts["w_out"], wts["ln1_w"], wts["ln1_b"],
                            wts["wr_hi"], wts["wr_lo"], wts["b_router"])

    tiles = n // MOE_TILE
    lpos_t, gate_t, tile_runs, tile_base, counts = _route(logits_t)
    tc, counts = tile_runs[:, :, 0], counts[:, 0]
    padded = (counts + EXPERT_ROWS - 1) // EXPERT_ROWS * EXPERT_ROWS
    seg_end = jnp.cumsum(padded)
    seg_start = seg_end - padded
    g0 = (seg_start[None, :] + tile_base[:, :, 0]).astype(I32)
    n_blocks = -(-(n * TOP_K + N_EXPERTS * tiles * (ROW_ALIGN - 1) + N_EXPERTS * (EXPERT_ROWS - 1))
                 // EXPERT_ROWS)
    block_row0 = jnp.arange(n_blocks, dtype=I32) * EXPERT_ROWS
    block_e = jnp.minimum(jnp.sum((seg_end[None, :] <= block_row0[:, None]).astype(I32), axis=1), N_EXPERTS - 1)
    n_used = (seg_end[-1:] // EXPERT_ROWS).astype(I32)
    seg = jnp.stack([seg_start, seg_end]).astype(I32)
    buf = _dispatch(x1, lpos_t, tc, g0, seg, n_blocks * EXPERT_ROWS)
    expert_out = _experts(buf, block_e, n_used, wts["w_gu"], wts["b_gu"], wts["w_down"], wts["b_down"])
    y_a, y_b = _final(tc, g0, lpos_t.T, gate_t.T, x1, p_a.reshape(-1, D_PLE), p_b.reshape(-1, D_PLE),
                      wts["ln2_w"], wts["ln2_b"], wts["w_ple_gate"], wts["b_ple_gate"], wts["w_ple"], expert_out)
    return y_a.reshape(shape_a), y_b.reshape(shape_b)


def _prep_weights(w_in, w_out, idx_k_norm_w, idx_k_norm_b, lb, hgrn_norm_w, ln1_w, ln1_b, w_router, b_router,
                  w_gu, b_gu, w_down, b_down, ln2_w, ln2_b, w_ple, w_ple_gate, b_ple_gate):
    n_a = COL_KW + IDX_DIM + N_IDX_HEADS
    w_pad = jnp.concatenate(
        [w_in[:, :n_a], jnp.zeros((D_MODEL, COL_HB - n_a), w_in.dtype), w_in[:, n_a:]], axis=1).astype(BF16)
    wr_t = w_router.T
    wr_hi = wr_t.astype(BF16)
    row = lambda a: a.reshape(1, -1)
    return dict(
        w_in=w_pad, idx_lnw=row(idx_k_norm_w), idx_lnb=row(idx_k_norm_b), lb=row(lb), hgrn_nw=row(hgrn_norm_w),
        w_out=w_out.astype(BF16), ln1_w=row(ln1_w), ln1_b=row(ln1_b),
        wr_hi=wr_hi, wr_lo=(wr_t - wr_hi.astype(F32)).astype(BF16), b_router=b_router.reshape(N_EXPERTS, 1),
        w_gu=w_gu, b_gu=b_gu.reshape(N_EXPERTS, 1, 2 * D_EXPERT),
        w_down=w_down, b_down=b_down.reshape(N_EXPERTS, 1, D_MODEL),
        ln2_w=row(ln2_w), ln2_b=row(ln2_b), w_ple=w_ple.astype(BF16), w_ple_gate=w_ple_gate.astype(BF16),
        b_ple_gate=row(b_ple_gate))


def kernel(x_prompt, x_sample, cache_k, cache_v, cache_kidx, state_hgrn, p_prompt, p_sample, w_in, w_out,
           idx_k_norm_w, idx_k_norm_b, hgrn_lb_logits, hgrn_norm_w, ln1_w, ln1_b, w_router, b_router, w_gu, b_gu,
           w_down, b_down, ln2_w, ln2_b, w_ple, w_ple_gate, b_ple_gate):
    lb_all = jnp.cumsum(jax.nn.softmax(hgrn_lb_logits.astype(F32), axis=0), axis=0)
    xp, xs = x_prompt, x_sample
    outs = [[] for _ in range(8)]
    for i in range(DEPTH):
        wts = _prep_weights(w_in[i], w_out[i], idx_k_norm_w[i], idx_k_norm_b[i], lb_all[i], hgrn_norm_w[i],
                            ln1_w[i], ln1_b[i], w_router[i], b_router[i], w_gu[i], b_gu[i], w_down[i], b_down[i],
                            ln2_w[i], ln2_b[i], w_ple[i], w_ple_gate[i], b_ple_gate[i])
        s0p = jnp.zeros((xp.shape[0], N_HEADS_B, HEAD_DIM_B, HEAD_DIM_B), F32)
        mix_p = _token_mixers(xp, None, None, None, s0p, 0, wts)
        mix_s = _token_mixers(xs, cache_k[i], cache_v[i], cache_kidx[i], state_hgrn[i], cache_k.shape[2], wts)
        xp, xs = _channel_mixer(mix_p, mix_s, xp, xs, p_prompt[i], p_sample[i], wts)
        for lst, val in zip(outs, mix_p[2:] + mix_s[2:]):
            lst.append(val)
    return (xp, xs) + tuple(jnp.stack(l) for l in outs)
```

```python
import functools
import itertools

import jax
import jax.numpy as jnp
from jax import lax
from jax.experimental import pallas as pl
from jax.experimental.pallas import tpu as pltpu

F32 = jnp.float32
BF16 = jnp.bfloat16
I32 = jnp.int32

D_MODEL = 1024
CHUNK = 64
CHUNK_SHIFT = 6
WIDTH_A = 512
HEAD_DIM_A = 64
N_HEADS_A = 8
N_KV_A = 2
N_IDX_HEADS = 4
IDX_DIM = 64
IDX_SCALE = IDX_DIM ** -0.5 * N_IDX_HEADS ** -0.5
TOPK_MAX = 256
WIDTH_B = 512
HEAD_DIM_B = 128
N_HEADS_B = 4
N_EXPERTS = 32
TOP_K = 4
D_EXPERT = 1024
SWIGLU_LIMIT = 7.0
SWIGLU_ALPHA = 1.702
D_PLE = 256
LN_EPS = 1e-5
RMS_EPS = 1e-6
DEPTH = 1
DN_ALPHA = (2 * DEPTH) ** 0.25

COL_QA, COL_KA, COL_VA, COL_QI, COL_KW, COL_HB, COL_END = 0, 512, 640, 768, 1024, 1152, 3200

VMEM_LIMIT = 56 * 1024 * 1024
INPROJ_ROWS = 1024
KEY_BLOCK = 256
V_ROWS = 80
LOG2_E = 1.4426950408889634
HGRN_SUB = 16
EXPERT_ROWS = 512
MOE_TILE = 512
ROW_ALIGN = 16
RUN_PIECES = (512, 256, 128, 64, 32, 16)
MOE_SLOTS = -(-(TOP_K * MOE_TILE + N_EXPERTS * (ROW_ALIGN - 1)) // 256) * 256
NEG_INF = float("-inf")
INT_MIN = -(2 ** 31)


def _cparams(*sem):
    return pltpu.CompilerParams(dimension_semantics=sem, vmem_limit_bytes=VMEM_LIMIT)


def _dot(a, b):
    return jnp.dot(a, b, preferred_element_type=F32)


def _dot_nt(a, b):
    return lax.dot_general(a, b, (((1,), (1,)), ((), ())), preferred_element_type=F32)


def _dot_tn(a, b):
    return lax.dot_general(a, b, (((0,), (0,)), ((), ())), preferred_element_type=F32)


def _layer_norm(z, w, b):
    mu = jnp.mean(z, axis=-1, keepdims=True)
    d = z - mu
    var = jnp.mean(d * d, axis=-1, keepdims=True)
    return d * lax.rsqrt(var + LN_EPS) * w + b


def _inproj_body(x_ref, w_ref, lnw_ref, lnb_ref, qa_ref, ka_ref, va_ref, kab_ref, vab_ref, qi_ref, ki_ref, kib_ref,
                 wi_ref, hb_ref):
    xb = x_ref[...].astype(BF16)

    def mm(c0, c1):
        return _dot(xb, w_ref[:, c0:c1])

    qa = (mm(COL_QA, COL_KA) * (HEAD_DIM_A ** -0.5 * LOG2_E)).astype(BF16)
    for h in range(N_HEADS_A):
        qa_ref[h] = qa[:, h * HEAD_DIM_A:(h + 1) * HEAD_DIM_A]
    for src, out3_ref, bf_ref in ((mm(COL_KA, COL_VA), ka_ref, kab_ref), (mm(COL_VA, COL_QI), va_ref, vab_ref)):
        for g in range(N_KV_A):
            out3_ref[:, g, :] = src[:, g * HEAD_DIM_A:(g + 1) * HEAD_DIM_A]
        bf_ref[...] = src.astype(BF16)
    qi = mm(COL_QI, COL_KW).astype(BF16)
    for h in range(N_IDX_HEADS):
        qi_ref[h] = qi[:, h * IDX_DIM:(h + 1) * IDX_DIM]
    kw = mm(COL_KW, COL_HB)
    ki = _layer_norm(kw[:, :IDX_DIM], lnw_ref[...], lnb_ref[...])
    ki_ref[...] = ki
    kib_ref[...] = ki.astype(BF16)
    wi_ref[...] = kw[:, IDX_DIM:IDX_DIM + N_IDX_HEADS]
    hb_ref[...] = mm(COL_HB, COL_END)


def _inproj(x2d, w_pad, lnw, lnb):
    n = x2d.shape[0]
    tm = min(INPROJ_ROWS, n)
    row = lambda width: pl.BlockSpec((tm, width), lambda i: (i, 0))
    full = lambda a: pl.BlockSpec(a.shape, lambda i: (0,) * a.ndim)
    heads = lambda nh, d: pl.BlockSpec((nh, tm, d), lambda i: (0, i, 0))
    kv = N_KV_A * HEAD_DIM_A
    kv3 = pl.BlockSpec((tm, N_KV_A, HEAD_DIM_A), lambda i: (i, 0, 0))
    sds = jax.ShapeDtypeStruct
    return pl.pallas_call(
        _inproj_body,
        grid=(n // tm,),
        in_specs=[row(D_MODEL), full(w_pad), full(lnw), full(lnb)],
        out_specs=[heads(N_HEADS_A, HEAD_DIM_A), kv3, kv3, row(kv), row(kv), heads(N_IDX_HEADS, IDX_DIM),
                   row(IDX_DIM), row(IDX_DIM), row(N_IDX_HEADS), row(4 * WIDTH_B)],
        out_shape=[sds((N_HEADS_A, n, HEAD_DIM_A), BF16), sds((n, N_KV_A, HEAD_DIM_A), F32),
                   sds((n, N_KV_A, HEAD_DIM_A), F32), sds((n, kv), BF16), sds((n, kv), BF16),
                   sds((N_IDX_HEADS, n, IDX_DIM), BF16), sds((n, IDX_DIM), F32), sds((n, IDX_DIM), BF16),
                   sds((n, N_IDX_HEADS), F32), sds((n, 4 * WIDTH_B), F32)],
        compiler_params=_cparams("parallel"),
        name="inproj",
    )(x2d, w_pad, lnw, lnb)


def _order_bits_to_f32(u):
    key = u ^ INT_MIN
    bits = key ^ ((key >> 31) & 0x7FFFFFFF)
    f = lax.bitcast_convert_type(bits, F32)
    return jnp.where(u >= 0, jnp.where(u <= 0x007FFFFF, NEG_INF, f), f)


def _dsa_body(qa_ref, qi_ref, wi_ref, kidx_ref, k_ref, vt_ref, o_ref, sc_scr, bias_scr, term_scr, s_scr, acc_scr,
              *, s_pad, s_real, q_rows, pair, pos0, topk):
    kb_rows = KEY_BLOCK
    heads_per_kv = N_HEADS_A // N_KV_A
    qb = pair * q_rows
    cdim = pair * HEAD_DIM_A
    v_rows = pair * V_ROWS
    j = pl.program_id(1)
    q_lo = pos0 + j * q_rows
    lane = lax.broadcasted_iota(I32, (1, qb), 1)
    q_chunk = (q_lo + (lane & (q_rows - 1))) >> CHUNK_SHIFT
    k_lim = (((q_lo + q_rows - 1) >> CHUNK_SHIFT) + 1) * CHUNK
    nkb = jnp.minimum(s_pad // kb_rows, (k_lim + kb_rows - 1) // kb_rows)

    def rows_of(kb):
        return pl.ds(pl.multiple_of(kb * kb_rows, kb_rows), kb_rows)

    def fold8(a):
        return a.reshape(kb_rows // 8, 8, qb)

    def score_blk(kb, _):
        rows = rows_of(kb)
        kidx = kidx_ref[0, rows, :]
        for h in range(N_IDX_HEADS):
            term_scr[h] = jnp.maximum(_dot_nt(kidx, qi_ref[h, 0]), 0.0) * wi_ref[0, h:h + 1, :]
        total = (term_scr[0] + term_scr[1]) + (term_scr[2] + term_scr[3])
        spos = kb * kb_rows + lax.broadcasted_iota(I32, (kb_rows, qb), 0)
        sc = jnp.where((spos >> CHUNK_SHIFT) <= q_chunk, total * IDX_SCALE + 0.0, NEG_INF)
        if s_real < s_pad:
            sc = jnp.where(spos < s_real, sc, NEG_INF)
        sc_scr[rows, :] = sc
        return 0

    lax.fori_loop(0, nkb, score_blk, 0)

    def count(thr, strict):
        def body(kb, acc):
            blk = sc_scr[rows_of(kb), :]
            hit = (blk > thr) if strict else (blk >= thr)
            return acc + jnp.sum(fold8(jnp.where(hit, 1, 0).astype(I32)), axis=0)
        acc = lax.fori_loop(0, nkb, body, jnp.zeros((8, qb), I32))
        return jnp.sum(acc, axis=0, keepdims=True)

    def bit_step(i, carry):
        prefix, cnt = carry
        cand = prefix | lax.shift_left(jnp.int32(1), 31 - i)
        c = count(_order_bits_to_f32(cand), False)
        take = c >= topk
        return jnp.where(take, cand, prefix), jnp.where(take, c, cnt)

    prefix, cnt_ge = lax.fori_loop(0, 32, bit_step,
                                   (jnp.zeros((1, qb), I32), jnp.full((1, qb), nkb * kb_rows, I32)))
    tau = _order_bits_to_f32(prefix)
    finite_tau = tau > NEG_INF
    tau_floor = jnp.maximum(tau, jnp.finfo(F32).min)
    tie_lanes = jnp.where(finite_tau, jnp.where(cnt_ge > topk, 1, 0), 0)
    has_ties = jnp.max(tie_lanes) > 0

    @pl.when(jnp.logical_not(has_ties))
    def _():
        def body(kb, _):
            blk = sc_scr[rows_of(kb), :]
            bias_scr[rows_of(kb), :] = jnp.where(blk >= tau_floor, 0.0, NEG_INF)
            return 0
        lax.fori_loop(0, nkb, body, 0)

    @pl.when(has_ties)
    def _():
        need = jnp.where(finite_tau, (topk - count(tau, True)).astype(F32), 0.0)
        r = lax.broadcasted_iota(I32, (kb_rows, kb_rows), 0)
        c = lax.broadcasted_iota(I32, (kb_rows, kb_rows), 1)
        tril = jnp.where(r >= c, 1.0, 0.0).astype(BF16)

        def body(kb, seen):
            blk = sc_scr[rows_of(kb), :]
            eq = blk == tau
            rank = _dot(tril, jnp.where(eq, 1.0, 0.0).astype(BF16)) + seen
            tie_bias = jnp.where(eq, jnp.where(rank <= need, 0.0, NEG_INF), NEG_INF)
            bias_scr[rows_of(kb), :] = jnp.where(blk > tau, 0.0, tie_bias)
            return rank[kb_rows - 1:kb_rows, :]
        lax.fori_loop(0, nkb, body, jnp.zeros((1, qb), F32))

    def pass1(kb, m8):
        rows = rows_of(kb)
        bias = bias_scr[rows, :]
        out = []
        for h in range(N_HEADS_A):
            g = h // heads_per_kv
            s = _dot_nt(k_ref[0, rows, g * cdim:(g + 1) * cdim], qa_ref[h, 0]) + bias
            s_scr[h, rows, :] = s
            out.append(jnp.maximum(m8[h], jnp.max(fold8(s), axis=0)))
        return tuple(out)

    m8 = lax.fori_loop(0, nkb, pass1, tuple(jnp.full((8, qb), NEG_INF, F32) for _ in range(N_HEADS_A)))
    m = [jnp.max(x, axis=0, keepdims=True) for x in m8]

    acc_scr[...] = jnp.zeros_like(acc_scr)

    def pass2(kb, _):
        rows = rows_of(kb)
        for h in range(N_HEADS_A):
            g = h // heads_per_kv
            p = jnp.exp2(s_scr[h, rows, :] - m[h]).astype(BF16)
            acc_scr[h] += _dot(vt_ref[0, g * v_rows:(g + 1) * v_rows, rows], p)
        return 0

    lax.fori_loop(0, nkb, pass2, 0)
    for h in range(N_HEADS_A):
        a = acc_scr[h]
        num, den = a[:HEAD_DIM_A], a[HEAD_DIM_A:HEAD_DIM_A + 1]
        for mb in range(1, pair):
            mine = lane >= mb * q_rows
            r0 = mb * V_ROWS
            num = jnp.where(mine, a[r0:r0 + HEAD_DIM_A], num)
            den = jnp.where(mine, a[r0 + HEAD_DIM_A:r0 + HEAD_DIM_A + 1], den)
        o_ref[0, h * HEAD_DIM_A:(h + 1) * HEAD_DIM_A, :] = (num * (1.0 / den)).astype(BF16)


def _dsa_pairing(b, t):
    return 2 if (2 * t <= 128 and b % 2 == 0) else 1


def _dsa(qa_h, qi_h, wi, keys, *, s_real, pos0, topk):
    _, b, t, _ = qa_h.shape
    s_pad = keys[0].shape[1]
    q_rows = min(256, t)
    pair = _dsa_pairing(b, t)
    bp = b // pair
    lanes, cdim, v_rows = pair * q_rows, pair * HEAD_DIM_A, pair * V_ROWS

    def block_diag(q):
        if pair == 1:
            return q
        nh, d = q.shape[0], q.shape[3]
        eye = jnp.eye(pair, dtype=q.dtype)
        q6 = q.reshape(nh, bp, pair, t, 1, d) * eye.reshape(1, 1, pair, 1, pair, 1)
        return q6.reshape(nh, bp, pair * t, pair * d)

    wi_op = wi.reshape(bp, pair, t, N_IDX_HEADS).transpose(0, 3, 1, 2).reshape(bp, N_IDX_HEADS, pair * t)
    ki_all, k_all, v_all = keys
    ki_op = ki_all.reshape(bp, pair, s_pad, IDX_DIM).transpose(0, 2, 1, 3).reshape(bp, s_pad, cdim)
    k_op = (k_all.reshape(bp, pair, s_pad, N_KV_A, HEAD_DIM_A).transpose(0, 2, 3, 1, 4)
            .reshape(bp, s_pad, N_KV_A * cdim))
    vt = v_all.reshape(b, s_pad, N_KV_A, HEAD_DIM_A).transpose(0, 2, 3, 1)
    vt = jnp.concatenate([vt, jnp.ones((b, N_KV_A, V_ROWS - HEAD_DIM_A, s_pad), vt.dtype)], axis=2)
    vt_op = (vt.reshape(bp, pair, N_KV_A, V_ROWS, s_pad).transpose(0, 2, 1, 3, 4)
             .reshape(bp, N_KV_A * v_rows, s_pad))

    body = functools.partial(_dsa_body, s_pad=s_pad, s_real=s_real, q_rows=q_rows, pair=pair, pos0=pos0,
                             topk=topk)
    o_t = pl.pallas_call(
        body,
        grid=(bp, t // q_rows),
        in_specs=[
            pl.BlockSpec((N_HEADS_A, 1, lanes, cdim), lambda i, j: (0, i, j, 0)),
            pl.BlockSpec((N_IDX_HEADS, 1, lanes, cdim), lambda i, j: (0, i, j, 0)),
            pl.BlockSpec((1, N_IDX_HEADS, lanes), lambda i, j: (i, 0, j)),
            pl.BlockSpec((1, s_pad, cdim), lambda i, j: (i, 0, 0)),
            pl.BlockSpec((1, s_pad, N_KV_A * cdim), lambda i, j: (i, 0, 0)),
            pl.BlockSpec((1, N_KV_A * v_rows, s_pad), lambda i, j: (i, 0, 0)),
        ],
        out_specs=pl.BlockSpec((1, WIDTH_A, lanes), lambda i, j: (i, 0, j)),
        out_shape=jax.ShapeDtypeStruct((bp, WIDTH_A, pair * t), BF16),
        scratch_shapes=[pltpu.VMEM((s_pad, lanes), F32), pltpu.VMEM((s_pad, lanes), F32),
                        pltpu.VMEM((N_IDX_HEADS, KEY_BLOCK, lanes), F32),
                        pltpu.VMEM((N_HEADS_A, s_pad, lanes), F32), pltpu.VMEM((N_HEADS_A, v_rows, lanes), F32)],
        compiler_params=_cparams("parallel", "parallel"),
        name="dsa",
    )(block_diag(qa_h), block_diag(qi_h), wi_op, ki_op, k_op, vt_op)
    return o_t.reshape(bp, WIDTH_A, pair, t).transpose(0, 2, 3, 1).reshape(b * t, WIDTH_A)


def _split3(a):
    hi = a.astype(BF16)
    r1 = a - hi.astype(F32)
    mid = r1.astype(BF16)
    lo = (r1 - mid.astype(F32)).astype(BF16)
    return hi, mid, lo


def _hgrn_body(hb_ref, s0_ref, lb_ref, nw_ref, ob_ref, sfin_ref, state_scr, *, tb, nb):
    t = pl.program_id(1)

    @pl.when(t == 0)
    def _():
        for bi, h in itertools.product(range(nb), range(N_HEADS_B)):
            state_scr[bi, h] = s0_ref[bi, h].T

    lb = lb_ref[...]
    r = lax.broadcasted_iota(I32, (CHUNK, CHUNK), 0)
    c = lax.broadcasted_iota(I32, (CHUNK, CHUNK), 1)
    causal = r >= c
    tril = jnp.where(causal, 1.0, 0.0).astype(BF16)

    for ci, bi in itertools.product(range(tb // CHUNK), range(nb)):
        rows = slice(ci * CHUNK, (ci + 1) * CHUNK)
        f = lb + (1.0 - lb) * jax.nn.sigmoid(hb_ref[bi, rows, WIDTH_B:2 * WIDTH_B])
        parts = _split3(jnp.log(f))
        bcum = _dot(tril, parts[0]) + _dot(tril, parts[1]) + _dot(tril, parts[2])
        for h in range(N_HEADS_B):
            lanes = slice(h * HEAD_DIM_B, (h + 1) * HEAD_DIM_B)
            q = hb_ref[bi, rows, lanes]
            k = 1.0 - f[:, lanes]
            v = hb_ref[bi, rows, 2 * WIDTH_B + h * HEAD_DIM_B:2 * WIDTH_B + (h + 1) * HEAD_DIM_B]
            gate = hb_ref[bi, rows, 3 * WIDTH_B + h * HEAD_DIM_B:3 * WIDTH_B + (h + 1) * HEAD_DIM_B]
            bh = bcum[:, lanes]
            b_last = bh[CHUNK - 1:CHUNK, :]
            vb = v.astype(BF16)
            state_t = state_scr[bi, h]
            o_inter = _dot_nt((q * jnp.exp(bh)).astype(BF16), state_t.astype(BF16))
            a_rows = []
            for i in range(CHUNK // HGRN_SUB):
                lo, hi = i * HGRN_SUB, (i + 1) * HGRN_SUB
                ref = bh[lo - 1:lo, :] if i else jnp.zeros((1, HEAD_DIM_B), F32)
                qs = (q[lo:hi] * jnp.exp(bh[lo:hi] - ref)).astype(BF16)
                ks = (k * jnp.exp(ref - bh)).astype(BF16)
                a_rows.append(_dot_nt(qs, ks))
            a = jnp.where(causal, jnp.concatenate(a_rows, axis=0), 0.0)
            o = o_inter + _dot(a.astype(BF16), vb)
            kdec = (k * jnp.exp(b_last - bh)).astype(BF16)
            state_scr[bi, h] = jnp.exp(b_last) * state_t + _dot_tn(vb, kdec)
            ms = jnp.mean(o * o, axis=-1, keepdims=True)
            y = o * lax.rsqrt(ms + RMS_EPS) * nw_ref[...] * (gate * jax.nn.sigmoid(gate))
            ob_ref[bi, rows, lanes] = y.astype(BF16)

    @pl.when(t == pl.num_programs(1) - 1)
    def _():
        for bi, h in itertools.product(range(nb), range(N_HEADS_B)):
            sfin_ref[bi, h] = state_scr[bi, h].T


def _hgrn(hb, s0, lb, nw):
    b, t, _ = hb.shape
    tb = min(256, t)
    nb = 2 if b % 2 == 0 else 1
    return pl.pallas_call(
        functools.partial(_hgrn_body, tb=tb, nb=nb),
        grid=(b // nb, t // tb),
        in_specs=[
            pl.BlockSpec((nb, tb, 4 * WIDTH_B), lambda i, j: (i, j, 0)),
            pl.BlockSpec((nb, N_HEADS_B, HEAD_DIM_B, HEAD_DIM_B), lambda i, j: (i, 0, 0, 0)),
            pl.BlockSpec((1, WIDTH_B), lambda i, j: (0, 0)),
            pl.BlockSpec((1, HEAD_DIM_B), lambda i, j: (0, 0)),
        ],
        out_specs=[
            pl.BlockSpec((nb, tb, WIDTH_B), lambda i, j: (i, j, 0)),
            pl.BlockSpec((nb, N_HEADS_B, HEAD_DIM_B, HEAD_DIM_B), lambda i, j: (i, 0, 0, 0)),
        ],
        out_shape=[jax.ShapeDtypeStruct((b, t, WIDTH_B), BF16),
                   jax.ShapeDtypeStruct((b, N_HEADS_B, HEAD_DIM_B, HEAD_DIM_B), F32)],
        scratch_shapes=[pltpu.VMEM((nb, N_HEADS_B, HEAD_DIM_B, HEAD_DIM_B), F32)],
        compiler_params=_cparams("parallel", "arbitrary"),
        name="hgrn",
    )(hb, s0, lb, nw)


def _outproj_body(oaa_ref, oab_ref, oba_ref, obb_ref, xa_ref, xb_ref, w_ref, lnw_ref, lnb_ref, wrh_ref, wrl_ref, br_ref,
                  x1_ref, lg_ref, *, steps_a):
    first = pl.program_id(0) < steps_a
    pick = lambda a_ref, b_ref: jnp.where(first, a_ref[...], b_ref[...])
    x = pick(xa_ref, xb_ref)
    y = _dot(pick(oaa_ref, oab_ref), w_ref[:WIDTH_A, :]) + _dot(pick(oba_ref, obb_ref), w_ref[WIDTH_A:, :])
    x1 = _layer_norm(DN_ALPHA * x + y, lnw_ref[...], lnb_ref[...])
    x1_ref[...] = x1
    hi = x1.astype(BF16)
    lo = (x1 - hi.astype(F32)).astype(BF16)
    lg_ref[...] = (_dot_nt(wrh_ref[...], hi) + _dot_nt(wrh_ref[...], lo) + _dot_nt(wrl_ref[...], hi)
                   + br_ref[...])


def _two_group_rows(tm, width, steps_a):
    return (pl.BlockSpec((tm, width), lambda i, *_: (jnp.minimum(i, steps_a - 1), 0)),
            pl.BlockSpec((tm, width), lambda i, *_: (jnp.maximum(i - steps_a, 0), 0)))


def _outproj(mix_a, mix_b, x_a, x_b, w_out, lnw, lnb, wr_hi, wr_lo, br):
    tm = MOE_TILE
    assert x_a.shape[0] % tm == 0 and x_b.shape[0] % tm == 0
    n = x_a.shape[0] + x_b.shape[0]
    steps_a = x_a.shape[0] // tm
    row = lambda width: pl.BlockSpec((tm, width), lambda i: (i, 0))
    full = lambda a: pl.BlockSpec(a.shape, lambda i: (0,) * a.ndim)
    return pl.pallas_call(
        functools.partial(_outproj_body, steps_a=steps_a),
        grid=(n // tm,),
        in_specs=[*_two_group_rows(tm, WIDTH_A, steps_a), *_two_group_rows(tm, WIDTH_B, steps_a),
                  *_two_group_rows(tm, D_MODEL, steps_a), full(w_out), full(lnw), full(lnb), full(wr_hi),
                  full(wr_lo), full(br)],
        out_specs=[row(D_MODEL), pl.BlockSpec((N_EXPERTS, tm), lambda i: (0, i))],
        out_shape=[jax.ShapeDtypeStruct((n, D_MODEL), F32), jax.ShapeDtypeStruct((N_EXPERTS, n), F32)],
        compiler_params=_cparams("parallel"),
        name="outproj",
    )(mix_a[0], mix_b[0], mix_a[1], mix_b[1], x_a, x_b, w_out, lnw, lnb, wr_hi, wr_lo, br)


def _route_body(lg_ref, lpos_ref, gate_ref, tc_ref, base_ref, cnt_ref, cnt_scr, *, tr):
    @pl.when(pl.program_id(0) == 0)
    def _():
        cnt_scr[...] = jnp.zeros_like(cnt_scr)

    l = lg_ref[...]
    rows = lax.broadcasted_iota(I32, (N_EXPERTS, tr), 0)
    vals, hots = [], []
    for j in range(TOP_K):
        m = jnp.max(l, axis=0, keepdims=True)
        idx = jnp.min(jnp.where(l == m, rows, N_EXPERTS), axis=0, keepdims=True)
        hot = rows == idx
        vals.append(m)
        hots.append(hot)
        l = jnp.where(hot, NEG_INF, l)
    es = [jnp.exp(v - vals[0]) for v in vals]
    inv = 1.0 / (es[0] + es[1] + es[2] + es[3])
    for j in range(TOP_K):
        gate_ref[j:j + 1, :] = es[j] * inv
    chosen = jnp.zeros((N_EXPERTS, tr), F32)
    for hot in hots:
        chosen = chosen + jnp.where(hot, 1.0, 0.0)
    r = lax.broadcasted_iota(I32, (tr, tr), 0)
    c = lax.broadcasted_iota(I32, (tr, tr), 1)
    before = jnp.where(r < c, 1.0, 0.0).astype(BF16)
    prior = _dot(chosen.astype(BF16), before)
    units = jnp.ceil(jnp.sum(chosen, axis=1, keepdims=True) * (1.0 / ROW_ALIGN))
    er = lax.broadcasted_iota(I32, (N_EXPERTS, N_EXPERTS), 0)
    ec = lax.broadcasted_iota(I32, (N_EXPERTS, N_EXPERTS), 1)
    earlier = jnp.where(ec < er, 1.0, 0.0).astype(BF16)
    run_len = jnp.broadcast_to(units, (N_EXPERTS, 128)) * ROW_ALIGN
    run_start = _dot(earlier, jnp.broadcast_to(units, (N_EXPERTS, 128)).astype(BF16)) * ROW_ALIGN
    where_in_tile = prior + run_start[:, 0:1]
    for j in range(TOP_K):
        lpos_ref[j:j + 1, :] = jnp.sum(jnp.where(hots[j], where_in_tile, 0.0), axis=0,
                                       keepdims=True).astype(I32)
    tc_ref[0] = run_len.astype(I32)
    base_ref[0] = cnt_scr[...].astype(I32)
    cnt_scr[...] = cnt_scr[...] + run_len
    cnt_ref[...] = cnt_scr[...].astype(I32)


def _route(logits_t):
    n = logits_t.shape[1]
    tr = MOE_TILE
    tiles = n // tr
    tok = lambda rows: pl.BlockSpec((rows, tr), lambda i: (0, i))
    per_tile = pl.BlockSpec((1, N_EXPERTS, 128), lambda i: (i, 0, 0))
    return pl.pallas_call(
        functools.partial(_route_body, tr=tr),
        grid=(tiles,),
        in_specs=[tok(N_EXPERTS)],
        out_specs=[tok(TOP_K), tok(TOP_K), per_tile, per_tile, pl.BlockSpec((N_EXPERTS, 128), lambda i: (0, 0))],
        out_shape=[jax.ShapeDtypeStruct((TOP_K, n), I32), jax.ShapeDtypeStruct((TOP_K, n), F32),
                   jax.ShapeDtypeStruct((tiles, N_EXPERTS, 128), I32),
                   jax.ShapeDtypeStruct((tiles, N_EXPERTS, 128), I32),
                   jax.ShapeDtypeStruct((N_EXPERTS, 128), I32)],
        scratch_shapes=[pltpu.VMEM((N_EXPERTS, 128), F32)],
        compiler_params=_cparams("arbitrary"),
        name="route",
    )(logits_t)


def _for_each_run_piece(tc_ref, g0_ref, t, fn):
    def per_expert(e, local):
        c = tc_ref[t, e]
        glob = g0_ref[t, e]
        lo = local
        for p in RUN_PIECES:
            take = (c & p) != 0

            @pl.when(take)
            def _():
                fn(pl.multiple_of(lo, ROW_ALIGN), pl.multiple_of(glob, ROW_ALIGN), p)

            step = jnp.where(take, p, 0)
            lo = lo + step
            glob = glob + step
        return local + c

    lax.fori_loop(0, N_EXPERTS, per_expert, 0)


def _dispatch_body(tc_ref, g0_ref, seg_ref, lpos_ref, x_ref, buf_ref, stage, zero_scr, zsem, sems, *, n_blocks):
    i = pl.program_id(0)
    slot = i % 2

    def run_copy(slot_, lo, glob, rows):
        return pltpu.make_async_copy(stage.at[slot_, pl.ds(lo, rows), :], buf_ref.at[pl.ds(glob, rows), :],
                                     sems.at[slot_])

    @pl.when(i == 0)
    def _():
        zero_scr[...] = jnp.zeros_like(zero_scr)
        n_used = seg_ref[1, N_EXPERTS - 1] // EXPERT_ROWS

        def block_copy(start):
            return pltpu.make_async_copy(
                zero_scr, buf_ref.at[pl.ds(pl.multiple_of(start, EXPERT_ROWS), EXPERT_ROWS), :], zsem)

        def clears(action):
            for e in range(N_EXPERTS):
                @pl.when(seg_ref[1, e] > seg_ref[0, e])
                def _():
                    action(block_copy(seg_ref[1, e] - EXPERT_ROWS))

                @pl.when(n_used + e < n_blocks)
                def _():
                    action(block_copy((n_used + e) * EXPERT_ROWS))

        clears(lambda cp: cp.start())
        clears(lambda cp: cp.wait())

    r = lax.broadcasted_iota(I32, (MOE_SLOTS, MOE_TILE), 0)
    onehot = jnp.zeros((MOE_SLOTS, MOE_TILE), F32)
    for j in range(TOP_K):
        onehot = jnp.where(r == lpos_ref[j:j + 1, :], 1.0, onehot)
    stage[slot] = _dot(onehot.astype(BF16), x_ref[...].astype(BF16)).astype(BF16)

    @pl.when(i > 0)
    def _():
        _for_each_run_piece(tc_ref, g0_ref, i - 1,
                            lambda lo, glob, rows: run_copy(1 - slot, lo, glob, rows).wait())

    _for_each_run_piece(tc_ref, g0_ref, i, lambda lo, glob, rows: run_copy(slot, lo, glob, rows).start())

    @pl.when(i == pl.num_programs(0) - 1)
    def _():
        _for_each_run_piece(tc_ref, g0_ref, i, lambda lo, glob, rows: run_copy(slot, lo, glob, rows).wait())


def _dispatch(x1, lpos_t, tc, g0, seg, n_rows):
    n = x1.shape[0]
    return pl.pallas_call(
        functools.partial(_dispatch_body, n_blocks=n_rows // EXPERT_ROWS),
        grid_spec=pltpu.PrefetchScalarGridSpec(
            num_scalar_prefetch=3,
            grid=(n // MOE_TILE,),
            in_specs=[pl.BlockSpec((TOP_K, MOE_TILE), lambda i, *_: (0, i)),
                      pl.BlockSpec((MOE_TILE, D_MODEL), lambda i, *_: (i, 0))],
            out_specs=pl.BlockSpec(memory_space=pl.ANY),
            scratch_shapes=[pltpu.VMEM((2, MOE_SLOTS, D_MODEL), BF16),
                            pltpu.VMEM((EXPERT_ROWS, D_MODEL), BF16),
                            pltpu.SemaphoreType.DMA(()), pltpu.SemaphoreType.DMA((2,))],
        ),
        out_shape=jax.ShapeDtypeStruct((n_rows, D_MODEL), BF16),
        compiler_params=_cparams("arbitrary"),
        name="dispatch",
    )(tc, g0, seg, lpos_t, x1)


def _experts_body(be_ref, nu_ref, x_ref, wgu_ref, bgu_ref, wd_ref, bd_ref, o_ref, wgu_bf, wd_bf):
    i = pl.program_id(0)

    @pl.when(jnp.logical_or(i == 0, be_ref[i] != be_ref[jnp.maximum(i - 1, 0)]))
    def _():
        for r in range(0, D_MODEL, 256):
            wgu_bf[r:r + 256, :] = wgu_ref[0, r:r + 256, :].astype(BF16)
            wd_bf[r:r + 256, :] = wd_ref[0, r:r + 256, :].astype(BF16)

    @pl.when(i < nu_ref[0])
    def _():
        gu = _dot(x_ref[...], wgu_bf[...]) + bgu_ref[0]
        gate = jnp.minimum(gu[:, :D_EXPERT], SWIGLU_LIMIT)
        up = jnp.clip(gu[:, D_EXPERT:], -SWIGLU_LIMIT, SWIGLU_LIMIT)
        hdn = (up + 1.0) * gate * jax.nn.sigmoid(SWIGLU_ALPHA * gate)
        o_ref[...] = (_dot(hdn.astype(BF16), wd_bf[...]) + bd_ref[0]).astype(BF16)

    @pl.when(i >= nu_ref[0])
    def _():
        o_ref[...] = jnp.zeros_like(o_ref)


def _experts(buf, block_e, n_used, w_gu, b_gu, w_down, b_down):
    n_blocks = buf.shape[0] // EXPERT_ROWS
    rows = pl.BlockSpec((EXPERT_ROWS, D_MODEL), lambda i, be, nu: (jnp.minimum(i, nu[0] - 1), 0))
    per_e = lambda a: pl.BlockSpec((1,) + a.shape[1:], lambda i, be, nu: (be[i],) + (0,) * (a.ndim - 1))
    return pl.pallas_call(
        _experts_body,
        grid_spec=pltpu.PrefetchScalarGridSpec(
            num_scalar_prefetch=2,
            grid=(n_blocks,),
            in_specs=[rows, per_e(w_gu), per_e(b_gu), per_e(w_down), per_e(b_down)],
            out_specs=pl.BlockSpec((EXPERT_ROWS, D_MODEL), lambda i, be, nu: (i, 0)),
            scratch_shapes=[pltpu.VMEM((D_MODEL, 2 * D_EXPERT), BF16), pltpu.VMEM((D_EXPERT, D_MODEL), BF16)],
        ),
        out_shape=jax.ShapeDtypeStruct(buf.shape, BF16),
        compiler_params=_cparams("arbitrary"),
        name="experts",
    )(block_e, n_used, buf, w_gu, b_gu, w_down, b_down)


def _final_body(tc_ref, g0_ref, lpos_ref, gate_ref, x1_ref, pa_ref, pb_ref, lnw_ref, lnb_ref, wg_ref, bg_ref, wp_ref,
                eo_ref, ya_ref, yb_ref, stage, sems, *, steps_a):
    i = pl.program_id(0)
    slot = i % 2

    def run_copy(slot_, lo, glob, rows):
        return pltpu.make_async_copy(eo_ref.at[pl.ds(glob, rows), :], stage.at[slot_, pl.ds(lo, rows), :],
                                     sems.at[slot_])

    def fetch(tile, slot_):
        stage[slot_] = jnp.zeros(stage.shape[1:], stage.dtype)
        _for_each_run_piece(tc_ref, g0_ref, tile, lambda lo, glob, rows: run_copy(slot_, lo, glob, rows).start())

    @pl.when(i == 0)
    def _():
        fetch(0, 0)

    @pl.when(i + 1 < pl.num_programs(0))
    def _():
        fetch(i + 1, 1 - slot)

    _for_each_run_piece(tc_ref, g0_ref, i, lambda lo, glob, rows: run_copy(slot, lo, glob, rows).wait())

    r = lax.broadcasted_iota(I32, (MOE_TILE, MOE_SLOTS), 1)
    g = jnp.zeros((MOE_TILE, MOE_SLOTS), F32)
    for j in range(TOP_K):
        g = jnp.where(r == lpos_ref[:, j:j + 1], gate_ref[:, j:j + 1], g)
    g_hi = g.astype(BF16)
    g_lo = (g - g_hi.astype(F32)).astype(BF16)
    g2 = jnp.concatenate([g_hi, g_lo], axis=0)
    both = _dot(g2, stage[slot])
    moe = both[:MOE_TILE] + both[MOE_TILE:]
    x2 = _layer_norm(DN_ALPHA * x1_ref[...] + moe, lnw_ref[...], lnb_ref[...])
    ple_gate = jax.nn.sigmoid(_dot(x2.astype(BF16), wg_ref[...]) + bg_ref[...])
    p = jnp.where(i < steps_a, pa_ref[...], pb_ref[...])
    y = x2 + ple_gate * _dot(p.astype(BF16), wp_ref[...])

    @pl.when(i < steps_a)
    def _():
        ya_ref[...] = y

    @pl.when(i >= steps_a)
    def _():
        yb_ref[...] = y


def _final(tc, g0, lpos, gate, x1, p_a, p_b, lnw, lnb, w_gate, b_gate, w_ple, expert_out):
    n = x1.shape[0]
    steps_a = p_a.shape[0] // MOE_TILE
    row = lambda width: pl.BlockSpec((MOE_TILE, width), lambda i, *_: (i, 0))
    full = lambda a: pl.BlockSpec(a.shape, lambda i, *_: (0,) * a.ndim)
    return pl.pallas_call(
        functools.partial(_final_body, steps_a=steps_a),
        grid_spec=pltpu.PrefetchScalarGridSpec(
            num_scalar_prefetch=2,
            grid=(n // MOE_TILE,),
            in_specs=[row(TOP_K), row(TOP_K), row(D_MODEL), *_two_group_rows(MOE_TILE, D_PLE, steps_a), full(lnw),
                      full(lnb), full(w_gate), full(b_gate), full(w_ple), pl.BlockSpec(memory_space=pl.ANY)],
            out_specs=list(_two_group_rows(MOE_TILE, D_MODEL, steps_a)),
            scratch_shapes=[pltpu.VMEM((2, MOE_SLOTS, D_MODEL), BF16), pltpu.SemaphoreType.DMA((2,))],
        ),
        out_shape=[jax.ShapeDtypeStruct((p_a.shape[0], D_MODEL), F32),
                   jax.ShapeDtypeStruct((p_b.shape[0], D_MODEL), F32)],
        compiler_params=_cparams("arbitrary"),
        name="combine_final",
    )(tc, g0, lpos, gate, x1, p_a, p_b, lnw, lnb, w_gate, b_gate, w_ple, expert_out)


def _round_up(a, m):
    return -(-a // m) * m


def _token_mixers(x, hist_k, hist_v, hist_kidx, s0, pos0, wts):
    b, t, _ = x.shape
    n = b * t
    x2d = x.reshape(n, D_MODEL)

    qa, ka, va, ka_bf, va_bf, qi, ki, ki_bf, wi, hb = _inproj(x2d, wts["w_in"], wts["idx_lnw"], wts["idx_lnb"])

    k_new = ka_bf.reshape(b, t, N_KV_A * HEAD_DIM_A)
    v_new = va_bf.reshape(b, t, N_KV_A * HEAD_DIM_A)
    ki_new = ki_bf.reshape(b, t, IDX_DIM)
    q_ops = (qa.reshape(N_HEADS_A, b, t, HEAD_DIM_A), qi.reshape(N_IDX_HEADS, b, t, IDX_DIM),
             wi.reshape(b, t, N_IDX_HEADS))
    past = 0 if hist_k is None else hist_k.shape[1]
    s_real = past + t
    topk = min(TOPK_MAX, s_real // 4)
    if past:
        k_all = jnp.concatenate([hist_k.reshape(b, past, -1).astype(BF16), k_new], axis=1)
        v_all = jnp.concatenate([hist_v.reshape(b, past, -1).astype(BF16), v_new], axis=1)
        ki_all = jnp.concatenate([hist_kidx.astype(BF16), ki_new], axis=1)
    else:
        k_all, v_all, ki_all = k_new, v_new, ki_new
    pad = ((0, 0), (0, _round_up(s_real, KEY_BLOCK) - s_real), (0, 0))
    keys = tuple(jnp.pad(a, pad) for a in (ki_all, k_all, v_all))
    oa = _dsa(*q_ops, keys, s_real=s_real, pos0=pos0, topk=topk)

    ob, s_fin = _hgrn(hb.reshape(b, t, 4 * WIDTH_B), s0, wts["lb"], wts["hgrn_nw"])
    return (oa, ob.reshape(n, WIDTH_B), ka.reshape(b, t, N_KV_A, HEAD_DIM_A), va.reshape(b, t, N_KV_A, HEAD_DIM_A),
            ki.reshape(b, t, IDX_DIM), s_fin)


def _channel_mixer(mix_a, mix_b, x_a, x_b, p_a, p_b, wts):
    shape_a, shape_b = x_a.shape, x_b.shape
    x_a, x_b = x_a.reshape(-1, D_MODEL), x_b.reshape(-1, D_MODEL)
    n = x_a.shape[0] + x_b.shape[0]
    x1, logits_t = _outproj(mix_a, mix_b, x_a, x_b, wts["w_out"], wts["ln1_w"], wts["ln1_b"],
                            wts["wr_hi"], wts["wr_lo"], wts["b_router"])

    tiles = n // MOE_TILE
    lpos_t, gate_t, tile_runs, tile_base, counts = _route(logits_t)
    tc, counts = tile_runs[:, :, 0], counts[:, 0]
    padded = (counts + EXPERT_ROWS - 1) // EXPERT_ROWS * EXPERT_ROWS
    seg_end = jnp.cumsum(padded)
    seg_start = seg_end - padded
    g0 = (seg_start[None, :] + tile_base[:, :, 0]).astype(I32)
    n_blocks = -(-(n * TOP_K + N_EXPERTS * tiles * (ROW_ALIGN - 1) + N_EXPERTS * (EXPERT_ROWS - 1))
                 // EXPERT_ROWS)
    block_row0 = jnp.arange(n_blocks, dtype=I32) * EXPERT_ROWS
    block_e = jnp.minimum(jnp.sum((seg_end[None, :] <= block_row0[:, None]).astype(I32), axis=1), N_EXPERTS - 1)
    n_used = (seg_end[-1:] // EXPERT_ROWS).astype(I32)
    seg = jnp.stack([seg_start, seg_end]).astype(I32)
    buf = _dispatch(x1, lpos_t, tc, g0, seg, n_blocks * EXPERT_ROWS)
    expert_out = _experts(buf, block_e, n_used, wts["w_gu"], wts["b_gu"], wts["w_down"], wts["b_down"])
    y_a, y_b = _final(tc, g0, lpos_t.T, gate_t.T, x1, p_a.reshape(-1, D_PLE), p_b.reshape(-1, D_PLE),
                      wts["ln2_w"], wts["ln2_b"], wts["w_ple_gate"], wts["b_ple_gate"], wts["w_ple"], expert_out)
    return y_a.reshape(shape_a), y_b.reshape(shape_b)


def _prep_weights(w_in, w_out, idx_k_norm_w, idx_k_norm_b, lb, hgrn_norm_w, ln1_w, ln1_b, w_router, b_router,
                  w_gu, b_gu, w_down, b_down, ln2_w, ln2_b, w_ple, w_ple_gate, b_ple_gate):
    n_a = COL_KW + IDX_DIM + N_IDX_HEADS
    w_pad = jnp.concatenate(
        [w_in[:, :n_a], jnp.zeros((D_MODEL, COL_HB - n_a), w_in.dtype), w_in[:, n_a:]], axis=1).astype(BF16)
    wr_t = w_router.T
    wr_hi = wr_t.astype(BF16)
    row = lambda a: a.reshape(1, -1)
    return dict(
        w_in=w_pad, idx_lnw=row(idx_k_norm_w), idx_lnb=row(idx_k_norm_b), lb=row(lb), hgrn_nw=row(hgrn_norm_w),
        w_out=w_out.astype(BF16), ln1_w=row(ln1_w), ln1_b=row(ln1_b),
        wr_hi=wr_hi, wr_lo=(wr_t - wr_hi.astype(F32)).astype(BF16), b_router=b_router.reshape(N_EXPERTS, 1),
        w_gu=w_gu, b_gu=b_gu.reshape(N_EXPERTS, 1, 2 * D_EXPERT),
        w_down=w_down, b_down=b_down.reshape(N_EXPERTS, 1, D_MODEL),
        ln2_w=row(ln2_w), ln2_b=row(ln2_b), w_ple=w_ple.astype(BF16), w_ple_gate=w_ple_gate.astype(BF16),
        b_ple_gate=row(b_ple_gate))


def kernel(x_prompt, x_sample, cache_k, cache_v, cache_kidx, state_hgrn, p_prompt, p_sample, w_in, w_out,
           idx_k_norm_w, idx_k_norm_b, hgrn_lb_logits, hgrn_norm_w, ln1_w, ln1_b, w_router, b_router, w_gu, b_gu,
           w_down, b_down, ln2_w, ln2_b, w_ple, w_ple_gate, b_ple_gate):
    lb_all = jnp.cumsum(jax.nn.softmax(hgrn_lb_logits.astype(F32), axis=0), axis=0)
    xp, xs = x_prompt, x_sample
    outs = [[] for _ in range(8)]
    for i in range(DEPTH):
        wts = _prep_weights(w_in[i], w_out[i], idx_k_norm_w[i], idx_k_norm_b[i], lb_all[i], hgrn_norm_w[i],
                            ln1_w[i], ln1_b[i], w_router[i], b_router[i], w_gu[i], b_gu[i], w_down[i], b_down[i],
                            ln2_w[i], ln2_b[i], w_ple[i], w_ple_gate[i], b_ple_gate[i])
        s0p = jnp.zeros((xp.shape[0], N_HEADS_B, HEAD_DIM_B, HEAD_DIM_B), F32)
        mix_p = _token_mixers(xp, None, None, None, s0p, 0, wts)
        mix_s = _token_mixers(xs, cache_k[i], cache_v[i], cache_kidx[i], state_hgrn[i], cache_k.shape[2], wts)
        xp, xs = _channel_mixer(mix_p, mix_s, xp, xs, p_prompt[i], p_sample[i], wts)
        for lst, val in zip(outs, mix_p[2:] + mix_s[2:]):
            lst.append(val)
    return (xp, xs) + tuple(jnp.stack(l) for l in outs)
```

```python
import functools
import itertools

import jax
import jax.numpy as jnp
from jax import lax
from jax.experimental import pallas as pl
from jax.experimental.pallas import tpu as pltpu

F32 = jnp.float32
BF16 = jnp.bfloat16
I32 = jnp.int32

D_MODEL = 1024
CHUNK = 64
CHUNK_SHIFT = 6
WIDTH_A = 512
HEAD_DIM_A = 64
N_HEADS_A = 8
N_KV_A = 2
N_IDX_HEADS = 4
IDX_DIM = 64
IDX_SCALE = IDX_DIM ** -0.5 * N_IDX_HEADS ** -0.5
TOPK_MAX = 256
WIDTH_B = 512
HEAD_DIM_B = 128
N_HEADS_B = 4
N_EXPERTS = 32
TOP_K = 4
D_EXPERT = 1024
SWIGLU_LIMIT = 7.0
SWIGLU_ALPHA = 1.702
D_PLE = 256
LN_EPS = 1e-5
RMS_EPS = 1e-6
DEPTH = 1
DN_ALPHA = (2 * DEPTH) ** 0.25

COL_QA, COL_KA, COL_VA, COL_QI, COL_KW, COL_HB, COL_END = 0, 512, 640, 768, 1024, 1152, 3200

VMEM_LIMIT = 56 * 1024 * 1024
INPROJ_ROWS = 1024
KEY_BLOCK = 256
V_ROWS = 80
LOG2_E = 1.4426950408889634
HGRN_SUB = 16
EXPERT_ROWS = 512
MOE_TILE = 512
ROW_ALIGN = 16
RUN_PIECES = (512, 256, 128, 64, 32, 16)
MOE_SLOTS = -(-(TOP_K * MOE_TILE + N_EXPERTS * (ROW_ALIGN - 1)) // 256) * 256
NEG_INF = float("-inf")
INT_MIN = -(2 ** 31)


def _cparams(*sem):
    return pltpu.CompilerParams(dimension_semantics=sem, vmem_limit_bytes=VMEM_LIMIT)


def _dot(a, b):
    return jnp.dot(a, b, preferred_element_type=F32)


def _dot_nt(a, b):
    return lax.dot_general(a, b, (((1,), (1,)), ((), ())), preferred_element_type=F32)


def _dot_tn(a, b):
    return lax.dot_general(a, b, (((0,), (0,)), ((), ())), preferred_element_type=F32)


def _layer_norm(z, w, b):
    mu = jnp.mean(z, axis=-1, keepdims=True)
    d = z - mu
    var = jnp.mean(d * d, axis=-1, keepdims=True)
    return d * lax.rsqrt(var + LN_EPS) * w + b


def _inproj_body(x_ref, w_ref, lnw_ref, lnb_ref, qa_ref, ka_ref, va_ref, kab_ref, vab_ref, qi_ref, ki_ref, kib_ref,
                 wi_ref, hb_ref):
    xb = x_ref[...].astype(BF16)

    def mm(c0, c1):
        return _dot(xb, w_ref[:, c0:c1])

    qa = (mm(COL_QA, COL_KA) * (HEAD_DIM_A ** -0.5 * LOG2_E)).astype(BF16)
    for h in range(N_HEADS_A):
        qa_ref[h] = qa[:, h * HEAD_DIM_A:(h + 1) * HEAD_DIM_A]
    for src, out3_ref, bf_ref in ((mm(COL_KA, COL_VA), ka_ref, kab_ref), (mm(COL_VA, COL_QI), va_ref, vab_ref)):
        for g in range(N_KV_A):
            out3_ref[:, g, :] = src[:, g * HEAD_DIM_A:(g + 1) * HEAD_DIM_A]
        bf_ref[...] = src.astype(BF16)
    qi = mm(COL_QI, COL_KW).astype(BF16)
    for h in range(N_IDX_HEADS):
        qi_ref[h] = qi[:, h * IDX_DIM:(h + 1) * IDX_DIM]
    kw = mm(COL_KW, COL_HB)
    ki = _layer_norm(kw[:, :IDX_DIM], lnw_ref[...], lnb_ref[...])
    ki_ref[...] = ki
    kib_ref[...] = ki.astype(BF16)
    wi_ref[...] = kw[:, IDX_DIM:IDX_DIM + N_IDX_HEADS]
    hb_ref[...] = mm(COL_HB, COL_END)


def _inproj(x2d, w_pad, lnw, lnb):
    n = x2d.shape[0]
    tm = min(INPROJ_ROWS, n)
    row = lambda width: pl.BlockSpec((tm, width), lambda i: (i, 0))
    full = lambda a: pl.BlockSpec(a.shape, lambda i: (0,) * a.ndim)
    heads = lambda nh, d: pl.BlockSpec((nh, tm, d), lambda i: (0, i, 0))
    kv = N_KV_A * HEAD_DIM_A
    kv3 = pl.BlockSpec((tm, N_KV_A, HEAD_DIM_A), lambda i: (i, 0, 0))
    sds = jax.ShapeDtypeStruct
    return pl.pallas_call(
        _inproj_body,
        grid=(n // tm,),
        in_specs=[row(D_MODEL), full(w_pad), full(lnw), full(lnb)],
        out_specs=[heads(N_HEADS_A, HEAD_DIM_A), kv3, kv3, row(kv), row(kv), heads(N_IDX_HEADS, IDX_DIM),
                   row(IDX_DIM), row(IDX_DIM), row(N_IDX_HEADS), row(4 * WIDTH_B)],
        out_shape=[sds((N_HEADS_A, n, HEAD_DIM_A), BF16), sds((n, N_KV_A, HEAD_DIM_A), F32),
                   sds((n, N_KV_A, HEAD_DIM_A), F32), sds((n, kv), BF16), sds((n, kv), BF16),
                   sds((N_IDX_HEADS, n, IDX_DIM), BF16), sds((n, IDX_DIM), F32), sds((n, IDX_DIM), BF16),
                   sds((n, N_IDX_HEADS), F32), sds((n, 4 * WIDTH_B), F32)],
        compiler_params=_cparams("parallel"),
        name="inproj",
    )(x2d, w_pad, lnw, lnb)


def _order_bits_to_f32(u):
    key = u ^ INT_MIN
    bits = key ^ ((key >> 31) & 0x7FFFFFFF)
    f = lax.bitcast_convert_type(bits, F32)
    return jnp.where(u >= 0, jnp.where(u <= 0x007FFFFF, NEG_INF, f), f)


def _dsa_body(qa_ref, qi_ref, wi_ref, kidx_ref, k_ref, vt_ref, o_ref, sc_scr, bias_scr, term_scr, s_scr, acc_scr,
              *, s_pad, s_real, q_rows, pair, pos0, topk):
    kb_rows = KEY_BLOCK
    heads_per_kv = N_HEADS_A // N_KV_A
    qb = pair * q_rows
    cdim = pair * HEAD_DIM_A
    v_rows = pair * V_ROWS
    j = pl.program_id(1)
    q_lo = pos0 + j * q_rows
    lane = lax.broadcasted_iota(I32, (1, qb), 1)
    q_chunk = (q_lo + (lane & (q_rows - 1))) >> CHUNK_SHIFT
    k_lim = (((q_lo + q_rows - 1) >> CHUNK_SHIFT) + 1) * CHUNK
    nkb = jnp.minimum(s_pad // kb_rows, (k_lim + kb_rows - 1) // kb_rows)

    def rows_of(kb):
        return pl.ds(pl.multiple_of(kb * kb_rows, kb_rows), kb_rows)

    def fold8(a):
        return a.reshape(kb_rows // 8, 8, qb)

    def over_blocks(body, init):
        def pair(i, carry):
            return body(2 * i + 1, body(2 * i, carry))
        carry = lax.fori_loop(0, nkb // 2, pair, init)
        return lax.cond(nkb % 2 == 1, lambda c: body(nkb - 1, c), lambda c: c, carry)

    def score_blk(kb, _):
        rows = rows_of(kb)
        kidx = kidx_ref[0, rows, :]
        for h in range(N_IDX_HEADS):
            term_scr[h] = jnp.maximum(_dot_nt(kidx, qi_ref[h, 0]), 0.0) * wi_ref[0, h:h + 1, :]
        total = (term_scr[0] + term_scr[1]) + (term_scr[2] + term_scr[3])
        spos = kb * kb_rows + lax.broadcasted_iota(I32, (kb_rows, qb), 0)
        sc = jnp.where((spos >> CHUNK_SHIFT) <= q_chunk, total * IDX_SCALE + 0.0, NEG_INF)
        if s_real < s_pad:
            sc = jnp.where(spos < s_real, sc, NEG_INF)
        sc_scr[rows, :] = sc
        return 0

    over_blocks(score_blk, 0)

    def count(thr, strict):
        def body(kb, acc):
            blk = sc_scr[rows_of(kb), :]
            hit = (blk > thr) if strict else (blk >= thr)
            return acc + jnp.sum(fold8(jnp.where(hit, 1, 0).astype(I32)), axis=0)
        acc = over_blocks(body, jnp.zeros((8, qb), I32))
        return jnp.sum(acc, axis=0, keepdims=True)

    def bit_step(i, carry):
        prefix, cnt = carry
        cand = prefix | lax.shift_left(jnp.int32(1), 31 - i)
        c = count(_order_bits_to_f32(cand), False)
        take = c >= topk
        return jnp.where(take, cand, prefix), jnp.where(take, c, cnt)

    prefix, cnt_ge = lax.fori_loop(0, 32, bit_step,
                                   (jnp.zeros((1, qb), I32), jnp.full((1, qb), nkb * kb_rows, I32)))
    tau = _order_bits_to_f32(prefix)
    finite_tau = tau > NEG_INF
    tau_floor = jnp.maximum(tau, jnp.finfo(F32).min)
    tie_lanes = jnp.where(finite_tau, jnp.where(cnt_ge > topk, 1, 0), 0)
    has_ties = jnp.max(tie_lanes) > 0

    @pl.when(jnp.logical_not(has_ties))
    def _():
        def body(kb, _):
            blk = sc_scr[rows_of(kb), :]
            bias_scr[rows_of(kb), :] = jnp.where(blk >= tau_floor, 0.0, NEG_INF)
            return 0
        over_blocks(body, 0)

    @pl.when(has_ties)
    def _():
        need = jnp.where(finite_tau, (topk - count(tau, True)).astype(F32), 0.0)
        r = lax.broadcasted_iota(I32, (kb_rows, kb_rows), 0)
        c = lax.broadcasted_iota(I32, (kb_rows, kb_rows), 1)
        tril = jnp.where(r >= c, 1.0, 0.0).astype(BF16)

        def body(kb, seen):
            blk = sc_scr[rows_of(kb), :]
            eq = blk == tau
            rank = _dot(tril, jnp.where(eq, 1.0, 0.0).astype(BF16)) + seen
            tie_bias = jnp.where(eq, jnp.where(rank <= need, 0.0, NEG_INF), NEG_INF)
            bias_scr[rows_of(kb), :] = jnp.where(blk > tau, 0.0, tie_bias)
            return rank[kb_rows - 1:kb_rows, :]
        lax.fori_loop(0, nkb, body, jnp.zeros((1, qb), F32))

    def pass1(kb, m8):
        rows = rows_of(kb)
        bias = bias_scr[rows, :]
        out = []
        for h in range(N_HEADS_A):
            g = h // heads_per_kv
            s = _dot_nt(k_ref[0, rows, g * cdim:(g + 1) * cdim], qa_ref[h, 0]) + bias
            s_scr[h, rows, :] = s
            out.append(jnp.maximum(m8[h], jnp.max(fold8(s), axis=0)))
        return tuple(out)

    m8 = over_blocks(pass1, tuple(jnp.full((8, qb), NEG_INF, F32) for _ in range(N_HEADS_A)))
    m = [jnp.max(x, axis=0, keepdims=True) for x in m8]

    acc_scr[...] = jnp.zeros_like(acc_scr)

    def pass2(kb, _):
        rows = rows_of(kb)
        for h in range(N_HEADS_A):
            g = h // heads_per_kv
            p = jnp.exp2(s_scr[h, rows, :] - m[h]).astype(BF16)
            acc_scr[h] += _dot(vt_ref[0, g * v_rows:(g + 1) * v_rows, rows], p)
        return 0

    over_blocks(pass2, 0)
    for h in range(N_HEADS_A):
        a = acc_scr[h]
        num, den = a[:HEAD_DIM_A], a[HEAD_DIM_A:HEAD_DIM_A + 1]
        for mb in range(1, pair):
            mine = lane >= mb * q_rows
            r0 = mb * V_ROWS
            num = jnp.where(mine, a[r0:r0 + HEAD_DIM_A], num)
            den = jnp.where(mine, a[r0 + HEAD_DIM_A:r0 + HEAD_DIM_A + 1], den)
        o_ref[0, h * HEAD_DIM_A:(h + 1) * HEAD_DIM_A, :] = (num * (1.0 / den)).astype(BF16)


def _dsa_pairing(b, t):
    return 2 if (2 * t <= 128 and b % 2 == 0) else 1


def _dsa(qa_h, qi_h, wi, keys, *, s_real, pos0, topk):
    _, b, t, _ = qa_h.shape
    s_pad = keys[0].shape[1]
    q_rows = min(256, t)
    pair = _dsa_pairing(b, t)
    bp = b // pair
    lanes, cdim, v_rows = pair * q_rows, pair * HEAD_DIM_A, pair * V_ROWS

    def block_diag(q):
        if pair == 1:
            return q
        nh, d = q.shape[0], q.shape[3]
        eye = jnp.eye(pair, dtype=q.dtype)
        q6 = q.reshape(nh, bp, pair, t, 1, d) * eye.reshape(1, 1, pair, 1, pair, 1)
        return q6.reshape(nh, bp, pair * t, pair * d)

    wi_op = wi.reshape(bp, pair, t, N_IDX_HEADS).transpose(0, 3, 1, 2).reshape(bp, N_IDX_HEADS, pair * t)
    ki_all, k_all, v_all = keys
    ki_op = ki_all.reshape(bp, pair, s_pad, IDX_DIM).transpose(0, 2, 1, 3).reshape(bp, s_pad, cdim)
    k_op = (k_all.reshape(bp, pair, s_pad, N_KV_A, HEAD_DIM_A).transpose(0, 2, 3, 1, 4)
            .reshape(bp, s_pad, N_KV_A * cdim))
    vt = v_all.reshape(b, s_pad, N_KV_A, HEAD_DIM_A).transpose(0, 2, 3, 1)
    vt = jnp.concatenate([vt, jnp.ones((b, N_KV_A, V_ROWS - HEAD_DIM_A, s_pad), vt.dtype)], axis=2)
    vt_op = (vt.reshape(bp, pair, N_KV_A, V_ROWS, s_pad).transpose(0, 2, 1, 3, 4)
             .reshape(bp, N_KV_A * v_rows, s_pad))

    body = functools.partial(_dsa_body, s_pad=s_pad, s_real=s_real, q_rows=q_rows, pair=pair, pos0=pos0,
                             topk=topk)
    o_t = pl.pallas_call(
        body,
        grid=(bp, t // q_rows),
        in_specs=[
            pl.BlockSpec((N_HEADS_A, 1, lanes, cdim), lambda i, j: (0, i, j, 0)),
            pl.BlockSpec((N_IDX_HEADS, 1, lanes, cdim), lambda i, j: (0, i, j, 0)),
            pl.BlockSpec((1, N_IDX_HEADS, lanes), lambda i, j: (i, 0, j)),
            pl.BlockSpec((1, s_pad, cdim), lambda i, j: (i, 0, 0)),
            pl.BlockSpec((1, s_pad, N_KV_A * cdim), lambda i, j: (i, 0, 0)),
            pl.BlockSpec((1, N_KV_A * v_rows, s_pad), lambda i, j: (i, 0, 0)),
        ],
        out_specs=pl.BlockSpec((1, WIDTH_A, lanes), lambda i, j: (i, 0, j)),
        out_shape=jax.ShapeDtypeStruct((bp, WIDTH_A, pair * t), BF16),
        scratch_shapes=[pltpu.VMEM((s_pad, lanes), F32), pltpu.VMEM((s_pad, lanes), F32),
                        pltpu.VMEM((N_IDX_HEADS, KEY_BLOCK, lanes), F32),
                        pltpu.VMEM((N_HEADS_A, s_pad, lanes), F32), pltpu.VMEM((N_HEADS_A, v_rows, lanes), F32)],
        compiler_params=_cparams("parallel", "parallel"),
        name="dsa",
    )(block_diag(qa_h), block_diag(qi_h), wi_op, ki_op, k_op, vt_op)
    return o_t.reshape(bp, WIDTH_A, pair, t).transpose(0, 2, 3, 1).reshape(b * t, WIDTH_A)


def _split3(a):
    hi = a.astype(BF16)
    r1 = a - hi.astype(F32)
    mid = r1.astype(BF16)
    lo = (r1 - mid.astype(F32)).astype(BF16)
    return hi, mid, lo


def _hgrn_body(hb_ref, s0_ref, lb_ref, nw_ref, ob_ref, sfin_ref, state_scr, *, tb, nb):
    t = pl.program_id(1)

    @pl.when(t == 0)
    def _():
        for bi, h in itertools.product(range(nb), range(N_HEADS_B)):
            state_scr[bi, h] = s0_ref[bi, h].T

    lb = lb_ref[...]
    r = lax.broadcasted_iota(I32, (CHUNK, CHUNK), 0)
    c = lax.broadcasted_iota(I32, (CHUNK, CHUNK), 1)
    causal = r >= c
    tril = jnp.where(causal, 1.0, 0.0).astype(BF16)

    for ci, bi in itertools.product(range(tb // CHUNK), range(nb)):
        rows = slice(ci * CHUNK, (ci + 1) * CHUNK)
        f = lb + (1.0 - lb) * jax.nn.sigmoid(hb_ref[bi, rows, WIDTH_B:2 * WIDTH_B])
        parts = _split3(jnp.log(f))
        bcum = _dot(tril, parts[0]) + _dot(tril, parts[1]) + _dot(tril, parts[2])
        for h in range(N_HEADS_B):
            lanes = slice(h * HEAD_DIM_B, (h + 1) * HEAD_DIM_B)
            q = hb_ref[bi, rows, lanes]
            k = 1.0 - f[:, lanes]
            v = hb_ref[bi, rows, 2 * WIDTH_B + h * HEAD_DIM_B:2 * WIDTH_B + (h + 1) * HEAD_DIM_B]
            gate = hb_ref[bi, rows, 3 * WIDTH_B + h * HEAD_DIM_B:3 * WIDTH_B + (h + 1) * HEAD_DIM_B]
            bh = bcum[:, lanes]
            b_last = bh[CHUNK - 1:CHUNK, :]
            vb = v.astype(BF16)
            state_t = state_scr[bi, h]
            o_inter = _dot_nt((q * jnp.exp(bh)).astype(BF16), state_t.astype(BF16))
            a_rows = []
            for i in range(CHUNK // HGRN_SUB):
                lo, hi = i * HGRN_SUB, (i + 1) * HGRN_SUB
                ref = bh[lo - 1:lo, :] if i else jnp.zeros((1, HEAD_DIM_B), F32)
                qs = (q[lo:hi] * jnp.exp(bh[lo:hi] - ref)).astype(BF16)
                ks = (k * jnp.exp(ref - bh)).astype(BF16)
                a_rows.append(_dot_nt(qs, ks))
            a = jnp.where(causal, jnp.concatenate(a_rows, axis=0), 0.0)
            o = o_inter + _dot(a.astype(BF16), vb)
            kdec = (k * jnp.exp(b_last - bh)).astype(BF16)
            state_scr[bi, h] = jnp.exp(b_last) * state_t + _dot_tn(vb, kdec)
            ms = jnp.mean(o * o, axis=-1, keepdims=True)
            y = o * lax.rsqrt(ms + RMS_EPS) * nw_ref[...] * (gate * jax.nn.sigmoid(gate))
            ob_ref[bi, rows, lanes] = y.astype(BF16)

    @pl.when(t == pl.num_programs(1) - 1)
    def _():
        for bi, h in itertools.product(range(nb), range(N_HEADS_B)):
            sfin_ref[bi, h] = state_scr[bi, h].T


def _hgrn(hb, s0, lb, nw):
    b, t, _ = hb.shape
    tb = min(256, t)
    nb = 2 if b % 2 == 0 else 1
    return pl.pallas_call(
        functools.partial(_hgrn_body, tb=tb, nb=nb),
        grid=(b // nb, t // tb),
        in_specs=[
            pl.BlockSpec((nb, tb, 4 * WIDTH_B), lambda i, j: (i, j, 0)),
            pl.BlockSpec((nb, N_HEADS_B, HEAD_DIM_B, HEAD_DIM_B), lambda i, j: (i, 0, 0, 0)),
            pl.BlockSpec((1, WIDTH_B), lambda i, j: (0, 0)),
            pl.BlockSpec((1, HEAD_DIM_B), lambda i, j: (0, 0)),
        ],
        out_specs=[
            pl.BlockSpec((nb, tb, WIDTH_B), lambda i, j: (i, j, 0)),
            pl.BlockSpec((nb, N_HEADS_B, HEAD_DIM_B, HEAD_DIM_B), lambda i, j: (i, 0, 0, 0)),
        ],
        out_shape=[jax.ShapeDtypeStruct((b, t, WIDTH_B), BF16),
                   jax.ShapeDtypeStruct((b, N_HEADS_B, HEAD_DIM_B, HEAD_DIM_B), F32)],
        scratch_shapes=[pltpu.VMEM((nb, N_HEADS_B, HEAD_DIM_B, HEAD_DIM_B), F32)],
        compiler_params=_cparams("parallel", "arbitrary"),
        name="hgrn",
    )(hb, s0, lb, nw)


def _outproj_body(oaa_ref, oab_ref, oba_ref, obb_ref, xa_ref, xb_ref, w_ref, lnw_ref, lnb_ref, wrh_ref, wrl_ref, br_ref,
                  x1_ref, lg_ref, *, steps_a):
    first = pl.program_id(0) < steps_a
    pick = lambda a_ref, b_ref: jnp.where(first, a_ref[...], b_ref[...])
    x = pick(xa_ref, xb_ref)
    y = _dot(pick(oaa_ref, oab_ref), w_ref[:WIDTH_A, :]) + _dot(pick(oba_ref, obb_ref), w_ref[WIDTH_A:, :])
    x1 = _layer_norm(DN_ALPHA * x + y, lnw_ref[...], lnb_ref[...])
    x1_ref[...] = x1
    hi = x1.astype(BF16)
    lo = (x1 - hi.astype(F32)).astype(BF16)
    lg_ref[...] = (_dot_nt(wrh_ref[...], hi) + _dot_nt(wrh_ref[...], lo) + _dot_nt(wrl_ref[...], hi)
                   + br_ref[...])


def _two_group_rows(tm, width, steps_a):
    return (pl.BlockSpec((tm, width), lambda i, *_: (jnp.minimum(i, steps_a - 1), 0)),
            pl.BlockSpec((tm, width), lambda i, *_: (jnp.maximum(i - steps_a, 0), 0)))


def _outproj(mix_a, mix_b, x_a, x_b, w_out, lnw, lnb, wr_hi, wr_lo, br):
    tm = MOE_TILE
    assert x_a.shape[0] % tm == 0 and x_b.shape[0] % tm == 0
    n = x_a.shape[0] + x_b.shape[0]
    steps_a = x_a.shape[0] // tm
    row = lambda width: pl.BlockSpec((tm, width), lambda i: (i, 0))
    full = lambda a: pl.BlockSpec(a.shape, lambda i: (0,) * a.ndim)
    return pl.pallas_call(
        functools.partial(_outproj_body, steps_a=steps_a),
        grid=(n // tm,),
        in_specs=[*_two_group_rows(tm, WIDTH_A, steps_a), *_two_group_rows(tm, WIDTH_B, steps_a),
                  *_two_group_rows(tm, D_MODEL, steps_a), full(w_out), full(lnw), full(lnb), full(wr_hi),
                  full(wr_lo), full(br)],
        out_specs=[row(D_MODEL), pl.BlockSpec((N_EXPERTS, tm), lambda i: (0, i))],
        out_shape=[jax.ShapeDtypeStruct((n, D_MODEL), F32), jax.ShapeDtypeStruct((N_EXPERTS, n), F32)],
        compiler_params=_cparams("parallel"),
        name="outproj",
    )(mix_a[0], mix_b[0], mix_a[1], mix_b[1], x_a, x_b, w_out, lnw, lnb, wr_hi, wr_lo, br)


def _route_body(lg_ref, lpos_ref, gate_ref, tc_ref, base_ref, cnt_ref, cnt_scr, *, tr):
    @pl.when(pl.program_id(0) == 0)
    def _():
        cnt_scr[...] = jnp.zeros_like(cnt_scr)

    l = lg_ref[...]
    rows = lax.broadcasted_iota(I32, (N_EXPERTS, tr), 0)
    vals, hots = [], []
    for j in range(TOP_K):
        m = jnp.max(l, axis=0, keepdims=True)
        idx = jnp.min(jnp.where(l == m, rows, N_EXPERTS), axis=0, keepdims=True)
        hot = rows == idx
        vals.append(m)
        hots.append(hot)
        l = jnp.where(hot, NEG_INF, l)
    es = [jnp.exp(v - vals[0]) for v in vals]
    inv = 1.0 / (es[0] + es[1] + es[2] + es[3])
    for j in range(TOP_K):
        gate_ref[j:j + 1, :] = es[j] * inv
    chosen = jnp.zeros((N_EXPERTS, tr), F32)
    for hot in hots:
        chosen = chosen + jnp.where(hot, 1.0, 0.0)
    r = lax.broadcasted_iota(I32, (tr, tr), 0)
    c = lax.broadcasted_iota(I32, (tr, tr), 1)
    before = jnp.where(r < c, 1.0, 0.0).astype(BF16)
    prior = _dot(chosen.astype(BF16), before)
    units = jnp.ceil(jnp.sum(chosen, axis=1, keepdims=True) * (1.0 / ROW_ALIGN))
    er = lax.broadcasted_iota(I32, (N_EXPERTS, N_EXPERTS), 0)
    ec = lax.broadcasted_iota(I32, (N_EXPERTS, N_EXPERTS), 1)
    earlier = jnp.where(ec < er, 1.0, 0.0).astype(BF16)
    run_len = jnp.broadcast_to(units, (N_EXPERTS, 128)) * ROW_ALIGN
    run_start = _dot(earlier, jnp.broadcast_to(units, (N_EXPERTS, 128)).astype(BF16)) * ROW_ALIGN
    where_in_tile = prior + run_start[:, 0:1]
    for j in range(TOP_K):
        lpos_ref[j:j + 1, :] = jnp.sum(jnp.where(hots[j], where_in_tile, 0.0), axis=0,
                                       keepdims=True).astype(I32)
    tc_ref[0] = run_len.astype(I32)
    base_ref[0] = cnt_scr[...].astype(I32)
    cnt_scr[...] = cnt_scr[...] + run_len
    cnt_ref[...] = cnt_scr[...].astype(I32)


def _route(logits_t):
    n = logits_t.shape[1]
    tr = MOE_TILE
    tiles = n // tr
    tok = lambda rows: pl.BlockSpec((rows, tr), lambda i: (0, i))
    per_tile = pl.BlockSpec((1, N_EXPERTS, 128), lambda i: (i, 0, 0))
    return pl.pallas_call(
        functools.partial(_route_body, tr=tr),
        grid=(tiles,),
        in_specs=[tok(N_EXPERTS)],
        out_specs=[tok(TOP_K), tok(TOP_K), per_tile, per_tile, pl.BlockSpec((N_EXPERTS, 128), lambda i: (0, 0))],
        out_shape=[jax.ShapeDtypeStruct((TOP_K, n), I32), jax.ShapeDtypeStruct((TOP_K, n), F32),
                   jax.ShapeDtypeStruct((tiles, N_EXPERTS, 128), I32),
                   jax.ShapeDtypeStruct((tiles, N_EXPERTS, 128), I32),
                   jax.ShapeDtypeStruct((N_EXPERTS, 128), I32)],
        scratch_shapes=[pltpu.VMEM((N_EXPERTS, 128), F32)],
        compiler_params=_cparams("arbitrary"),
        name="route",
    )(logits_t)


def _for_each_run_piece(tc_ref, g0_ref, t, fn):
    def per_expert(e, local):
        c = tc_ref[t, e]
        glob = g0_ref[t, e]
        lo = local
        for p in RUN_PIECES:
            take = (c & p) != 0

            @pl.when(take)
            def _():
                fn(pl.multiple_of(lo, ROW_ALIGN), pl.multiple_of(glob, ROW_ALIGN), p)

            step = jnp.where(take, p, 0)
            lo = lo + step
            glob = glob + step
        return local + c

    lax.fori_loop(0, N_EXPERTS, per_expert, 0)


def _dispatch_body(tc_ref, g0_ref, seg_ref, lpos_ref, x_ref, buf_ref, stage, zero_scr, zsem, sems, *, n_blocks):
    i = pl.program_id(0)
    slot = i % 2

    def run_copy(slot_, lo, glob, rows):
        return pltpu.make_async_copy(stage.at[slot_, pl.ds(lo, rows), :], buf_ref.at[pl.ds(glob, rows), :],
                                     sems.at[slot_])

    @pl.when(i == 0)
    def _():
        zero_scr[...] = jnp.zeros_like(zero_scr)
        n_used = seg_ref[1, N_EXPERTS - 1] // EXPERT_ROWS

        def block_copy(start):
            return pltpu.make_async_copy(
                zero_scr, buf_ref.at[pl.ds(pl.multiple_of(start, EXPERT_ROWS), EXPERT_ROWS), :], zsem)

        def clears(action):
            for e in range(N_EXPERTS):
                @pl.when(seg_ref[1, e] > seg_ref[0, e])
                def _():
                    action(block_copy(seg_ref[1, e] - EXPERT_ROWS))

                @pl.when(n_used + e < n_blocks)
                def _():
                    action(block_copy((n_used + e) * EXPERT_ROWS))

        clears(lambda cp: cp.start())
        clears(lambda cp: cp.wait())

    r = lax.broadcasted_iota(I32, (MOE_SLOTS, MOE_TILE), 0)
    onehot = jnp.zeros((MOE_SLOTS, MOE_TILE), F32)
    for j in range(TOP_K):
        onehot = jnp.where(r == lpos_ref[j:j + 1, :], 1.0, onehot)
    stage[slot] = _dot(onehot.astype(BF16), x_ref[...].astype(BF16)).astype(BF16)

    @pl.when(i > 0)
    def _():
        _for_each_run_piece(tc_ref, g0_ref, i - 1,
                            lambda lo, glob, rows: run_copy(1 - slot, lo, glob, rows).wait())

    _for_each_run_piece(tc_ref, g0_ref, i, lambda lo, glob, rows: run_copy(slot, lo, glob, rows).start())

    @pl.when(i == pl.num_programs(0) - 1)
    def _():
        _for_each_run_piece(tc_ref, g0_ref, i, lambda lo, glob, rows: run_copy(slot, lo, glob, rows).wait())


def _dispatch(x1, lpos_t, tc, g0, seg, n_rows):
    n = x1.shape[0]
    return pl.pallas_call(
        functools.partial(_dispatch_body, n_blocks=n_rows // EXPERT_ROWS),
        grid_spec=pltpu.PrefetchScalarGridSpec(
            num_scalar_prefetch=3,
            grid=(n // MOE_TILE,),
            in_specs=[pl.BlockSpec((TOP_K, MOE_TILE), lambda i, *_: (0, i)),
                      pl.BlockSpec((MOE_TILE, D_MODEL), lambda i, *_: (i, 0))],
            out_specs=pl.BlockSpec(memory_space=pl.ANY),
            scratch_shapes=[pltpu.VMEM((2, MOE_SLOTS, D_MODEL), BF16),
                            pltpu.VMEM((EXPERT_ROWS, D_MODEL), BF16),
                            pltpu.SemaphoreType.DMA(()), pltpu.SemaphoreType.DMA((2,))],
        ),
        out_shape=jax.ShapeDtypeStruct((n_rows, D_MODEL), BF16),
        compiler_params=_cparams("arbitrary"),
        name="dispatch",
    )(tc, g0, seg, lpos_t, x1)


def _experts_body(be_ref, nu_ref, x_ref, wgu_ref, bgu_ref, wd_ref, bd_ref, o_ref, wgu_bf, wd_bf):
    i = pl.program_id(0)

    @pl.when(jnp.logical_or(i == 0, be_ref[i] != be_ref[jnp.maximum(i - 1, 0)]))
    def _():
        for r in range(0, D_MODEL, 256):
            wgu_bf[r:r + 256, :] = wgu_ref[0, r:r + 256, :].astype(BF16)
            wd_bf[r:r + 256, :] = wd_ref[0, r:r + 256, :].astype(BF16)

    @pl.when(i < nu_ref[0])
    def _():
        gu = _dot(x_ref[...], wgu_bf[...]) + bgu_ref[0]
        gate = jnp.minimum(gu[:, :D_EXPERT], SWIGLU_LIMIT)
        up = jnp.clip(gu[:, D_EXPERT:], -SWIGLU_LIMIT, SWIGLU_LIMIT)
        hdn = (up + 1.0) * gate * jax.nn.sigmoid(SWIGLU_ALPHA * gate)
        o_ref[...] = (_dot(hdn.astype(BF16), wd_bf[...]) + bd_ref[0]).astype(BF16)

    @pl.when(i >= nu_ref[0])
    def _():
        o_ref[...] = jnp.zeros_like(o_ref)


def _experts(buf, block_e, n_used, w_gu, b_gu, w_down, b_down):
    n_blocks = buf.shape[0] // EXPERT_ROWS
    rows = pl.BlockSpec((EXPERT_ROWS, D_MODEL), lambda i, be, nu: (jnp.minimum(i, nu[0] - 1), 0))
    per_e = lambda a: pl.BlockSpec((1,) + a.shape[1:], lambda i, be, nu: (be[i],) + (0,) * (a.ndim - 1))
    return pl.pallas_call(
        _experts_body,
        grid_spec=pltpu.PrefetchScalarGridSpec(
            num_scalar_prefetch=2,
            grid=(n_blocks,),
            in_specs=[rows, per_e(w_gu), per_e(b_gu), per_e(w_down), per_e(b_down)],
            out_specs=pl.BlockSpec((EXPERT_ROWS, D_MODEL), lambda i, be, nu: (i, 0)),
            scratch_shapes=[pltpu.VMEM((D_MODEL, 2 * D_EXPERT), BF16), pltpu.VMEM((D_EXPERT, D_MODEL), BF16)],
        ),
        out_shape=jax.ShapeDtypeStruct(buf.shape, BF16),
        compiler_params=_cparams("arbitrary"),
        name="experts",
    )(block_e, n_used, buf, w_gu, b_gu, w_down, b_down)


def _final_body(tc_ref, g0_ref, lpos_ref, gate_ref, x1_ref, pa_ref, pb_ref, lnw_ref, lnb_ref, wg_ref, bg_ref, wp_ref,
                eo_ref, ya_ref, yb_ref, stage, sems, *, steps_a):
    i = pl.program_id(0)
    slot = i % 2

    def run_copy(slot_, lo, glob, rows):
        return pltpu.make_async_copy(eo_ref.at[pl.ds(glob, rows), :], stage.at[slot_, pl.ds(lo, rows), :],
                                     sems.at[slot_])

    def fetch(tile, slot_):
        stage[slot_] = jnp.zeros(stage.shape[1:], stage.dtype)
        _for_each_run_piece(tc_ref, g0_ref, tile, lambda lo, glob, rows: run_copy(slot_, lo, glob, rows).start())

    @pl.when(i == 0)
    def _():
        fetch(0, 0)

    @pl.when(i + 1 < pl.num_programs(0))
    def _():
        fetch(i + 1, 1 - slot)

    _for_each_run_piece(tc_ref, g0_ref, i, lambda lo, glob, rows: run_copy(slot, lo, glob, rows).wait())

    r = lax.broadcasted_iota(I32, (MOE_TILE, MOE_SLOTS), 1)
    g = jnp.zeros((MOE_TILE, MOE_SLOTS), F32)
    for j in range(TOP_K):
        g = jnp.where(r == lpos_ref[:, j:j + 1], gate_ref[:, j:j + 1], g)
    g_hi = g.astype(BF16)
    g_lo = (g - g_hi.astype(F32)).astype(BF16)
    g2 = jnp.concatenate([g_hi, g_lo], axis=0)
    both = _dot(g2, stage[slot])
    moe = both[:MOE_TILE] + both[MOE_TILE:]
    x2 = _layer_norm(DN_ALPHA * x1_ref[...] + moe, lnw_ref[...], lnb_ref[...])
    ple_gate = jax.nn.sigmoid(_dot(x2.astype(BF16), wg_ref[...]) + bg_ref[...])
    p = jnp.where(i < steps_a, pa_ref[...], pb_ref[...])
    y = x2 + ple_gate * _dot(p.astype(BF16), wp_ref[...])

    @pl.when(i < steps_a)
    def _():
        ya_ref[...] = y

    @pl.when(i >= steps_a)
    def _():
        yb_ref[...] = y


def _final(tc, g0, lpos, gate, x1, p_a, p_b, lnw, lnb, w_gate, b_gate, w_ple, expert_out):
    n = x1.shape[0]
    steps_a = p_a.shape[0] // MOE_TILE
    row = lambda width: pl.BlockSpec((MOE_TILE, width), lambda i, *_: (i, 0))
    full = lambda a: pl.BlockSpec(a.shape, lambda i, *_: (0,) * a.ndim)
    return pl.pallas_call(
        functools.partial(_final_body, steps_a=steps_a),
        grid_spec=pltpu.PrefetchScalarGridSpec(
            num_scalar_prefetch=2,
            grid=(n // MOE_TILE,),
            in_specs=[row(TOP_K), row(TOP_K), row(D_MODEL), *_two_group_rows(MOE_TILE, D_PLE, steps_a), full(lnw),
                      full(lnb), full(w_gate), full(b_gate), full(w_ple), pl.BlockSpec(memory_space=pl.ANY)],
            out_specs=list(_two_group_rows(MOE_TILE, D_MODEL, steps_a)),
            scratch_shapes=[pltpu.VMEM((2, MOE_SLOTS, D_MODEL), BF16), pltpu.SemaphoreType.DMA((2,))],
        ),
        out_shape=[jax.ShapeDtypeStruct((p_a.shape[0], D_MODEL), F32),
                   jax.ShapeDtypeStruct((p_b.shape[0], D_MODEL), F32)],
        compiler_params=_cparams("arbitrary"),
        name="combine_final",
    )(tc, g0, lpos, gate, x1, p_a, p_b, lnw, lnb, w_gate, b_gate, w_ple, expert_out)


def _round_up(a, m):
    return -(-a // m) * m


def _token_mixers(x, hist_k, hist_v, hist_kidx, s0, pos0, wts):
    b, t, _ = x.shape
    n = b * t
    x2d = x.reshape(n, D_MODEL)

    qa, ka, va, ka_bf, va_bf, qi, ki, ki_bf, wi, hb = _inproj(x2d, wts["w_in"], wts["idx_lnw"], wts["idx_lnb"])

    k_new = ka_bf.reshape(b, t, N_KV_A * HEAD_DIM_A)
    v_new = va_bf.reshape(b, t, N_KV_A * HEAD_DIM_A)
    ki_new = ki_bf.reshape(b, t, IDX_DIM)
    q_ops = (qa.reshape(N_HEADS_A, b, t, HEAD_DIM_A), qi.reshape(N_IDX_HEADS, b, t, IDX_DIM),
             wi.reshape(b, t, N_IDX_HEADS))
    past = 0 if hist_k is None else hist_k.shape[1]
    s_real = past + t
    topk = min(TOPK_MAX, s_real // 4)
    if past:
        k_all = jnp.concatenate([hist_k.reshape(b, past, -1).astype(BF16), k_new], axis=1)
        v_all = jnp.concatenate([hist_v.reshape(b, past, -1).astype(BF16), v_new], axis=1)
        ki_all = jnp.concatenate([hist_kidx.astype(BF16), ki_new], axis=1)
    else:
        k_all, v_all, ki_all = k_new, v_new, ki_new
    pad = ((0, 0), (0, _round_up(s_real, KEY_BLOCK) - s_real), (0, 0))
    keys = tuple(jnp.pad(a, pad) for a in (ki_all, k_all, v_all))
    oa = _dsa(*q_ops, keys, s_real=s_real, pos0=pos0, topk=topk)

    ob, s_fin = _hgrn(hb.reshape(b, t, 4 * WIDTH_B), s0, wts["lb"], wts["hgrn_nw"])
    return (oa, ob.reshape(n, WIDTH_B), ka.reshape(b, t, N_KV_A, HEAD_DIM_A), va.reshape(b, t, N_KV_A, HEAD_DIM_A),
            ki.reshape(b, t, IDX_DIM), s_fin)


def _channel_mixer(mix_a, mix_b, x_a, x_b, p_a, p_b, wts):
    shape_a, shape_b = x_a.shape, x_b.shape
    x_a, x_b = x_a.reshape(-1, D_MODEL), x_b.reshape(-1, D_MODEL)
    n = x_a.shape[0] + x_b.shape[0]
    x1, logits_t = _outproj(mix_a, mix_b, x_a, x_b, wts["w_out"], wts["ln1_w"], wts["ln1_b"],
                            wts["wr_hi"], wts["wr_lo"], wts["b_router"])

    tiles = n // MOE_TILE
    lpos_t, gate_t, tile_runs, tile_base, counts = _route(logits_t)
    tc, counts = tile_runs[:, :, 0], counts[:, 0]
    padded = (counts + EXPERT_ROWS - 1) // EXPERT_ROWS * EXPERT_ROWS
    seg_end = jnp.cumsum(padded)
    seg_start = seg_end - padded
    g0 = (seg_start[None, :] + tile_base[:, :, 0]).astype(I32)
    n_blocks = -(-(n * TOP_K + N_EXPERTS * tiles * (ROW_ALIGN - 1) + N_EXPERTS * (EXPERT_ROWS - 1))
                 // EXPERT_ROWS)
    block_row0 = jnp.arange(n_blocks, dtype=I32) * EXPERT_ROWS
    block_e = jnp.minimum(jnp.sum((seg_end[None, :] <= block_row0[:, None]).astype(I32), axis=1), N_EXPERTS - 1)
    n_used = (seg_end[-1:] // EXPERT_ROWS).astype(I32)
    seg = jnp.stack([seg_start, seg_end]).astype(I32)
    buf = _dispatch(x1, lpos_t, tc, g0, seg, n_blocks * EXPERT_ROWS)
    expert_out = _experts(buf, block_e, n_used, wts["w_gu"], wts["b_gu"], wts["w_down"], wts["b_down"])
    y_a, y_b = _final(tc, g0, lpos_t.T, gate_t.T, x1, p_a.reshape(-1, D_PLE), p_b.reshape(-1, D_PLE),
                      wts["ln2_w"], wts["ln2_b"], wts["w_ple_gate"], wts["b_ple_gate"], wts["w_ple"], expert_out)
    return y_a.reshape(shape_a), y_b.reshape(shape_b)


def _prep_weights(w_in, w_out, idx_k_norm_w, idx_k_norm_b, lb, hgrn_norm_w, ln1_w, ln1_b, w_router, b_router,
                  w_gu, b_gu, w_down, b_down, ln2_w, ln2_b, w_ple, w_ple_gate, b_ple_gate):
    n_a = COL_KW + IDX_DIM + N_IDX_HEADS
    w_pad = jnp.concatenate(
        [w_in[:, :n_a], jnp.zeros((D_MODEL, COL_HB - n_a), w_in.dtype), w_in[:, n_a:]], axis=1).astype(BF16)
    wr_t = w_router.T
    wr_hi = wr_t.astype(BF16)
    row = lambda a: a.reshape(1, -1)
    return dict(
        w_in=w_pad, idx_lnw=row(idx_k_norm_w), idx_lnb=row(idx_k_norm_b), lb=row(lb), hgrn_nw=row(hgrn_norm_w),
        w_out=w_out.astype(BF16), ln1_w=row(ln1_w), ln1_b=row(ln1_b),
        wr_hi=wr_hi, wr_lo=(wr_t - wr_hi.astype(F32)).astype(BF16), b_router=b_router.reshape(N_EXPERTS, 1),
        w_gu=w_gu, b_gu=b_gu.reshape(N_EXPERTS, 1, 2 * D_EXPERT),
        w_down=w_down, b_down=b_down.reshape(N_EXPERTS, 1, D_MODEL),
        ln2_w=row(ln2_w), ln2_b=row(ln2_b), w_ple=w_ple.astype(BF16), w_ple_gate=w_ple_gate.astype(BF16),
        b_ple_gate=row(b_ple_gate))


def kernel(x_prompt, x_sample, cache_k, cache_v, cache_kidx, state_hgrn, p_prompt, p_sample, w_in, w_out,
           idx_k_norm_w, idx_k_norm_b, hgrn_lb_logits, hgrn_norm_w, ln1_w, ln1_b, w_router, b_router, w_gu, b_gu,
           w_down, b_down, ln2_w, ln2_b, w_ple, w_ple_gate, b_ple_gate):
    lb_all = jnp.cumsum(jax.nn.softmax(hgrn_lb_logits.astype(F32), axis=0), axis=0)
    xp, xs = x_prompt, x_sample
    outs = [[] for _ in range(8)]
    for i in range(DEPTH):
        wts = _prep_weights(w_in[i], w_out[i], idx_k_norm_w[i], idx_k_norm_b[i], lb_all[i], hgrn_norm_w[i],
                            ln1_w[i], ln1_b[i], w_router[i], b_router[i], w_gu[i], b_gu[i], w_down[i], b_down[i],
                            ln2_w[i], ln2_b[i], w_ple[i], w_ple_gate[i], b_ple_gate[i])
        s0p = jnp.zeros((xp.shape[0], N_HEADS_B, HEAD_DIM_B, HEAD_DIM_B), F32)
        mix_p = _token_mixers(xp, None, None, None, s0p, 0, wts)
        mix_s = _token_mixers(xs, cache_k[i], cache_v[i], cache_kidx[i], state_hgrn[i], cache_k.shape[2], wts)
        xp, xs = _channel_mixer(mix_p, mix_s, xp, xs, p_prompt[i], p_sample[i], wts)
        for lst, val in zip(outs, mix_p[2:] + mix_s[2:]):
            lst.append(val)
    return (xp, xs) + tuple(jnp.stack(l) for l in outs)
```

```python
import functools
import itertools

import jax
import jax.numpy as jnp
from jax import lax
from jax.experimental import pallas as pl
from jax.experimental.pallas import tpu as pltpu

F32 = jnp.float32
BF16 = jnp.bfloat16
I32 = jnp.int32

D_MODEL = 1024
CHUNK = 64
CHUNK_SHIFT = 6
WIDTH_A = 512
HEAD_DIM_A = 64
N_HEADS_A = 8
N_KV_A = 2
N_IDX_HEADS = 4
IDX_DIM = 64
IDX_SCALE = IDX_DIM ** -0.5 * N_IDX_HEADS ** -0.5
TOPK_MAX = 256
WIDTH_B = 512
HEAD_DIM_B = 128
N_HEADS_B = 4
N_EXPERTS = 32
TOP_K = 4
D_EXPERT = 1024
SWIGLU_LIMIT = 7.0
SWIGLU_ALPHA = 1.702
D_PLE = 256
LN_EPS = 1e-5
RMS_EPS = 1e-6
DEPTH = 1
DN_ALPHA = (2 * DEPTH) ** 0.25

COL_QA, COL_KA, COL_VA, COL_QI, COL_KW, COL_HB, COL_END = 0, 512, 640, 768, 1024, 1152, 3200

VMEM_LIMIT = 56 * 1024 * 1024
INPROJ_ROWS = 1024
KEY_BLOCK = 256
V_ROWS = 80
LOG2_E = 1.4426950408889634
HGRN_SUB = 16
EXPERT_ROWS = 512
MOE_TILE = 512
ROW_ALIGN = 16
RUN_PIECES = (512, 256, 128, 64, 32, 16)
MOE_SLOTS = -(-(TOP_K * MOE_TILE + N_EXPERTS * (ROW_ALIGN - 1)) // 256) * 256
NEG_INF = float("-inf")
INT_MIN = -(2 ** 31)


def _cparams(*sem):
    return pltpu.CompilerParams(dimension_semantics=sem, vmem_limit_bytes=VMEM_LIMIT)


def _dot(a, b):
    return jnp.dot(a, b, preferred_element_type=F32)


def _dot_nt(a, b):
    return lax.dot_general(a, b, (((1,), (1,)), ((), ())), preferred_element_type=F32)


def _dot_tn(a, b):
    return lax.dot_general(a, b, (((0,), (0,)), ((), ())), preferred_element_type=F32)


def _layer_norm(z, w, b):
    mu = jnp.mean(z, axis=-1, keepdims=True)
    d = z - mu
    var = jnp.mean(d * d, axis=-1, keepdims=True)
    return d * lax.rsqrt(var + LN_EPS) * w + b


def _inproj_body(x_ref, w_ref, lnw_ref, lnb_ref, qa_ref, ka_ref, va_ref, kab_ref, vab_ref, qi_ref, ki_ref, kib_ref,
                 wi_ref, hb_ref):
    xb = x_ref[...].astype(BF16)

    def mm(c0, c1):
        return _dot(xb, w_ref[:, c0:c1])

    qa = (mm(COL_QA, COL_KA) * (HEAD_DIM_A ** -0.5 * LOG2_E)).astype(BF16)
    for h in range(N_HEADS_A):
        qa_ref[h] = qa[:, h * HEAD_DIM_A:(h + 1) * HEAD_DIM_A]
    for src, out3_ref, bf_ref in ((mm(COL_KA, COL_VA), ka_ref, kab_ref), (mm(COL_VA, COL_QI), va_ref, vab_ref)):
        for g in range(N_KV_A):
            out3_ref[:, g, :] = src[:, g * HEAD_DIM_A:(g + 1) * HEAD_DIM_A]
        bf_ref[...] = src.astype(BF16)
    qi = mm(COL_QI, COL_KW).astype(BF16)
    for h in range(N_IDX_HEADS):
        qi_ref[h] = qi[:, h * IDX_DIM:(h + 1) * IDX_DIM]
    kw = mm(COL_KW, COL_HB)
    ki = _layer_norm(kw[:, :IDX_DIM], lnw_ref[...], lnb_ref[...])
    ki_ref[...] = ki
    kib_ref[...] = ki.astype(BF16)
    wi_ref[...] = kw[:, IDX_DIM:IDX_DIM + N_IDX_HEADS]
    hb_ref[...] = mm(COL_HB, COL_END)


def _inproj(x2d, w_pad, lnw, lnb):
    n = x2d.shape[0]
    tm = min(INPROJ_ROWS, n)
    row = lambda width: pl.BlockSpec((tm, width), lambda i: (i, 0))
    full = lambda a: pl.BlockSpec(a.shape, lambda i: (0,) * a.ndim)
    heads = lambda nh, d: pl.BlockSpec((nh, tm, d), lambda i: (0, i, 0))
    kv = N_KV_A * HEAD_DIM_A
    kv3 = pl.BlockSpec((tm, N_KV_A, HEAD_DIM_A), lambda i: (i, 0, 0))
    sds = jax.ShapeDtypeStruct
    return pl.pallas_call(
        _inproj_body,
        grid=(n // tm,),
        in_specs=[row(D_MODEL), full(w_pad), full(lnw), full(lnb)],
        out_specs=[heads(N_HEADS_A, HEAD_DIM_A), kv3, kv3, row(kv), row(kv), heads(N_IDX_HEADS, IDX_DIM),
                   row(IDX_DIM), row(IDX_DIM), row(N_IDX_HEADS), row(4 * WIDTH_B)],
        out_shape=[sds((N_HEADS_A, n, HEAD_DIM_A), BF16), sds((n, N_KV_A, HEAD_DIM_A), F32),
                   sds((n, N_KV_A, HEAD_DIM_A), F32), sds((n, kv), BF16), sds((n, kv), BF16),
                   sds((N_IDX_HEADS, n, IDX_DIM), BF16), sds((n, IDX_DIM), F32), sds((n, IDX_DIM), BF16),
                   sds((n, N_IDX_HEADS), F32), sds((n, 4 * WIDTH_B), F32)],
        compiler_params=_cparams("parallel"),
        name="inproj",
    )(x2d, w_pad, lnw, lnb)


def _order_bits_to_f32(u):
    key = u ^ INT_MIN
    bits = key ^ ((key >> 31) & 0x7FFFFFFF)
    f = lax.bitcast_convert_type(bits, F32)
    return jnp.where(u >= 0, jnp.where(u <= 0x007FFFFF, NEG_INF, f), f)


def _dsa_body(qa_ref, qi_ref, wi_ref, kidx_ref, k_ref, vt_ref, o_ref, sc_scr, bias_scr, term_scr, s_scr, acc_scr,
              *, s_pad, s_real, q_rows, pair, pos0, topk):
    kb_rows = KEY_BLOCK
    heads_per_kv = N_HEADS_A // N_KV_A
    qb = pair * q_rows
    cdim = pair * HEAD_DIM_A
    v_rows = pair * V_ROWS
    j = pl.program_id(1)
    q_lo = pos0 + j * q_rows
    lane = lax.broadcasted_iota(I32, (1, qb), 1)
    q_chunk = (q_lo + (lane & (q_rows - 1))) >> CHUNK_SHIFT
    k_lim = (((q_lo + q_rows - 1) >> CHUNK_SHIFT) + 1) * CHUNK
    nkb = jnp.minimum(s_pad // kb_rows, (k_lim + kb_rows - 1) // kb_rows)

    def rows_of(kb):
        return pl.ds(pl.multiple_of(kb * kb_rows, kb_rows), kb_rows)

    def fold8(a):
        return a.reshape(kb_rows // 8, 8, qb)

    def over_blocks(body, init):
        def quad(i, carry):
            for u in range(4):
                carry = body(4 * i + u, carry)
            return carry
        carry = lax.fori_loop(0, nkb // 4, quad, init)
        done = (nkb // 4) * 4
        carry = lax.cond((nkb & 2) != 0, lambda c: body(done + 1, body(done, c)), lambda c: c, carry)
        done = done + (nkb & 2)
        return lax.cond((nkb & 1) != 0, lambda c: body(done, c), lambda c: c, carry)

    def score_blk(kb, _):
        rows = rows_of(kb)
        kidx = kidx_ref[0, rows, :]
        for h in range(N_IDX_HEADS):
            term_scr[h] = jnp.maximum(_dot_nt(kidx, qi_ref[h, 0]), 0.0) * wi_ref[0, h:h + 1, :]
        total = (term_scr[0] + term_scr[1]) + (term_scr[2] + term_scr[3])
        spos = kb * kb_rows + lax.broadcasted_iota(I32, (kb_rows, qb), 0)
        sc = jnp.where((spos >> CHUNK_SHIFT) <= q_chunk, total * IDX_SCALE + 0.0, NEG_INF)
        if s_real < s_pad:
            sc = jnp.where(spos < s_real, sc, NEG_INF)
        sc_scr[rows, :] = sc
        return 0

    over_blocks(score_blk, 0)

    def count(thr, strict):
        def body(kb, acc):
            blk = sc_scr[rows_of(kb), :]
            hit = (blk > thr) if strict else (blk >= thr)
            return acc + jnp.sum(fold8(jnp.where(hit, 1, 0).astype(I32)), axis=0)
        acc = over_blocks(body, jnp.zeros((8, qb), I32))
        return jnp.sum(acc, axis=0, keepdims=True)

    def bit_step(i, carry):
        prefix, cnt = carry
        cand = prefix | lax.shift_left(jnp.int32(1), 31 - i)
        c = count(_order_bits_to_f32(cand), False)
        take = c >= topk
        return jnp.where(take, cand, prefix), jnp.where(take, c, cnt)

    prefix, cnt_ge = lax.fori_loop(0, 32, bit_step,
                                   (jnp.zeros((1, qb), I32), jnp.full((1, qb), nkb * kb_rows, I32)))
    tau = _order_bits_to_f32(prefix)
    finite_tau = tau > NEG_INF
    tau_floor = jnp.maximum(tau, jnp.finfo(F32).min)
    tie_lanes = jnp.where(finite_tau, jnp.where(cnt_ge > topk, 1, 0), 0)
    has_ties = jnp.max(tie_lanes) > 0

    @pl.when(jnp.logical_not(has_ties))
    def _():
        def body(kb, _):
            blk = sc_scr[rows_of(kb), :]
            bias_scr[rows_of(kb), :] = jnp.where(blk >= tau_floor, 0.0, NEG_INF)
            return 0
        over_blocks(body, 0)

    @pl.when(has_ties)
    def _():
        need = jnp.where(finite_tau, (topk - count(tau, True)).astype(F32), 0.0)
        r = lax.broadcasted_iota(I32, (kb_rows, kb_rows), 0)
        c = lax.broadcasted_iota(I32, (kb_rows, kb_rows), 1)
        tril = jnp.where(r >= c, 1.0, 0.0).astype(BF16)

        def body(kb, seen):
            blk = sc_scr[rows_of(kb), :]
            eq = blk == tau
            rank = _dot(tril, jnp.where(eq, 1.0, 0.0).astype(BF16)) + seen
            tie_bias = jnp.where(eq, jnp.where(rank <= need, 0.0, NEG_INF), NEG_INF)
            bias_scr[rows_of(kb), :] = jnp.where(blk > tau, 0.0, tie_bias)
            return rank[kb_rows - 1:kb_rows, :]
        lax.fori_loop(0, nkb, body, jnp.zeros((1, qb), F32))

    def pass1(kb, m8):
        rows = rows_of(kb)
        bias = bias_scr[rows, :]
        out = []
        for h in range(N_HEADS_A):
            g = h // heads_per_kv
            s = _dot_nt(k_ref[0, rows, g * cdim:(g + 1) * cdim], qa_ref[h, 0]) + bias
            s_scr[h, rows, :] = s
            out.append(jnp.maximum(m8[h], jnp.max(fold8(s), axis=0)))
        return tuple(out)

    m8 = over_blocks(pass1, tuple(jnp.full((8, qb), NEG_INF, F32) for _ in range(N_HEADS_A)))
    m = [jnp.max(x, axis=0, keepdims=True) for x in m8]

    acc_scr[...] = jnp.zeros_like(acc_scr)

    def pass2(kb, _):
        rows = rows_of(kb)
        for h in range(N_HEADS_A):
            g = h // heads_per_kv
            p = jnp.exp2(s_scr[h, rows, :] - m[h]).astype(BF16)
            acc_scr[h] += _dot(vt_ref[0, g * v_rows:(g + 1) * v_rows, rows], p)
        return 0

    over_blocks(pass2, 0)
    for h in range(N_HEADS_A):
        a = acc_scr[h]
        num, den = a[:HEAD_DIM_A], a[HEAD_DIM_A:HEAD_DIM_A + 1]
        for mb in range(1, pair):
            mine = lane >= mb * q_rows
            r0 = mb * V_ROWS
            num = jnp.where(mine, a[r0:r0 + HEAD_DIM_A], num)
            den = jnp.where(mine, a[r0 + HEAD_DIM_A:r0 + HEAD_DIM_A + 1], den)
        o_ref[0, h * HEAD_DIM_A:(h + 1) * HEAD_DIM_A, :] = (num * (1.0 / den)).astype(BF16)


def _dsa_pairing(b, t):
    return 2 if (2 * t <= 128 and b % 2 == 0) else 1


def _dsa(qa_h, qi_h, wi, keys, *, s_real, pos0, topk):
    _, b, t, _ = qa_h.shape
    s_pad = keys[0].shape[1]
    q_rows = min(256, t)
    pair = _dsa_pairing(b, t)
    bp = b // pair
    lanes, cdim, v_rows = pair * q_rows, pair * HEAD_DIM_A, pair * V_ROWS

    def block_diag(q):
        if pair == 1:
            return q
        nh, d = q.shape[0], q.shape[3]
        eye = jnp.eye(pair, dtype=q.dtype)
        q6 = q.reshape(nh, bp, pair, t, 1, d) * eye.reshape(1, 1, pair, 1, pair, 1)
        return q6.reshape(nh, bp, pair * t, pair * d)

    wi_op = wi.reshape(bp, pair, t, N_IDX_HEADS).transpose(0, 3, 1, 2).reshape(bp, N_IDX_HEADS, pair * t)
    ki_all, k_all, v_all = keys
    ki_op = ki_all.reshape(bp, pair, s_pad, IDX_DIM).transpose(0, 2, 1, 3).reshape(bp, s_pad, cdim)
    k_op = (k_all.reshape(bp, pair, s_pad, N_KV_A, HEAD_DIM_A).transpose(0, 2, 3, 1, 4)
            .reshape(bp, s_pad, N_KV_A * cdim))
    vt = v_all.reshape(b, s_pad, N_KV_A, HEAD_DIM_A).transpose(0, 2, 3, 1)
    vt = jnp.concatenate([vt, jnp.ones((b, N_KV_A, V_ROWS - HEAD_DIM_A, s_pad), vt.dtype)], axis=2)
    vt_op = (vt.reshape(bp, pair, N_KV_A, V_ROWS, s_pad).transpose(0, 2, 1, 3, 4)
             .reshape(bp, N_KV_A * v_rows, s_pad))

    body = functools.partial(_dsa_body, s_pad=s_pad, s_real=s_real, q_rows=q_rows, pair=pair, pos0=pos0,
                             topk=topk)
    o_t = pl.pallas_call(
        body,
        grid=(bp, t // q_rows),
        in_specs=[
            pl.BlockSpec((N_HEADS_A, 1, lanes, cdim), lambda i, j: (0, i, j, 0)),
            pl.BlockSpec((N_IDX_HEADS, 1, lanes, cdim), lambda i, j: (0, i, j, 0)),
            pl.BlockSpec((1, N_IDX_HEADS, lanes), lambda i, j: (i, 0, j)),
            pl.BlockSpec((1, s_pad, cdim), lambda i, j: (i, 0, 0)),
            pl.BlockSpec((1, s_pad, N_KV_A * cdim), lambda i, j: (i, 0, 0)),
            pl.BlockSpec((1, N_KV_A * v_rows, s_pad), lambda i, j: (i, 0, 0)),
        ],
        out_specs=pl.BlockSpec((1, WIDTH_A, lanes), lambda i, j: (i, 0, j)),
        out_shape=jax.ShapeDtypeStruct((bp, WIDTH_A, pair * t), BF16),
        scratch_shapes=[pltpu.VMEM((s_pad, lanes), F32), pltpu.VMEM((s_pad, lanes), F32),
                        pltpu.VMEM((N_IDX_HEADS, KEY_BLOCK, lanes), F32),
                        pltpu.VMEM((N_HEADS_A, s_pad, lanes), F32), pltpu.VMEM((N_HEADS_A, v_rows, lanes), F32)],
        compiler_params=_cparams("parallel", "parallel"),
        name="dsa",
    )(block_diag(qa_h), block_diag(qi_h), wi_op, ki_op, k_op, vt_op)
    return o_t.reshape(bp, WIDTH_A, pair, t).transpose(0, 2, 3, 1).reshape(b * t, WIDTH_A)


def _split3(a):
    hi = a.astype(BF16)
    r1 = a - hi.astype(F32)
    mid = r1.astype(BF16)
    lo = (r1 - mid.astype(F32)).astype(BF16)
    return hi, mid, lo


def _hgrn_body(hb_ref, s0_ref, lb_ref, nw_ref, ob_ref, sfin_ref, state_scr, *, tb, nb):
    t = pl.program_id(1)

    @pl.when(t == 0)
    def _():
        for bi, h in itertools.product(range(nb), range(N_HEADS_B)):
            state_scr[bi, h] = s0_ref[bi, h].T

    lb = lb_ref[...]
    r = lax.broadcasted_iota(I32, (CHUNK, CHUNK), 0)
    c = lax.broadcasted_iota(I32, (CHUNK, CHUNK), 1)
    causal = r >= c
    tril = jnp.where(causal, 1.0, 0.0).astype(BF16)

    for ci, bi in itertools.product(range(tb // CHUNK), range(nb)):
        rows = slice(ci * CHUNK, (ci + 1) * CHUNK)
        f = lb + (1.0 - lb) * jax.nn.sigmoid(hb_ref[bi, rows, WIDTH_B:2 * WIDTH_B])
        parts = _split3(jnp.log(f))
        bcum = _dot(tril, parts[0]) + _dot(tril, parts[1]) + _dot(tril, parts[2])
        for h in range(N_HEADS_B):
            lanes = slice(h * HEAD_DIM_B, (h + 1) * HEAD_DIM_B)
            q = hb_ref[bi, rows, lanes]
            k = 1.0 - f[:, lanes]
            v = hb_ref[bi, rows, 2 * WIDTH_B + h * HEAD_DIM_B:2 * WIDTH_B + (h + 1) * HEAD_DIM_B]
            gate = hb_ref[bi, rows, 3 * WIDTH_B + h * HEAD_DIM_B:3 * WIDTH_B + (h + 1) * HEAD_DIM_B]
            bh = bcum[:, lanes]
            b_last = bh[CHUNK - 1:CHUNK, :]
            vb = v.astype(BF16)
            state_t = state_scr[bi, h]
            o_inter = _dot_nt((q * jnp.exp(bh)).astype(BF16), state_t.astype(BF16))
            a_rows = []
            for i in range(CHUNK // HGRN_SUB):
                lo, hi = i * HGRN_SUB, (i + 1) * HGRN_SUB
                ref = bh[lo - 1:lo, :] if i else jnp.zeros((1, HEAD_DIM_B), F32)
                qs = (q[lo:hi] * jnp.exp(bh[lo:hi] - ref)).astype(BF16)
                ks = (k * jnp.exp(ref - bh)).astype(BF16)
                a_rows.append(_dot_nt(qs, ks))
            a = jnp.where(causal, jnp.concatenate(a_rows, axis=0), 0.0)
            o = o_inter + _dot(a.astype(BF16), vb)
            kdec = (k * jnp.exp(b_last - bh)).astype(BF16)
            state_scr[bi, h] = jnp.exp(b_last) * state_t + _dot_tn(vb, kdec)
            ms = jnp.mean(o * o, axis=-1, keepdims=True)
            y = o * lax.rsqrt(ms + RMS_EPS) * nw_ref[...] * (gate * jax.nn.sigmoid(gate))
            ob_ref[bi, rows, lanes] = y.astype(BF16)

    @pl.when(t == pl.num_programs(1) - 1)
    def _():
        for bi, h in itertools.product(range(nb), range(N_HEADS_B)):
            sfin_ref[bi, h] = state_scr[bi, h].T


def _hgrn(hb, s0, lb, nw):
    b, t, _ = hb.shape
    tb = min(256, t)
    nb = 2 if b % 2 == 0 else 1
    return pl.pallas_call(
        functools.partial(_hgrn_body, tb=tb, nb=nb),
        grid=(b // nb, t // tb),
        in_specs=[
            pl.BlockSpec((nb, tb, 4 * WIDTH_B), lambda i, j: (i, j, 0)),
            pl.BlockSpec((nb, N_HEADS_B, HEAD_DIM_B, HEAD_DIM_B), lambda i, j: (i, 0, 0, 0)),
            pl.BlockSpec((1, WIDTH_B), lambda i, j: (0, 0)),
            pl.BlockSpec((1, HEAD_DIM_B), lambda i, j: (0, 0)),
        ],
        out_specs=[
            pl.BlockSpec((nb, tb, WIDTH_B), lambda i, j: (i, j, 0)),
            pl.BlockSpec((nb, N_HEADS_B, HEAD_DIM_B, HEAD_DIM_B), lambda i, j: (i, 0, 0, 0)),
        ],
        out_shape=[jax.ShapeDtypeStruct((b, t, WIDTH_B), BF16),
                   jax.ShapeDtypeStruct((b, N_HEADS_B, HEAD_DIM_B, HEAD_DIM_B), F32)],
        scratch_shapes=[pltpu.VMEM((nb, N_HEADS_B, HEAD_DIM_B, HEAD_DIM_B), F32)],
        compiler_params=_cparams("parallel", "arbitrary"),
        name="hgrn",
    )(hb, s0, lb, nw)


def _outproj_body(oaa_ref, oab_ref, oba_ref, obb_ref, xa_ref, xb_ref, w_ref, lnw_ref, lnb_ref, wrh_ref, wrl_ref, br_ref,
                  x1_ref, lg_ref, *, steps_a):
    first = pl.program_id(0) < steps_a
    pick = lambda a_ref, b_ref: jnp.where(first, a_ref[...], b_ref[...])
    x = pick(xa_ref, xb_ref)
    y = _dot(pick(oaa_ref, oab_ref), w_ref[:WIDTH_A, :]) + _dot(pick(oba_ref, obb_ref), w_ref[WIDTH_A:, :])
    x1 = _layer_norm(DN_ALPHA * x + y, lnw_ref[...], lnb_ref[...])
    x1_ref[...] = x1
    hi = x1.astype(BF16)
    lo = (x1 - hi.astype(F32)).astype(BF16)
    lg_ref[...] = (_dot_nt(wrh_ref[...], hi) + _dot_nt(wrh_ref[...], lo) + _dot_nt(wrl_ref[...], hi)
                   + br_ref[...])


def _two_group_rows(tm, width, steps_a):
    return (pl.BlockSpec((tm, width), lambda i, *_: (jnp.minimum(i, steps_a - 1), 0)),
            pl.BlockSpec((tm, width), lambda i, *_: (jnp.maximum(i - steps_a, 0), 0)))


def _outproj(mix_a, mix_b, x_a, x_b, w_out, lnw, lnb, wr_hi, wr_lo, br):
    tm = MOE_TILE
    assert x_a.shape[0] % tm == 0 and x_b.shape[0] % tm == 0
    n = x_a.shape[0] + x_b.shape[0]
    steps_a = x_a.shape[0] // tm
    row = lambda width: pl.BlockSpec((tm, width), lambda i: (i, 0))
    full = lambda a: pl.BlockSpec(a.shape, lambda i: (0,) * a.ndim)
    return pl.pallas_call(
        functools.partial(_outproj_body, steps_a=steps_a),
        grid=(n // tm,),
        in_specs=[*_two_group_rows(tm, WIDTH_A, steps_a), *_two_group_rows(tm, WIDTH_B, steps_a),
                  *_two_group_rows(tm, D_MODEL, steps_a), full(w_out), full(lnw), full(lnb), full(wr_hi),
                  full(wr_lo), full(br)],
        out_specs=[row(D_MODEL), pl.BlockSpec((N_EXPERTS, tm), lambda i: (0, i))],
        out_shape=[jax.ShapeDtypeStruct((n, D_MODEL), F32), jax.ShapeDtypeStruct((N_EXPERTS, n), F32)],
        compiler_params=_cparams("parallel"),
        name="outproj",
    )(mix_a[0], mix_b[0], mix_a[1], mix_b[1], x_a, x_b, w_out, lnw, lnb, wr_hi, wr_lo, br)


def _route_body(lg_ref, lpos_ref, gate_ref, tc_ref, base_ref, cnt_ref, cnt_scr, *, tr):
    @pl.when(pl.program_id(0) == 0)
    def _():
        cnt_scr[...] = jnp.zeros_like(cnt_scr)

    l = lg_ref[...]
    rows = lax.broadcasted_iota(I32, (N_EXPERTS, tr), 0)
    vals, hots = [], []
    for j in range(TOP_K):
        m = jnp.max(l, axis=0, keepdims=True)
        idx = jnp.min(jnp.where(l == m, rows, N_EXPERTS), axis=0, keepdims=True)
        hot = rows == idx
        vals.append(m)
        hots.append(hot)
        l = jnp.where(hot, NEG_INF, l)
    es = [jnp.exp(v - vals[0]) for v in vals]
    inv = 1.0 / (es[0] + es[1] + es[2] + es[3])
    for j in range(TOP_K):
        gate_ref[j:j + 1, :] = es[j] * inv
    chosen = jnp.zeros((N_EXPERTS, tr), F32)
    for hot in hots:
        chosen = chosen + jnp.where(hot, 1.0, 0.0)
    r = lax.broadcasted_iota(I32, (tr, tr), 0)
    c = lax.broadcasted_iota(I32, (tr, tr), 1)
    before = jnp.where(r < c, 1.0, 0.0).astype(BF16)
    prior = _dot(chosen.astype(BF16), before)
    units = jnp.ceil(jnp.sum(chosen, axis=1, keepdims=True) * (1.0 / ROW_ALIGN))
    er = lax.broadcasted_iota(I32, (N_EXPERTS, N_EXPERTS), 0)
    ec = lax.broadcasted_iota(I32, (N_EXPERTS, N_EXPERTS), 1)
    earlier = jnp.where(ec < er, 1.0, 0.0).astype(BF16)
    run_len = jnp.broadcast_to(units, (N_EXPERTS, 128)) * ROW_ALIGN
    run_start = _dot(earlier, jnp.broadcast_to(units, (N_EXPERTS, 128)).astype(BF16)) * ROW_ALIGN
    where_in_tile = prior + run_start[:, 0:1]
    for j in range(TOP_K):
        lpos_ref[j:j + 1, :] = jnp.sum(jnp.where(hots[j], where_in_tile, 0.0), axis=0,
                                       keepdims=True).astype(I32)
    tc_ref[0] = run_len.astype(I32)
    base_ref[0] = cnt_scr[...].astype(I32)
    cnt_scr[...] = cnt_scr[...] + run_len
    cnt_ref[...] = cnt_scr[...].astype(I32)


def _route(logits_t):
    n = logits_t.shape[1]
    tr = MOE_TILE
    tiles = n // tr
    tok = lambda rows: pl.BlockSpec((rows, tr), lambda i: (0, i))
    per_tile = pl.BlockSpec((1, N_EXPERTS, 128), lambda i: (i, 0, 0))
    return pl.pallas_call(
        functools.partial(_route_body, tr=tr),
        grid=(tiles,),
        in_specs=[tok(N_EXPERTS)],
        out_specs=[tok(TOP_K), tok(TOP_K), per_tile, per_tile, pl.BlockSpec((N_EXPERTS, 128), lambda i: (0, 0))],
        out_shape=[jax.ShapeDtypeStruct((TOP_K, n), I32), jax.ShapeDtypeStruct((TOP_K, n), F32),
                   jax.ShapeDtypeStruct((tiles, N_EXPERTS, 128), I32),
                   jax.ShapeDtypeStruct((tiles, N_EXPERTS, 128), I32),
                   jax.ShapeDtypeStruct((N_EXPERTS, 128), I32)],
        scratch_shapes=[pltpu.VMEM((N_EXPERTS, 128), F32)],
        compiler_params=_cparams("arbitrary"),
        name="route",
    )(logits_t)


def _for_each_run_piece(tc_ref, g0_ref, t, fn):
    def per_expert(e, local):
        c = tc_ref[t, e]
        glob = g0_ref[t, e]
        lo = local
        for p in RUN_PIECES:
            take = (c & p) != 0

            @pl.when(take)
            def _():
                fn(pl.multiple_of(lo, ROW_ALIGN), pl.multiple_of(glob, ROW_ALIGN), p)

            step = jnp.where(take, p, 0)
            lo = lo + step
            glob = glob + step
        return local + c

    lax.fori_loop(0, N_EXPERTS, per_expert, 0)


def _dispatch_body(tc_ref, g0_ref, seg_ref, lpos_ref, x_ref, buf_ref, stage, zero_scr, zsem, sems, *, n_blocks):
    i = pl.program_id(0)
    slot = i % 2

    def run_copy(slot_, lo, glob, rows):
        return pltpu.make_async_copy(stage.at[slot_, pl.ds(lo, rows), :], buf_ref.at[pl.ds(glob, rows), :],
                                     sems.at[slot_])

    @pl.when(i == 0)
    def _():
        zero_scr[...] = jnp.zeros_like(zero_scr)
        n_used = seg_ref[1, N_EXPERTS - 1] // EXPERT_ROWS

        def block_copy(start):
            return pltpu.make_async_copy(
                zero_scr, buf_ref.at[pl.ds(pl.multiple_of(start, EXPERT_ROWS), EXPERT_ROWS), :], zsem)

        def clears(action):
            for e in range(N_EXPERTS):
                @pl.when(seg_ref[1, e] > seg_ref[0, e])
                def _():
                    action(block_copy(seg_ref[1, e] - EXPERT_ROWS))

                @pl.when(n_used + e < n_blocks)
                def _():
                    action(block_copy((n_used + e) * EXPERT_ROWS))

        clears(lambda cp: cp.start())
        clears(lambda cp: cp.wait())

    r = lax.broadcasted_iota(I32, (MOE_SLOTS, MOE_TILE), 0)
    onehot = jnp.zeros((MOE_SLOTS, MOE_TILE), F32)
    for j in range(TOP_K):
        onehot = jnp.where(r == lpos_ref[j:j + 1, :], 1.0, onehot)
    stage[slot] = _dot(onehot.astype(BF16), x_ref[...].astype(BF16)).astype(BF16)

    @pl.when(i > 0)
    def _():
        _for_each_run_piece(tc_ref, g0_ref, i - 1,
                            lambda lo, glob, rows: run_copy(1 - slot, lo, glob, rows).wait())

    _for_each_run_piece(tc_ref, g0_ref, i, lambda lo, glob, rows: run_copy(slot, lo, glob, rows).start())

    @pl.when(i == pl.num_programs(0) - 1)
    def _():
        _for_each_run_piece(tc_ref, g0_ref, i, lambda lo, glob, rows: run_copy(slot, lo, glob, rows).wait())


def _dispatch(x1, lpos_t, tc, g0, seg, n_rows):
    n = x1.shape[0]
    return pl.pallas_call(
        functools.partial(_dispatch_body, n_blocks=n_rows // EXPERT_ROWS),
        grid_spec=pltpu.PrefetchScalarGridSpec(
            num_scalar_prefetch=3,
            grid=(n // MOE_TILE,),
            in_specs=[pl.BlockSpec((TOP_K, MOE_TILE), lambda i, *_: (0, i)),
                      pl.BlockSpec((MOE_TILE, D_MODEL), lambda i, *_: (i, 0))],
            out_specs=pl.BlockSpec(memory_space=pl.ANY),
            scratch_shapes=[pltpu.VMEM((2, MOE_SLOTS, D_MODEL), BF16),
                            pltpu.VMEM((EXPERT_ROWS, D_MODEL), BF16),
                            pltpu.SemaphoreType.DMA(()), pltpu.SemaphoreType.DMA((2,))],
        ),
        out_shape=jax.ShapeDtypeStruct((n_rows, D_MODEL), BF16),
        compiler_params=_cparams("arbitrary"),
        name="dispatch",
    )(tc, g0, seg, lpos_t, x1)


def _experts_body(be_ref, nu_ref, x_ref, wgu_ref, bgu_ref, wd_ref, bd_ref, o_ref, wgu_bf, wd_bf):
    i = pl.program_id(0)

    @pl.when(jnp.logical_or(i == 0, be_ref[i] != be_ref[jnp.maximum(i - 1, 0)]))
    def _():
        for r in range(0, D_MODEL, 256):
            wgu_bf[r:r + 256, :] = wgu_ref[0, r:r + 256, :].astype(BF16)
            wd_bf[r:r + 256, :] = wd_ref[0, r:r + 256, :].astype(BF16)

    @pl.when(i < nu_ref[0])
    def _():
        gu = _dot(x_ref[...], wgu_bf[...]) + bgu_ref[0]
        gate = jnp.minimum(gu[:, :D_EXPERT], SWIGLU_LIMIT)
        up = jnp.clip(gu[:, D_EXPERT:], -SWIGLU_LIMIT, SWIGLU_LIMIT)
        hdn = (up + 1.0) * gate * jax.nn.sigmoid(SWIGLU_ALPHA * gate)
        o_ref[...] = (_dot(hdn.astype(BF16), wd_bf[...]) + bd_ref[0]).astype(BF16)

    @pl.when(i >= nu_ref[0])
    def _():
        o_ref[...] = jnp.zeros_like(o_ref)


def _experts(buf, block_e, n_used, w_gu, b_gu, w_down, b_down):
    n_blocks = buf.shape[0] // EXPERT_ROWS
    rows = pl.BlockSpec((EXPERT_ROWS, D_MODEL), lambda i, be, nu: (jnp.minimum(i, nu[0] - 1), 0))
    per_e = lambda a: pl.BlockSpec((1,) + a.shape[1:], lambda i, be, nu: (be[i],) + (0,) * (a.ndim - 1))
    return pl.pallas_call(
        _experts_body,
        grid_spec=pltpu.PrefetchScalarGridSpec(
            num_scalar_prefetch=2,
            grid=(n_blocks,),
            in_specs=[rows, per_e(w_gu), per_e(b_gu), per_e(w_down), per_e(b_down)],
            out_specs=pl.BlockSpec((EXPERT_ROWS, D_MODEL), lambda i, be, nu: (i, 0)),
            scratch_shapes=[pltpu.VMEM((D_MODEL, 2 * D_EXPERT), BF16), pltpu.VMEM((D_EXPERT, D_MODEL), BF16)],
        ),
        out_shape=jax.ShapeDtypeStruct(buf.shape, BF16),
        compiler_params=_cparams("arbitrary"),
        name="experts",
    )(block_e, n_used, buf, w_gu, b_gu, w_down, b_down)


def _final_body(tc_ref, g0_ref, lpos_ref, gate_ref, x1_ref, pa_ref, pb_ref, lnw_ref, lnb_ref, wg_ref, bg_ref, wp_ref,
                eo_ref, ya_ref, yb_ref, stage, sems, *, steps_a):
    i = pl.program_id(0)
    slot = i % 2

    def run_copy(slot_, lo, glob, rows):
        return pltpu.make_async_copy(eo_ref.at[pl.ds(glob, rows), :], stage.at[slot_, pl.ds(lo, rows), :],
                                     sems.at[slot_])

    def fetch(tile, slot_):
        stage[slot_] = jnp.zeros(stage.shape[1:], stage.dtype)
        _for_each_run_piece(tc_ref, g0_ref, tile, lambda lo, glob, rows: run_copy(slot_, lo, glob, rows).start())

    @pl.when(i == 0)
    def _():
        fetch(0, 0)

    @pl.when(i + 1 < pl.num_programs(0))
    def _():
        fetch(i + 1, 1 - slot)

    _for_each_run_piece(tc_ref, g0_ref, i, lambda lo, glob, rows: run_copy(slot, lo, glob, rows).wait())

    r = lax.broadcasted_iota(I32, (MOE_TILE, MOE_SLOTS), 1)
    g = jnp.zeros((MOE_TILE, MOE_SLOTS), F32)
    for j in range(TOP_K):
        g = jnp.where(r == lpos_ref[:, j:j + 1], gate_ref[:, j:j + 1], g)
    g_hi = g.astype(BF16)
    g_lo = (g - g_hi.astype(F32)).astype(BF16)
    g2 = jnp.concatenate([g_hi, g_lo], axis=0)
    both = _dot(g2, stage[slot])
    moe = both[:MOE_TILE] + both[MOE_TILE:]
    x2 = _layer_norm(DN_ALPHA * x1_ref[...] + moe, lnw_ref[...], lnb_ref[...])
    ple_gate = jax.nn.sigmoid(_dot(x2.astype(BF16), wg_ref[...]) + bg_ref[...])
    p = jnp.where(i < steps_a, pa_ref[...], pb_ref[...])
    y = x2 + ple_gate * _dot(p.astype(BF16), wp_ref[...])

    @pl.when(i < steps_a)
    def _():
        ya_ref[...] = y

    @pl.when(i >= steps_a)
    def _():
        yb_ref[...] = y


def _final(tc, g0, lpos, gate, x1, p_a, p_b, lnw, lnb, w_gate, b_gate, w_ple, expert_out):
    n = x1.shape[0]
    steps_a = p_a.shape[0] // MOE_TILE
    row = lambda width: pl.BlockSpec((MOE_TILE, width), lambda i, *_: (i, 0))
    full = lambda a: pl.BlockSpec(a.shape, lambda i, *_: (0,) * a.ndim)
    return pl.pallas_call(
        functools.partial(_final_body, steps_a=steps_a),
        grid_spec=pltpu.PrefetchScalarGridSpec(
            num_scalar_prefetch=2,
            grid=(n // MOE_TILE,),
            in_specs=[row(TOP_K), row(TOP_K), row(D_MODEL), *_two_group_rows(MOE_TILE, D_PLE, steps_a), full(lnw),
                      full(lnb), full(w_gate), full(b_gate), full(w_ple), pl.BlockSpec(memory_space=pl.ANY)],
            out_specs=list(_two_group_rows(MOE_TILE, D_MODEL, steps_a)),
            scratch_shapes=[pltpu.VMEM((2, MOE_SLOTS, D_MODEL), BF16), pltpu.SemaphoreType.DMA((2,))],
        ),
        out_shape=[jax.ShapeDtypeStruct((p_a.shape[0], D_MODEL), F32),
                   jax.ShapeDtypeStruct((p_b.shape[0], D_MODEL), F32)],
        compiler_params=_cparams("arbitrary"),
        name="combine_final",
    )(tc, g0, lpos, gate, x1, p_a, p_b, lnw, lnb, w_gate, b_gate, w_ple, expert_out)


def _round_up(a, m):
    return -(-a // m) * m


def _token_mixers(x, hist_k, hist_v, hist_kidx, s0, pos0, wts):
    b, t, _ = x.shape
    n = b * t
    x2d = x.reshape(n, D_MODEL)

    qa, ka, va, ka_bf, va_bf, qi, ki, ki_bf, wi, hb = _inproj(x2d, wts["w_in"], wts["idx_lnw"], wts["idx_lnb"])

    k_new = ka_bf.reshape(b, t, N_KV_A * HEAD_DIM_A)
    v_new = va_bf.reshape(b, t, N_KV_A * HEAD_DIM_A)
    ki_new = ki_bf.reshape(b, t, IDX_DIM)
    q_ops = (qa.reshape(N_HEADS_A, b, t, HEAD_DIM_A), qi.reshape(N_IDX_HEADS, b, t, IDX_DIM),
             wi.reshape(b, t, N_IDX_HEADS))
    past = 0 if hist_k is None else hist_k.shape[1]
    s_real = past + t
    topk = min(TOPK_MAX, s_real // 4)
    if past:
        k_all = jnp.concatenate([hist_k.reshape(b, past, -1).astype(BF16), k_new], axis=1)
        v_all = jnp.concatenate([hist_v.reshape(b, past, -1).astype(BF16), v_new], axis=1)
        ki_all = jnp.concatenate([hist_kidx.astype(BF16), ki_new], axis=1)
    else:
        k_all, v_all, ki_all = k_new, v_new, ki_new
    pad = ((0, 0), (0, _round_up(s_real, KEY_BLOCK) - s_real), (0, 0))
    keys = tuple(jnp.pad(a, pad) for a in (ki_all, k_all, v_all))
    oa = _dsa(*q_ops, keys, s_real=s_real, pos0=pos0, topk=topk)

    ob, s_fin = _hgrn(hb.reshape(b, t, 4 * WIDTH_B), s0, wts["lb"], wts["hgrn_nw"])
    return (oa, ob.reshape(n, WIDTH_B), ka.reshape(b, t, N_KV_A, HEAD_DIM_A), va.reshape(b, t, N_KV_A, HEAD_DIM_A),
            ki.reshape(b, t, IDX_DIM), s_fin)


def _channel_mixer(mix_a, mix_b, x_a, x_b, p_a, p_b, wts):
    shape_a, shape_b = x_a.shape, x_b.shape
    x_a, x_b = x_a.reshape(-1, D_MODEL), x_b.reshape(-1, D_MODEL)
    n = x_a.shape[0] + x_b.shape[0]
    x1, logits_t = _outproj(mix_a, mix_b, x_a, x_b, wts["w_out"], wts["ln1_w"], wts["ln1_b"],
                            wts["wr_hi"], wts["wr_lo"], wts["b_router"])

    tiles = n // MOE_TILE
    lpos_t, gate_t, tile_runs, tile_base, counts = _route(logits_t)
    tc, counts = tile_runs[:, :, 0], counts[:, 0]
    padded = (counts + EXPERT_ROWS - 1) // EXPERT_ROWS * EXPERT_ROWS
    seg_end = jnp.cumsum(padded)
    seg_start = seg_end - padded
    g0 = (seg_start[None, :] + tile_base[:, :, 0]).astype(I32)
    n_blocks = -(-(n * TOP_K + N_EXPERTS * tiles * (ROW_ALIGN - 1) + N_EXPERTS * (EXPERT_ROWS - 1))
                 // EXPERT_ROWS)
    block_row0 = jnp.arange(n_blocks, dtype=I32) * EXPERT_ROWS
    block_e = jnp.minimum(jnp.sum((seg_end[None, :] <= block_row0[:, None]).astype(I32), axis=1), N_EXPERTS - 1)
    n_used = (seg_end[-1:] // EXPERT_ROWS).astype(I32)
    seg = jnp.stack([seg_start, seg_end]).astype(I32)
    buf = _dispatch(x1, lpos_t, tc, g0, seg, n_blocks * EXPERT_ROWS)
    expert_out = _experts(buf, block_e, n_used, wts["w_gu"], wts["b_gu"], wts["w_down"], wts["b_down"])
    y_a, y_b = _final(tc, g0, lpos_t.T, gate_t.T, x1, p_a.reshape(-1, D_PLE), p_b.reshape(-1, D_PLE),
                      wts["ln2_w"], wts["ln2_b"], wts["w_ple_gate"], wts["b_ple_gate"], wts["w_ple"], expert_out)
    return y_a.reshape(shape_a), y_b.reshape(shape_b)


def _prep_weights(w_in, w_out, idx_k_norm_w, idx_k_norm_b, lb, hgrn_norm_w, ln1_w, ln1_b, w_router, b_router,
                  w_gu, b_gu, w_down, b_down, ln2_w, ln2_b, w_ple, w_ple_gate, b_ple_gate):
    n_a = COL_KW + IDX_DIM + N_IDX_HEADS
    w_pad = jnp.concatenate(
        [w_in[:, :n_a], jnp.zeros((D_MODEL, COL_HB - n_a), w_in.dtype), w_in[:, n_a:]], axis=1).astype(BF16)
    wr_t = w_router.T
    wr_hi = wr_t.astype(BF16)
    row = lambda a: a.reshape(1, -1)
    return dict(
        w_in=w_pad, idx_lnw=row(idx_k_norm_w), idx_lnb=row(idx_k_norm_b), lb=row(lb), hgrn_nw=row(hgrn_norm_w),
        w_out=w_out.astype(BF16), ln1_w=row(ln1_w), ln1_b=row(ln1_b),
        wr_hi=wr_hi, wr_lo=(wr_t - wr_hi.astype(F32)).astype(BF16), b_router=b_router.reshape(N_EXPERTS, 1),
        w_gu=w_gu, b_gu=b_gu.reshape(N_EXPERTS, 1, 2 * D_EXPERT),
        w_down=w_down, b_down=b_down.reshape(N_EXPERTS, 1, D_MODEL),
        ln2_w=row(ln2_w), ln2_b=row(ln2_b), w_ple=w_ple.astype(BF16), w_ple_gate=w_ple_gate.astype(BF16),
        b_ple_gate=row(b_ple_gate))


def kernel(x_prompt, x_sample, cache_k, cache_v, cache_kidx, state_hgrn, p_prompt, p_sample, w_in, w_out,
           idx_k_norm_w, idx_k_norm_b, hgrn_lb_logits, hgrn_norm_w, ln1_w, ln1_b, w_router, b_router, w_gu, b_gu,
           w_down, b_down, ln2_w, ln2_b, w_ple, w_ple_gate, b_ple_gate):
    lb_all = jnp.cumsum(jax.nn.softmax(hgrn_lb_logits.astype(F32), axis=0), axis=0)
    xp, xs = x_prompt, x_sample
    outs = [[] for _ in range(8)]
    for i in range(DEPTH):
        wts = _prep_weights(w_in[i], w_out[i], idx_k_norm_w[i], idx_k_norm_b[i], lb_all[i], hgrn_norm_w[i],
                            ln1_w[i], ln1_b[i], w_router[i], b_router[i], w_gu[i], b_gu[i], w_down[i], b_down[i],
                            ln2_w[i], ln2_b[i], w_ple[i], w_ple_gate[i], b_ple_gate[i])
        s0p = jnp.zeros((xp.shape[0], N_HEADS_B, HEAD_DIM_B, HEAD_DIM_B), F32)
        mix_p = _token_mixers(xp, None, None, None, s0p, 0, wts)
        mix_s = _token_mixers(xs, cache_k[i], cache_v[i], cache_kidx[i], state_hgrn[i], cache_k.shape[2], wts)
        xp, xs = _channel_mixer(mix_p, mix_s, xp, xs, p_prompt[i], p_sample[i], wts)
        for lst, val in zip(outs, mix_p[2:] + mix_s[2:]):
            lst.append(val)
    return (xp, xs) + tuple(jnp.stack(l) for l in outs)
```

```python
import functools
import itertools

import jax
import jax.numpy as jnp
from jax import lax
from jax.experimental import pallas as pl
from jax.experimental.pallas import tpu as pltpu

F32 = jnp.float32
BF16 = jnp.bfloat16
I32 = jnp.int32

D_MODEL = 1024
CHUNK = 64
CHUNK_SHIFT = 6
WIDTH_A = 512
HEAD_DIM_A = 64
N_HEADS_A = 8
N_KV_A = 2
N_IDX_HEADS = 4
IDX_DIM = 64
IDX_SCALE = IDX_DIM ** -0.5 * N_IDX_HEADS ** -0.5
TOPK_MAX = 256
WIDTH_B = 512
HEAD_DIM_B = 128
N_HEADS_B = 4
N_EXPERTS = 32
TOP_K = 4
D_EXPERT = 1024
SWIGLU_LIMIT = 7.0
SWIGLU_ALPHA = 1.702
D_PLE = 256
LN_EPS = 1e-5
RMS_EPS = 1e-6
DEPTH = 1
DN_ALPHA = (2 * DEPTH) ** 0.25

COL_QA, COL_KA, COL_VA, COL_QI, COL_KW, COL_HB, COL_END = 0, 512, 640, 768, 1024, 1152, 3200

VMEM_LIMIT = 56 * 1024 * 1024
INPROJ_ROWS = 1024
KEY_BLOCK = 256
V_ROWS = 80
LOG2_E = 1.4426950408889634
HGRN_SUB = 16
EXPERT_ROWS = 512
MOE_TILE = 512
ROW_ALIGN = 16
RUN_PIECES = (512, 256, 128, 64, 32, 16)
MOE_SLOTS = -(-(TOP_K * MOE_TILE + N_EXPERTS * (ROW_ALIGN - 1)) // 256) * 256
NEG_INF = float("-inf")
INT_MIN = -(2 ** 31)


def _cparams(*sem):
    return pltpu.CompilerParams(dimension_semantics=sem, vmem_limit_bytes=VMEM_LIMIT)


def _dot(a, b):
    return jnp.dot(a, b, preferred_element_type=F32)


def _dot_nt(a, b):
    return lax.dot_general(a, b, (((1,), (1,)), ((), ())), preferred_element_type=F32)


def _dot_tn(a, b):
    return lax.dot_general(a, b, (((0,), (0,)), ((), ())), preferred_element_type=F32)


def _layer_norm(z, w, b):
    mu = jnp.mean(z, axis=-1, keepdims=True)
    d = z - mu
    var = jnp.mean(d * d, axis=-1, keepdims=True)
    return d * lax.rsqrt(var + LN_EPS) * w + b


def _inproj_body(x_ref, w_ref, lnw_ref, lnb_ref, qa_ref, ka_ref, va_ref, kab_ref, vab_ref, qi_ref, ki_ref, kib_ref,
                 wi_ref, hb_ref):
    xb = x_ref[...].astype(BF16)

    def mm(c0, c1):
        return _dot(xb, w_ref[:, c0:c1])

    qa = (mm(COL_QA, COL_KA) * (HEAD_DIM_A ** -0.5 * LOG2_E)).astype(BF16)
    for h in range(N_HEADS_A):
        qa_ref[h] = qa[:, h * HEAD_DIM_A:(h + 1) * HEAD_DIM_A]
    for src, out3_ref, bf_ref in ((mm(COL_KA, COL_VA), ka_ref, kab_ref), (mm(COL_VA, COL_QI), va_ref, vab_ref)):
        for g in range(N_KV_A):
            out3_ref[:, g, :] = src[:, g * HEAD_DIM_A:(g + 1) * HEAD_DIM_A]
        bf_ref[...] = src.astype(BF16)
    qi = mm(COL_QI, COL_KW).astype(BF16)
    for h in range(N_IDX_HEADS):
        qi_ref[h] = qi[:, h * IDX_DIM:(h + 1) * IDX_DIM]
    kw = mm(COL_KW, COL_HB)
    ki = _layer_norm(kw[:, :IDX_DIM], lnw_ref[...], lnb_ref[...])
    ki_ref[...] = ki
    kib_ref[...] = ki.astype(BF16)
    wi_ref[...] = kw[:, IDX_DIM:IDX_DIM + N_IDX_HEADS]
    hb_ref[...] = mm(COL_HB, COL_END)


def _inproj(x2d, w_pad, lnw, lnb):
    n = x2d.shape[0]
    tm = min(INPROJ_ROWS, n)
    row = lambda width: pl.BlockSpec((tm, width), lambda i: (i, 0))
    full = lambda a: pl.BlockSpec(a.shape, lambda i: (0,) * a.ndim)
    heads = lambda nh, d: pl.BlockSpec((nh, tm, d), lambda i: (0, i, 0))
    kv = N_KV_A * HEAD_DIM_A
    kv3 = pl.BlockSpec((tm, N_KV_A, HEAD_DIM_A), lambda i: (i, 0, 0))
    sds = jax.ShapeDtypeStruct
    return pl.pallas_call(
        _inproj_body,
        grid=(n // tm,),
        in_specs=[row(D_MODEL), full(w_pad), full(lnw), full(lnb)],
        out_specs=[heads(N_HEADS_A, HEAD_DIM_A), kv3, kv3, row(kv), row(kv), heads(N_IDX_HEADS, IDX_DIM),
                   row(IDX_DIM), row(IDX_DIM), row(N_IDX_HEADS), row(4 * WIDTH_B)],
        out_shape=[sds((N_HEADS_A, n, HEAD_DIM_A), BF16), sds((n, N_KV_A, HEAD_DIM_A), F32),
                   sds((n, N_KV_A, HEAD_DIM_A), F32), sds((n, kv), BF16), sds((n, kv), BF16),
                   sds((N_IDX_HEADS, n, IDX_DIM), BF16), sds((n, IDX_DIM), F32), sds((n, IDX_DIM), BF16),
                   sds((n, N_IDX_HEADS), F32), sds((n, 4 * WIDTH_B), F32)],
        compiler_params=_cparams("parallel"),
        name="inproj",
    )(x2d, w_pad, lnw, lnb)


def _order_bits_to_f32(u):
    key = u ^ INT_MIN
    bits = key ^ ((key >> 31) & 0x7FFFFFFF)
    f = lax.bitcast_convert_type(bits, F32)
    return jnp.where(u >= 0, jnp.where(u <= 0x007FFFFF, NEG_INF, f), f)


def _dsa_body(qa_ref, qi_ref, wi_ref, kidx_ref, k_ref, vt_ref, o_ref, sc_scr, bias_scr, term_scr, s_scr, acc_scr,
              *, s_pad, s_real, q_rows, pair, pos0, topk):
    kb_rows = KEY_BLOCK
    heads_per_kv = N_HEADS_A // N_KV_A
    qb = pair * q_rows
    cdim = pair * HEAD_DIM_A
    v_rows = pair * V_ROWS
    j = pl.program_id(1)
    q_lo = pos0 + j * q_rows
    lane = lax.broadcasted_iota(I32, (1, qb), 1)
    q_chunk = (q_lo + (lane & (q_rows - 1))) >> CHUNK_SHIFT
    k_lim = (((q_lo + q_rows - 1) >> CHUNK_SHIFT) + 1) * CHUNK
    nkb = jnp.minimum(s_pad // kb_rows, (k_lim + kb_rows - 1) // kb_rows)

    def rows_of(kb):
        return pl.ds(pl.multiple_of(kb * kb_rows, kb_rows), kb_rows)

    def fold8(a):
        return a.reshape(kb_rows // 8, 8, qb)

    def over_blocks(body, init):
        def quad(i, carry):
            for u in range(4):
                carry = body(4 * i + u, carry)
            return carry
        carry = lax.fori_loop(0, nkb // 4, quad, init)
        done = (nkb // 4) * 4
        carry = lax.cond((nkb & 2) != 0, lambda c: body(done + 1, body(done, c)), lambda c: c, carry)
        done = done + (nkb & 2)
        return lax.cond((nkb & 1) != 0, lambda c: body(done, c), lambda c: c, carry)

    def score_blk(kb, _):
        rows = rows_of(kb)
        kidx = kidx_ref[0, rows, :]
        for h in range(N_IDX_HEADS):
            term_scr[h] = jnp.maximum(_dot_nt(kidx, qi_ref[h, 0]), 0.0) * wi_ref[0, h:h + 1, :]
        total = (term_scr[0] + term_scr[1]) + (term_scr[2] + term_scr[3])
        spos = kb * kb_rows + lax.broadcasted_iota(I32, (kb_rows, qb), 0)
        sc = jnp.where((spos >> CHUNK_SHIFT) <= q_chunk, total * IDX_SCALE + 0.0, NEG_INF)
        if s_real < s_pad:
            sc = jnp.where(spos < s_real, sc, NEG_INF)
        sc_scr[rows, :] = sc
        return 0

    over_blocks(score_blk, 0)

    def count(thr, strict):
        def body(kb, acc):
            blk = sc_scr[rows_of(kb), :]
            hit = (blk > thr) if strict else (blk >= thr)
            return acc + jnp.sum(fold8(jnp.where(hit, 1, 0).astype(I32)), axis=0)
        acc = over_blocks(body, jnp.zeros((8, qb), I32))
        return jnp.sum(acc, axis=0, keepdims=True)

    def bit_step(i, carry):
        prefix, cnt = carry
        cand = prefix | lax.shift_left(jnp.int32(1), 31 - i)
        c = count(_order_bits_to_f32(cand), False)
        take = c >= topk
        return jnp.where(take, cand, prefix), jnp.where(take, c, cnt)

    prefix, cnt_ge = lax.fori_loop(0, 32, bit_step,
                                   (jnp.zeros((1, qb), I32), jnp.full((1, qb), nkb * kb_rows, I32)))
    tau = _order_bits_to_f32(prefix)
    finite_tau = tau > NEG_INF
    tau_floor = jnp.maximum(tau, jnp.finfo(F32).min)
    tie_lanes = jnp.where(finite_tau, jnp.where(cnt_ge > topk, 1, 0), 0)
    has_ties = jnp.max(tie_lanes) > 0

    @pl.when(jnp.logical_not(has_ties))
    def _():
        def body(kb, _):
            blk = sc_scr[rows_of(kb), :]
            bias_scr[rows_of(kb), :] = jnp.where(blk >= tau_floor, 0.0, NEG_INF)
            return 0
        over_blocks(body, 0)

    @pl.when(has_ties)
    def _():
        need = jnp.where(finite_tau, (topk - count(tau, True)).astype(F32), 0.0)
        r = lax.broadcasted_iota(I32, (kb_rows, kb_rows), 0)
        c = lax.broadcasted_iota(I32, (kb_rows, kb_rows), 1)
        tril = jnp.where(r >= c, 1.0, 0.0).astype(BF16)

        def body(kb, seen):
            blk = sc_scr[rows_of(kb), :]
            eq = blk == tau
            rank = _dot(tril, jnp.where(eq, 1.0, 0.0).astype(BF16)) + seen
            tie_bias = jnp.where(eq, jnp.where(rank <= need, 0.0, NEG_INF), NEG_INF)
            bias_scr[rows_of(kb), :] = jnp.where(blk > tau, 0.0, tie_bias)
            return rank[kb_rows - 1:kb_rows, :]
        lax.fori_loop(0, nkb, body, jnp.zeros((1, qb), F32))

    def pass1(kb, m8):
        rows = rows_of(kb)
        bias = bias_scr[rows, :]
        out = []
        for h in range(N_HEADS_A):
            g = h // heads_per_kv
            s = _dot_nt(k_ref[0, rows, g * cdim:(g + 1) * cdim], qa_ref[h, 0]) + bias
            s_scr[h, rows, :] = s
            out.append(jnp.maximum(m8[h], jnp.max(fold8(s), axis=0)))
        return tuple(out)

    m8 = over_blocks(pass1, tuple(jnp.full((8, qb), NEG_INF, F32) for _ in range(N_HEADS_A)))
    m = [jnp.max(x, axis=0, keepdims=True) for x in m8]

    acc_scr[...] = jnp.zeros_like(acc_scr)

    def pass2(kb, _):
        rows = rows_of(kb)
        for h in range(N_HEADS_A):
            g = h // heads_per_kv
            p = jnp.exp2(s_scr[h, rows, :] - m[h]).astype(BF16)
            acc_scr[h] += _dot(vt_ref[0, g * v_rows:(g + 1) * v_rows, rows], p)
        return 0

    over_blocks(pass2, 0)
    for h in range(N_HEADS_A):
        a = acc_scr[h]
        num, den = a[:HEAD_DIM_A], a[HEAD_DIM_A:HEAD_DIM_A + 1]
        for mb in range(1, pair):
            mine = lane >= mb * q_rows
            r0 = mb * V_ROWS
            num = jnp.where(mine, a[r0:r0 + HEAD_DIM_A], num)
            den = jnp.where(mine, a[r0 + HEAD_DIM_A:r0 + HEAD_DIM_A + 1], den)
        o_ref[0, h * HEAD_DIM_A:(h + 1) * HEAD_DIM_A, :] = (num * (1.0 / den)).astype(BF16)


def _dsa_pairing(b, t):
    return 2 if (2 * t <= 128 and b % 2 == 0) else 1


def _dsa(qa_h, qi_h, wi, keys, *, s_real, pos0, topk):
    _, b, t, _ = qa_h.shape
    s_pad = keys[0].shape[1]
    q_rows = min(256, t)
    pair = _dsa_pairing(b, t)
    bp = b // pair
    lanes, cdim, v_rows = pair * q_rows, pair * HEAD_DIM_A, pair * V_ROWS

    def block_diag(q):
        if pair == 1:
            return q
        nh, d = q.shape[0], q.shape[3]
        eye = jnp.eye(pair, dtype=q.dtype)
        q6 = q.reshape(nh, bp, pair, t, 1, d) * eye.reshape(1, 1, pair, 1, pair, 1)
        return q6.reshape(nh, bp, pair * t, pair * d)

    wi_op = wi.reshape(bp, pair, t, N_IDX_HEADS).transpose(0, 3, 1, 2).reshape(bp, N_IDX_HEADS, pair * t)
    ki_all, k_all, v_all = keys
    ki_op = ki_all.reshape(bp, pair, s_pad, IDX_DIM).transpose(0, 2, 1, 3).reshape(bp, s_pad, cdim)
    k_op = (k_all.reshape(bp, pair, s_pad, N_KV_A, HEAD_DIM_A).transpose(0, 2, 3, 1, 4)
            .reshape(bp, s_pad, N_KV_A * cdim))
    vt = v_all.reshape(b, s_pad, N_KV_A, HEAD_DIM_A).transpose(0, 2, 3, 1)
    vt = jnp.concatenate([vt, jnp.ones((b, N_KV_A, V_ROWS - HEAD_DIM_A, s_pad), vt.dtype)], axis=2)
    vt_op = (vt.reshape(bp, pair, N_KV_A, V_ROWS, s_pad).transpose(0, 2, 1, 3, 4)
             .reshape(bp, N_KV_A * v_rows, s_pad))

    body = functools.partial(_dsa_body, s_pad=s_pad, s_real=s_real, q_rows=q_rows, pair=pair, pos0=pos0,
                             topk=topk)
    o_t = pl.pallas_call(
        body,
        grid=(bp, t // q_rows),
        in_specs=[
            pl.BlockSpec((N_HEADS_A, 1, lanes, cdim), lambda i, j: (0, i, j, 0)),
            pl.BlockSpec((N_IDX_HEADS, 1, lanes, cdim), lambda i, j: (0, i, j, 0)),
            pl.BlockSpec((1, N_IDX_HEADS, lanes), lambda i, j: (i, 0, j)),
            pl.BlockSpec((1, s_pad, cdim), lambda i, j: (i, 0, 0)),
            pl.BlockSpec((1, s_pad, N_KV_A * cdim), lambda i, j: (i, 0, 0)),
            pl.BlockSpec((1, N_KV_A * v_rows, s_pad), lambda i, j: (i, 0, 0)),
        ],
        out_specs=pl.BlockSpec((1, WIDTH_A, lanes), lambda i, j: (i, 0, j)),
        out_shape=jax.ShapeDtypeStruct((bp, WIDTH_A, pair * t), BF16),
        scratch_shapes=[pltpu.VMEM((s_pad, lanes), F32), pltpu.VMEM((s_pad, lanes), F32),
                        pltpu.VMEM((N_IDX_HEADS, KEY_BLOCK, lanes), F32),
                        pltpu.VMEM((N_HEADS_A, s_pad, lanes), F32), pltpu.VMEM((N_HEADS_A, v_rows, lanes), F32)],
        compiler_params=_cparams("parallel", "parallel"),
        name="dsa",
    )(block_diag(qa_h), block_diag(qi_h), wi_op, ki_op, k_op, vt_op)
    return o_t.reshape(bp, WIDTH_A, pair, t).transpose(0, 2, 3, 1).reshape(b * t, WIDTH_A)


def _split3(a):
    hi = a.astype(BF16)
    r1 = a - hi.astype(F32)
    mid = r1.astype(BF16)
    lo = (r1 - mid.astype(F32)).astype(BF16)
    return hi, mid, lo


def _hgrn_body(hb_ref, s0_ref, lb_ref, nw_ref, ob_ref, sfin_ref, state_scr, *, tb, nb):
    t = pl.program_id(1)

    @pl.when(t == 0)
    def _():
        for bi, h in itertools.product(range(nb), range(N_HEADS_B)):
            state_scr[bi, h] = s0_ref[bi, h].T

    lb = lb_ref[...]
    r = lax.broadcasted_iota(I32, (CHUNK, CHUNK), 0)
    c = lax.broadcasted_iota(I32, (CHUNK, CHUNK), 1)
    causal = r >= c
    tril = jnp.where(causal, 1.0, 0.0).astype(BF16)

    for ci, bi in itertools.product(range(tb // CHUNK), range(nb)):
        rows = slice(ci * CHUNK, (ci + 1) * CHUNK)
        f = lb + (1.0 - lb) * jax.nn.sigmoid(hb_ref[bi, rows, WIDTH_B:2 * WIDTH_B])
        parts = _split3(jnp.log(f))
        bcum = _dot(tril, parts[0]) + _dot(tril, parts[1]) + _dot(tril, parts[2])
        for h in range(N_HEADS_B):
            lanes = slice(h * HEAD_DIM_B, (h + 1) * HEAD_DIM_B)
            q = hb_ref[bi, rows, lanes]
            k = 1.0 - f[:, lanes]
            v = hb_ref[bi, rows, 2 * WIDTH_B + h * HEAD_DIM_B:2 * WIDTH_B + (h + 1) * HEAD_DIM_B]
            gate = hb_ref[bi, rows, 3 * WIDTH_B + h * HEAD_DIM_B:3 * WIDTH_B + (h + 1) * HEAD_DIM_B]
            bh = bcum[:, lanes]
            b_last = bh[CHUNK - 1:CHUNK, :]
            vb = v.astype(BF16)
            state_t = state_scr[bi, h]
            o_inter = _dot_nt((q * jnp.exp(bh)).astype(BF16), state_t.astype(BF16))
            a_rows = []
            for i in range(CHUNK // HGRN_SUB):
                lo, hi = i * HGRN_SUB, (i + 1) * HGRN_SUB
                ref = bh[lo - 1:lo, :] if i else jnp.zeros((1, HEAD_DIM_B), F32)
                qs = (q[lo:hi] * jnp.exp(bh[lo:hi] - ref)).astype(BF16)
                ks = (k * jnp.exp(ref - bh)).astype(BF16)
                a_rows.append(_dot_nt(qs, ks))
            a = jnp.where(causal, jnp.concatenate(a_rows, axis=0), 0.0)
            o = o_inter + _dot(a.astype(BF16), vb)
            kdec = (k * jnp.exp(b_last - bh)).astype(BF16)
            state_scr[bi, h] = jnp.exp(b_last) * state_t + _dot_tn(vb, kdec)
            ms = jnp.mean(o * o, axis=-1, keepdims=True)
            y = o * lax.rsqrt(ms + RMS_EPS) * nw_ref[...] * (gate * jax.nn.sigmoid(gate))
            ob_ref[bi, rows, lanes] = y.astype(BF16)

    @pl.when(t == pl.num_programs(1) - 1)
    def _():
        for bi, h in itertools.product(range(nb), range(N_HEADS_B)):
            sfin_ref[bi, h] = state_scr[bi, h].T


def _hgrn(hb, s0, lb, nw):
    b, t, _ = hb.shape
    tb = min(256, t)
    nb = 2 if b % 2 == 0 else 1
    return pl.pallas_call(
        functools.partial(_hgrn_body, tb=tb, nb=nb),
        grid=(b // nb, t // tb),
        in_specs=[
            pl.BlockSpec((nb, tb, 4 * WIDTH_B), lambda i, j: (i, j, 0)),
            pl.BlockSpec((nb, N_HEADS_B, HEAD_DIM_B, HEAD_DIM_B), lambda i, j: (i, 0, 0, 0)),
            pl.BlockSpec((1, WIDTH_B), lambda i, j: (0, 0)),
            pl.BlockSpec((1, HEAD_DIM_B), lambda i, j: (0, 0)),
        ],
        out_specs=[
            pl.BlockSpec((nb, tb, WIDTH_B), lambda i, j: (i, j, 0)),
            pl.BlockSpec((nb, N_HEADS_B, HEAD_DIM_B, HEAD_DIM_B), lambda i, j: (i, 0, 0, 0)),
        ],
        out_shape=[jax.ShapeDtypeStruct((b, t, WIDTH_B), BF16),
                   jax.ShapeDtypeStruct((b, N_HEADS_B, HEAD_DIM_B, HEAD_DIM_B), F32)],
        scratch_shapes=[pltpu.VMEM((nb, N_HEADS_B, HEAD_DIM_B, HEAD_DIM_B), F32)],
        compiler_params=_cparams("parallel", "arbitrary"),
        name="hgrn",
    )(hb, s0, lb, nw)


def _outproj_body(oaa_ref, oab_ref, oba_ref, obb_ref, xa_ref, xb_ref, w_ref, lnw_ref, lnb_ref, wrh_ref, wrl_ref, br_ref,
                  x1_ref, lg_ref, *, steps_a):
    first = pl.program_id(0) < steps_a
    pick = lambda a_ref, b_ref: jnp.where(first, a_ref[...], b_ref[...])
    x = pick(xa_ref, xb_ref)
    y = _dot(pick(oaa_ref, oab_ref), w_ref[:WIDTH_A, :]) + _dot(pick(oba_ref, obb_ref), w_ref[WIDTH_A:, :])
    x1 = _layer_norm(DN_ALPHA * x + y, lnw_ref[...], lnb_ref[...])
    x1_ref[...] = x1
    hi = x1.astype(BF16)
    lo = (x1 - hi.astype(F32)).astype(BF16)
    lg_ref[...] = (_dot_nt(wrh_ref[...], hi) + _dot_nt(wrh_ref[...], lo) + _dot_nt(wrl_ref[...], hi)
                   + br_ref[...])


def _two_group_rows(tm, width, steps_a):
    return (pl.BlockSpec((tm, width), lambda i, *_: (jnp.minimum(i, steps_a - 1), 0)),
            pl.BlockSpec((tm, width), lambda i, *_: (jnp.maximum(i - steps_a, 0), 0)))


def _outproj(mix_a, mix_b, x_a, x_b, w_out, lnw, lnb, wr_hi, wr_lo, br):
    tm = MOE_TILE
    assert x_a.shape[0] % tm == 0 and x_b.shape[0] % tm == 0
    n = x_a.shape[0] + x_b.shape[0]
    steps_a = x_a.shape[0] // tm
    row = lambda width: pl.BlockSpec((tm, width), lambda i: (i, 0))
    full = lambda a: pl.BlockSpec(a.shape, lambda i: (0,) * a.ndim)
    return pl.pallas_call(
        functools.partial(_outproj_body, steps_a=steps_a),
        grid=(n // tm,),
        in_specs=[*_two_group_rows(tm, WIDTH_A, steps_a), *_two_group_rows(tm, WIDTH_B, steps_a),
                  *_two_group_rows(tm, D_MODEL, steps_a), full(w_out), full(lnw), full(lnb), full(wr_hi),
                  full(wr_lo), full(br)],
        out_specs=[row(D_MODEL), pl.BlockSpec((N_EXPERTS, tm), lambda i: (0, i))],
        out_shape=[jax.ShapeDtypeStruct((n, D_MODEL), F32), jax.ShapeDtypeStruct((N_EXPERTS, n), F32)],
        compiler_params=_cparams("parallel"),
        name="outproj",
    )(mix_a[0], mix_b[0], mix_a[1], mix_b[1], x_a, x_b, w_out, lnw, lnb, wr_hi, wr_lo, br)


def _route_body(lg_ref, lpos_ref, gate_ref, tc_ref, base_ref, cnt_ref, cnt_scr, *, tr):
    @pl.when(pl.program_id(0) == 0)
    def _():
        cnt_scr[...] = jnp.zeros_like(cnt_scr)

    l = lg_ref[...]
    rows = lax.broadcasted_iota(I32, (N_EXPERTS, tr), 0)
    vals, hots = [], []
    for j in range(TOP_K):
        m = jnp.max(l, axis=0, keepdims=True)
        idx = jnp.min(jnp.where(l == m, rows, N_EXPERTS), axis=0, keepdims=True)
        hot = rows == idx
        vals.append(m)
        hots.append(hot)
        l = jnp.where(hot, NEG_INF, l)
    es = [jnp.exp(v - vals[0]) for v in vals]
    inv = 1.0 / (es[0] + es[1] + es[2] + es[3])
    for j in range(TOP_K):
        gate_ref[j:j + 1, :] = es[j] * inv
    chosen = jnp.zeros((N_EXPERTS, tr), F32)
    for hot in hots:
        chosen = chosen + jnp.where(hot, 1.0, 0.0)
    r = lax.broadcasted_iota(I32, (tr, tr), 0)
    c = lax.broadcasted_iota(I32, (tr, tr), 1)
    before = jnp.where(r < c, 1.0, 0.0).astype(BF16)
    prior = _dot(chosen.astype(BF16), before)
    units = jnp.ceil(jnp.sum(chosen, axis=1, keepdims=True) * (1.0 / ROW_ALIGN))
    er = lax.broadcasted_iota(I32, (N_EXPERTS, N_EXPERTS), 0)
    ec = lax.broadcasted_iota(I32, (N_EXPERTS, N_EXPERTS), 1)
    earlier = jnp.where(ec < er, 1.0, 0.0).astype(BF16)
    run_len = jnp.broadcast_to(units, (N_EXPERTS, 128)) * ROW_ALIGN
    run_start = _dot(earlier, jnp.broadcast_to(units, (N_EXPERTS, 128)).astype(BF16)) * ROW_ALIGN
    where_in_tile = prior + run_start[:, 0:1]
    for j in range(TOP_K):
        lpos_ref[j:j + 1, :] = jnp.sum(jnp.where(hots[j], where_in_tile, 0.0), axis=0,
                                       keepdims=True).astype(I32)
    tc_ref[0] = run_len.astype(I32)
    base_ref[0] = cnt_scr[...].astype(I32)
    cnt_scr[...] = cnt_scr[...] + run_len
    cnt_ref[...] = cnt_scr[...].astype(I32)


def _route(logits_t):
    n = logits_t.shape[1]
    tr = MOE_TILE
    tiles = n // tr
    tok = lambda rows: pl.BlockSpec((rows, tr), lambda i: (0, i))
    per_tile = pl.BlockSpec((1, N_EXPERTS, 128), lambda i: (i, 0, 0))
    return pl.pallas_call(
        functools.partial(_route_body, tr=tr),
        grid=(tiles,),
        in_specs=[tok(N_EXPERTS)],
        out_specs=[tok(TOP_K), tok(TOP_K), per_tile, per_tile, pl.BlockSpec((N_EXPERTS, 128), lambda i: (0, 0))],
        out_shape=[jax.ShapeDtypeStruct((TOP_K, n), I32), jax.ShapeDtypeStruct((TOP_K, n), F32),
                   jax.ShapeDtypeStruct((tiles, N_EXPERTS, 128), I32),
                   jax.ShapeDtypeStruct((tiles, N_EXPERTS, 128), I32),
                   jax.ShapeDtypeStruct((N_EXPERTS, 128), I32)],
        scratch_shapes=[pltpu.VMEM((N_EXPERTS, 128), F32)],
        compiler_params=_cparams("arbitrary"),
        name="route",
    )(logits_t)


def _for_each_run_piece(tc_ref, g0_ref, t, fn):
    def per_expert(e, local):
        c = tc_ref[t, e]
        glob = g0_ref[t, e]
        lo = local
        for p in RUN_PIECES:
            take = (c & p) != 0

            @pl.when(take)
            def _():
                fn(pl.multiple_of(lo, ROW_ALIGN), pl.multiple_of(glob, ROW_ALIGN), p)

            step = jnp.where(take, p, 0)
            lo = lo + step
            glob = glob + step
        return local + c

    lax.fori_loop(0, N_EXPERTS, per_expert, 0)


def _dispatch_body(tc_ref, g0_ref, seg_ref, lpos_ref, x_ref, buf_ref, stage, zero_scr, zsem, sems, *, n_blocks):
    i = pl.program_id(0)
    slot = i % 2

    def run_copy(slot_, lo, glob, rows):
        return pltpu.make_async_copy(stage.at[slot_, pl.ds(lo, rows), :], buf_ref.at[pl.ds(glob, rows), :],
                                     sems.at[slot_])

    @pl.when(i == 0)
    def _():
        zero_scr[...] = jnp.zeros_like(zero_scr)
        n_used = seg_ref[1, N_EXPERTS - 1] // EXPERT_ROWS

        def block_copy(start):
            return pltpu.make_async_copy(
                zero_scr, buf_ref.at[pl.ds(pl.multiple_of(start, EXPERT_ROWS), EXPERT_ROWS), :], zsem)

        def clears(action):
            for e in range(N_EXPERTS):
                @pl.when(seg_ref[1, e] > seg_ref[0, e])
                def _():
                    action(block_copy(seg_ref[1, e] - EXPERT_ROWS))

            def tail(blk, _):
                action(block_copy(blk * EXPERT_ROWS))
                return 0
            lax.fori_loop(n_used, n_blocks, tail, 0)

        clears(lambda cp: cp.start())
        clears(lambda cp: cp.wait())

    r = lax.broadcasted_iota(I32, (MOE_SLOTS, MOE_TILE), 0)
    onehot = jnp.zeros((MOE_SLOTS, MOE_TILE), F32)
    for j in range(TOP_K):
        onehot = jnp.where(r == lpos_ref[j:j + 1, :], 1.0, onehot)
    stage[slot] = _dot(onehot.astype(BF16), x_ref[...].astype(BF16)).astype(BF16)

    @pl.when(i > 0)
    def _():
        _for_each_run_piece(tc_ref, g0_ref, i - 1,
                            lambda lo, glob, rows: run_copy(1 - slot, lo, glob, rows).wait())

    _for_each_run_piece(tc_ref, g0_ref, i, lambda lo, glob, rows: run_copy(slot, lo, glob, rows).start())

    @pl.when(i == pl.num_programs(0) - 1)
    def _():
        _for_each_run_piece(tc_ref, g0_ref, i, lambda lo, glob, rows: run_copy(slot, lo, glob, rows).wait())


def _dispatch(x1, lpos_t, tc, g0, seg, n_rows):
    n = x1.shape[0]
    return pl.pallas_call(
        functools.partial(_dispatch_body, n_blocks=n_rows // EXPERT_ROWS),
        grid_spec=pltpu.PrefetchScalarGridSpec(
            num_scalar_prefetch=3,
            grid=(n // MOE_TILE,),
            in_specs=[pl.BlockSpec((TOP_K, MOE_TILE), lambda i, *_: (0, i)),
                      pl.BlockSpec((MOE_TILE, D_MODEL), lambda i, *_: (i, 0))],
            out_specs=pl.BlockSpec(memory_space=pl.ANY),
            scratch_shapes=[pltpu.VMEM((2, MOE_SLOTS, D_MODEL), BF16),
                            pltpu.VMEM((EXPERT_ROWS, D_MODEL), BF16),
                            pltpu.SemaphoreType.DMA(()), pltpu.SemaphoreType.DMA((2,))],
        ),
        out_shape=jax.ShapeDtypeStruct((n_rows, D_MODEL), BF16),
        compiler_params=_cparams("arbitrary"),
        name="dispatch",
    )(tc, g0, seg, lpos_t, x1)


def _experts_body(be_ref, nu_ref, x_ref, wgu_ref, bgu_ref, wd_ref, bd_ref, o_ref, wgu_bf, wd_bf):
    i = pl.program_id(0)

    @pl.when(jnp.logical_or(i == 0, be_ref[i] != be_ref[jnp.maximum(i - 1, 0)]))
    def _():
        for r in range(0, D_MODEL, 256):
            wgu_bf[r:r + 256, :] = wgu_ref[0, r:r + 256, :].astype(BF16)
            wd_bf[r:r + 256, :] = wd_ref[0, r:r + 256, :].astype(BF16)

    @pl.when(i < nu_ref[0])
    def _():
        gu = _dot(x_ref[...], wgu_bf[...]) + bgu_ref[0]
        gate = jnp.minimum(gu[:, :D_EXPERT], SWIGLU_LIMIT)
        up = jnp.clip(gu[:, D_EXPERT:], -SWIGLU_LIMIT, SWIGLU_LIMIT)
        hdn = (up + 1.0) * gate * jax.nn.sigmoid(SWIGLU_ALPHA * gate)
        o_ref[...] = (_dot(hdn.astype(BF16), wd_bf[...]) + bd_ref[0]).astype(BF16)

    @pl.when(i >= nu_ref[0])
    def _():
        o_ref[...] = jnp.zeros_like(o_ref)


def _experts(buf, block_e, n_used, w_gu, b_gu, w_down, b_down):
    n_blocks = buf.shape[0] // EXPERT_ROWS
    rows = pl.BlockSpec((EXPERT_ROWS, D_MODEL), lambda i, be, nu: (jnp.minimum(i, nu[0] - 1), 0))
    per_e = lambda a: pl.BlockSpec((1,) + a.shape[1:], lambda i, be, nu: (be[i],) + (0,) * (a.ndim - 1))
    return pl.pallas_call(
        _experts_body,
        grid_spec=pltpu.PrefetchScalarGridSpec(
            num_scalar_prefetch=2,
            grid=(n_blocks,),
            in_specs=[rows, per_e(w_gu), per_e(b_gu), per_e(w_down), per_e(b_down)],
            out_specs=pl.BlockSpec((EXPERT_ROWS, D_MODEL), lambda i, be, nu: (i, 0)),
            scratch_shapes=[pltpu.VMEM((D_MODEL, 2 * D_EXPERT), BF16), pltpu.VMEM((D_EXPERT, D_MODEL), BF16)],
        ),
        out_shape=jax.ShapeDtypeStruct(buf.shape, BF16),
        compiler_params=_cparams("arbitrary"),
        name="experts",
    )(block_e, n_used, buf, w_gu, b_gu, w_down, b_down)


def _final_body(tc_ref, g0_ref, lpos_ref, gate_ref, x1_ref, pa_ref, pb_ref, lnw_ref, lnb_ref, wg_ref, bg_ref, wp_ref,
                eo_ref, ya_ref, yb_ref, stage, sems, *, steps_a):
    i = pl.program_id(0)
    slot = i % 2

    def run_copy(slot_, lo, glob, rows):
        return pltpu.make_async_copy(eo_ref.at[pl.ds(glob, rows), :], stage.at[slot_, pl.ds(lo, rows), :],
                                     sems.at[slot_])

    def fetch(tile, slot_):
        stage[slot_] = jnp.zeros(stage.shape[1:], stage.dtype)
        _for_each_run_piece(tc_ref, g0_ref, tile, lambda lo, glob, rows: run_copy(slot_, lo, glob, rows).start())

    @pl.when(i == 0)
    def _():
        fetch(0, 0)

    @pl.when(i + 1 < pl.num_programs(0))
    def _():
        fetch(i + 1, 1 - slot)

    _for_each_run_piece(tc_ref, g0_ref, i, lambda lo, glob, rows: run_copy(slot, lo, glob, rows).wait())

    r = lax.broadcasted_iota(I32, (MOE_TILE, MOE_SLOTS), 1)
    g = jnp.zeros((MOE_TILE, MOE_SLOTS), F32)
    for j in range(TOP_K):
        g = jnp.where(r == lpos_ref[:, j:j + 1], gate_ref[:, j:j + 1], g)
    g_hi = g.astype(BF16)
    g_lo = (g - g_hi.astype(F32)).astype(BF16)
    g2 = jnp.concatenate([g_hi, g_lo], axis=0)
    both = _dot(g2, stage[slot])
    moe = both[:MOE_TILE] + both[MOE_TILE:]
    x2 = _layer_norm(DN_ALPHA * x1_ref[...] + moe, lnw_ref[...], lnb_ref[...])
    ple_gate = jax.nn.sigmoid(_dot(x2.astype(BF16), wg_ref[...]) + bg_ref[...])
    p = jnp.where(i < steps_a, pa_ref[...], pb_ref[...])
    y = x2 + ple_gate * _dot(p.astype(BF16), wp_ref[...])

    @pl.when(i < steps_a)
    def _():
        ya_ref[...] = y

    @pl.when(i >= steps_a)
    def _():
        yb_ref[...] = y


def _final(tc, g0, lpos, gate, x1, p_a, p_b, lnw, lnb, w_gate, b_gate, w_ple, expert_out):
    n = x1.shape[0]
    steps_a = p_a.shape[0] // MOE_TILE
    row = lambda width: pl.BlockSpec((MOE_TILE, width), lambda i, *_: (i, 0))
    full = lambda a: pl.BlockSpec(a.shape, lambda i, *_: (0,) * a.ndim)
    return pl.pallas_call(
        functools.partial(_final_body, steps_a=steps_a),
        grid_spec=pltpu.PrefetchScalarGridSpec(
            num_scalar_prefetch=2,
            grid=(n // MOE_TILE,),
            in_specs=[row(TOP_K), row(TOP_K), row(D_MODEL), *_two_group_rows(MOE_TILE, D_PLE, steps_a), full(lnw),
                      full(lnb), full(w_gate), full(b_gate), full(w_ple), pl.BlockSpec(memory_space=pl.ANY)],
            out_specs=list(_two_group_rows(MOE_TILE, D_MODEL, steps_a)),
            scratch_shapes=[pltpu.VMEM((2, MOE_SLOTS, D_MODEL), BF16), pltpu.SemaphoreType.DMA((2,))],
        ),
        out_shape=[jax.ShapeDtypeStruct((p_a.shape[0], D_MODEL), F32),
                   jax.ShapeDtypeStruct((p_b.shape[0], D_MODEL), F32)],
        compiler_params=_cparams("arbitrary"),
        name="combine_final",
    )(tc, g0, lpos, gate, x1, p_a, p_b, lnw, lnb, w_gate, b_gate, w_ple, expert_out)


def _round_up(a, m):
    return -(-a // m) * m


def _token_mixers(x, hist_k, hist_v, hist_kidx, s0, pos0, wts):
    b, t, _ = x.shape
    n = b * t
    x2d = x.reshape(n, D_MODEL)

    qa, ka, va, ka_bf, va_bf, qi, ki, ki_bf, wi, hb = _inproj(x2d, wts["w_in"], wts["idx_lnw"], wts["idx_lnb"])

    k_new = ka_bf.reshape(b, t, N_KV_A * HEAD_DIM_A)
    v_new = va_bf.reshape(b, t, N_KV_A * HEAD_DIM_A)
    ki_new = ki_bf.reshape(b, t, IDX_DIM)
    q_ops = (qa.reshape(N_HEADS_A, b, t, HEAD_DIM_A), qi.reshape(N_IDX_HEADS, b, t, IDX_DIM),
             wi.reshape(b, t, N_IDX_HEADS))
    past = 0 if hist_k is None else hist_k.shape[1]
    s_real = past + t
    topk = min(TOPK_MAX, s_real // 4)
    if past:
        k_all = jnp.concatenate([hist_k.reshape(b, past, -1).astype(BF16), k_new], axis=1)
        v_all = jnp.concatenate([hist_v.reshape(b, past, -1).astype(BF16), v_new], axis=1)
        ki_all = jnp.concatenate([hist_kidx.astype(BF16), ki_new], axis=1)
    else:
        k_all, v_all, ki_all = k_new, v_new, ki_new
    pad = ((0, 0), (0, _round_up(s_real, KEY_BLOCK) - s_real), (0, 0))
    keys = tuple(jnp.pad(a, pad) for a in (ki_all, k_all, v_all))
    oa = _dsa(*q_ops, keys, s_real=s_real, pos0=pos0, topk=topk)

    ob, s_fin = _hgrn(hb.reshape(b, t, 4 * WIDTH_B), s0, wts["lb"], wts["hgrn_nw"])
    return (oa, ob.reshape(n, WIDTH_B), ka.reshape(b, t, N_KV_A, HEAD_DIM_A), va.reshape(b, t, N_KV_A, HEAD_DIM_A),
            ki.reshape(b, t, IDX_DIM), s_fin)


def _channel_mixer(mix_a, mix_b, x_a, x_b, p_a, p_b, wts):
    shape_a, shape_b = x_a.shape, x_b.shape
    x_a, x_b = x_a.reshape(-1, D_MODEL), x_b.reshape(-1, D_MODEL)
    n = x_a.shape[0] + x_b.shape[0]
    x1, logits_t = _outproj(mix_a, mix_b, x_a, x_b, wts["w_out"], wts["ln1_w"], wts["ln1_b"],
                            wts["wr_hi"], wts["wr_lo"], wts["b_router"])

    tiles = n // MOE_TILE
    lpos_t, gate_t, tile_runs, tile_base, counts = _route(logits_t)
    tc, counts = tile_runs[:, :, 0], counts[:, 0]
    padded = (counts + EXPERT_ROWS - 1) // EXPERT_ROWS * EXPERT_ROWS
    seg_end = jnp.cumsum(padded)
    seg_start = seg_end - padded
    g0 = (seg_start[None, :] + tile_base[:, :, 0]).astype(I32)
    n_blocks = -(-(n * TOP_K + N_EXPERTS * tiles * (ROW_ALIGN - 1) + N_EXPERTS * (EXPERT_ROWS - 1))
                 // EXPERT_ROWS)
    block_row0 = jnp.arange(n_blocks, dtype=I32) * EXPERT_ROWS
    block_e = jnp.minimum(jnp.sum((seg_end[None, :] <= block_row0[:, None]).astype(I32), axis=1), N_EXPERTS - 1)
    n_used = (seg_end[-1:] // EXPERT_ROWS).astype(I32)
    seg = jnp.stack([seg_start, seg_end]).astype(I32)
    buf = _dispatch(x1, lpos_t, tc, g0, seg, n_blocks * EXPERT_ROWS)
    expert_out = _experts(buf, block_e, n_used, wts["w_gu"], wts["b_gu"], wts["w_down"], wts["b_down"])
    y_a, y_b = _final(tc, g0, lpos_t.T, gate_t.T, x1, p_a.reshape(-1, D_PLE), p_b.reshape(-1, D_PLE),
                      wts["ln2_w"], wts["ln2_b"], wts["w_ple_gate"], wts["b_ple_gate"], wts["w_ple"], expert_out)
    return y_a.reshape(shape_a), y_b.reshape(shape_b)


def _prep_weights(w_in, w_out, idx_k_norm_w, idx_k_norm_b, lb, hgrn_norm_w, ln1_w, ln1_b, w_router, b_router,
                  w_gu, b_gu, w_down, b_down, ln2_w, ln2_b, w_ple, w_ple_gate, b_ple_gate):
    n_a = COL_KW + IDX_DIM + N_IDX_HEADS
    w_pad = jnp.concatenate(
        [w_in[:, :n_a], jnp.zeros((D_MODEL, COL_HB - n_a), w_in.dtype), w_in[:, n_a:]], axis=1).astype(BF16)
    wr_t = w_router.T
    wr_hi = wr_t.astype(BF16)
    row = lambda a: a.reshape(1, -1)
    return dict(
        w_in=w_pad, idx_lnw=row(idx_k_norm_w), idx_lnb=row(idx_k_norm_b), lb=row(lb), hgrn_nw=row(hgrn_norm_w),
        w_out=w_out.astype(BF16), ln1_w=row(ln1_w), ln1_b=row(ln1_b),
        wr_hi=wr_hi, wr_lo=(wr_t - wr_hi.astype(F32)).astype(BF16), b_router=b_router.reshape(N_EXPERTS, 1),
        w_gu=w_gu, b_gu=b_gu.reshape(N_EXPERTS, 1, 2 * D_EXPERT),
        w_down=w_down, b_down=b_down.reshape(N_EXPERTS, 1, D_MODEL),
        ln2_w=row(ln2_w), ln2_b=row(ln2_b), w_ple=w_ple.astype(BF16), w_ple_gate=w_ple_gate.astype(BF16),
        b_ple_gate=row(b_ple_gate))


def kernel(x_prompt, x_sample, cache_k, cache_v, cache_kidx, state_hgrn, p_prompt, p_sample, w_in, w_out,
           idx_k_norm_w, idx_k_norm_b, hgrn_lb_logits, hgrn_norm_w, ln1_w, ln1_b, w_router, b_router, w_gu, b_gu,
           w_down, b_down, ln2_w, ln2_b, w_ple, w_ple_gate, b_ple_gate):
    lb_all = jnp.cumsum(jax.nn.softmax(hgrn_lb_logits.astype(F32), axis=0), axis=0)
    xp, xs = x_prompt, x_sample
    outs = [[] for _ in range(8)]
    for i in range(DEPTH):
        wts = _prep_weights(w_in[i], w_out[i], idx_k_norm_w[i], idx_k_norm_b[i], lb_all[i], hgrn_norm_w[i],
                            ln1_w[i], ln1_b[i], w_router[i], b_router[i], w_gu[i], b_gu[i], w_down[i], b_down[i],
                            ln2_w[i], ln2_b[i], w_ple[i], w_ple_gate[i], b_ple_gate[i])
        s0p = jnp.zeros((xp.shape[0], N_HEADS_B, HEAD_DIM_B, HEAD_DIM_B), F32)
        mix_p = _token_mixers(xp, None, None, None, s0p, 0, wts)
        mix_s = _token_mixers(xs, cache_k[i], cache_v[i], cache_kidx[i], state_hgrn[i], cache_k.shape[2], wts)
        xp, xs = _channel_mixer(mix_p, mix_s, xp, xs, p_prompt[i], p_sample[i], wts)
        for lst, val in zip(outs, mix_p[2:] + mix_s[2:]):
            lst.append(val)
    return (xp, xs) + tuple(jnp.stack(l) for l in outs)
```

```python
import functools
import itertools

import jax
import jax.numpy as jnp
from jax import lax
from jax.experimental import pallas as pl
from jax.experimental.pallas import tpu as pltpu

F32 = jnp.float32
BF16 = jnp.bfloat16
I32 = jnp.int32

D_MODEL = 1024
CHUNK = 64
CHUNK_SHIFT = 6
WIDTH_A = 512
HEAD_DIM_A = 64
N_HEADS_A = 8
N_KV_A = 2
N_IDX_HEADS = 4
IDX_DIM = 64
IDX_SCALE = IDX_DIM ** -0.5 * N_IDX_HEADS ** -0.5
TOPK_MAX = 256
WIDTH_B = 512
HEAD_DIM_B = 128
N_HEADS_B = 4
N_EXPERTS = 32
TOP_K = 4
D_EXPERT = 1024
SWIGLU_LIMIT = 7.0
SWIGLU_ALPHA = 1.702
D_PLE = 256
LN_EPS = 1e-5
RMS_EPS = 1e-6
DEPTH = 1
DN_ALPHA = (2 * DEPTH) ** 0.25

COL_QA, COL_KA, COL_VA, COL_QI, COL_KW, COL_HB, COL_END = 0, 512, 640, 768, 1024, 1152, 3200

VMEM_LIMIT = 56 * 1024 * 1024
INPROJ_ROWS = 1024
KEY_BLOCK = 256
BLOCKS_PER_TRIP = 8
V_ROWS = 80
LOG2_E = 1.4426950408889634
HGRN_SUB = 16
EXPERT_ROWS = 512
MOE_TILE = 512
ROW_ALIGN = 16
RUN_PIECES = (512, 256, 128, 64, 32, 16)
MOE_SLOTS = -(-(TOP_K * MOE_TILE + N_EXPERTS * (ROW_ALIGN - 1)) // 256) * 256
NEG_INF = float("-inf")
INT_MIN = -(2 ** 31)


def _cparams(*sem):
    return pltpu.CompilerParams(dimension_semantics=sem, vmem_limit_bytes=VMEM_LIMIT)


def _dot(a, b):
    return jnp.dot(a, b, preferred_element_type=F32)


def _dot_nt(a, b):
    return lax.dot_general(a, b, (((1,), (1,)), ((), ())), preferred_element_type=F32)


def _dot_tn(a, b):
    return lax.dot_general(a, b, (((0,), (0,)), ((), ())), preferred_element_type=F32)


def _layer_norm(z, w, b):
    mu = jnp.mean(z, axis=-1, keepdims=True)
    d = z - mu
    var = jnp.mean(d * d, axis=-1, keepdims=True)
    return d * lax.rsqrt(var + LN_EPS) * w + b


def _inproj_body(x_ref, w_ref, lnw_ref, lnb_ref, qa_ref, ka_ref, va_ref, kab_ref, vab_ref, qi_ref, ki_ref, kib_ref,
                 wi_ref, hb_ref):
    xb = x_ref[...].astype(BF16)

    def mm(c0, c1):
        return _dot(xb, w_ref[:, c0:c1])

    qa = (mm(COL_QA, COL_KA) * (HEAD_DIM_A ** -0.5 * LOG2_E)).astype(BF16)
    for h in range(N_HEADS_A):
        qa_ref[h] = qa[:, h * HEAD_DIM_A:(h + 1) * HEAD_DIM_A]
    for src, out3_ref, bf_ref in ((mm(COL_KA, COL_VA), ka_ref, kab_ref), (mm(COL_VA, COL_QI), va_ref, vab_ref)):
        for g in range(N_KV_A):
            out3_ref[:, g, :] = src[:, g * HEAD_DIM_A:(g + 1) * HEAD_DIM_A]
        bf_ref[...] = src.astype(BF16)
    qi = mm(COL_QI, COL_KW).astype(BF16)
    for h in range(N_IDX_HEADS):
        qi_ref[h] = qi[:, h * IDX_DIM:(h + 1) * IDX_DIM]
    kw = mm(COL_KW, COL_HB)
    ki = _layer_norm(kw[:, :IDX_DIM], lnw_ref[...], lnb_ref[...])
    ki_ref[...] = ki
    kib_ref[...] = ki.astype(BF16)
    wi_ref[...] = kw[:, IDX_DIM:IDX_DIM + N_IDX_HEADS]
    hb_ref[...] = mm(COL_HB, COL_END)


def _inproj(x2d, w_pad, lnw, lnb):
    n = x2d.shape[0]
    tm = min(INPROJ_ROWS, n)
    row = lambda width: pl.BlockSpec((tm, width), lambda i: (i, 0))
    full = lambda a: pl.BlockSpec(a.shape, lambda i: (0,) * a.ndim)
    heads = lambda nh, d: pl.BlockSpec((nh, tm, d), lambda i: (0, i, 0))
    kv = N_KV_A * HEAD_DIM_A
    kv3 = pl.BlockSpec((tm, N_KV_A, HEAD_DIM_A), lambda i: (i, 0, 0))
    sds = jax.ShapeDtypeStruct
    return pl.pallas_call(
        _inproj_body,
        grid=(n // tm,),
        in_specs=[row(D_MODEL), full(w_pad), full(lnw), full(lnb)],
        out_specs=[heads(N_HEADS_A, HEAD_DIM_A), kv3, kv3, row(kv), row(kv), heads(N_IDX_HEADS, IDX_DIM),
                   row(IDX_DIM), row(IDX_DIM), row(N_IDX_HEADS), row(4 * WIDTH_B)],
        out_shape=[sds((N_HEADS_A, n, HEAD_DIM_A), BF16), sds((n, N_KV_A, HEAD_DIM_A), F32),
                   sds((n, N_KV_A, HEAD_DIM_A), F32), sds((n, kv), BF16), sds((n, kv), BF16),
                   sds((N_IDX_HEADS, n, IDX_DIM), BF16), sds((n, IDX_DIM), F32), sds((n, IDX_DIM), BF16),
                   sds((n, N_IDX_HEADS), F32), sds((n, 4 * WIDTH_B), F32)],
        compiler_params=_cparams("parallel"),
        name="inproj",
    )(x2d, w_pad, lnw, lnb)


def _order_bits_to_f32(u):
    key = u ^ INT_MIN
    bits = key ^ ((key >> 31) & 0x7FFFFFFF)
    f = lax.bitcast_convert_type(bits, F32)
    return jnp.where(u >= 0, jnp.where(u <= 0x007FFFFF, NEG_INF, f), f)


def _dsa_body(qa_ref, qi_ref, wi_ref, kidx_ref, k_ref, vt_ref, o_ref, sc_scr, bias_scr, term_scr, s_scr, acc_scr,
              *, s_pad, s_real, q_rows, pair, pos0, topk):
    kb_rows = KEY_BLOCK
    heads_per_kv = N_HEADS_A // N_KV_A
    qb = pair * q_rows
    cdim = pair * HEAD_DIM_A
    v_rows = pair * V_ROWS
    j = pl.program_id(1)
    q_lo = pos0 + j * q_rows
    lane = lax.broadcasted_iota(I32, (1, qb), 1)
    q_chunk = (q_lo + (lane & (q_rows - 1))) >> CHUNK_SHIFT
    k_lim = (((q_lo + q_rows - 1) >> CHUNK_SHIFT) + 1) * CHUNK
    nkb = jnp.minimum(s_pad // kb_rows, (k_lim + kb_rows - 1) // kb_rows)

    def rows_of(kb):
        return pl.ds(pl.multiple_of(kb * kb_rows, kb_rows), kb_rows)

    def fold8(a):
        return a.reshape(kb_rows // 8, 8, qb)

    def over_blocks(body, init):
        def group(first, width, carry):
            for u in range(width):
                carry = body(first + u, carry)
            return carry
        carry = lax.fori_loop(0, nkb // BLOCKS_PER_TRIP,
                              lambda i, c: group(i * BLOCKS_PER_TRIP, BLOCKS_PER_TRIP, c), init)
        done = (nkb // BLOCKS_PER_TRIP) * BLOCKS_PER_TRIP
        width = BLOCKS_PER_TRIP // 2
        while width:
            carry = lax.cond((nkb & width) != 0, functools.partial(group, done, width), lambda c: c, carry)
            done = done + (nkb & width)
            width //= 2
        return carry

    def score_blk(kb, _):
        rows = rows_of(kb)
        kidx = kidx_ref[0, rows, :]
        for h in range(N_IDX_HEADS):
            term_scr[h] = jnp.maximum(_dot_nt(kidx, qi_ref[h, 0]), 0.0) * wi_ref[0, h:h + 1, :]
        total = (term_scr[0] + term_scr[1]) + (term_scr[2] + term_scr[3])
        spos = kb * kb_rows + lax.broadcasted_iota(I32, (kb_rows, qb), 0)
        sc = jnp.where((spos >> CHUNK_SHIFT) <= q_chunk, total * IDX_SCALE + 0.0, NEG_INF)
        if s_real < s_pad:
            sc = jnp.where(spos < s_real, sc, NEG_INF)
        sc_scr[rows, :] = sc
        return 0

    over_blocks(score_blk, 0)

    def count(thr, strict):
        def body(kb, acc):
            blk = sc_scr[rows_of(kb), :]
            hit = (blk > thr) if strict else (blk >= thr)
            return acc + jnp.sum(fold8(jnp.where(hit, 1, 0).astype(I32)), axis=0)
        acc = over_blocks(body, jnp.zeros((8, qb), I32))
        return jnp.sum(acc, axis=0, keepdims=True)

    def bit_step(i, carry):
        prefix, cnt = carry
        cand = prefix | lax.shift_left(jnp.int32(1), 31 - i)
        c = count(_order_bits_to_f32(cand), False)
        take = c >= topk
        return jnp.where(take, cand, prefix), jnp.where(take, c, cnt)

    prefix, cnt_ge = lax.fori_loop(0, 32, bit_step,
                                   (jnp.zeros((1, qb), I32), jnp.full((1, qb), nkb * kb_rows, I32)))
    tau = _order_bits_to_f32(prefix)
    finite_tau = tau > NEG_INF
    tau_floor = jnp.maximum(tau, jnp.finfo(F32).min)
    tie_lanes = jnp.where(finite_tau, jnp.where(cnt_ge > topk, 1, 0), 0)
    has_ties = jnp.max(tie_lanes) > 0

    @pl.when(jnp.logical_not(has_ties))
    def _():
        def body(kb, _):
            blk = sc_scr[rows_of(kb), :]
            bias_scr[rows_of(kb), :] = jnp.where(blk >= tau_floor, 0.0, NEG_INF)
            return 0
        over_blocks(body, 0)

    @pl.when(has_ties)
    def _():
        need = jnp.where(finite_tau, (topk - count(tau, True)).astype(F32), 0.0)
        r = lax.broadcasted_iota(I32, (kb_rows, kb_rows), 0)
        c = lax.broadcasted_iota(I32, (kb_rows, kb_rows), 1)
        tril = jnp.where(r >= c, 1.0, 0.0).astype(BF16)

        def body(kb, seen):
            blk = sc_scr[rows_of(kb), :]
            eq = blk == tau
            rank = _dot(tril, jnp.where(eq, 1.0, 0.0).astype(BF16)) + seen
            tie_bias = jnp.where(eq, jnp.where(rank <= need, 0.0, NEG_INF), NEG_INF)
            bias_scr[rows_of(kb), :] = jnp.where(blk > tau, 0.0, tie_bias)
            return rank[kb_rows - 1:kb_rows, :]
        lax.fori_loop(0, nkb, body, jnp.zeros((1, qb), F32))

    def pass1(kb, m8):
        rows = rows_of(kb)
        bias = bias_scr[rows, :]
        out = []
        for h in range(N_HEADS_A):
            g = h // heads_per_kv
            s = _dot_nt(k_ref[0, rows, g * cdim:(g + 1) * cdim], qa_ref[h, 0]) + bias
            s_scr[h, rows, :] = s
            out.append(jnp.maximum(m8[h], jnp.max(fold8(s), axis=0)))
        return tuple(out)

    m8 = over_blocks(pass1, tuple(jnp.full((8, qb), NEG_INF, F32) for _ in range(N_HEADS_A)))
    m = [jnp.max(x, axis=0, keepdims=True) for x in m8]

    acc_scr[...] = jnp.zeros_like(acc_scr)

    def pass2(kb, _):
        rows = rows_of(kb)
        for h in range(N_HEADS_A):
            g = h // heads_per_kv
            p = jnp.exp2(s_scr[h, rows, :] - m[h]).astype(BF16)
            acc_scr[h] += _dot(vt_ref[0, g * v_rows:(g + 1) * v_rows, rows], p)
        return 0

    over_blocks(pass2, 0)
    for h in range(N_HEADS_A):
        a = acc_scr[h]
        num, den = a[:HEAD_DIM_A], a[HEAD_DIM_A:HEAD_DIM_A + 1]
        for mb in range(1, pair):
            mine = lane >= mb * q_rows
            r0 = mb * V_ROWS
            num = jnp.where(mine, a[r0:r0 + HEAD_DIM_A], num)
            den = jnp.where(mine, a[r0 + HEAD_DIM_A:r0 + HEAD_DIM_A + 1], den)
        o_ref[0, h * HEAD_DIM_A:(h + 1) * HEAD_DIM_A, :] = (num * (1.0 / den)).astype(BF16)


def _dsa_pairing(b, t):
    return 2 if (2 * t <= 128 and b % 2 == 0) else 1


def _dsa(qa_h, qi_h, wi, keys, *, s_real, pos0, topk):
    _, b, t, _ = qa_h.shape
    s_pad = keys[0].shape[1]
    q_rows = min(256, t)
    pair = _dsa_pairing(b, t)
    bp = b // pair
    lanes, cdim, v_rows = pair * q_rows, pair * HEAD_DIM_A, pair * V_ROWS

    def block_diag(q):
        if pair == 1:
            return q
        nh, d = q.shape[0], q.shape[3]
        eye = jnp.eye(pair, dtype=q.dtype)
        q6 = q.reshape(nh, bp, pair, t, 1, d) * eye.reshape(1, 1, pair, 1, pair, 1)
        return q6.reshape(nh, bp, pair * t, pair * d)

    wi_op = wi.reshape(bp, pair, t, N_IDX_HEADS).transpose(0, 3, 1, 2).reshape(bp, N_IDX_HEADS, pair * t)
    ki_all, k_all, v_all = keys
    ki_op = ki_all.reshape(bp, pair, s_pad, IDX_DIM).transpose(0, 2, 1, 3).reshape(bp, s_pad, cdim)
    k_op = (k_all.reshape(bp, pair, s_pad, N_KV_A, HEAD_DIM_A).transpose(0, 2, 3, 1, 4)
            .reshape(bp, s_pad, N_KV_A * cdim))
    vt = v_all.reshape(b, s_pad, N_KV_A, HEAD_DIM_A).transpose(0, 2, 3, 1)
    vt = jnp.concatenate([vt, jnp.ones((b, N_KV_A, V_ROWS - HEAD_DIM_A, s_pad), vt.dtype)], axis=2)
    vt_op = (vt.reshape(bp, pair, N_KV_A, V_ROWS, s_pad).transpose(0, 2, 1, 3, 4)
             .reshape(bp, N_KV_A * v_rows, s_pad))

    body = functools.partial(_dsa_body, s_pad=s_pad, s_real=s_real, q_rows=q_rows, pair=pair, pos0=pos0,
                             topk=topk)
    o_t = pl.pallas_call(
        body,
        grid=(bp, t // q_rows),
        in_specs=[
            pl.BlockSpec((N_HEADS_A, 1, lanes, cdim), lambda i, j: (0, i, j, 0)),
            pl.BlockSpec((N_IDX_HEADS, 1, lanes, cdim), lambda i, j: (0, i, j, 0)),
            pl.BlockSpec((1, N_IDX_HEADS, lanes), lambda i, j: (i, 0, j)),
            pl.BlockSpec((1, s_pad, cdim), lambda i, j: (i, 0, 0)),
            pl.BlockSpec((1, s_pad, N_KV_A * cdim), lambda i, j: (i, 0, 0)),
            pl.BlockSpec((1, N_KV_A * v_rows, s_pad), lambda i, j: (i, 0, 0)),
        ],
        out_specs=pl.BlockSpec((1, WIDTH_A, lanes), lambda i, j: (i, 0, j)),
        out_shape=jax.ShapeDtypeStruct((bp, WIDTH_A, pair * t), BF16),
        scratch_shapes=[pltpu.VMEM((s_pad, lanes), F32), pltpu.VMEM((s_pad, lanes), F32),
                        pltpu.VMEM((N_IDX_HEADS, KEY_BLOCK, lanes), F32),
                        pltpu.VMEM((N_HEADS_A, s_pad, lanes), F32), pltpu.VMEM((N_HEADS_A, v_rows, lanes), F32)],
        compiler_params=_cparams("parallel", "parallel"),
        name="dsa",
    )(block_diag(qa_h), block_diag(qi_h), wi_op, ki_op, k_op, vt_op)
    return o_t.reshape(bp, WIDTH_A, pair, t).transpose(0, 2, 3, 1).reshape(b * t, WIDTH_A)


def _split3(a):
    hi = a.astype(BF16)
    r1 = a - hi.astype(F32)
    mid = r1.astype(BF16)
    lo = (r1 - mid.astype(F32)).astype(BF16)
    return hi, mid, lo


def _hgrn_body(hb_ref, s0_ref, lb_ref, nw_ref, ob_ref, sfin_ref, state_scr, *, tb, nb):
    t = pl.program_id(1)

    @pl.when(t == 0)
    def _():
        for bi, h in itertools.product(range(nb), range(N_HEADS_B)):
            state_scr[bi, h] = s0_ref[bi, h].T

    lb = lb_ref[...]
    r = lax.broadcasted_iota(I32, (CHUNK, CHUNK), 0)
    c = lax.broadcasted_iota(I32, (CHUNK, CHUNK), 1)
    causal = r >= c
    tril = jnp.where(causal, 1.0, 0.0).astype(BF16)

    for ci, bi in itertools.product(range(tb // CHUNK), range(nb)):
        rows = slice(ci * CHUNK, (ci + 1) * CHUNK)
        f = lb + (1.0 - lb) * jax.nn.sigmoid(hb_ref[bi, rows, WIDTH_B:2 * WIDTH_B])
        parts = _split3(jnp.log(f))
        bcum = _dot(tril, parts[0]) + _dot(tril, parts[1]) + _dot(tril, parts[2])
        for h in range(N_HEADS_B):
            lanes = slice(h * HEAD_DIM_B, (h + 1) * HEAD_DIM_B)
            q = hb_ref[bi, rows, lanes]
            k = 1.0 - f[:, lanes]
            v = hb_ref[bi, rows, 2 * WIDTH_B + h * HEAD_DIM_B:2 * WIDTH_B + (h + 1) * HEAD_DIM_B]
            gate = hb_ref[bi, rows, 3 * WIDTH_B + h * HEAD_DIM_B:3 * WIDTH_B + (h + 1) * HEAD_DIM_B]
            bh = bcum[:, lanes]
            b_last = bh[CHUNK - 1:CHUNK, :]
            vb = v.astype(BF16)
            state_t = state_scr[bi, h]
            o_inter = _dot_nt((q * jnp.exp(bh)).astype(BF16), state_t.astype(BF16))
            a_rows = []
            for i in range(CHUNK // HGRN_SUB):
                lo, hi = i * HGRN_SUB, (i + 1) * HGRN_SUB
                ref = bh[lo - 1:lo, :] if i else jnp.zeros((1, HEAD_DIM_B), F32)
                qs = (q[lo:hi] * jnp.exp(bh[lo:hi] - ref)).astype(BF16)
                ks = (k * jnp.exp(ref - bh)).astype(BF16)
                a_rows.append(_dot_nt(qs, ks))
            a = jnp.where(causal, jnp.concatenate(a_rows, axis=0), 0.0)
            o = o_inter + _dot(a.astype(BF16), vb)
            kdec = (k * jnp.exp(b_last - bh)).astype(BF16)
            state_scr[bi, h] = jnp.exp(b_last) * state_t + _dot_tn(vb, kdec)
            ms = jnp.mean(o * o, axis=-1, keepdims=True)
            y = o * lax.rsqrt(ms + RMS_EPS) * nw_ref[...] * (gate * jax.nn.sigmoid(gate))
            ob_ref[bi, rows, lanes] = y.astype(BF16)

    @pl.when(t == pl.num_programs(1) - 1)
    def _():
        for bi, h in itertools.product(range(nb), range(N_HEADS_B)):
            sfin_ref[bi, h] = state_scr[bi, h].T


def _hgrn(hb, s0, lb, nw):
    b, t, _ = hb.shape
    tb = min(256, t)
    nb = 2 if b % 2 == 0 else 1
    return pl.pallas_call(
        functools.partial(_hgrn_body, tb=tb, nb=nb),
        grid=(b // nb, t // tb),
        in_specs=[
            pl.BlockSpec((nb, tb, 4 * WIDTH_B), lambda i, j: (i, j, 0)),
            pl.BlockSpec((nb, N_HEADS_B, HEAD_DIM_B, HEAD_DIM_B), lambda i, j: (i, 0, 0, 0)),
            pl.BlockSpec((1, WIDTH_B), lambda i, j: (0, 0)),
            pl.BlockSpec((1, HEAD_DIM_B), lambda i, j: (0, 0)),
        ],
        out_specs=[
            pl.BlockSpec((nb, tb, WIDTH_B), lambda i, j: (i, j, 0)),
            pl.BlockSpec((nb, N_HEADS_B, HEAD_DIM_B, HEAD_DIM_B), lambda i, j: (i, 0, 0, 0)),
        ],
        out_shape=[jax.ShapeDtypeStruct((b, t, WIDTH_B), BF16),
                   jax.ShapeDtypeStruct((b, N_HEADS_B, HEAD_DIM_B, HEAD_DIM_B), F32)],
        scratch_shapes=[pltpu.VMEM((nb, N_HEADS_B, HEAD_DIM_B, HEAD_DIM_B), F32)],
        compiler_params=_cparams("parallel", "arbitrary"),
        name="hgrn",
    )(hb, s0, lb, nw)


def _outproj_body(oaa_ref, oab_ref, oba_ref, obb_ref, xa_ref, xb_ref, w_ref, lnw_ref, lnb_ref, wrh_ref, wrl_ref, br_ref,
                  x1_ref, lg_ref, *, steps_a):
    first = pl.program_id(0) < steps_a
    pick = lambda a_ref, b_ref: jnp.where(first, a_ref[...], b_ref[...])
    x = pick(xa_ref, xb_ref)
    y = _dot(pick(oaa_ref, oab_ref), w_ref[:WIDTH_A, :]) + _dot(pick(oba_ref, obb_ref), w_ref[WIDTH_A:, :])
    x1 = _layer_norm(DN_ALPHA * x + y, lnw_ref[...], lnb_ref[...])
    x1_ref[...] = x1
    hi = x1.astype(BF16)
    lo = (x1 - hi.astype(F32)).astype(BF16)
    lg_ref[...] = (_dot_nt(wrh_ref[...], hi) + _dot_nt(wrh_ref[...], lo) + _dot_nt(wrl_ref[...], hi)
                   + br_ref[...])


def _two_group_rows(tm, width, steps_a):
    return (pl.BlockSpec((tm, width), lambda i, *_: (jnp.minimum(i, steps_a - 1), 0)),
            pl.BlockSpec((tm, width), lambda i, *_: (jnp.maximum(i - steps_a, 0), 0)))


def _outproj(mix_a, mix_b, x_a, x_b, w_out, lnw, lnb, wr_hi, wr_lo, br):
    tm = MOE_TILE
    assert x_a.shape[0] % tm == 0 and x_b.shape[0] % tm == 0
    n = x_a.shape[0] + x_b.shape[0]
    steps_a = x_a.shape[0] // tm
    row = lambda width: pl.BlockSpec((tm, width), lambda i: (i, 0))
    full = lambda a: pl.BlockSpec(a.shape, lambda i: (0,) * a.ndim)
    return pl.pallas_call(
        functools.partial(_outproj_body, steps_a=steps_a),
        grid=(n // tm,),
        in_specs=[*_two_group_rows(tm, WIDTH_A, steps_a), *_two_group_rows(tm, WIDTH_B, steps_a),
                  *_two_group_rows(tm, D_MODEL, steps_a), full(w_out), full(lnw), full(lnb), full(wr_hi),
                  full(wr_lo), full(br)],
        out_specs=[row(D_MODEL), pl.BlockSpec((N_EXPERTS, tm), lambda i: (0, i))],
        out_shape=[jax.ShapeDtypeStruct((n, D_MODEL), F32), jax.ShapeDtypeStruct((N_EXPERTS, n), F32)],
        compiler_params=_cparams("parallel"),
        name="outproj",
    )(mix_a[0], mix_b[0], mix_a[1], mix_b[1], x_a, x_b, w_out, lnw, lnb, wr_hi, wr_lo, br)


def _route_body(lg_ref, lpos_ref, gate_ref, tc_ref, base_ref, cnt_ref, cnt_scr, *, tr):
    @pl.when(pl.program_id(0) == 0)
    def _():
        cnt_scr[...] = jnp.zeros_like(cnt_scr)

    l = lg_ref[...]
    rows = lax.broadcasted_iota(I32, (N_EXPERTS, tr), 0)
    vals, hots = [], []
    for j in range(TOP_K):
        m = jnp.max(l, axis=0, keepdims=True)
        idx = jnp.min(jnp.where(l == m, rows, N_EXPERTS), axis=0, keepdims=True)
        hot = rows == idx
        vals.append(m)
        hots.append(hot)
        l = jnp.where(hot, NEG_INF, l)
    es = [jnp.exp(v - vals[0]) for v in vals]
    inv = 1.0 / (es[0] + es[1] + es[2] + es[3])
    for j in range(TOP_K):
        gate_ref[j:j + 1, :] = es[j] * inv
    chosen = jnp.zeros((N_EXPERTS, tr), F32)
    for hot in hots:
        chosen = chosen + jnp.where(hot, 1.0, 0.0)
    r = lax.broadcasted_iota(I32, (tr, tr), 0)
    c = lax.broadcasted_iota(I32, (tr, tr), 1)
    before = jnp.where(r < c, 1.0, 0.0).astype(BF16)
    prior = _dot(chosen.astype(BF16), before)
    units = jnp.ceil(jnp.sum(chosen, axis=1, keepdims=True) * (1.0 / ROW_ALIGN))
    er = lax.broadcasted_iota(I32, (N_EXPERTS, N_EXPERTS), 0)
    ec = lax.broadcasted_iota(I32, (N_EXPERTS, N_EXPERTS), 1)
    earlier = jnp.where(ec < er, 1.0, 0.0).astype(BF16)
    run_len = jnp.broadcast_to(units, (N_EXPERTS, 128)) * ROW_ALIGN
    run_start = _dot(earlier, jnp.broadcast_to(units, (N_EXPERTS, 128)).astype(BF16)) * ROW_ALIGN
    where_in_tile = prior + run_start[:, 0:1]
    for j in range(TOP_K):
        lpos_ref[j:j + 1, :] = jnp.sum(jnp.where(hots[j], where_in_tile, 0.0), axis=0,
                                       keepdims=True).astype(I32)
    tc_ref[0] = run_len.astype(I32)
    base_ref[0] = cnt_scr[...].astype(I32)
    cnt_scr[...] = cnt_scr[...] + run_len
    cnt_ref[...] = cnt_scr[...].astype(I32)


def _route(logits_t):
    n = logits_t.shape[1]
    tr = MOE_TILE
    tiles = n // tr
    tok = lambda rows: pl.BlockSpec((rows, tr), lambda i: (0, i))
    per_tile = pl.BlockSpec((1, N_EXPERTS, 128), lambda i: (i, 0, 0))
    return pl.pallas_call(
        functools.partial(_route_body, tr=tr),
        grid=(tiles,),
        in_specs=[tok(N_EXPERTS)],
        out_specs=[tok(TOP_K), tok(TOP_K), per_tile, per_tile, pl.BlockSpec((N_EXPERTS, 128), lambda i: (0, 0))],
        out_shape=[jax.ShapeDtypeStruct((TOP_K, n), I32), jax.ShapeDtypeStruct((TOP_K, n), F32),
                   jax.ShapeDtypeStruct((tiles, N_EXPERTS, 128), I32),
                   jax.ShapeDtypeStruct((tiles, N_EXPERTS, 128), I32),
                   jax.ShapeDtypeStruct((N_EXPERTS, 128), I32)],
        scratch_shapes=[pltpu.VMEM((N_EXPERTS, 128), F32)],
        compiler_params=_cparams("arbitrary"),
        name="route",
    )(logits_t)


def _for_each_run_piece(tc_ref, g0_ref, t, fn):
    def per_expert(e, local):
        c = tc_ref[t, e]
        glob = g0_ref[t, e]
        lo = local
        for p in RUN_PIECES:
            take = (c & p) != 0

            @pl.when(take)
            def _():
                fn(pl.multiple_of(lo, ROW_ALIGN), pl.multiple_of(glob, ROW_ALIGN), p)

            step = jnp.where(take, p, 0)
            lo = lo + step
            glob = glob + step
        return local + c

    lax.fori_loop(0, N_EXPERTS, per_expert, 0)


def _dispatch_body(tc_ref, g0_ref, seg_ref, lpos_ref, x_ref, buf_ref, stage, zero_scr, zsem, sems, *, n_blocks):
    i = pl.program_id(0)
    slot = i % 2

    def run_copy(slot_, lo, glob, rows):
        return pltpu.make_async_copy(stage.at[slot_, pl.ds(lo, rows), :], buf_ref.at[pl.ds(glob, rows), :],
                                     sems.at[slot_])

    @pl.when(i == 0)
    def _():
        zero_scr[...] = jnp.zeros_like(zero_scr)
        n_used = seg_ref[1, N_EXPERTS - 1] // EXPERT_ROWS

        def block_copy(start):
            return pltpu.make_async_copy(
                zero_scr, buf_ref.at[pl.ds(pl.multiple_of(start, EXPERT_ROWS), EXPERT_ROWS), :], zsem)

        def clears(action):
            for e in range(N_EXPERTS):
                @pl.when(seg_ref[1, e] > seg_ref[0, e])
                def _():
                    action(block_copy(seg_ref[1, e] - EXPERT_ROWS))

            def tail(blk, _):
                action(block_copy(blk * EXPERT_ROWS))
                return 0
            lax.fori_loop(n_used, n_blocks, tail, 0)

        clears(lambda cp: cp.start())
        clears(lambda cp: cp.wait())

    r = lax.broadcasted_iota(I32, (MOE_SLOTS, MOE_TILE), 0)
    onehot = jnp.zeros((MOE_SLOTS, MOE_TILE), F32)
    for j in range(TOP_K):
        onehot = jnp.where(r == lpos_ref[j:j + 1, :], 1.0, onehot)
    stage[slot] = _dot(onehot.astype(BF16), x_ref[...].astype(BF16)).astype(BF16)

    @pl.when(i > 0)
    def _():
        _for_each_run_piece(tc_ref, g0_ref, i - 1,
                            lambda lo, glob, rows: run_copy(1 - slot, lo, glob, rows).wait())

    _for_each_run_piece(tc_ref, g0_ref, i, lambda lo, glob, rows: run_copy(slot, lo, glob, rows).start())

    @pl.when(i == pl.num_programs(0) - 1)
    def _():
        _for_each_run_piece(tc_ref, g0_ref, i, lambda lo, glob, rows: run_copy(slot, lo, glob, rows).wait())


def _dispatch(x1, lpos_t, tc, g0, seg, n_rows):
    n = x1.shape[0]
    return pl.pallas_call(
        functools.partial(_dispatch_body, n_blocks=n_rows // EXPERT_ROWS),
        grid_spec=pltpu.PrefetchScalarGridSpec(
            num_scalar_prefetch=3,
            grid=(n // MOE_TILE,),
            in_specs=[pl.BlockSpec((TOP_K, MOE_TILE), lambda i, *_: (0, i)),
                      pl.BlockSpec((MOE_TILE, D_MODEL), lambda i, *_: (i, 0))],
            out_specs=pl.BlockSpec(memory_space=pl.ANY),
            scratch_shapes=[pltpu.VMEM((2, MOE_SLOTS, D_MODEL), BF16),
                            pltpu.VMEM((EXPERT_ROWS, D_MODEL), BF16),
                            pltpu.SemaphoreType.DMA(()), pltpu.SemaphoreType.DMA((2,))],
        ),
        out_shape=jax.ShapeDtypeStruct((n_rows, D_MODEL), BF16),
        compiler_params=_cparams("arbitrary"),
        name="dispatch",
    )(tc, g0, seg, lpos_t, x1)


def _experts_body(be_ref, nu_ref, x_ref, wgu_ref, bgu_ref, wd_ref, bd_ref, o_ref, wgu_bf, wd_bf):
    i = pl.program_id(0)

    @pl.when(jnp.logical_or(i == 0, be_ref[i] != be_ref[jnp.maximum(i - 1, 0)]))
    def _():
        for r in range(0, D_MODEL, 256):
            wgu_bf[r:r + 256, :] = wgu_ref[0, r:r + 256, :].astype(BF16)
            wd_bf[r:r + 256, :] = wd_ref[0, r:r + 256, :].astype(BF16)

    @pl.when(i < nu_ref[0])
    def _():
        gu = _dot(x_ref[...], wgu_bf[...]) + bgu_ref[0]
        gate = jnp.minimum(gu[:, :D_EXPERT], SWIGLU_LIMIT)
        up = jnp.clip(gu[:, D_EXPERT:], -SWIGLU_LIMIT, SWIGLU_LIMIT)
        hdn = (up + 1.0) * gate * jax.nn.sigmoid(SWIGLU_ALPHA * gate)
        o_ref[...] = (_dot(hdn.astype(BF16), wd_bf[...]) + bd_ref[0]).astype(BF16)

    @pl.when(i >= nu_ref[0])
    def _():
        o_ref[...] = jnp.zeros_like(o_ref)


def _experts(buf, block_e, n_used, w_gu, b_gu, w_down, b_down):
    n_blocks = buf.shape[0] // EXPERT_ROWS
    rows = pl.BlockSpec((EXPERT_ROWS, D_MODEL), lambda i, be, nu: (jnp.minimum(i, nu[0] - 1), 0))
    per_e = lambda a: pl.BlockSpec((1,) + a.shape[1:], lambda i, be, nu: (be[i],) + (0,) * (a.ndim - 1))
    return pl.pallas_call(
        _experts_body,
        grid_spec=pltpu.PrefetchScalarGridSpec(
            num_scalar_prefetch=2,
            grid=(n_blocks,),
            in_specs=[rows, per_e(w_gu), per_e(b_gu), per_e(w_down), per_e(b_down)],
            out_specs=pl.BlockSpec((EXPERT_ROWS, D_MODEL), lambda i, be, nu: (i, 0)),
            scratch_shapes=[pltpu.VMEM((D_MODEL, 2 * D_EXPERT), BF16), pltpu.VMEM((D_EXPERT, D_MODEL), BF16)],
        ),
        out_shape=jax.ShapeDtypeStruct(buf.shape, BF16),
        compiler_params=_cparams("arbitrary"),
        name="experts",
    )(block_e, n_used, buf, w_gu, b_gu, w_down, b_down)


def _final_body(tc_ref, g0_ref, lpos_ref, gate_ref, x1_ref, pa_ref, pb_ref, lnw_ref, lnb_ref, wg_ref, bg_ref, wp_ref,
                eo_ref, ya_ref, yb_ref, stage, sems, *, steps_a):
    i = pl.program_id(0)
    slot = i % 2

    def run_copy(slot_, lo, glob, rows):
        return pltpu.make_async_copy(eo_ref.at[pl.ds(glob, rows), :], stage.at[slot_, pl.ds(lo, rows), :],
                                     sems.at[slot_])

    def fetch(tile, slot_):
        stage[slot_] = jnp.zeros(stage.shape[1:], stage.dtype)
        _for_each_run_piece(tc_ref, g0_ref, tile, lambda lo, glob, rows: run_copy(slot_, lo, glob, rows).start())

    @pl.when(i == 0)
    def _():
        fetch(0, 0)

    @pl.when(i + 1 < pl.num_programs(0))
    def _():
        fetch(i + 1, 1 - slot)

    _for_each_run_piece(tc_ref, g0_ref, i, lambda lo, glob, rows: run_copy(slot, lo, glob, rows).wait())

    r = lax.broadcasted_iota(I32, (MOE_TILE, MOE_SLOTS), 1)
    g = jnp.zeros((MOE_TILE, MOE_SLOTS), F32)
    for j in range(TOP_K):
        g = jnp.where(r == lpos_ref[:, j:j + 1], gate_ref[:, j:j + 1], g)
    g_hi = g.astype(BF16)
    g_lo = (g - g_hi.astype(F32)).astype(BF16)
    g2 = jnp.concatenate([g_hi, g_lo], axis=0)
    both = _dot(g2, stage[slot])
    moe = both[:MOE_TILE] + both[MOE_TILE:]
    x2 = _layer_norm(DN_ALPHA * x1_ref[...] + moe, lnw_ref[...], lnb_ref[...])
    ple_gate = jax.nn.sigmoid(_dot(x2.astype(BF16), wg_ref[...]) + bg_ref[...])
    p = jnp.where(i < steps_a, pa_ref[...], pb_ref[...])
    y = x2 + ple_gate * _dot(p.astype(BF16), wp_ref[...])

    @pl.when(i < steps_a)
    def _():
        ya_ref[...] = y

    @pl.when(i >= steps_a)
    def _():
        yb_ref[...] = y


def _final(tc, g0, lpos, gate, x1, p_a, p_b, lnw, lnb, w_gate, b_gate, w_ple, expert_out):
    n = x1.shape[0]
    steps_a = p_a.shape[0] // MOE_TILE
    row = lambda width: pl.BlockSpec((MOE_TILE, width), lambda i, *_: (i, 0))
    full = lambda a: pl.BlockSpec(a.shape, lambda i, *_: (0,) * a.ndim)
    return pl.pallas_call(
        functools.partial(_final_body, steps_a=steps_a),
        grid_spec=pltpu.PrefetchScalarGridSpec(
            num_scalar_prefetch=2,
            grid=(n // MOE_TILE,),
            in_specs=[row(TOP_K), row(TOP_K), row(D_MODEL), *_two_group_rows(MOE_TILE, D_PLE, steps_a), full(lnw),
                      full(lnb), full(w_gate), full(b_gate), full(w_ple), pl.BlockSpec(memory_space=pl.ANY)],
            out_specs=list(_two_group_rows(MOE_TILE, D_MODEL, steps_a)),
            scratch_shapes=[pltpu.VMEM((2, MOE_SLOTS, D_MODEL), BF16), pltpu.SemaphoreType.DMA((2,))],
        ),
        out_shape=[jax.ShapeDtypeStruct((p_a.shape[0], D_MODEL), F32),
                   jax.ShapeDtypeStruct((p_b.shape[0], D_MODEL), F32)],
        compiler_params=_cparams("arbitrary"),
        name="combine_final",
    )(tc, g0, lpos, gate, x1, p_a, p_b, lnw, lnb, w_gate, b_gate, w_ple, expert_out)


def _round_up(a, m):
    return -(-a // m) * m


def _token_mixers(x, hist_k, hist_v, hist_kidx, s0, pos0, wts):
    b, t, _ = x.shape
    n = b * t
    x2d = x.reshape(n, D_MODEL)

    qa, ka, va, ka_bf, va_bf, qi, ki, ki_bf, wi, hb = _inproj(x2d, wts["w_in"], wts["idx_lnw"], wts["idx_lnb"])

    k_new = ka_bf.reshape(b, t, N_KV_A * HEAD_DIM_A)
    v_new = va_bf.reshape(b, t, N_KV_A * HEAD_DIM_A)
    ki_new = ki_bf.reshape(b, t, IDX_DIM)
    q_ops = (qa.reshape(N_HEADS_A, b, t, HEAD_DIM_A), qi.reshape(N_IDX_HEADS, b, t, IDX_DIM),
             wi.reshape(b, t, N_IDX_HEADS))
    past = 0 if hist_k is None else hist_k.shape[1]
    s_real = past + t
    topk = min(TOPK_MAX, s_real // 4)
    if past:
        k_all = jnp.concatenate([hist_k.reshape(b, past, -1).astype(BF16), k_new], axis=1)
        v_all = jnp.concatenate([hist_v.reshape(b, past, -1).astype(BF16), v_new], axis=1)
        ki_all = jnp.concatenate([hist_kidx.astype(BF16), ki_new], axis=1)
    else:
        k_all, v_all, ki_all = k_new, v_new, ki_new
    pad = ((0, 0), (0, _round_up(s_real, KEY_BLOCK) - s_real), (0, 0))
    keys = tuple(jnp.pad(a, pad) for a in (ki_all, k_all, v_all))
    oa = _dsa(*q_ops, keys, s_real=s_real, pos0=pos0, topk=topk)

    ob, s_fin = _hgrn(hb.reshape(b, t, 4 * WIDTH_B), s0, wts["lb"], wts["hgrn_nw"])
    return (oa, ob.reshape(n, WIDTH_B), ka.reshape(b, t, N_KV_A, HEAD_DIM_A), va.reshape(b, t, N_KV_A, HEAD_DIM_A),
            ki.reshape(b, t, IDX_DIM), s_fin)


def _channel_mixer(mix_a, mix_b, x_a, x_b, p_a, p_b, wts):
    shape_a, shape_b = x_a.shape, x_b.shape
    x_a, x_b = x_a.reshape(-1, D_MODEL), x_b.reshape(-1, D_MODEL)
    n = x_a.shape[0] + x_b.shape[0]
    x1, logits_t = _outproj(mix_a, mix_b, x_a, x_b, wts["w_out"], wts["ln1_w"], wts["ln1_b"],
                            wts["wr_hi"], wts["wr_lo"], wts["b_router"])

    tiles = n // MOE_TILE
    lpos_t, gate_t, tile_runs, tile_base, counts = _route(logits_t)
    tc, counts = tile_runs[:, :, 0], counts[:, 0]
    padded = (counts + EXPERT_ROWS - 1) // EXPERT_ROWS * EXPERT_ROWS
    seg_end = jnp.cumsum(padded)
    seg_start = seg_end - padded
    g0 = (seg_start[None, :] + tile_base[:, :, 0]).astype(I32)
    n_blocks = -(-(n * TOP_K + N_EXPERTS * tiles * (ROW_ALIGN - 1) + N_EXPERTS * (EXPERT_ROWS - 1))
                 // EXPERT_ROWS)
    block_row0 = jnp.arange(n_blocks, dtype=I32) * EXPERT_ROWS
    block_e = jnp.minimum(jnp.sum((seg_end[None, :] <= block_row0[:, None]).astype(I32), axis=1), N_EXPERTS - 1)
    n_used = (seg_end[-1:] // EXPERT_ROWS).astype(I32)
    seg = jnp.stack([seg_start, seg_end]).astype(I32)
    buf = _dispatch(x1, lpos_t, tc, g0, seg, n_blocks * EXPERT_ROWS)
    expert_out = _experts(buf, block_e, n_used, wts["w_gu"], wts["b_gu"], wts["w_down"], wts["b_down"])
    y_a, y_b = _final(tc, g0, lpos_t.T, gate_t.T, x1, p_a.reshape(-1, D_PLE), p_b.reshape(-1, D_PLE),
                      wts["ln2_w"], wts["ln2_b"], wts["w_ple_gate"], wts["b_ple_gate"], wts["w_ple"], expert_out)
    return y_a.reshape(shape_a), y_b.reshape(shape_b)


def _prep_weights(w_in, w_out, idx_k_norm_w, idx_k_norm_b, lb, hgrn_norm_w, ln1_w, ln1_b, w_router, b_router,
                  w_gu, b_gu, w_down, b_down, ln2_w, ln2_b, w_ple, w_ple_gate, b_ple_gate):
    n_a = COL_KW + IDX_DIM + N_IDX_HEADS
    w_pad = jnp.concatenate(
        [w_in[:, :n_a], jnp.zeros((D_MODEL, COL_HB - n_a), w_in.dtype), w_in[:, n_a:]], axis=1).astype(BF16)
    wr_t = w_router.T
    wr_hi = wr_t.astype(BF16)
    row = lambda a: a.reshape(1, -1)
    return dict(
        w_in=w_pad, idx_lnw=row(idx_k_norm_w), idx_lnb=row(idx_k_norm_b), lb=row(lb), hgrn_nw=row(hgrn_norm_w),
        w_out=w_out.astype(BF16), ln1_w=row(ln1_w), ln1_b=row(ln1_b),
        wr_hi=wr_hi, wr_lo=(wr_t - wr_hi.astype(F32)).astype(BF16), b_router=b_router.reshape(N_EXPERTS, 1),
        w_gu=w_gu, b_gu=b_gu.reshape(N_EXPERTS, 1, 2 * D_EXPERT),
        w_down=w_down, b_down=b_down.reshape(N_EXPERTS, 1, D_MODEL),
        ln2_w=row(ln2_w), ln2_b=row(ln2_b), w_ple=w_ple.astype(BF16), w_ple_gate=w_ple_gate.astype(BF16),
        b_ple_gate=row(b_ple_gate))


def kernel(x_prompt, x_sample, cache_k, cache_v, cache_kidx, state_hgrn, p_prompt, p_sample, w_in, w_out,
           idx_k_norm_w, idx_k_norm_b, hgrn_lb_logits, hgrn_norm_w, ln1_w, ln1_b, w_router, b_router, w_gu, b_gu,
           w_down, b_down, ln2_w, ln2_b, w_ple, w_ple_gate, b_ple_gate):
    lb_all = jnp.cumsum(jax.nn.softmax(hgrn_lb_logits.astype(F32), axis=0), axis=0)
    xp, xs = x_prompt, x_sample
    outs = [[] for _ in range(8)]
    for i in range(DEPTH):
        wts = _prep_weights(w_in[i], w_out[i], idx_k_norm_w[i], idx_k_norm_b[i], lb_all[i], hgrn_norm_w[i],
                            ln1_w[i], ln1_b[i], w_router[i], b_router[i], w_gu[i], b_gu[i], w_down[i], b_down[i],
                            ln2_w[i], ln2_b[i], w_ple[i], w_ple_gate[i], b_ple_gate[i])
        s0p = jnp.zeros((xp.shape[0], N_HEADS_B, HEAD_DIM_B, HEAD_DIM_B), F32)
        mix_p = _token_mixers(xp, None, None, None, s0p, 0, wts)
        mix_s = _token_mixers(xs, cache_k[i], cache_v[i], cache_kidx[i], state_hgrn[i], cache_k.shape[2], wts)
        xp, xs = _channel_mixer(mix_p, mix_s, xp, xs, p_prompt[i], p_sample[i], wts)
        for lst, val in zip(outs, mix_p[2:] + mix_s[2:]):
            lst.append(val)
    return (xp, xs) + tuple(jnp.stack(l) for l in outs)
```

```python
import functools
import itertools

import jax
import jax.numpy as jnp
from jax import lax
from jax.experimental import pallas as pl
from jax.experimental.pallas import tpu as pltpu

F32 = jnp.float32
BF16 = jnp.bfloat16
I32 = jnp.int32

D_MODEL = 1024
CHUNK = 64
CHUNK_SHIFT = 6
WIDTH_A = 512
HEAD_DIM_A = 64
N_HEADS_A = 8
N_KV_A = 2
N_IDX_HEADS = 4
IDX_DIM = 64
IDX_SCALE = IDX_DIM ** -0.5 * N_IDX_HEADS ** -0.5
TOPK_MAX = 256
WIDTH_B = 512
HEAD_DIM_B = 128
N_HEADS_B = 4
N_EXPERTS = 32
TOP_K = 4
D_EXPERT = 1024
SWIGLU_LIMIT = 7.0
SWIGLU_ALPHA = 1.702
D_PLE = 256
LN_EPS = 1e-5
RMS_EPS = 1e-6
DEPTH = 1
DN_ALPHA = (2 * DEPTH) ** 0.25

COL_QA, COL_KA, COL_VA, COL_QI, COL_KW, COL_HB, COL_END = 0, 512, 640, 768, 1024, 1152, 3200

VMEM_LIMIT = 56 * 1024 * 1024
INPROJ_ROWS = 1024
KEY_BLOCK = 256
BLOCKS_PER_TRIP = 8
V_ROWS = 80
LOG2_E = 1.4426950408889634
HGRN_SUB = 16
EXPERT_ROWS = 512
MOE_TILE = 512
ROW_ALIGN = 16
RUN_PIECES = (512, 256, 128, 64, 32, 16)
MOE_SLOTS = -(-(TOP_K * MOE_TILE + N_EXPERTS * (ROW_ALIGN - 1)) // 256) * 256
NEG_INF = float("-inf")
INT_MIN = -(2 ** 31)


def _cparams(*sem):
    return pltpu.CompilerParams(dimension_semantics=sem, vmem_limit_bytes=VMEM_LIMIT)


def _dot(a, b):
    return jnp.dot(a, b, preferred_element_type=F32)


def _dot_nt(a, b):
    return lax.dot_general(a, b, (((1,), (1,)), ((), ())), preferred_element_type=F32)


def _dot_tn(a, b):
    return lax.dot_general(a, b, (((0,), (0,)), ((), ())), preferred_element_type=F32)


def _layer_norm(z, w, b):
    mu = jnp.mean(z, axis=-1, keepdims=True)
    d = z - mu
    var = jnp.mean(d * d, axis=-1, keepdims=True)
    return d * lax.rsqrt(var + LN_EPS) * w + b


def _inproj_body(x_ref, w_ref, lnw_ref, lnb_ref, qa_ref, ka_ref, va_ref, kab_ref, vab_ref, qi_ref, ki_ref, kib_ref,
                 wi_ref, hb_ref):
    xb = x_ref[...].astype(BF16)

    def mm(c0, c1):
        return _dot(xb, w_ref[:, c0:c1])

    qa = (mm(COL_QA, COL_KA) * (HEAD_DIM_A ** -0.5 * LOG2_E)).astype(BF16)
    for h in range(N_HEADS_A):
        qa_ref[h] = qa[:, h * HEAD_DIM_A:(h + 1) * HEAD_DIM_A]
    for src, out3_ref, bf_ref in ((mm(COL_KA, COL_VA), ka_ref, kab_ref), (mm(COL_VA, COL_QI), va_ref, vab_ref)):
        for g in range(N_KV_A):
            out3_ref[:, g, :] = src[:, g * HEAD_DIM_A:(g + 1) * HEAD_DIM_A]
        bf_ref[...] = src.astype(BF16)
    qi = mm(COL_QI, COL_KW).astype(BF16)
    for h in range(N_IDX_HEADS):
        qi_ref[h] = qi[:, h * IDX_DIM:(h + 1) * IDX_DIM]
    kw = mm(COL_KW, COL_HB)
    ki = _layer_norm(kw[:, :IDX_DIM], lnw_ref[...], lnb_ref[...])
    ki_ref[...] = ki
    kib_ref[...] = ki.astype(BF16)
    wi_ref[...] = kw[:, IDX_DIM:IDX_DIM + N_IDX_HEADS]
    hb_ref[...] = mm(COL_HB, COL_END)


def _inproj(x2d, w_pad, lnw, lnb):
    n = x2d.shape[0]
    tm = min(INPROJ_ROWS, n)
    row = lambda width: pl.BlockSpec((tm, width), lambda i: (i, 0))
    full = lambda a: pl.BlockSpec(a.shape, lambda i: (0,) * a.ndim)
    heads = lambda nh, d: pl.BlockSpec((nh, tm, d), lambda i: (0, i, 0))
    kv = N_KV_A * HEAD_DIM_A
    kv3 = pl.BlockSpec((tm, N_KV_A, HEAD_DIM_A), lambda i: (i, 0, 0))
    sds = jax.ShapeDtypeStruct
    return pl.pallas_call(
        _inproj_body,
        grid=(n // tm,),
        in_specs=[row(D_MODEL), full(w_pad), full(lnw), full(lnb)],
        out_specs=[heads(N_HEADS_A, HEAD_DIM_A), kv3, kv3, row(kv), row(kv), heads(N_IDX_HEADS, IDX_DIM),
                   row(IDX_DIM), row(IDX_DIM), row(N_IDX_HEADS), row(4 * WIDTH_B)],
        out_shape=[sds((N_HEADS_A, n, HEAD_DIM_A), BF16), sds((n, N_KV_A, HEAD_DIM_A), F32),
                   sds((n, N_KV_A, HEAD_DIM_A), F32), sds((n, kv), BF16), sds((n, kv), BF16),
                   sds((N_IDX_HEADS, n, IDX_DIM), BF16), sds((n, IDX_DIM), F32), sds((n, IDX_DIM), BF16),
                   sds((n, N_IDX_HEADS), F32), sds((n, 4 * WIDTH_B), F32)],
        compiler_params=_cparams("parallel"),
        name="inproj",
    )(x2d, w_pad, lnw, lnb)


def _order_bits_to_f32(u):
    key = u ^ INT_MIN
    bits = key ^ ((key >> 31) & 0x7FFFFFFF)
    f = lax.bitcast_convert_type(bits, F32)
    return jnp.where(u >= 0, jnp.where(u <= 0x007FFFFF, NEG_INF, f), f)


def _dsa_body(qa_ref, qi_ref, wi_ref, kidx_ref, k_ref, vt_ref, o_ref, sc_scr, bias_scr, term_scr, s_scr, acc_scr,
              *, s_pad, s_real, q_rows, pair, pos0, topk):
    kb_rows = KEY_BLOCK
    heads_per_kv = N_HEADS_A // N_KV_A
    qb = pair * q_rows
    cdim = pair * HEAD_DIM_A
    v_rows = pair * V_ROWS
    j = pl.program_id(1)
    q_lo = pos0 + j * q_rows
    lane = lax.broadcasted_iota(I32, (1, qb), 1)
    q_chunk = (q_lo + (lane & (q_rows - 1))) >> CHUNK_SHIFT
    k_lim = (((q_lo + q_rows - 1) >> CHUNK_SHIFT) + 1) * CHUNK
    nkb = jnp.minimum(s_pad // kb_rows, (k_lim + kb_rows - 1) // kb_rows)

    def rows_of(kb):
        return pl.ds(pl.multiple_of(kb * kb_rows, kb_rows), kb_rows)

    def fold8(a):
        return a.reshape(kb_rows // 8, 8, qb)

    def over_blocks(body, init):
        def group(first, width, carry):
            for u in range(width):
                carry = body(first + u, carry)
            return carry
        carry = lax.fori_loop(0, nkb // BLOCKS_PER_TRIP,
                              lambda i, c: group(i * BLOCKS_PER_TRIP, BLOCKS_PER_TRIP, c), init)
        done = (nkb // BLOCKS_PER_TRIP) * BLOCKS_PER_TRIP
        width = BLOCKS_PER_TRIP // 2
        while width:
            carry = lax.cond((nkb & width) != 0, functools.partial(group, done, width), lambda c: c, carry)
            done = done + (nkb & width)
            width //= 2
        return carry

    def score_blk(kb, _):
        rows = rows_of(kb)
        kidx = kidx_ref[0, rows, :]
        for h in range(N_IDX_HEADS):
            term_scr[h] = jnp.maximum(_dot_nt(kidx, qi_ref[h, 0]), 0.0) * wi_ref[0, h:h + 1, :]
        total = (term_scr[0] + term_scr[1]) + (term_scr[2] + term_scr[3])
        spos = kb * kb_rows + lax.broadcasted_iota(I32, (kb_rows, qb), 0)
        sc = jnp.where((spos >> CHUNK_SHIFT) <= q_chunk, total * IDX_SCALE + 0.0, NEG_INF)
        if s_real < s_pad:
            sc = jnp.where(spos < s_real, sc, NEG_INF)
        sc_scr[rows, :] = sc
        return 0

    over_blocks(score_blk, 0)

    def count(thr, strict):
        def body(kb, acc):
            blk = sc_scr[rows_of(kb), :]
            hit = (blk > thr) if strict else (blk >= thr)
            return acc + jnp.sum(fold8(jnp.where(hit, 1, 0).astype(I32)), axis=0)
        acc = over_blocks(body, jnp.zeros((8, qb), I32))
        return jnp.sum(acc, axis=0, keepdims=True)

    def bit_step(i, carry):
        prefix, cnt = carry
        cand = prefix | lax.shift_left(jnp.int32(1), 31 - i)
        c = count(_order_bits_to_f32(cand), False)
        take = c >= topk
        return jnp.where(take, cand, prefix), jnp.where(take, c, cnt)

    prefix, cnt_ge = lax.fori_loop(0, 32, bit_step,
                                   (jnp.zeros((1, qb), I32), jnp.full((1, qb), nkb * kb_rows, I32)))
    tau = _order_bits_to_f32(prefix)
    finite_tau = tau > NEG_INF
    tau_floor = jnp.maximum(tau, jnp.finfo(F32).min)
    tie_lanes = jnp.where(finite_tau, jnp.where(cnt_ge > topk, 1, 0), 0)
    has_ties = jnp.max(tie_lanes) > 0

    @pl.when(jnp.logical_not(has_ties))
    def _():
        def body(kb, _):
            blk = sc_scr[rows_of(kb), :]
            bias_scr[rows_of(kb), :] = jnp.where(blk >= tau_floor, 0.0, NEG_INF)
            return 0
        over_blocks(body, 0)

    @pl.when(has_ties)
    def _():
        need = jnp.where(finite_tau, (topk - count(tau, True)).astype(F32), 0.0)
        r = lax.broadcasted_iota(I32, (kb_rows, kb_rows), 0)
        c = lax.broadcasted_iota(I32, (kb_rows, kb_rows), 1)
        tril = jnp.where(r >= c, 1.0, 0.0).astype(BF16)

        def body(kb, seen):
            blk = sc_scr[rows_of(kb), :]
            eq = blk == tau
            rank = _dot(tril, jnp.where(eq, 1.0, 0.0).astype(BF16)) + seen
            tie_bias = jnp.where(eq, jnp.where(rank <= need, 0.0, NEG_INF), NEG_INF)
            bias_scr[rows_of(kb), :] = jnp.where(blk > tau, 0.0, tie_bias)
            return rank[kb_rows - 1:kb_rows, :]
        lax.fori_loop(0, nkb, body, jnp.zeros((1, qb), F32))

    def pass1(kb, m8):
        rows = rows_of(kb)
        bias = bias_scr[rows, :]
        out = []
        for h in range(N_HEADS_A):
            g = h // heads_per_kv
            s = _dot_nt(k_ref[0, rows, g * cdim:(g + 1) * cdim], qa_ref[h, 0]) + bias
            s_scr[h, rows, :] = s
            out.append(jnp.maximum(m8[h], jnp.max(fold8(s), axis=0)))
        return tuple(out)

    m8 = over_blocks(pass1, tuple(jnp.full((8, qb), NEG_INF, F32) for _ in range(N_HEADS_A)))
    m = [jnp.max(x, axis=0, keepdims=True) for x in m8]

    acc_scr[...] = jnp.zeros_like(acc_scr)

    def pass2(kb, _):
        rows = rows_of(kb)
        for h in range(N_HEADS_A):
            g = h // heads_per_kv
            p = jnp.exp2(s_scr[h, rows, :] - m[h]).astype(BF16)
            acc_scr[h] += _dot(vt_ref[0, g * v_rows:(g + 1) * v_rows, rows], p)
        return 0

    over_blocks(pass2, 0)
    for h in range(N_HEADS_A):
        a = acc_scr[h]
        num, den = a[:HEAD_DIM_A], a[HEAD_DIM_A:HEAD_DIM_A + 1]
        for mb in range(1, pair):
            mine = lane >= mb * q_rows
            r0 = mb * V_ROWS
            num = jnp.where(mine, a[r0:r0 + HEAD_DIM_A], num)
            den = jnp.where(mine, a[r0 + HEAD_DIM_A:r0 + HEAD_DIM_A + 1], den)
        o_ref[0, h * HEAD_DIM_A:(h + 1) * HEAD_DIM_A, :] = (num * (1.0 / den)).astype(BF16)


def _dsa_pairing(b, t):
    return 2 if (2 * t <= 128 and b % 2 == 0) else 1


def _dsa(qa_h, qi_h, wi, keys, *, s_real, pos0, topk):
    _, b, t, _ = qa_h.shape
    s_pad = keys[0].shape[1]
    q_rows = min(256, t)
    pair = _dsa_pairing(b, t)
    bp = b // pair
    lanes, cdim, v_rows = pair * q_rows, pair * HEAD_DIM_A, pair * V_ROWS

    def block_diag(q):
        if pair == 1:
            return q
        nh, d = q.shape[0], q.shape[3]
        eye = jnp.eye(pair, dtype=q.dtype)
        q6 = q.reshape(nh, bp, pair, t, 1, d) * eye.reshape(1, 1, pair, 1, pair, 1)
        return q6.reshape(nh, bp, pair * t, pair * d)

    wi_op = wi.reshape(bp, pair, t, N_IDX_HEADS).transpose(0, 3, 1, 2).reshape(bp, N_IDX_HEADS, pair * t)
    ki_all, k_all, v_all = keys
    ki_op = ki_all.reshape(bp, pair, s_pad, IDX_DIM).transpose(0, 2, 1, 3).reshape(bp, s_pad, cdim)
    k_op = (k_all.reshape(bp, pair, s_pad, N_KV_A, HEAD_DIM_A).transpose(0, 2, 3, 1, 4)
            .reshape(bp, s_pad, N_KV_A * cdim))
    vt = v_all.reshape(b, s_pad, N_KV_A, HEAD_DIM_A).transpose(0, 2, 3, 1)
    vt = jnp.concatenate([vt, jnp.ones((b, N_KV_A, V_ROWS - HEAD_DIM_A, s_pad), vt.dtype)], axis=2)
    vt_op = (vt.reshape(bp, pair, N_KV_A, V_ROWS, s_pad).transpose(0, 2, 1, 3, 4)
             .reshape(bp, N_KV_A * v_rows, s_pad))

    body = functools.partial(_dsa_body, s_pad=s_pad, s_real=s_real, q_rows=q_rows, pair=pair, pos0=pos0,
                             topk=topk)
    o_t = pl.pallas_call(
        body,
        grid=(bp, t // q_rows),
        in_specs=[
            pl.BlockSpec((N_HEADS_A, 1, lanes, cdim), lambda i, j: (0, i, j, 0)),
            pl.BlockSpec((N_IDX_HEADS, 1, lanes, cdim), lambda i, j: (0, i, j, 0)),
            pl.BlockSpec((1, N_IDX_HEADS, lanes), lambda i, j: (i, 0, j)),
            pl.BlockSpec((1, s_pad, cdim), lambda i, j: (i, 0, 0)),
            pl.BlockSpec((1, s_pad, N_KV_A * cdim), lambda i, j: (i, 0, 0)),
            pl.BlockSpec((1, N_KV_A * v_rows, s_pad), lambda i, j: (i, 0, 0)),
        ],
        out_specs=pl.BlockSpec((1, WIDTH_A, lanes), lambda i, j: (i, 0, j)),
        out_shape=jax.ShapeDtypeStruct((bp, WIDTH_A, pair * t), BF16),
        scratch_shapes=[pltpu.VMEM((s_pad, lanes), F32), pltpu.VMEM((s_pad, lanes), F32),
                        pltpu.VMEM((N_IDX_HEADS, KEY_BLOCK, lanes), F32),
                        pltpu.VMEM((N_HEADS_A, s_pad, lanes), F32), pltpu.VMEM((N_HEADS_A, v_rows, lanes), F32)],
        compiler_params=_cparams("parallel", "parallel"),
        name="dsa",
    )(block_diag(qa_h), block_diag(qi_h), wi_op, ki_op, k_op, vt_op)
    return o_t.reshape(bp, WIDTH_A, pair, t).transpose(0, 2, 3, 1).reshape(b * t, WIDTH_A)


def _split3(a):
    hi = a.astype(BF16)
    r1 = a - hi.astype(F32)
    mid = r1.astype(BF16)
    lo = (r1 - mid.astype(F32)).astype(BF16)
    return hi, mid, lo


def _hgrn_body(hb_ref, s0_ref, lb_ref, nw_ref, ob_ref, sfin_ref, state_scr, *, tb, nb):
    t = pl.program_id(1)

    @pl.when(t == 0)
    def _():
        for bi, h in itertools.product(range(nb), range(N_HEADS_B)):
            state_scr[bi, h] = s0_ref[bi, h].T

    lb = lb_ref[...]
    r = lax.broadcasted_iota(I32, (CHUNK, CHUNK), 0)
    c = lax.broadcasted_iota(I32, (CHUNK, CHUNK), 1)
    causal = r >= c
    tril = jnp.where(causal, 1.0, 0.0).astype(BF16)

    for ci, bi in itertools.product(range(tb // CHUNK), range(nb)):
        rows = slice(ci * CHUNK, (ci + 1) * CHUNK)
        f = lb + (1.0 - lb) * jax.nn.sigmoid(hb_ref[bi, rows, WIDTH_B:2 * WIDTH_B])
        parts = _split3(jnp.log(f))
        bcum = _dot(tril, parts[0]) + _dot(tril, parts[1]) + _dot(tril, parts[2])
        for h in range(N_HEADS_B):
            lanes = slice(h * HEAD_DIM_B, (h + 1) * HEAD_DIM_B)
            q = hb_ref[bi, rows, lanes]
            k = 1.0 - f[:, lanes]
            v = hb_ref[bi, rows, 2 * WIDTH_B + h * HEAD_DIM_B:2 * WIDTH_B + (h + 1) * HEAD_DIM_B]
            gate = hb_ref[bi, rows, 3 * WIDTH_B + h * HEAD_DIM_B:3 * WIDTH_B + (h + 1) * HEAD_DIM_B]
            bh = bcum[:, lanes]
            b_last = bh[CHUNK - 1:CHUNK, :]
            vb = v.astype(BF16)
            state_t = state_scr[bi, h]
            o_inter = _dot_nt((q * jnp.exp(bh)).astype(BF16), state_t.astype(BF16))
            a_rows = []
            for i in range(CHUNK // HGRN_SUB):
                lo, hi = i * HGRN_SUB, (i + 1) * HGRN_SUB
                ref = bh[lo - 1:lo, :] if i else jnp.zeros((1, HEAD_DIM_B), F32)
                qs = (q[lo:hi] * jnp.exp(bh[lo:hi] - ref)).astype(BF16)
                ks = (k * jnp.exp(ref - bh)).astype(BF16)
                a_rows.append(_dot_nt(qs, ks))
            a = jnp.where(causal, jnp.concatenate(a_rows, axis=0), 0.0)
            o = o_inter + _dot(a.astype(BF16), vb)
            kdec = (k * jnp.exp(b_last - bh)).astype(BF16)
            state_scr[bi, h] = jnp.exp(b_last) * state_t + _dot_tn(vb, kdec)
            ms = jnp.mean(o * o, axis=-1, keepdims=True)
            y = o * lax.rsqrt(ms + RMS_EPS) * nw_ref[...] * (gate * jax.nn.sigmoid(gate))
            ob_ref[bi, rows, lanes] = y.astype(BF16)

    @pl.when(t == pl.num_programs(1) - 1)
    def _():
        for bi, h in itertools.product(range(nb), range(N_HEADS_B)):
            sfin_ref[bi, h] = state_scr[bi, h].T


def _hgrn(hb, s0, lb, nw):
    b, t, _ = hb.shape
    tb = min(256, t)
    nb = next(m for m in (4, 2, 1) if b % m == 0)
    return pl.pallas_call(
        functools.partial(_hgrn_body, tb=tb, nb=nb),
        grid=(b // nb, t // tb),
        in_specs=[
            pl.BlockSpec((nb, tb, 4 * WIDTH_B), lambda i, j: (i, j, 0)),
            pl.BlockSpec((nb, N_HEADS_B, HEAD_DIM_B, HEAD_DIM_B), lambda i, j: (i, 0, 0, 0)),
            pl.BlockSpec((1, WIDTH_B), lambda i, j: (0, 0)),
            pl.BlockSpec((1, HEAD_DIM_B), lambda i, j: (0, 0)),
        ],
        out_specs=[
            pl.BlockSpec((nb, tb, WIDTH_B), lambda i, j: (i, j, 0)),
            pl.BlockSpec((nb, N_HEADS_B, HEAD_DIM_B, HEAD_DIM_B), lambda i, j: (i, 0, 0, 0)),
        ],
        out_shape=[jax.ShapeDtypeStruct((b, t, WIDTH_B), BF16),
                   jax.ShapeDtypeStruct((b, N_HEADS_B, HEAD_DIM_B, HEAD_DIM_B), F32)],
        scratch_shapes=[pltpu.VMEM((nb, N_HEADS_B, HEAD_DIM_B, HEAD_DIM_B), F32)],
        compiler_params=_cparams("parallel", "arbitrary"),
        name="hgrn",
    )(hb, s0, lb, nw)


def _outproj_body(oaa_ref, oab_ref, oba_ref, obb_ref, xa_ref, xb_ref, w_ref, lnw_ref, lnb_ref, wrh_ref, wrl_ref, br_ref,
                  x1_ref, lg_ref, *, steps_a):
    first = pl.program_id(0) < steps_a
    pick = lambda a_ref, b_ref: jnp.where(first, a_ref[...], b_ref[...])
    x = pick(xa_ref, xb_ref)
    y = _dot(pick(oaa_ref, oab_ref), w_ref[:WIDTH_A, :]) + _dot(pick(oba_ref, obb_ref), w_ref[WIDTH_A:, :])
    x1 = _layer_norm(DN_ALPHA * x + y, lnw_ref[...], lnb_ref[...])
    x1_ref[...] = x1
    hi = x1.astype(BF16)
    lo = (x1 - hi.astype(F32)).astype(BF16)
    lg_ref[...] = (_dot_nt(wrh_ref[...], hi) + _dot_nt(wrh_ref[...], lo) + _dot_nt(wrl_ref[...], hi)
                   + br_ref[...])


def _two_group_rows(tm, width, steps_a):
    return (pl.BlockSpec((tm, width), lambda i, *_: (jnp.minimum(i, steps_a - 1), 0)),
            pl.BlockSpec((tm, width), lambda i, *_: (jnp.maximum(i - steps_a, 0), 0)))


def _outproj(mix_a, mix_b, x_a, x_b, w_out, lnw, lnb, wr_hi, wr_lo, br):
    tm = MOE_TILE
    assert x_a.shape[0] % tm == 0 and x_b.shape[0] % tm == 0
    n = x_a.shape[0] + x_b.shape[0]
    steps_a = x_a.shape[0] // tm
    row = lambda width: pl.BlockSpec((tm, width), lambda i: (i, 0))
    full = lambda a: pl.BlockSpec(a.shape, lambda i: (0,) * a.ndim)
    return pl.pallas_call(
        functools.partial(_outproj_body, steps_a=steps_a),
        grid=(n // tm,),
        in_specs=[*_two_group_rows(tm, WIDTH_A, steps_a), *_two_group_rows(tm, WIDTH_B, steps_a),
                  *_two_group_rows(tm, D_MODEL, steps_a), full(w_out), full(lnw), full(lnb), full(wr_hi),
                  full(wr_lo), full(br)],
        out_specs=[row(D_MODEL), pl.BlockSpec((N_EXPERTS, tm), lambda i: (0, i))],
        out_shape=[jax.ShapeDtypeStruct((n, D_MODEL), F32), jax.ShapeDtypeStruct((N_EXPERTS, n), F32)],
        compiler_params=_cparams("parallel"),
        name="outproj",
    )(mix_a[0], mix_b[0], mix_a[1], mix_b[1], x_a, x_b, w_out, lnw, lnb, wr_hi, wr_lo, br)


def _route_body(lg_ref, lpos_ref, gate_ref, tc_ref, base_ref, cnt_ref, cnt_scr, *, tr):
    @pl.when(pl.program_id(0) == 0)
    def _():
        cnt_scr[...] = jnp.zeros_like(cnt_scr)

    l = lg_ref[...]
    rows = lax.broadcasted_iota(I32, (N_EXPERTS, tr), 0)
    vals, hots = [], []
    for j in range(TOP_K):
        m = jnp.max(l, axis=0, keepdims=True)
        idx = jnp.min(jnp.where(l == m, rows, N_EXPERTS), axis=0, keepdims=True)
        hot = rows == idx
        vals.append(m)
        hots.append(hot)
        l = jnp.where(hot, NEG_INF, l)
    es = [jnp.exp(v - vals[0]) for v in vals]
    inv = 1.0 / (es[0] + es[1] + es[2] + es[3])
    for j in range(TOP_K):
        gate_ref[j:j + 1, :] = es[j] * inv
    chosen = jnp.zeros((N_EXPERTS, tr), F32)
    for hot in hots:
        chosen = chosen + jnp.where(hot, 1.0, 0.0)
    r = lax.broadcasted_iota(I32, (tr, tr), 0)
    c = lax.broadcasted_iota(I32, (tr, tr), 1)
    before = jnp.where(r < c, 1.0, 0.0).astype(BF16)
    prior = _dot(chosen.astype(BF16), before)
    units = jnp.ceil(jnp.sum(chosen, axis=1, keepdims=True) * (1.0 / ROW_ALIGN))
    er = lax.broadcasted_iota(I32, (N_EXPERTS, N_EXPERTS), 0)
    ec = lax.broadcasted_iota(I32, (N_EXPERTS, N_EXPERTS), 1)
    earlier = jnp.where(ec < er, 1.0, 0.0).astype(BF16)
    run_len = jnp.broadcast_to(units, (N_EXPERTS, 128)) * ROW_ALIGN
    run_start = _dot(earlier, jnp.broadcast_to(units, (N_EXPERTS, 128)).astype(BF16)) * ROW_ALIGN
    where_in_tile = prior + run_start[:, 0:1]
    for j in range(TOP_K):
        lpos_ref[j:j + 1, :] = jnp.sum(jnp.where(hots[j], where_in_tile, 0.0), axis=0,
                                       keepdims=True).astype(I32)
    tc_ref[0] = run_len.astype(I32)
    base_ref[0] = cnt_scr[...].astype(I32)
    cnt_scr[...] = cnt_scr[...] + run_len
    cnt_ref[...] = cnt_scr[...].astype(I32)


def _route(logits_t):
    n = logits_t.shape[1]
    tr = MOE_TILE
    tiles = n // tr
    tok = lambda rows: pl.BlockSpec((rows, tr), lambda i: (0, i))
    per_tile = pl.BlockSpec((1, N_EXPERTS, 128), lambda i: (i, 0, 0))
    return pl.pallas_call(
        functools.partial(_route_body, tr=tr),
        grid=(tiles,),
        in_specs=[tok(N_EXPERTS)],
        out_specs=[tok(TOP_K), tok(TOP_K), per_tile, per_tile, pl.BlockSpec((N_EXPERTS, 128), lambda i: (0, 0))],
        out_shape=[jax.ShapeDtypeStruct((TOP_K, n), I32), jax.ShapeDtypeStruct((TOP_K, n), F32),
                   jax.ShapeDtypeStruct((tiles, N_EXPERTS, 128), I32),
                   jax.ShapeDtypeStruct((tiles, N_EXPERTS, 128), I32),
                   jax.ShapeDtypeStruct((N_EXPERTS, 128), I32)],
        scratch_shapes=[pltpu.VMEM((N_EXPERTS, 128), F32)],
        compiler_params=_cparams("arbitrary"),
        name="route",
    )(logits_t)


def _for_each_run_piece(tc_ref, g0_ref, t, fn):
    def per_expert(e, local):
        c = tc_ref[t, e]
        glob = g0_ref[t, e]
        lo = local
        for p in RUN_PIECES:
            take = (c & p) != 0

            @pl.when(take)
            def _():
                fn(pl.multiple_of(lo, ROW_ALIGN), pl.multiple_of(glob, ROW_ALIGN), p)

            step = jnp.where(take, p, 0)
            lo = lo + step
            glob = glob + step
        return local + c

    lax.fori_loop(0, N_EXPERTS, per_expert, 0)


def _dispatch_body(tc_ref, g0_ref, seg_ref, lpos_ref, x_ref, buf_ref, stage, zero_scr, zsem, sems, *, n_blocks):
    i = pl.program_id(0)
    slot = i % 2

    def run_copy(slot_, lo, glob, rows):
        return pltpu.make_async_copy(stage.at[slot_, pl.ds(lo, rows), :], buf_ref.at[pl.ds(glob, rows), :],
                                     sems.at[slot_])

    @pl.when(i == 0)
    def _():
        zero_scr[...] = jnp.zeros_like(zero_scr)
        n_used = seg_ref[1, N_EXPERTS - 1] // EXPERT_ROWS

        def block_copy(start):
            return pltpu.make_async_copy(
                zero_scr, buf_ref.at[pl.ds(pl.multiple_of(start, EXPERT_ROWS), EXPERT_ROWS), :], zsem)

        def clears(action):
            for e in range(N_EXPERTS):
                @pl.when(seg_ref[1, e] > seg_ref[0, e])
                def _():
                    action(block_copy(seg_ref[1, e] - EXPERT_ROWS))

            def tail(blk, _):
                action(block_copy(blk * EXPERT_ROWS))
                return 0
            lax.fori_loop(n_used, n_blocks, tail, 0)

        clears(lambda cp: cp.start())
        clears(lambda cp: cp.wait())

    r = lax.broadcasted_iota(I32, (MOE_SLOTS, MOE_TILE), 0)
    onehot = jnp.zeros((MOE_SLOTS, MOE_TILE), F32)
    for j in range(TOP_K):
        onehot = jnp.where(r == lpos_ref[j:j + 1, :], 1.0, onehot)
    stage[slot] = _dot(onehot.astype(BF16), x_ref[...].astype(BF16)).astype(BF16)

    @pl.when(i > 0)
    def _():
        _for_each_run_piece(tc_ref, g0_ref, i - 1,
                            lambda lo, glob, rows: run_copy(1 - slot, lo, glob, rows).wait())

    _for_each_run_piece(tc_ref, g0_ref, i, lambda lo, glob, rows: run_copy(slot, lo, glob, rows).start())

    @pl.when(i == pl.num_programs(0) - 1)
    def _():
        _for_each_run_piece(tc_ref, g0_ref, i, lambda lo, glob, rows: run_copy(slot, lo, glob, rows).wait())


def _dispatch(x1, lpos_t, tc, g0, seg, n_rows):
    n = x1.shape[0]
    return pl.pallas_call(
        functools.partial(_dispatch_body, n_blocks=n_rows // EXPERT_ROWS),
        grid_spec=pltpu.PrefetchScalarGridSpec(
            num_scalar_prefetch=3,
            grid=(n // MOE_TILE,),
            in_specs=[pl.BlockSpec((TOP_K, MOE_TILE), lambda i, *_: (0, i)),
                      pl.BlockSpec((MOE_TILE, D_MODEL), lambda i, *_: (i, 0))],
            out_specs=pl.BlockSpec(memory_space=pl.ANY),
            scratch_shapes=[pltpu.VMEM((2, MOE_SLOTS, D_MODEL), BF16),
                            pltpu.VMEM((EXPERT_ROWS, D_MODEL), BF16),
                            pltpu.SemaphoreType.DMA(()), pltpu.SemaphoreType.DMA((2,))],
        ),
        out_shape=jax.ShapeDtypeStruct((n_rows, D_MODEL), BF16),
        compiler_params=_cparams("arbitrary"),
        name="dispatch",
    )(tc, g0, seg, lpos_t, x1)


def _experts_body(be_ref, nu_ref, x_ref, wgu_ref, bgu_ref, wd_ref, bd_ref, o_ref, wgu_bf, wd_bf):
    i = pl.program_id(0)

    @pl.when(jnp.logical_or(i == 0, be_ref[i] != be_ref[jnp.maximum(i - 1, 0)]))
    def _():
        for r in range(0, D_MODEL, 256):
            wgu_bf[r:r + 256, :] = wgu_ref[0, r:r + 256, :].astype(BF16)
            wd_bf[r:r + 256, :] = wd_ref[0, r:r + 256, :].astype(BF16)

    @pl.when(i < nu_ref[0])
    def _():
        gu = _dot(x_ref[...], wgu_bf[...]) + bgu_ref[0]
        gate = jnp.minimum(gu[:, :D_EXPERT], SWIGLU_LIMIT)
        up = jnp.clip(gu[:, D_EXPERT:], -SWIGLU_LIMIT, SWIGLU_LIMIT)
        hdn = (up + 1.0) * gate * jax.nn.sigmoid(SWIGLU_ALPHA * gate)
        o_ref[...] = (_dot(hdn.astype(BF16), wd_bf[...]) + bd_ref[0]).astype(BF16)

    @pl.when(i >= nu_ref[0])
    def _():
        o_ref[...] = jnp.zeros_like(o_ref)


def _experts(buf, block_e, n_used, w_gu, b_gu, w_down, b_down):
    n_blocks = buf.shape[0] // EXPERT_ROWS
    rows = pl.BlockSpec((EXPERT_ROWS, D_MODEL), lambda i, be, nu: (jnp.minimum(i, nu[0] - 1), 0))
    per_e = lambda a: pl.BlockSpec((1,) + a.shape[1:], lambda i, be, nu: (be[i],) + (0,) * (a.ndim - 1))
    return pl.pallas_call(
        _experts_body,
        grid_spec=pltpu.PrefetchScalarGridSpec(
            num_scalar_prefetch=2,
            grid=(n_blocks,),
            in_specs=[rows, per_e(w_gu), per_e(b_gu), per_e(w_down), per_e(b_down)],
            out_specs=pl.BlockSpec((EXPERT_ROWS, D_MODEL), lambda i, be, nu: (i, 0)),
            scratch_shapes=[pltpu.VMEM((D_MODEL, 2 * D_EXPERT), BF16), pltpu.VMEM((D_EXPERT, D_MODEL), BF16)],
        ),
        out_shape=jax.ShapeDtypeStruct(buf.shape, BF16),
        compiler_params=_cparams("arbitrary"),
        name="experts",
    )(block_e, n_used, buf, w_gu, b_gu, w_down, b_down)


def _final_body(tc_ref, g0_ref, lpos_ref, gate_ref, x1_ref, pa_ref, pb_ref, lnw_ref, lnb_ref, wg_ref, bg_ref, wp_ref,
                eo_ref, ya_ref, yb_ref, stage, sems, *, steps_a):
    i = pl.program_id(0)
    slot = i % 2

    def run_copy(slot_, lo, glob, rows):
        return pltpu.make_async_copy(eo_ref.at[pl.ds(glob, rows), :], stage.at[slot_, pl.ds(lo, rows), :],
                                     sems.at[slot_])

    def fetch(tile, slot_):
        stage[slot_] = jnp.zeros(stage.shape[1:], stage.dtype)
        _for_each_run_piece(tc_ref, g0_ref, tile, lambda lo, glob, rows: run_copy(slot_, lo, glob, rows).start())

    @pl.when(i == 0)
    def _():
        fetch(0, 0)

    @pl.when(i + 1 < pl.num_programs(0))
    def _():
        fetch(i + 1, 1 - slot)

    _for_each_run_piece(tc_ref, g0_ref, i, lambda lo, glob, rows: run_copy(slot, lo, glob, rows).wait())

    r = lax.broadcasted_iota(I32, (MOE_TILE, MOE_SLOTS), 1)
    g = jnp.zeros((MOE_TILE, MOE_SLOTS), F32)
    for j in range(TOP_K):
        g = jnp.where(r == lpos_ref[:, j:j + 1], gate_ref[:, j:j + 1], g)
    moe = _dot(g.astype(BF16), stage[slot])
    x2 = _layer_norm(DN_ALPHA * x1_ref[...] + moe, lnw_ref[...], lnb_ref[...])
    ple_gate = jax.nn.sigmoid(_dot(x2.astype(BF16), wg_ref[...]) + bg_ref[...])
    p = jnp.where(i < steps_a, pa_ref[...], pb_ref[...])
    y = x2 + ple_gate * _dot(p.astype(BF16), wp_ref[...])

    @pl.when(i < steps_a)
    def _():
        ya_ref[...] = y

    @pl.when(i >= steps_a)
    def _():
        yb_ref[...] = y


def _final(tc, g0, lpos, gate, x1, p_a, p_b, lnw, lnb, w_gate, b_gate, w_ple, expert_out):
    n = x1.shape[0]
    steps_a = p_a.shape[0] // MOE_TILE
    row = lambda width: pl.BlockSpec((MOE_TILE, width), lambda i, *_: (i, 0))
    full = lambda a: pl.BlockSpec(a.shape, lambda i, *_: (0,) * a.ndim)
    return pl.pallas_call(
        functools.partial(_final_body, steps_a=steps_a),
        grid_spec=pltpu.PrefetchScalarGridSpec(
            num_scalar_prefetch=2,
            grid=(n // MOE_TILE,),
            in_specs=[row(TOP_K), row(TOP_K), row(D_MODEL), *_two_group_rows(MOE_TILE, D_PLE, steps_a), full(lnw),
                      full(lnb), full(w_gate), full(b_gate), full(w_ple), pl.BlockSpec(memory_space=pl.ANY)],
            out_specs=list(_two_group_rows(MOE_TILE, D_MODEL, steps_a)),
            scratch_shapes=[pltpu.VMEM((2, MOE_SLOTS, D_MODEL), BF16), pltpu.SemaphoreType.DMA((2,))],
        ),
        out_shape=[jax.ShapeDtypeStruct((p_a.shape[0], D_MODEL), F32),
                   jax.ShapeDtypeStruct((p_b.shape[0], D_MODEL), F32)],
        compiler_params=_cparams("arbitrary"),
        name="combine_final",
    )(tc, g0, lpos, gate, x1, p_a, p_b, lnw, lnb, w_gate, b_gate, w_ple, expert_out)


def _round_up(a, m):
    return -(-a // m) * m


def _token_mixers(x, hist_k, hist_v, hist_kidx, s0, pos0, wts):
    b, t, _ = x.shape
    n = b * t
    x2d = x.reshape(n, D_MODEL)

    qa, ka, va, ka_bf, va_bf, qi, ki, ki_bf, wi, hb = _inproj(x2d, wts["w_in"], wts["idx_lnw"], wts["idx_lnb"])

    k_new = ka_bf.reshape(b, t, N_KV_A * HEAD_DIM_A)
    v_new = va_bf.reshape(b, t, N_KV_A * HEAD_DIM_A)
    ki_new = ki_bf.reshape(b, t, IDX_DIM)
    q_ops = (qa.reshape(N_HEADS_A, b, t, HEAD_DIM_A), qi.reshape(N_IDX_HEADS, b, t, IDX_DIM),
             wi.reshape(b, t, N_IDX_HEADS))
    past = 0 if hist_k is None else hist_k.shape[1]
    s_real = past + t
    topk = min(TOPK_MAX, s_real // 4)
    if past:
        k_all = jnp.concatenate([hist_k.reshape(b, past, -1).astype(BF16), k_new], axis=1)
        v_all = jnp.concatenate([hist_v.reshape(b, past, -1).astype(BF16), v_new], axis=1)
        ki_all = jnp.concatenate([hist_kidx.astype(BF16), ki_new], axis=1)
    else:
        k_all, v_all, ki_all = k_new, v_new, ki_new
    pad = ((0, 0), (0, _round_up(s_real, KEY_BLOCK) - s_real), (0, 0))
    keys = tuple(jnp.pad(a, pad) for a in (ki_all, k_all, v_all))
    oa = _dsa(*q_ops, keys, s_real=s_real, pos0=pos0, topk=topk)

    ob, s_fin = _hgrn(hb.reshape(b, t, 4 * WIDTH_B), s0, wts["lb"], wts["hgrn_nw"])
    return (oa, ob.reshape(n, WIDTH_B), ka.reshape(b, t, N_KV_A, HEAD_DIM_A), va.reshape(b, t, N_KV_A, HEAD_DIM_A),
            ki.reshape(b, t, IDX_DIM), s_fin)


def _channel_mixer(mix_a, mix_b, x_a, x_b, p_a, p_b, wts):
    shape_a, shape_b = x_a.shape, x_b.shape
    x_a, x_b = x_a.reshape(-1, D_MODEL), x_b.reshape(-1, D_MODEL)
    n = x_a.shape[0] + x_b.shape[0]
    x1, logits_t = _outproj(mix_a, mix_b, x_a, x_b, wts["w_out"], wts["ln1_w"], wts["ln1_b"],
                            wts["wr_hi"], wts["wr_lo"], wts["b_router"])

    tiles = n // MOE_TILE
    lpos_t, gate_t, tile_runs, tile_base, counts = _route(logits_t)
    tc, counts = tile_runs[:, :, 0], counts[:, 0]
    padded = (counts + EXPERT_ROWS - 1) // EXPERT_ROWS * EXPERT_ROWS
    seg_end = jnp.cumsum(padded)
    seg_start = seg_end - padded
    g0 = (seg_start[None, :] + tile_base[:, :, 0]).astype(I32)
    n_blocks = -(-(n * TOP_K + N_EXPERTS * tiles * (ROW_ALIGN - 1) + N_EXPERTS * (EXPERT_ROWS - 1))
                 // EXPERT_ROWS)
    block_row0 = jnp.arange(n_blocks, dtype=I32) * EXPERT_ROWS
    block_e = jnp.minimum(jnp.sum((seg_end[None, :] <= block_row0[:, None]).astype(I32), axis=1), N_EXPERTS - 1)
    n_used = (seg_end[-1:] // EXPERT_ROWS).astype(I32)
    seg = jnp.stack([seg_start, seg_end]).astype(I32)
    buf = _dispatch(x1, lpos_t, tc, g0, seg, n_blocks * EXPERT_ROWS)
    expert_out = _experts(buf, block_e, n_used, wts["w_gu"], wts["b_gu"], wts["w_down"], wts["b_down"])
    y_a, y_b = _final(tc, g0, lpos_t.T, gate_t.T, x1, p_a.reshape(-1, D_PLE), p_b.reshape(-1, D_PLE),
                      wts["ln2_w"], wts["ln2_b"], wts["w_ple_gate"], wts["b_ple_gate"], wts["w_ple"], expert_out)
    return y_a.reshape(shape_a), y_b.reshape(shape_b)


def _prep_weights(w_in, w_out, idx_k_norm_w, idx_k_norm_b, lb, hgrn_norm_w, ln1_w, ln1_b, w_router, b_router,
                  w_gu, b_gu, w_down, b_down, ln2_w, ln2_b, w_ple, w_ple_gate, b_ple_gate):
    n_a = COL_KW + IDX_DIM + N_IDX_HEADS
    w_pad = jnp.concatenate(
        [w_in[:, :n_a], jnp.zeros((D_MODEL, COL_HB - n_a), w_in.dtype), w_in[:, n_a:]], axis=1).astype(BF16)
    wr_t = w_router.T
    wr_hi = wr_t.astype(BF16)
    row = lambda a: a.reshape(1, -1)
    return dict(
        w_in=w_pad, idx_lnw=row(idx_k_norm_w), idx_lnb=row(idx_k_norm_b), lb=row(lb), hgrn_nw=row(hgrn_norm_w),
        w_out=w_out.astype(BF16), ln1_w=row(ln1_w), ln1_b=row(ln1_b),
        wr_hi=wr_hi, wr_lo=(wr_t - wr_hi.astype(F32)).astype(BF16), b_router=b_router.reshape(N_EXPERTS, 1),
        w_gu=w_gu, b_gu=b_gu.reshape(N_EXPERTS, 1, 2 * D_EXPERT),
        w_down=w_down, b_down=b_down.reshape(N_EXPERTS, 1, D_MODEL),
        ln2_w=row(ln2_w), ln2_b=row(ln2_b), w_ple=w_ple.astype(BF16), w_ple_gate=w_ple_gate.astype(BF16),
        b_ple_gate=row(b_ple_gate))


def kernel(x_prompt, x_sample, cache_k, cache_v, cache_kidx, state_hgrn, p_prompt, p_sample, w_in, w_out,
           idx_k_norm_w, idx_k_norm_b, hgrn_lb_logits, hgrn_norm_w, ln1_w, ln1_b, w_router, b_router, w_gu, b_gu,
           w_down, b_down, ln2_w, ln2_b, w_ple, w_ple_gate, b_ple_gate):
    lb_all = jnp.cumsum(jax.nn.softmax(hgrn_lb_logits.astype(F32), axis=0), axis=0)
    xp, xs = x_prompt, x_sample
    outs = [[] for _ in range(8)]
    for i in range(DEPTH):
        wts = _prep_weights(w_in[i], w_out[i], idx_k_norm_w[i], idx_k_norm_b[i], lb_all[i], hgrn_norm_w[i],
                            ln1_w[i], ln1_b[i], w_router[i], b_router[i], w_gu[i], b_gu[i], w_down[i], b_down[i],
                            ln2_w[i], ln2_b[i], w_ple[i], w_ple_gate[i], b_ple_gate[i])
        s0p = jnp.zeros((xp.shape[0], N_HEADS_B, HEAD_DIM_B, HEAD_DIM_B), F32)
        mix_p = _token_mixers(xp, None, None, None, s0p, 0, wts)
        mix_s = _token_mixers(xs, cache_k[i], cache_v[i], cache_kidx[i], state_hgrn[i], cache_k.shape[2], wts)
        xp, xs = _channel_mixer(mix_p, mix_s, xp, xs, p_prompt[i], p_sample[i], wts)
        for lst, val in zip(outs, mix_p[2:] + mix_s[2:]):
            lst.append(val)
    return (xp, xs) + tuple(jnp.stack(l) for l in outs)
```

```python
import functools
import itertools

import jax
import jax.numpy as jnp
from jax import lax
from jax.experimental import pallas as pl
from jax.experimental.pallas import tpu as pltpu

F32 = jnp.float32
BF16 = jnp.bfloat16
I32 = jnp.int32

D_MODEL = 1024
CHUNK = 64
CHUNK_SHIFT = 6
WIDTH_A = 512
HEAD_DIM_A = 64
N_HEADS_A = 8
N_KV_A = 2
N_IDX_HEADS = 4
IDX_DIM = 64
IDX_SCALE = IDX_DIM ** -0.5 * N_IDX_HEADS ** -0.5
TOPK_MAX = 256
WIDTH_B = 512
HEAD_DIM_B = 128
N_HEADS_B = 4
N_EXPERTS = 32
TOP_K = 4
D_EXPERT = 1024
SWIGLU_LIMIT = 7.0
SWIGLU_ALPHA = 1.702
D_PLE = 256
LN_EPS = 1e-5
RMS_EPS = 1e-6
DEPTH = 1
DN_ALPHA = (2 * DEPTH) ** 0.25

COL_QA, COL_KA, COL_VA, COL_QI, COL_KW, COL_HB, COL_END = 0, 512, 640, 768, 1024, 1152, 3200

VMEM_LIMIT = 56 * 1024 * 1024
INPROJ_ROWS = 1024
KEY_BLOCK = 256
BLOCKS_PER_TRIP = 8
V_ROWS = 80
LOG2_E = 1.4426950408889634
HGRN_SUB = 16
EXPERT_ROWS = 512
MOE_TILE = 512
ROW_ALIGN = 16
RUN_PIECES = (512, 256, 128, 64, 32, 16)
MOE_SLOTS = -(-(TOP_K * MOE_TILE + N_EXPERTS * (ROW_ALIGN - 1)) // 256) * 256
NEG_INF = float("-inf")
INT_MIN = -(2 ** 31)


def _cparams(*sem):
    return pltpu.CompilerParams(dimension_semantics=sem, vmem_limit_bytes=VMEM_LIMIT)


def _dot(a, b):
    return jnp.dot(a, b, preferred_element_type=F32)


def _dot_nt(a, b):
    return lax.dot_general(a, b, (((1,), (1,)), ((), ())), preferred_element_type=F32)


def _dot_tn(a, b):
    return lax.dot_general(a, b, (((0,), (0,)), ((), ())), preferred_element_type=F32)


def _layer_norm(z, w, b):
    mu = jnp.mean(z, axis=-1, keepdims=True)
    d = z - mu
    var = jnp.mean(d * d, axis=-1, keepdims=True)
    return d * lax.rsqrt(var + LN_EPS) * w + b


def _inproj_body(x_ref, w_ref, lnw_ref, lnb_ref, qa_ref, ka_ref, va_ref, kab_ref, vab_ref, qi_ref, ki_ref, kib_ref,
                 wi_ref, hb_ref):
    xb = x_ref[...].astype(BF16)

    def mm(c0, c1):
        return _dot(xb, w_ref[:, c0:c1])

    qa = (mm(COL_QA, COL_KA) * (HEAD_DIM_A ** -0.5 * LOG2_E)).astype(BF16)
    for h in range(N_HEADS_A):
        qa_ref[h] = qa[:, h * HEAD_DIM_A:(h + 1) * HEAD_DIM_A]
    for src, out3_ref, bf_ref in ((mm(COL_KA, COL_VA), ka_ref, kab_ref), (mm(COL_VA, COL_QI), va_ref, vab_ref)):
        for g in range(N_KV_A):
            out3_ref[:, g, :] = src[:, g * HEAD_DIM_A:(g + 1) * HEAD_DIM_A]
        bf_ref[...] = src.astype(BF16)
    qi = mm(COL_QI, COL_KW).astype(BF16)
    for h in range(N_IDX_HEADS):
        qi_ref[h] = qi[:, h * IDX_DIM:(h + 1) * IDX_DIM]
    kw = mm(COL_KW, COL_HB)
    ki = _layer_norm(kw[:, :IDX_DIM], lnw_ref[...], lnb_ref[...])
    ki_ref[...] = ki
    kib_ref[...] = ki.astype(BF16)
    wi_ref[...] = kw[:, IDX_DIM:IDX_DIM + N_IDX_HEADS]
    hb_ref[...] = mm(COL_HB, COL_END)


def _inproj(x2d, w_pad, lnw, lnb):
    n = x2d.shape[0]
    tm = min(INPROJ_ROWS, n)
    row = lambda width: pl.BlockSpec((tm, width), lambda i: (i, 0))
    full = lambda a: pl.BlockSpec(a.shape, lambda i: (0,) * a.ndim)
    heads = lambda nh, d: pl.BlockSpec((nh, tm, d), lambda i: (0, i, 0))
    kv = N_KV_A * HEAD_DIM_A
    kv3 = pl.BlockSpec((tm, N_KV_A, HEAD_DIM_A), lambda i: (i, 0, 0))
    sds = jax.ShapeDtypeStruct
    return pl.pallas_call(
        _inproj_body,
        grid=(n // tm,),
        in_specs=[row(D_MODEL), full(w_pad), full(lnw), full(lnb)],
        out_specs=[heads(N_HEADS_A, HEAD_DIM_A), kv3, kv3, row(kv), row(kv), heads(N_IDX_HEADS, IDX_DIM),
                   row(IDX_DIM), row(IDX_DIM), row(N_IDX_HEADS), row(4 * WIDTH_B)],
        out_shape=[sds((N_HEADS_A, n, HEAD_DIM_A), BF16), sds((n, N_KV_A, HEAD_DIM_A), F32),
                   sds((n, N_KV_A, HEAD_DIM_A), F32), sds((n, kv), BF16), sds((n, kv), BF16),
                   sds((N_IDX_HEADS, n, IDX_DIM), BF16), sds((n, IDX_DIM), F32), sds((n, IDX_DIM), BF16),
                   sds((n, N_IDX_HEADS), F32), sds((n, 4 * WIDTH_B), F32)],
        compiler_params=_cparams("parallel"),
        name="inproj",
    )(x2d, w_pad, lnw, lnb)


def _order_bits_to_f32(u):
    key = u ^ INT_MIN
    bits = key ^ ((key >> 31) & 0x7FFFFFFF)
    f = lax.bitcast_convert_type(bits, F32)
    return jnp.where(u >= 0, jnp.where(u <= 0x007FFFFF, NEG_INF, f), f)


def _high_half(x):
    top = lax.bitcast_convert_type(x, I32) & jnp.int32(-65536)
    return lax.bitcast_convert_type(top, F32).astype(BF16)


def _dsa_body(qa_ref, qi_ref, wi_ref, kidx_ref, k_ref, vt_ref, o_ref, sc_scr, hi_scr, bias_scr, term_scr, s_scr, acc_scr,
              *, s_pad, s_real, q_rows, pair, pos0, topk):
    kb_rows = KEY_BLOCK
    heads_per_kv = N_HEADS_A // N_KV_A
    qb = pair * q_rows
    cdim = pair * HEAD_DIM_A
    v_rows = pair * V_ROWS
    j = pl.program_id(1)
    q_lo = pos0 + j * q_rows
    lane = lax.broadcasted_iota(I32, (1, qb), 1)
    q_chunk = (q_lo + (lane & (q_rows - 1))) >> CHUNK_SHIFT
    k_lim = (((q_lo + q_rows - 1) >> CHUNK_SHIFT) + 1) * CHUNK
    nkb = jnp.minimum(s_pad // kb_rows, (k_lim + kb_rows - 1) // kb_rows)

    def rows_of(kb):
        return pl.ds(pl.multiple_of(kb * kb_rows, kb_rows), kb_rows)

    def fold8(a):
        return a.reshape(kb_rows // 8, 8, qb)

    def over_blocks(body, init):
        def group(first, width, carry):
            for u in range(width):
                carry = body(first + u, carry)
            return carry
        carry = lax.fori_loop(0, nkb // BLOCKS_PER_TRIP,
                              lambda i, c: group(i * BLOCKS_PER_TRIP, BLOCKS_PER_TRIP, c), init)
        done = (nkb // BLOCKS_PER_TRIP) * BLOCKS_PER_TRIP
        width = BLOCKS_PER_TRIP // 2
        while width:
            carry = lax.cond((nkb & width) != 0, functools.partial(group, done, width), lambda c: c, carry)
            done = done + (nkb & width)
            width //= 2
        return carry

    def score_blk(kb, _):
        rows = rows_of(kb)
        kidx = kidx_ref[0, rows, :]
        for h in range(N_IDX_HEADS):
            term_scr[h] = jnp.maximum(_dot_nt(kidx, qi_ref[h, 0]), 0.0) * wi_ref[0, h:h + 1, :]
        total = (term_scr[0] + term_scr[1]) + (term_scr[2] + term_scr[3])
        spos = kb * kb_rows + lax.broadcasted_iota(I32, (kb_rows, qb), 0)
        sc = jnp.where((spos >> CHUNK_SHIFT) <= q_chunk, total, NEG_INF)
        if s_real < s_pad:
            sc = jnp.where(spos < s_real, sc, NEG_INF)
        sc_scr[rows, :] = sc
        hi_scr[rows, :] = _high_half(sc)
        return 0

    over_blocks(score_blk, 0)

    def count(thr, strict):
        def body(kb, acc):
            blk = sc_scr[rows_of(kb), :]
            hit = (blk > thr) if strict else (blk >= thr)
            return acc + jnp.sum(fold8(jnp.where(hit, 1, 0).astype(I32)), axis=0)
        acc = over_blocks(body, jnp.zeros((8, qb), I32))
        return jnp.sum(acc, axis=0, keepdims=True)

    def count_high(thr):
        thr_hi = _high_half(thr)
        def body(kb, acc):
            hit = hi_scr[rows_of(kb), :] >= thr_hi
            part = jnp.where(hit, jnp.ones((), BF16), jnp.zeros((), BF16))
            while part.shape[0] > 16:
                half = part.shape[0] // 2
                part = part[:half] + part[half:]
            return acc + part.astype(F32)
        acc = over_blocks(body, jnp.zeros((16, qb), F32))
        return jnp.sum(acc, axis=0, keepdims=True).astype(I32)

    def bit_step(counter, i, carry):
        prefix, cnt = carry
        cand = prefix | lax.shift_left(jnp.int32(1), 31 - i)
        c = counter(_order_bits_to_f32(cand))
        take = c >= topk
        return jnp.where(take, cand, prefix), jnp.where(take, c, cnt)

    carry = (jnp.zeros((1, qb), I32), jnp.full((1, qb), nkb * kb_rows, I32))
    carry = lax.fori_loop(0, 16, functools.partial(bit_step, count_high), carry)
    prefix, cnt_ge = lax.fori_loop(16, 32, functools.partial(bit_step, lambda thr: count(thr, False)), carry)
    tau = _order_bits_to_f32(prefix)
    finite_tau = tau > NEG_INF
    tau_floor = jnp.maximum(tau, jnp.finfo(F32).min)
    tie_lanes = jnp.where(finite_tau, jnp.where(cnt_ge > topk, 1, 0), 0)
    has_ties = jnp.max(tie_lanes) > 0

    @pl.when(jnp.logical_not(has_ties))
    def _():
        def body(kb, _):
            blk = sc_scr[rows_of(kb), :]
            bias_scr[rows_of(kb), :] = jnp.where(blk >= tau_floor, 0.0, NEG_INF)
            return 0
        over_blocks(body, 0)

    @pl.when(has_ties)
    def _():
        need = jnp.where(finite_tau, (topk - count(tau, True)).astype(F32), 0.0)
        r = lax.broadcasted_iota(I32, (kb_rows, kb_rows), 0)
        c = lax.broadcasted_iota(I32, (kb_rows, kb_rows), 1)
        tril = jnp.where(r >= c, 1.0, 0.0).astype(BF16)

        def body(kb, seen):
            blk = sc_scr[rows_of(kb), :]
            eq = blk == tau
            rank = _dot(tril, jnp.where(eq, 1.0, 0.0).astype(BF16)) + seen
            tie_bias = jnp.where(eq, jnp.where(rank <= need, 0.0, NEG_INF), NEG_INF)
            bias_scr[rows_of(kb), :] = jnp.where(blk > tau, 0.0, tie_bias)
            return rank[kb_rows - 1:kb_rows, :]
        lax.fori_loop(0, nkb, body, jnp.zeros((1, qb), F32))

    def pass1(kb, m8):
        rows = rows_of(kb)
        bias = bias_scr[rows, :]
        out = []
        for h in range(N_HEADS_A):
            g = h // heads_per_kv
            s = _dot_nt(k_ref[0, rows, g * cdim:(g + 1) * cdim], qa_ref[h, 0]) + bias
            s_scr[h, rows, :] = s
            out.append(jnp.maximum(m8[h], jnp.max(fold8(s), axis=0)))
        return tuple(out)

    m8 = over_blocks(pass1, tuple(jnp.full((8, qb), NEG_INF, F32) for _ in range(N_HEADS_A)))
    m = [jnp.max(x, axis=0, keepdims=True) for x in m8]

    acc_scr[...] = jnp.zeros_like(acc_scr)

    def pass2(kb, _):
        rows = rows_of(kb)
        for h in range(N_HEADS_A):
            g = h // heads_per_kv
            p = jnp.exp2(s_scr[h, rows, :] - m[h]).astype(BF16)
            acc_scr[h] += _dot(vt_ref[0, g * v_rows:(g + 1) * v_rows, rows], p)
        return 0

    over_blocks(pass2, 0)
    for h in range(N_HEADS_A):
        a = acc_scr[h]
        num, den = a[:HEAD_DIM_A], a[HEAD_DIM_A:HEAD_DIM_A + 1]
        for mb in range(1, pair):
            mine = lane >= mb * q_rows
            r0 = mb * V_ROWS
            num = jnp.where(mine, a[r0:r0 + HEAD_DIM_A], num)
            den = jnp.where(mine, a[r0 + HEAD_DIM_A:r0 + HEAD_DIM_A + 1], den)
        o_ref[0, h * HEAD_DIM_A:(h + 1) * HEAD_DIM_A, :] = (num * (1.0 / den)).astype(BF16)


def _dsa_pairing(b, t):
    return 2 if (2 * t <= 128 and b % 2 == 0) else 1


def _dsa(qa_h, qi_h, wi, keys, *, s_real, pos0, topk):
    _, b, t, _ = qa_h.shape
    s_pad = keys[0].shape[1]
    q_rows = min(256, t)
    pair = _dsa_pairing(b, t)
    bp = b // pair
    lanes, cdim, v_rows = pair * q_rows, pair * HEAD_DIM_A, pair * V_ROWS

    def block_diag(q):
        if pair == 1:
            return q
        nh, d = q.shape[0], q.shape[3]
        eye = jnp.eye(pair, dtype=q.dtype)
        q6 = q.reshape(nh, bp, pair, t, 1, d) * eye.reshape(1, 1, pair, 1, pair, 1)
        return q6.reshape(nh, bp, pair * t, pair * d)

    wi_op = ((wi * IDX_SCALE).reshape(bp, pair, t, N_IDX_HEADS).transpose(0, 3, 1, 2)
             .reshape(bp, N_IDX_HEADS, pair * t))
    ki_all, k_all, v_all = keys
    ki_op = ki_all.reshape(bp, pair, s_pad, IDX_DIM).transpose(0, 2, 1, 3).reshape(bp, s_pad, cdim)
    k_op = (k_all.reshape(bp, pair, s_pad, N_KV_A, HEAD_DIM_A).transpose(0, 2, 3, 1, 4)
            .reshape(bp, s_pad, N_KV_A * cdim))
    vt = v_all.reshape(b, s_pad, N_KV_A, HEAD_DIM_A).transpose(0, 2, 3, 1)
    vt = jnp.concatenate([vt, jnp.ones((b, N_KV_A, V_ROWS - HEAD_DIM_A, s_pad), vt.dtype)], axis=2)
    vt_op = (vt.reshape(bp, pair, N_KV_A, V_ROWS, s_pad).transpose(0, 2, 1, 3, 4)
             .reshape(bp, N_KV_A * v_rows, s_pad))

    body = functools.partial(_dsa_body, s_pad=s_pad, s_real=s_real, q_rows=q_rows, pair=pair, pos0=pos0,
                             topk=topk)
    o_t = pl.pallas_call(
        body,
        grid=(bp, t // q_rows),
        in_specs=[
            pl.BlockSpec((N_HEADS_A, 1, lanes, cdim), lambda i, j: (0, i, j, 0)),
            pl.BlockSpec((N_IDX_HEADS, 1, lanes, cdim), lambda i, j: (0, i, j, 0)),
            pl.BlockSpec((1, N_IDX_HEADS, lanes), lambda i, j: (i, 0, j)),
            pl.BlockSpec((1, s_pad, cdim), lambda i, j: (i, 0, 0)),
            pl.BlockSpec((1, s_pad, N_KV_A * cdim), lambda i, j: (i, 0, 0)),
            pl.BlockSpec((1, N_KV_A * v_rows, s_pad), lambda i, j: (i, 0, 0)),
        ],
        out_specs=pl.BlockSpec((1, WIDTH_A, lanes), lambda i, j: (i, 0, j)),
        out_shape=jax.ShapeDtypeStruct((bp, WIDTH_A, pair * t), BF16),
        scratch_shapes=[pltpu.VMEM((s_pad, lanes), F32), pltpu.VMEM((s_pad, lanes), BF16),
                        pltpu.VMEM((s_pad, lanes), F32),
                        pltpu.VMEM((N_IDX_HEADS, KEY_BLOCK, lanes), F32),
                        pltpu.VMEM((N_HEADS_A, s_pad, lanes), F32), pltpu.VMEM((N_HEADS_A, v_rows, lanes), F32)],
        compiler_params=_cparams("parallel", "parallel"),
        name="dsa",
    )(block_diag(qa_h), block_diag(qi_h), wi_op, ki_op, k_op, vt_op)
    return o_t.reshape(bp, WIDTH_A, pair, t).transpose(0, 2, 3, 1).reshape(b * t, WIDTH_A)


def _split3(a):
    hi = a.astype(BF16)
    r1 = a - hi.astype(F32)
    mid = r1.astype(BF16)
    lo = (r1 - mid.astype(F32)).astype(BF16)
    return hi, mid, lo


def _hgrn_body(hb_ref, s0_ref, lb_ref, nw_ref, ob_ref, sfin_ref, state_scr, *, tb, nb):
    t = pl.program_id(1)

    @pl.when(t == 0)
    def _():
        for bi, h in itertools.product(range(nb), range(N_HEADS_B)):
            state_scr[bi, h] = s0_ref[bi, h].T

    lb = lb_ref[...]
    r = lax.broadcasted_iota(I32, (CHUNK, CHUNK), 0)
    c = lax.broadcasted_iota(I32, (CHUNK, CHUNK), 1)
    causal = r >= c
    tril = jnp.where(causal, 1.0, 0.0).astype(BF16)

    for ci, bi in itertools.product(range(tb // CHUNK), range(nb)):
        rows = slice(ci * CHUNK, (ci + 1) * CHUNK)
        f = lb + (1.0 - lb) * jax.nn.sigmoid(hb_ref[bi, rows, WIDTH_B:2 * WIDTH_B])
        parts = _split3(jnp.log(f))
        bcum = _dot(tril, parts[0]) + _dot(tril, parts[1]) + _dot(tril, parts[2])
        for h in range(N_HEADS_B):
            lanes = slice(h * HEAD_DIM_B, (h + 1) * HEAD_DIM_B)
            q = hb_ref[bi, rows, lanes]
            k = 1.0 - f[:, lanes]
            v = hb_ref[bi, rows, 2 * WIDTH_B + h * HEAD_DIM_B:2 * WIDTH_B + (h + 1) * HEAD_DIM_B]
            gate = hb_ref[bi, rows, 3 * WIDTH_B + h * HEAD_DIM_B:3 * WIDTH_B + (h + 1) * HEAD_DIM_B]
            bh = bcum[:, lanes]
            b_last = bh[CHUNK - 1:CHUNK, :]
            vb = v.astype(BF16)
            state_t = state_scr[bi, h]
            o_inter = _dot_nt((q * jnp.exp(bh)).astype(BF16), state_t.astype(BF16))
            a_rows = []
            for i in range(CHUNK // HGRN_SUB):
                lo, hi = i * HGRN_SUB, (i + 1) * HGRN_SUB
                ref = bh[lo - 1:lo, :] if i else jnp.zeros((1, HEAD_DIM_B), F32)
                qs = (q[lo:hi] * jnp.exp(bh[lo:hi] - ref)).astype(BF16)
                ks = (k * jnp.exp(ref - bh)).astype(BF16)
                a_rows.append(_dot_nt(qs, ks))
            a = jnp.where(causal, jnp.concatenate(a_rows, axis=0), 0.0)
            o = o_inter + _dot(a.astype(BF16), vb)
            kdec = (k * jnp.exp(b_last - bh)).astype(BF16)
            state_scr[bi, h] = jnp.exp(b_last) * state_t + _dot_tn(vb, kdec)
            ms = jnp.mean(o * o, axis=-1, keepdims=True)
            y = o * lax.rsqrt(ms + RMS_EPS) * nw_ref[...] * (gate * jax.nn.sigmoid(gate))
            ob_ref[bi, rows, lanes] = y.astype(BF16)

    @pl.when(t == pl.num_programs(1) - 1)
    def _():
        for bi, h in itertools.product(range(nb), range(N_HEADS_B)):
            sfin_ref[bi, h] = state_scr[bi, h].T


def _hgrn(hb, s0, lb, nw):
    b, t, _ = hb.shape
    tb = min(256, t)
    nb = next(m for m in (4, 2, 1) if b % m == 0)
    return pl.pallas_call(
        functools.partial(_hgrn_body, tb=tb, nb=nb),
        grid=(b // nb, t // tb),
        in_specs=[
            pl.BlockSpec((nb, tb, 4 * WIDTH_B), lambda i, j: (i, j, 0)),
            pl.BlockSpec((nb, N_HEADS_B, HEAD_DIM_B, HEAD_DIM_B), lambda i, j: (i, 0, 0, 0)),
            pl.BlockSpec((1, WIDTH_B), lambda i, j: (0, 0)),
            pl.BlockSpec((1, HEAD_DIM_B), lambda i, j: (0, 0)),
        ],
        out_specs=[
            pl.BlockSpec((nb, tb, WIDTH_B), lambda i, j: (i, j, 0)),
            pl.BlockSpec((nb, N_HEADS_B, HEAD_DIM_B, HEAD_DIM_B), lambda i, j: (i, 0, 0, 0)),
        ],
        out_shape=[jax.ShapeDtypeStruct((b, t, WIDTH_B), BF16),
                   jax.ShapeDtypeStruct((b, N_HEADS_B, HEAD_DIM_B, HEAD_DIM_B), F32)],
        scratch_shapes=[pltpu.VMEM((nb, N_HEADS_B, HEAD_DIM_B, HEAD_DIM_B), F32)],
        compiler_params=_cparams("parallel", "arbitrary"),
        name="hgrn",
    )(hb, s0, lb, nw)


def _outproj_body(oaa_ref, oab_ref, oba_ref, obb_ref, xa_ref, xb_ref, w_ref, lnw_ref, lnb_ref, wrh_ref, wrl_ref, br_ref,
                  x1_ref, lg_ref, *, steps_a):
    first = pl.program_id(0) < steps_a
    pick = lambda a_ref, b_ref: jnp.where(first, a_ref[...], b_ref[...])
    x = pick(xa_ref, xb_ref)
    y = _dot(pick(oaa_ref, oab_ref), w_ref[:WIDTH_A, :]) + _dot(pick(oba_ref, obb_ref), w_ref[WIDTH_A:, :])
    x1 = _layer_norm(DN_ALPHA * x + y, lnw_ref[...], lnb_ref[...])
    x1_ref[...] = x1
    hi = x1.astype(BF16)
    lo = (x1 - hi.astype(F32)).astype(BF16)
    lg_ref[...] = (_dot_nt(wrh_ref[...], hi) + _dot_nt(wrh_ref[...], lo) + _dot_nt(wrl_ref[...], hi)
                   + br_ref[...])


def _two_group_rows(tm, width, steps_a):
    return (pl.BlockSpec((tm, width), lambda i, *_: (jnp.minimum(i, steps_a - 1), 0)),
            pl.BlockSpec((tm, width), lambda i, *_: (jnp.maximum(i - steps_a, 0), 0)))


def _outproj(mix_a, mix_b, x_a, x_b, w_out, lnw, lnb, wr_hi, wr_lo, br):
    tm = MOE_TILE
    assert x_a.shape[0] % tm == 0 and x_b.shape[0] % tm == 0
    n = x_a.shape[0] + x_b.shape[0]
    steps_a = x_a.shape[0] // tm
    row = lambda width: pl.BlockSpec((tm, width), lambda i: (i, 0))
    full = lambda a: pl.BlockSpec(a.shape, lambda i: (0,) * a.ndim)
    return pl.pallas_call(
        functools.partial(_outproj_body, steps_a=steps_a),
        grid=(n // tm,),
        in_specs=[*_two_group_rows(tm, WIDTH_A, steps_a), *_two_group_rows(tm, WIDTH_B, steps_a),
                  *_two_group_rows(tm, D_MODEL, steps_a), full(w_out), full(lnw), full(lnb), full(wr_hi),
                  full(wr_lo), full(br)],
        out_specs=[row(D_MODEL), pl.BlockSpec((N_EXPERTS, tm), lambda i: (0, i))],
        out_shape=[jax.ShapeDtypeStruct((n, D_MODEL), F32), jax.ShapeDtypeStruct((N_EXPERTS, n), F32)],
        compiler_params=_cparams("parallel"),
        name="outproj",
    )(mix_a[0], mix_b[0], mix_a[1], mix_b[1], x_a, x_b, w_out, lnw, lnb, wr_hi, wr_lo, br)


def _route_body(lg_ref, lpos_ref, gate_ref, tc_ref, base_ref, cnt_ref, cnt_scr, *, tr):
    @pl.when(pl.program_id(0) == 0)
    def _():
        cnt_scr[...] = jnp.zeros_like(cnt_scr)

    l = lg_ref[...]
    rows = lax.broadcasted_iota(I32, (N_EXPERTS, tr), 0)
    vals, hots = [], []
    for j in range(TOP_K):
        m = jnp.max(l, axis=0, keepdims=True)
        idx = jnp.min(jnp.where(l == m, rows, N_EXPERTS), axis=0, keepdims=True)
        hot = rows == idx
        vals.append(m)
        hots.append(hot)
        l = jnp.where(hot, NEG_INF, l)
    es = [jnp.exp(v - vals[0]) for v in vals]
    inv = 1.0 / (es[0] + es[1] + es[2] + es[3])
    for j in range(TOP_K):
        gate_ref[j:j + 1, :] = es[j] * inv
    chosen = jnp.zeros((N_EXPERTS, tr), F32)
    for hot in hots:
        chosen = chosen + jnp.where(hot, 1.0, 0.0)
    r = lax.broadcasted_iota(I32, (tr, tr), 0)
    c = lax.broadcasted_iota(I32, (tr, tr), 1)
    before = jnp.where(r < c, 1.0, 0.0).astype(BF16)
    prior = _dot(chosen.astype(BF16), before)
    units = jnp.ceil(jnp.sum(chosen, axis=1, keepdims=True) * (1.0 / ROW_ALIGN))
    er = lax.broadcasted_iota(I32, (N_EXPERTS, N_EXPERTS), 0)
    ec = lax.broadcasted_iota(I32, (N_EXPERTS, N_EXPERTS), 1)
    earlier = jnp.where(ec < er, 1.0, 0.0).astype(BF16)
    run_len = jnp.broadcast_to(units, (N_EXPERTS, 128)) * ROW_ALIGN
    run_start = _dot(earlier, jnp.broadcast_to(units, (N_EXPERTS, 128)).astype(BF16)) * ROW_ALIGN
    where_in_tile = prior + run_start[:, 0:1]
    for j in range(TOP_K):
        lpos_ref[j:j + 1, :] = jnp.sum(jnp.where(hots[j], where_in_tile, 0.0), axis=0,
                                       keepdims=True).astype(I32)
    tc_ref[0] = run_len.astype(I32)
    base_ref[0] = cnt_scr[...].astype(I32)
    cnt_scr[...] = cnt_scr[...] + run_len
    cnt_ref[...] = cnt_scr[...].astype(I32)


def _route(logits_t):
    n = logits_t.shape[1]
    tr = MOE_TILE
    tiles = n // tr
    tok = lambda rows: pl.BlockSpec((rows, tr), lambda i: (0, i))
    per_tile = pl.BlockSpec((1, N_EXPERTS, 128), lambda i: (i, 0, 0))
    return pl.pallas_call(
        functools.partial(_route_body, tr=tr),
        grid=(tiles,),
        in_specs=[tok(N_EXPERTS)],
        out_specs=[tok(TOP_K), tok(TOP_K), per_tile, per_tile, pl.BlockSpec((N_EXPERTS, 128), lambda i: (0, 0))],
        out_shape=[jax.ShapeDtypeStruct((TOP_K, n), I32), jax.ShapeDtypeStruct((TOP_K, n), F32),
                   jax.ShapeDtypeStruct((tiles, N_EXPERTS, 128), I32),
                   jax.ShapeDtypeStruct((tiles, N_EXPERTS, 128), I32),
                   jax.ShapeDtypeStruct((N_EXPERTS, 128), I32)],
        scratch_shapes=[pltpu.VMEM((N_EXPERTS, 128), F32)],
        compiler_params=_cparams("arbitrary"),
        name="route",
    )(logits_t)


def _for_each_run_piece(tc_ref, g0_ref, t, fn):
    def per_expert(e, local):
        c = tc_ref[t, e]
        glob = g0_ref[t, e]
        lo = local
        for p in RUN_PIECES:
            take = (c & p) != 0

            @pl.when(take)
            def _():
                fn(pl.multiple_of(lo, ROW_ALIGN), pl.multiple_of(glob, ROW_ALIGN), p)

            step = jnp.where(take, p, 0)
            lo = lo + step
            glob = glob + step
        return local + c

    lax.fori_loop(0, N_EXPERTS, per_expert, 0)


def _dispatch_body(tc_ref, g0_ref, seg_ref, lpos_ref, x_ref, buf_ref, stage, zero_scr, zsem, sems, *, n_blocks):
    i = pl.program_id(0)
    slot = i % 2

    def run_copy(slot_, lo, glob, rows):
        return pltpu.make_async_copy(stage.at[slot_, pl.ds(lo, rows), :], buf_ref.at[pl.ds(glob, rows), :],
                                     sems.at[slot_])

    @pl.when(i == 0)
    def _():
        zero_scr[...] = jnp.zeros_like(zero_scr)
        n_used = seg_ref[1, N_EXPERTS - 1] // EXPERT_ROWS

        def block_copy(start):
            return pltpu.make_async_copy(
                zero_scr, buf_ref.at[pl.ds(pl.multiple_of(start, EXPERT_ROWS), EXPERT_ROWS), :], zsem)

        def clears(action):
            for e in range(N_EXPERTS):
                @pl.when(seg_ref[1, e] > seg_ref[0, e])
                def _():
                    action(block_copy(seg_ref[1, e] - EXPERT_ROWS))

            def tail(blk, _):
                action(block_copy(blk * EXPERT_ROWS))
                return 0
            lax.fori_loop(n_used, n_blocks, tail, 0)

        clears(lambda cp: cp.start())
        clears(lambda cp: cp.wait())

    r = lax.broadcasted_iota(I32, (MOE_SLOTS, MOE_TILE), 0)
    onehot = jnp.zeros((MOE_SLOTS, MOE_TILE), F32)
    for j in range(TOP_K):
        onehot = jnp.where(r == lpos_ref[j:j + 1, :], 1.0, onehot)
    stage[slot] = _dot(onehot.astype(BF16), x_ref[...].astype(BF16)).astype(BF16)

    @pl.when(i > 0)
    def _():
        _for_each_run_piece(tc_ref, g0_ref, i - 1,
                            lambda lo, glob, rows: run_copy(1 - slot, lo, glob, rows).wait())

    _for_each_run_piece(tc_ref, g0_ref, i, lambda lo, glob, rows: run_copy(slot, lo, glob, rows).start())

    @pl.when(i == pl.num_programs(0) - 1)
    def _():
        _for_each_run_piece(tc_ref, g0_ref, i, lambda lo, glob, rows: run_copy(slot, lo, glob, rows).wait())


def _dispatch(x1, lpos_t, tc, g0, seg, n_rows):
    n = x1.shape[0]
    return pl.pallas_call(
        functools.partial(_dispatch_body, n_blocks=n_rows // EXPERT_ROWS),
        grid_spec=pltpu.PrefetchScalarGridSpec(
            num_scalar_prefetch=3,
            grid=(n // MOE_TILE,),
            in_specs=[pl.BlockSpec((TOP_K, MOE_TILE), lambda i, *_: (0, i)),
                      pl.BlockSpec((MOE_TILE, D_MODEL), lambda i, *_: (i, 0))],
            out_specs=pl.BlockSpec(memory_space=pl.ANY),
            scratch_shapes=[pltpu.VMEM((2, MOE_SLOTS, D_MODEL), BF16),
                            pltpu.VMEM((EXPERT_ROWS, D_MODEL), BF16),
                            pltpu.SemaphoreType.DMA(()), pltpu.SemaphoreType.DMA((2,))],
        ),
        out_shape=jax.ShapeDtypeStruct((n_rows, D_MODEL), BF16),
        compiler_params=_cparams("arbitrary"),
        name="dispatch",
    )(tc, g0, seg, lpos_t, x1)


def _experts_body(be_ref, nu_ref, x_ref, wgu_ref, bgu_ref, wd_ref, bd_ref, o_ref, wgu_bf, wd_bf):
    i = pl.program_id(0)

    @pl.when(jnp.logical_or(i == 0, be_ref[i] != be_ref[jnp.maximum(i - 1, 0)]))
    def _():
        for r in range(0, D_MODEL, 256):
            wgu_bf[r:r + 256, :] = wgu_ref[0, r:r + 256, :].astype(BF16)
            wd_bf[r:r + 256, :] = wd_ref[0, r:r + 256, :].astype(BF16)

    @pl.when(i < nu_ref[0])
    def _():
        gu = _dot(x_ref[...], wgu_bf[...]) + bgu_ref[0]
        gate = jnp.minimum(gu[:, :D_EXPERT], SWIGLU_LIMIT)
        up = jnp.clip(gu[:, D_EXPERT:], -SWIGLU_LIMIT, SWIGLU_LIMIT)
        hdn = (up + 1.0) * gate * jax.nn.sigmoid(SWIGLU_ALPHA * gate)
        o_ref[...] = (_dot(hdn.astype(BF16), wd_bf[...]) + bd_ref[0]).astype(BF16)

    @pl.when(i >= nu_ref[0])
    def _():
        o_ref[...] = jnp.zeros_like(o_ref)


def _experts(buf, block_e, n_used, w_gu, b_gu, w_down, b_down):
    n_blocks = buf.shape[0] // EXPERT_ROWS
    rows = pl.BlockSpec((EXPERT_ROWS, D_MODEL), lambda i, be, nu: (jnp.minimum(i, nu[0] - 1), 0))
    per_e = lambda a: pl.BlockSpec((1,) + a.shape[1:], lambda i, be, nu: (be[i],) + (0,) * (a.ndim - 1))
    return pl.pallas_call(
        _experts_body,
        grid_spec=pltpu.PrefetchScalarGridSpec(
            num_scalar_prefetch=2,
            grid=(n_blocks,),
            in_specs=[rows, per_e(w_gu), per_e(b_gu), per_e(w_down), per_e(b_down)],
            out_specs=pl.BlockSpec((EXPERT_ROWS, D_MODEL), lambda i, be, nu: (i, 0)),
            scratch_shapes=[pltpu.VMEM((D_MODEL, 2 * D_EXPERT), BF16), pltpu.VMEM((D_EXPERT, D_MODEL), BF16)],
        ),
        out_shape=jax.ShapeDtypeStruct(buf.shape, BF16),
        compiler_params=_cparams("arbitrary"),
        name="experts",
    )(block_e, n_used, buf, w_gu, b_gu, w_down, b_down)


def _final_body(tc_ref, g0_ref, lpos_ref, gate_ref, x1_ref, pa_ref, pb_ref, lnw_ref, lnb_ref, wg_ref, bg_ref, wp_ref,
                eo_ref, ya_ref, yb_ref, stage, sems, *, steps_a):
    i = pl.program_id(0)
    slot = i % 2

    def run_copy(slot_, lo, glob, rows):
        return pltpu.make_async_copy(eo_ref.at[pl.ds(glob, rows), :], stage.at[slot_, pl.ds(lo, rows), :],
                                     sems.at[slot_])

    def fetch(tile, slot_):
        stage[slot_] = jnp.zeros(stage.shape[1:], stage.dtype)
        _for_each_run_piece(tc_ref, g0_ref, tile, lambda lo, glob, rows: run_copy(slot_, lo, glob, rows).start())

    @pl.when(i == 0)
    def _():
        fetch(0, 0)

    @pl.when(i + 1 < pl.num_programs(0))
    def _():
        fetch(i + 1, 1 - slot)

    _for_each_run_piece(tc_ref, g0_ref, i, lambda lo, glob, rows: run_copy(slot, lo, glob, rows).wait())

    r = lax.broadcasted_iota(I32, (MOE_TILE, MOE_SLOTS), 1)
    g = jnp.zeros((MOE_TILE, MOE_SLOTS), F32)
    for j in range(TOP_K):
        g = jnp.where(r == lpos_ref[:, j:j + 1], gate_ref[:, j:j + 1], g)
    moe = _dot(g.astype(BF16), stage[slot])
    x2 = _layer_norm(DN_ALPHA * x1_ref[...] + moe, lnw_ref[...], lnb_ref[...])
    ple_gate = jax.nn.sigmoid(_dot(x2.astype(BF16), wg_ref[...]) + bg_ref[...])
    p = jnp.where(i < steps_a, pa_ref[...], pb_ref[...])
    y = x2 + ple_gate * _dot(p.astype(BF16), wp_ref[...])

    @pl.when(i < steps_a)
    def _():
        ya_ref[...] = y

    @pl.when(i >= steps_a)
    def _():
        yb_ref[...] = y


def _final(tc, g0, lpos, gate, x1, p_a, p_b, lnw, lnb, w_gate, b_gate, w_ple, expert_out):
    n = x1.shape[0]
    steps_a = p_a.shape[0] // MOE_TILE
    row = lambda width: pl.BlockSpec((MOE_TILE, width), lambda i, *_: (i, 0))
    full = lambda a: pl.BlockSpec(a.shape, lambda i, *_: (0,) * a.ndim)
    return pl.pallas_call(
        functools.partial(_final_body, steps_a=steps_a),
        grid_spec=pltpu.PrefetchScalarGridSpec(
            num_scalar_prefetch=2,
            grid=(n // MOE_TILE,),
            in_specs=[row(TOP_K), row(TOP_K), row(D_MODEL), *_two_group_rows(MOE_TILE, D_PLE, steps_a), full(lnw),
                      full(lnb), full(w_gate), full(b_gate), full(w_ple), pl.BlockSpec(memory_space=pl.ANY)],
            out_specs=list(_two_group_rows(MOE_TILE, D_MODEL, steps_a)),
            scratch_shapes=[pltpu.VMEM((2, MOE_SLOTS, D_MODEL), BF16), pltpu.SemaphoreType.DMA((2,))],
        ),
        out_shape=[jax.ShapeDtypeStruct((p_a.shape[0], D_MODEL), F32),
                   jax.ShapeDtypeStruct((p_b.shape[0], D_MODEL), F32)],
        compiler_params=_cparams("arbitrary"),
        name="combine_final",
    )(tc, g0, lpos, gate, x1, p_a, p_b, lnw, lnb, w_gate, b_gate, w_ple, expert_out)


def _round_up(a, m):
    return -(-a // m) * m


def _token_mixers(x, hist_k, hist_v, hist_kidx, s0, pos0, wts):
    b, t, _ = x.shape
    n = b * t
    x2d = x.reshape(n, D_MODEL)

    qa, ka, va, ka_bf, va_bf, qi, ki, ki_bf, wi, hb = _inproj(x2d, wts["w_in"], wts["idx_lnw"], wts["idx_lnb"])

    k_new = ka_bf.reshape(b, t, N_KV_A * HEAD_DIM_A)
    v_new = va_bf.reshape(b, t, N_KV_A * HEAD_DIM_A)
    ki_new = ki_bf.reshape(b, t, IDX_DIM)
    q_ops = (qa.reshape(N_HEADS_A, b, t, HEAD_DIM_A), qi.reshape(N_IDX_HEADS, b, t, IDX_DIM),
             wi.reshape(b, t, N_IDX_HEADS))
    past = 0 if hist_k is None else hist_k.shape[1]
    s_real = past + t
    topk = min(TOPK_MAX, s_real // 4)
    if past:
        k_all = jnp.concatenate([hist_k.reshape(b, past, -1).astype(BF16), k_new], axis=1)
        v_all = jnp.concatenate([hist_v.reshape(b, past, -1).astype(BF16), v_new], axis=1)
        ki_all = jnp.concatenate([hist_kidx.astype(BF16), ki_new], axis=1)
    else:
        k_all, v_all, ki_all = k_new, v_new, ki_new
    pad = ((0, 0), (0, _round_up(s_real, KEY_BLOCK) - s_real), (0, 0))
    keys = tuple(jnp.pad(a, pad) for a in (ki_all, k_all, v_all))
    oa = _dsa(*q_ops, keys, s_real=s_real, pos0=pos0, topk=topk)

    ob, s_fin = _hgrn(hb.reshape(b, t, 4 * WIDTH_B), s0, wts["lb"], wts["hgrn_nw"])
    return (oa, ob.reshape(n, WIDTH_B), ka.reshape(b, t, N_KV_A, HEAD_DIM_A), va.reshape(b, t, N_KV_A, HEAD_DIM_A),
            ki.reshape(b, t, IDX_DIM), s_fin)


def _channel_mixer(mix_a, mix_b, x_a, x_b, p_a, p_b, wts):
    shape_a, shape_b = x_a.shape, x_b.shape
    x_a, x_b = x_a.reshape(-1, D_MODEL), x_b.reshape(-1, D_MODEL)
    n = x_a.shape[0] + x_b.shape[0]
    x1, logits_t = _outproj(mix_a, mix_b, x_a, x_b, wts["w_out"], wts["ln1_w"], wts["ln1_b"],
                            wts["wr_hi"], wts["wr_lo"], wts["b_router"])

    tiles = n // MOE_TILE
    lpos_t, gate_t, tile_runs, tile_base, counts = _route(logits_t)
    tc, counts = tile_runs[:, :, 0], counts[:, 0]
    padded = (counts + EXPERT_ROWS - 1) // EXPERT_ROWS * EXPERT_ROWS
    seg_end = jnp.cumsum(padded)
    seg_start = seg_end - padded
    g0 = (seg_start[None, :] + tile_base[:, :, 0]).astype(I32)
    n_blocks = -(-(n * TOP_K + N_EXPERTS * tiles * (ROW_ALIGN - 1) + N_EXPERTS * (EXPERT_ROWS - 1))
                 // EXPERT_ROWS)
    block_row0 = jnp.arange(n_blocks, dtype=I32) * EXPERT_ROWS
    block_e = jnp.minimum(jnp.sum((seg_end[None, :] <= block_row0[:, None]).astype(I32), axis=1), N_EXPERTS - 1)
    n_used = (seg_end[-1:] // EXPERT_ROWS).astype(I32)
    seg = jnp.stack([seg_start, seg_end]).astype(I32)
    buf = _dispatch(x1, lpos_t, tc, g0, seg, n_blocks * EXPERT_ROWS)
    expert_out = _experts(buf, block_e, n_used, wts["w_gu"], wts["b_gu"], wts["w_down"], wts["b_down"])
    y_a, y_b = _final(tc, g0, lpos_t.T, gate_t.T, x1, p_a.reshape(-1, D_PLE), p_b.reshape(-1, D_PLE),
                      wts["ln2_w"], wts["ln2_b"], wts["w_ple_gate"], wts["b_ple_gate"], wts["w_ple"], expert_out)
    return y_a.reshape(shape_a), y_b.reshape(shape_b)


def _prep_weights(w_in, w_out, idx_k_norm_w, idx_k_norm_b, lb, hgrn_norm_w, ln1_w, ln1_b, w_router, b_router,
                  w_gu, b_gu, w_down, b_down, ln2_w, ln2_b, w_ple, w_ple_gate, b_ple_gate):
    n_a = COL_KW + IDX_DIM + N_IDX_HEADS
    w_pad = jnp.concatenate(
        [w_in[:, :n_a], jnp.zeros((D_MODEL, COL_HB - n_a), w_in.dtype), w_in[:, n_a:]], axis=1).astype(BF16)
    wr_t = w_router.T
    wr_hi = wr_t.astype(BF16)
    row = lambda a: a.reshape(1, -1)
    return dict(
        w_in=w_pad, idx_lnw=row(idx_k_norm_w), idx_lnb=row(idx_k_norm_b), lb=row(lb), hgrn_nw=row(hgrn_norm_w),
        w_out=w_out.astype(BF16), ln1_w=row(ln1_w), ln1_b=row(ln1_b),
        wr_hi=wr_hi, wr_lo=(wr_t - wr_hi.astype(F32)).astype(BF16), b_router=b_router.reshape(N_EXPERTS, 1),
        w_gu=w_gu, b_gu=b_gu.reshape(N_EXPERTS, 1, 2 * D_EXPERT),
        w_down=w_down, b_down=b_down.reshape(N_EXPERTS, 1, D_MODEL),
        ln2_w=row(ln2_w), ln2_b=row(ln2_b), w_ple=w_ple.astype(BF16), w_ple_gate=w_ple_gate.astype(BF16),
        b_ple_gate=row(b_ple_gate))


def kernel(x_prompt, x_sample, cache_k, cache_v, cache_kidx, state_hgrn, p_prompt, p_sample, w_in, w_out,
           idx_k_norm_w, idx_k_norm_b, hgrn_lb_logits, hgrn_norm_w, ln1_w, ln1_b, w_router, b_router, w_gu, b_gu,
           w_down, b_down, ln2_w, ln2_b, w_ple, w_ple_gate, b_ple_gate):
    lb_all = jnp.cumsum(jax.nn.softmax(hgrn_lb_logits.astype(F32), axis=0), axis=0)
    xp, xs = x_prompt, x_sample
    outs = [[] for _ in range(8)]
    for i in range(DEPTH):
        wts = _prep_weights(w_in[i], w_out[i], idx_k_norm_w[i], idx_k_norm_b[i], lb_all[i], hgrn_norm_w[i],
                            ln1_w[i], ln1_b[i], w_router[i], b_router[i], w_gu[i], b_gu[i], w_down[i], b_down[i],
                            ln2_w[i], ln2_b[i], w_ple[i], w_ple_gate[i], b_ple_gate[i])
        s0p = jnp.zeros((xp.shape[0], N_HEADS_B, HEAD_DIM_B, HEAD_DIM_B), F32)
        mix_p = _token_mixers(xp, None, None, None, s0p, 0, wts)
        mix_s = _token_mixers(xs, cache_k[i], cache_v[i], cache_kidx[i], state_hgrn[i], cache_k.shape[2], wts)
        xp, xs = _channel_mixer(mix_p, mix_s, xp, xs, p_prompt[i], p_sample[i], wts)
        for lst, val in zip(outs, mix_p[2:] + mix_s[2:]):
            lst.append(val)
    return (xp, xs) + tuple(jnp.stack(l) for l in outs)
```

```python
import functools
import itertools

import jax
import jax.numpy as jnp
from jax import lax
from jax.experimental import pallas as pl
from jax.experimental.pallas import tpu as pltpu

F32 = jnp.float32
BF16 = jnp.bfloat16
I32 = jnp.int32

D_MODEL = 1024
CHUNK = 64
CHUNK_SHIFT = 6
WIDTH_A = 512
HEAD_DIM_A = 64
N_HEADS_A = 8
N_KV_A = 2
N_IDX_HEADS = 4
IDX_DIM = 64
IDX_SCALE = IDX_DIM ** -0.5 * N_IDX_HEADS ** -0.5
TOPK_MAX = 256
WIDTH_B = 512
HEAD_DIM_B = 128
N_HEADS_B = 4
N_EXPERTS = 32
TOP_K = 4
D_EXPERT = 1024
SWIGLU_LIMIT = 7.0
SWIGLU_ALPHA = 1.702
D_PLE = 256
LN_EPS = 1e-5
RMS_EPS = 1e-6
DEPTH = 1
DN_ALPHA = (2 * DEPTH) ** 0.25

COL_QA, COL_KA, COL_VA, COL_QI, COL_KW, COL_HB, COL_END = 0, 512, 640, 768, 1024, 1152, 3200

VMEM_LIMIT = 56 * 1024 * 1024
INPROJ_ROWS = 1024
KEY_BLOCK = 256
BLOCKS_PER_TRIP = 8
V_ROWS = 80
LOG2_E = 1.4426950408889634
HGRN_SUB = 16
EXPERT_ROWS = 512
MOE_TILE = 512
ROW_ALIGN = 16
RUN_PIECES = (512, 256, 128, 64, 32, 16)
MOE_SLOTS = -(-(TOP_K * MOE_TILE + N_EXPERTS * (ROW_ALIGN - 1)) // 256) * 256
NEG_INF = float("-inf")
INT_MIN = -(2 ** 31)


def _cparams(*sem):
    return pltpu.CompilerParams(dimension_semantics=sem, vmem_limit_bytes=VMEM_LIMIT)


def _dot(a, b):
    return jnp.dot(a, b, preferred_element_type=F32)


def _dot_nt(a, b):
    return lax.dot_general(a, b, (((1,), (1,)), ((), ())), preferred_element_type=F32)


def _dot_tn(a, b):
    return lax.dot_general(a, b, (((0,), (0,)), ((), ())), preferred_element_type=F32)


def _layer_norm(z, w, b):
    mu = jnp.mean(z, axis=-1, keepdims=True)
    d = z - mu
    var = jnp.mean(d * d, axis=-1, keepdims=True)
    return d * lax.rsqrt(var + LN_EPS) * w + b


def _inproj_body(x_ref, w_ref, lnw_ref, lnb_ref, qa_ref, ka_ref, va_ref, kab_ref, vab_ref, qi_ref, ki_ref, kib_ref,
                 wi_ref, hb_ref):
    xb = x_ref[...].astype(BF16)

    def mm(c0, c1):
        return _dot(xb, w_ref[:, c0:c1])

    qa = (mm(COL_QA, COL_KA) * (HEAD_DIM_A ** -0.5 * LOG2_E)).astype(BF16)
    for h in range(N_HEADS_A):
        qa_ref[h] = qa[:, h * HEAD_DIM_A:(h + 1) * HEAD_DIM_A]
    for src, out3_ref, bf_ref in ((mm(COL_KA, COL_VA), ka_ref, kab_ref), (mm(COL_VA, COL_QI), va_ref, vab_ref)):
        for g in range(N_KV_A):
            out3_ref[:, g, :] = src[:, g * HEAD_DIM_A:(g + 1) * HEAD_DIM_A]
        bf_ref[...] = src.astype(BF16)
    qi = mm(COL_QI, COL_KW).astype(BF16)
    for h in range(N_IDX_HEADS):
        qi_ref[h] = qi[:, h * IDX_DIM:(h + 1) * IDX_DIM]
    kw = mm(COL_KW, COL_HB)
    ki = _layer_norm(kw[:, :IDX_DIM], lnw_ref[...], lnb_ref[...])
    ki_ref[...] = ki
    kib_ref[...] = ki.astype(BF16)
    wi_ref[...] = kw[:, IDX_DIM:IDX_DIM + N_IDX_HEADS]
    hb_ref[...] = mm(COL_HB, COL_END)


def _inproj(x2d, w_pad, lnw, lnb):
    n = x2d.shape[0]
    tm = min(INPROJ_ROWS, n)
    row = lambda width: pl.BlockSpec((tm, width), lambda i: (i, 0))
    full = lambda a: pl.BlockSpec(a.shape, lambda i: (0,) * a.ndim)
    heads = lambda nh, d: pl.BlockSpec((nh, tm, d), lambda i: (0, i, 0))
    kv = N_KV_A * HEAD_DIM_A
    kv3 = pl.BlockSpec((tm, N_KV_A, HEAD_DIM_A), lambda i: (i, 0, 0))
    sds = jax.ShapeDtypeStruct
    return pl.pallas_call(
        _inproj_body,
        grid=(n // tm,),
        in_specs=[row(D_MODEL), full(w_pad), full(lnw), full(lnb)],
        out_specs=[heads(N_HEADS_A, HEAD_DIM_A), kv3, kv3, row(kv), row(kv), heads(N_IDX_HEADS, IDX_DIM),
                   row(IDX_DIM), row(IDX_DIM), row(N_IDX_HEADS), row(4 * WIDTH_B)],
        out_shape=[sds((N_HEADS_A, n, HEAD_DIM_A), BF16), sds((n, N_KV_A, HEAD_DIM_A), F32),
                   sds((n, N_KV_A, HEAD_DIM_A), F32), sds((n, kv), BF16), sds((n, kv), BF16),
                   sds((N_IDX_HEADS, n, IDX_DIM), BF16), sds((n, IDX_DIM), F32), sds((n, IDX_DIM), BF16),
                   sds((n, N_IDX_HEADS), F32), sds((n, 4 * WIDTH_B), F32)],
        compiler_params=_cparams("parallel"),
        name="inproj",
    )(x2d, w_pad, lnw, lnb)


def _order_bits_to_f32(u):
    key = u ^ INT_MIN
    bits = key ^ ((key >> 31) & 0x7FFFFFFF)
    f = lax.bitcast_convert_type(bits, F32)
    return jnp.where(u >= 0, jnp.where(u <= 0x007FFFFF, NEG_INF, f), f)


def _dsa_body(qa_ref, qi_ref, wi_ref, kidx_ref, k_ref, vt_ref, o_ref, sc_scr, bias_scr, term_scr, s_scr, acc_scr,
              *, s_pad, s_real, q_rows, pair, pos0, topk):
    kb_rows = KEY_BLOCK
    heads_per_kv = N_HEADS_A // N_KV_A
    qb = pair * q_rows
    cdim = pair * HEAD_DIM_A
    v_rows = pair * V_ROWS
    j = pl.program_id(1)
    q_lo = pos0 + j * q_rows
    lane = lax.broadcasted_iota(I32, (1, qb), 1)
    q_chunk = (q_lo + (lane & (q_rows - 1))) >> CHUNK_SHIFT
    k_lim = (((q_lo + q_rows - 1) >> CHUNK_SHIFT) + 1) * CHUNK
    nkb = jnp.minimum(s_pad // kb_rows, (k_lim + kb_rows - 1) // kb_rows)

    def rows_of(kb):
        return pl.ds(pl.multiple_of(kb * kb_rows, kb_rows), kb_rows)

    def fold8(a):
        return a.reshape(kb_rows // 8, 8, qb)

    def over_blocks(body, init):
        def group(first, width, carry):
            for u in range(width):
                carry = body(first + u, carry)
            return carry
        carry = lax.fori_loop(0, nkb // BLOCKS_PER_TRIP,
                              lambda i, c: group(i * BLOCKS_PER_TRIP, BLOCKS_PER_TRIP, c), init)
        done = (nkb // BLOCKS_PER_TRIP) * BLOCKS_PER_TRIP
        width = BLOCKS_PER_TRIP // 2
        while width:
            carry = lax.cond((nkb & width) != 0, functools.partial(group, done, width), lambda c: c, carry)
            done = done + (nkb & width)
            width //= 2
        return carry

    def score_blk(kb, _):
        rows = rows_of(kb)
        kidx = kidx_ref[0, rows, :]
        for h in range(N_IDX_HEADS):
            term_scr[h] = jnp.maximum(_dot_nt(kidx, qi_ref[h, 0]), 0.0) * wi_ref[0, h:h + 1, :]
        total = (term_scr[0] + term_scr[1]) + (term_scr[2] + term_scr[3])
        spos = kb * kb_rows + lax.broadcasted_iota(I32, (kb_rows, qb), 0)
        sc = jnp.where((spos >> CHUNK_SHIFT) <= q_chunk, total * IDX_SCALE + 0.0, NEG_INF)
        if s_real < s_pad:
            sc = jnp.where(spos < s_real, sc, NEG_INF)
        sc_scr[rows, :] = sc
        return 0

    over_blocks(score_blk, 0)

    def count(thr, strict):
        def body(kb, acc):
            blk = sc_scr[rows_of(kb), :]
            hit = (blk > thr) if strict else (blk >= thr)
            return acc + jnp.sum(fold8(jnp.where(hit, 1, 0).astype(I32)), axis=0)
        acc = over_blocks(body, jnp.zeros((8, qb), I32))
        return jnp.sum(acc, axis=0, keepdims=True)

    def bit_step(i, carry):
        prefix, cnt = carry
        cand = prefix | lax.shift_left(jnp.int32(1), 31 - i)
        c = count(_order_bits_to_f32(cand), False)
        take = c >= topk
        return jnp.where(take, cand, prefix), jnp.where(take, c, cnt)

    prefix, cnt_ge = lax.fori_loop(0, 32, bit_step,
                                   (jnp.zeros((1, qb), I32), jnp.full((1, qb), nkb * kb_rows, I32)))
    tau = _order_bits_to_f32(prefix)
    finite_tau = tau > NEG_INF
    tau_floor = jnp.maximum(tau, jnp.finfo(F32).min)
    tie_lanes = jnp.where(finite_tau, jnp.where(cnt_ge > topk, 1, 0), 0)
    has_ties = jnp.max(tie_lanes) > 0

    @pl.when(jnp.logical_not(has_ties))
    def _():
        def body(kb, _):
            blk = sc_scr[rows_of(kb), :]
            bias_scr[rows_of(kb), :] = jnp.where(blk >= tau_floor, 0.0, NEG_INF)
            return 0
        over_blocks(body, 0)

    @pl.when(has_ties)
    def _():
        need = jnp.where(finite_tau, (topk - count(tau, True)).astype(F32), 0.0)
        r = lax.broadcasted_iota(I32, (kb_rows, kb_rows), 0)
        c = lax.broadcasted_iota(I32, (kb_rows, kb_rows), 1)
        tril = jnp.where(r >= c, 1.0, 0.0).astype(BF16)

        def body(kb, seen):
            blk = sc_scr[rows_of(kb), :]
            eq = blk == tau
            rank = _dot(tril, jnp.where(eq, 1.0, 0.0).astype(BF16)) + seen
            tie_bias = jnp.where(eq, jnp.where(rank <= need, 0.0, NEG_INF), NEG_INF)
            bias_scr[rows_of(kb), :] = jnp.where(blk > tau, 0.0, tie_bias)
            return rank[kb_rows - 1:kb_rows, :]
        lax.fori_loop(0, nkb, body, jnp.zeros((1, qb), F32))

    def pass1(kb, m8):
        rows = rows_of(kb)
        bias = bias_scr[rows, :]
        out = []
        for h in range(N_HEADS_A):
            g = h // heads_per_kv
            s = _dot_nt(k_ref[0, rows, g * cdim:(g + 1) * cdim], qa_ref[h, 0]) + bias
            s_scr[h, rows, :] = s
            out.append(jnp.maximum(m8[h], jnp.max(fold8(s), axis=0)))
        return tuple(out)

    m8 = over_blocks(pass1, tuple(jnp.full((8, qb), NEG_INF, F32) for _ in range(N_HEADS_A)))
    m = [jnp.max(x, axis=0, keepdims=True) for x in m8]

    acc_scr[...] = jnp.zeros_like(acc_scr)

    def pass2(kb, _):
        rows = rows_of(kb)
        for h in range(N_HEADS_A):
            g = h // heads_per_kv
            p = jnp.exp2(s_scr[h, rows, :] - m[h]).astype(BF16)
            acc_scr[h] += _dot(vt_ref[0, g * v_rows:(g + 1) * v_rows, rows], p)
        return 0

    over_blocks(pass2, 0)
    for h in range(N_HEADS_A):
        a = acc_scr[h]
        num, den = a[:HEAD_DIM_A], a[HEAD_DIM_A:HEAD_DIM_A + 1]
        for mb in range(1, pair):
            mine = lane >= mb * q_rows
            r0 = mb * V_ROWS
            num = jnp.where(mine, a[r0:r0 + HEAD_DIM_A], num)
            den = jnp.where(mine, a[r0 + HEAD_DIM_A:r0 + HEAD_DIM_A + 1], den)
        o_ref[0, h * HEAD_DIM_A:(h + 1) * HEAD_DIM_A, :] = (num * (1.0 / den)).astype(BF16)


def _dsa_pairing(b, t):
    return 2 if (2 * t <= 128 and b % 2 == 0) else 1


def _dsa(qa_h, qi_h, wi, keys, *, s_real, pos0, topk):
    _, b, t, _ = qa_h.shape
    s_pad = keys[0].shape[1]
    q_rows = min(256, t)
    pair = _dsa_pairing(b, t)
    bp = b // pair
    lanes, cdim, v_rows = pair * q_rows, pair * HEAD_DIM_A, pair * V_ROWS

    def block_diag(q):
        if pair == 1:
            return q
        nh, d = q.shape[0], q.shape[3]
        eye = jnp.eye(pair, dtype=q.dtype)
        q6 = q.reshape(nh, bp, pair, t, 1, d) * eye.reshape(1, 1, pair, 1, pair, 1)
        return q6.reshape(nh, bp, pair * t, pair * d)

    wi_op = wi.reshape(bp, pair, t, N_IDX_HEADS).transpose(0, 3, 1, 2).reshape(bp, N_IDX_HEADS, pair * t)
    ki_all, k_all, v_all = keys
    ki_op = ki_all.reshape(bp, pair, s_pad, IDX_DIM).transpose(0, 2, 1, 3).reshape(bp, s_pad, cdim)
    k_op = (k_all.reshape(bp, pair, s_pad, N_KV_A, HEAD_DIM_A).transpose(0, 2, 3, 1, 4)
            .reshape(bp, s_pad, N_KV_A * cdim))
    vt = v_all.reshape(b, s_pad, N_KV_A, HEAD_DIM_A).transpose(0, 2, 3, 1)
    vt = jnp.concatenate([vt, jnp.ones((b, N_KV_A, V_ROWS - HEAD_DIM_A, s_pad), vt.dtype)], axis=2)
    vt_op = (vt.reshape(bp, pair, N_KV_A, V_ROWS, s_pad).transpose(0, 2, 1, 3, 4)
             .reshape(bp, N_KV_A * v_rows, s_pad))

    body = functools.partial(_dsa_body, s_pad=s_pad, s_real=s_real, q_rows=q_rows, pair=pair, pos0=pos0,
                             topk=topk)
    o_t = pl.pallas_call(
        body,
        grid=(bp, t // q_rows),
        in_specs=[
            pl.BlockSpec((N_HEADS_A, 1, lanes, cdim), lambda i, j: (0, i, j, 0)),
            pl.BlockSpec((N_IDX_HEADS, 1, lanes, cdim), lambda i, j: (0, i, j, 0)),
            pl.BlockSpec((1, N_IDX_HEADS, lanes), lambda i, j: (i, 0, j)),
            pl.BlockSpec((1, s_pad, cdim), lambda i, j: (i, 0, 0)),
            pl.BlockSpec((1, s_pad, N_KV_A * cdim), lambda i, j: (i, 0, 0)),
            pl.BlockSpec((1, N_KV_A * v_rows, s_pad), lambda i, j: (i, 0, 0)),
        ],
        out_specs=pl.BlockSpec((1, WIDTH_A, lanes), lambda i, j: (i, 0, j)),
        out_shape=jax.ShapeDtypeStruct((bp, WIDTH_A, pair * t), BF16),
        scratch_shapes=[pltpu.VMEM((s_pad, lanes), F32), pltpu.VMEM((s_pad, lanes), F32),
                        pltpu.VMEM((N_IDX_HEADS, KEY_BLOCK, lanes), F32),
                        pltpu.VMEM((N_HEADS_A, s_pad, lanes), F32), pltpu.VMEM((N_HEADS_A, v_rows, lanes), F32)],
        compiler_params=_cparams("parallel", "parallel"),
        name="dsa",
    )(block_diag(qa_h), block_diag(qi_h), wi_op, ki_op, k_op, vt_op)
    return o_t.reshape(bp, WIDTH_A, pair, t).transpose(0, 2, 3, 1).reshape(b * t, WIDTH_A)


def _split3(a):
    hi = a.astype(BF16)
    r1 = a - hi.astype(F32)
    mid = r1.astype(BF16)
    lo = (r1 - mid.astype(F32)).astype(BF16)
    return hi, mid, lo


def _hgrn_body(hb_ref, s0_ref, lb_ref, nw_ref, ob_ref, sfin_ref, state_scr, *, tb, nb):
    t = pl.program_id(1)

    @pl.when(t == 0)
    def _():
        for bi, h in itertools.product(range(nb), range(N_HEADS_B)):
            state_scr[bi, h] = s0_ref[bi, h].T

    lb = lb_ref[...]
    r = lax.broadcasted_iota(I32, (CHUNK, CHUNK), 0)
    c = lax.broadcasted_iota(I32, (CHUNK, CHUNK), 1)
    causal = r >= c
    tril = jnp.where(causal, 1.0, 0.0).astype(BF16)

    for ci, bi in itertools.product(range(tb // CHUNK), range(nb)):
        rows = slice(ci * CHUNK, (ci + 1) * CHUNK)
        f = lb + (1.0 - lb) * jax.nn.sigmoid(hb_ref[bi, rows, WIDTH_B:2 * WIDTH_B])
        parts = _split3(jnp.log(f))
        bcum = _dot(tril, parts[0]) + _dot(tril, parts[1]) + _dot(tril, parts[2])
        for h in range(N_HEADS_B):
            lanes = slice(h * HEAD_DIM_B, (h + 1) * HEAD_DIM_B)
            q = hb_ref[bi, rows, lanes]
            k = 1.0 - f[:, lanes]
            v = hb_ref[bi, rows, 2 * WIDTH_B + h * HEAD_DIM_B:2 * WIDTH_B + (h + 1) * HEAD_DIM_B]
            gate = hb_ref[bi, rows, 3 * WIDTH_B + h * HEAD_DIM_B:3 * WIDTH_B + (h + 1) * HEAD_DIM_B]
            bh = bcum[:, lanes]
            b_last = bh[CHUNK - 1:CHUNK, :]
            vb = v.astype(BF16)
            state_t = state_scr[bi, h]
            o_inter = _dot_nt((q * jnp.exp(bh)).astype(BF16), state_t.astype(BF16))
            a_rows = []
            for i in range(CHUNK // HGRN_SUB):
                lo, hi = i * HGRN_SUB, (i + 1) * HGRN_SUB
                ref = bh[lo - 1:lo, :] if i else jnp.zeros((1, HEAD_DIM_B), F32)
                qs = (q[lo:hi] * jnp.exp(bh[lo:hi] - ref)).astype(BF16)
                ks = (k * jnp.exp(ref - bh)).astype(BF16)
                a_rows.append(_dot_nt(qs, ks))
            a = jnp.where(causal, jnp.concatenate(a_rows, axis=0), 0.0)
            o = o_inter + _dot(a.astype(BF16), vb)
            kdec = (k * jnp.exp(b_last - bh)).astype(BF16)
            state_scr[bi, h] = jnp.exp(b_last) * state_t + _dot_tn(vb, kdec)
            ms = jnp.mean(o * o, axis=-1, keepdims=True)
            y = o * lax.rsqrt(ms + RMS_EPS) * nw_ref[...] * (gate * jax.nn.sigmoid(gate))
            ob_ref[bi, rows, lanes] = y.astype(BF16)

    @pl.when(t == pl.num_programs(1) - 1)
    def _():
        for bi, h in itertools.product(range(nb), range(N_HEADS_B)):
            sfin_ref[bi, h] = state_scr[bi, h].T


def _hgrn(hb, s0, lb, nw):
    b, t, _ = hb.shape
    tb = min(256, t)
    nb = next(m for m in (4, 2, 1) if b % m == 0)
    return pl.pallas_call(
        functools.partial(_hgrn_body, tb=tb, nb=nb),
        grid=(b // nb, t // tb),
        in_specs=[
            pl.BlockSpec((nb, tb, 4 * WIDTH_B), lambda i, j: (i, j, 0)),
            pl.BlockSpec((nb, N_HEADS_B, HEAD_DIM_B, HEAD_DIM_B), lambda i, j: (i, 0, 0, 0)),
            pl.BlockSpec((1, WIDTH_B), lambda i, j: (0, 0)),
            pl.BlockSpec((1, HEAD_DIM_B), lambda i, j: (0, 0)),
        ],
        out_specs=[
            pl.BlockSpec((nb, tb, WIDTH_B), lambda i, j: (i, j, 0)),
            pl.BlockSpec((nb, N_HEADS_B, HEAD_DIM_B, HEAD_DIM_B), lambda i, j: (i, 0, 0, 0)),
        ],
        out_shape=[jax.ShapeDtypeStruct((b, t, WIDTH_B), BF16),
                   jax.ShapeDtypeStruct((b, N_HEADS_B, HEAD_DIM_B, HEAD_DIM_B), F32)],
        scratch_shapes=[pltpu.VMEM((nb, N_HEADS_B, HEAD_DIM_B, HEAD_DIM_B), F32)],
        compiler_params=_cparams("parallel", "arbitrary"),
        name="hgrn",
    )(hb, s0, lb, nw)


def _outproj_body(oaa_ref, oab_ref, oba_ref, obb_ref, xa_ref, xb_ref, w_ref, lnw_ref, lnb_ref, wrh_ref, wrl_ref, br_ref,
                  x1_ref, lg_ref, *, steps_a):
    first = pl.program_id(0) < steps_a
    pick = lambda a_ref, b_ref: jnp.where(first, a_ref[...], b_ref[...])
    x = pick(xa_ref, xb_ref)
    y = _dot(pick(oaa_ref, oab_ref), w_ref[:WIDTH_A, :]) + _dot(pick(oba_ref, obb_ref), w_ref[WIDTH_A:, :])
    x1 = _layer_norm(DN_ALPHA * x + y, lnw_ref[...], lnb_ref[...])
    x1_ref[...] = x1
    hi = x1.astype(BF16)
    lo = (x1 - hi.astype(F32)).astype(BF16)
    lg_ref[...] = (_dot_nt(wrh_ref[...], hi) + _dot_nt(wrh_ref[...], lo) + _dot_nt(wrl_ref[...], hi)
                   + br_ref[...])


def _two_group_rows(tm, width, steps_a):
    return (pl.BlockSpec((tm, width), lambda i, *_: (jnp.minimum(i, steps_a - 1), 0)),
            pl.BlockSpec((tm, width), lambda i, *_: (jnp.maximum(i - steps_a, 0), 0)))


def _outproj(mix_a, mix_b, x_a, x_b, w_out, lnw, lnb, wr_hi, wr_lo, br):
    tm = MOE_TILE
    assert x_a.shape[0] % tm == 0 and x_b.shape[0] % tm == 0
    n = x_a.shape[0] + x_b.shape[0]
    steps_a = x_a.shape[0] // tm
    row = lambda width: pl.BlockSpec((tm, width), lambda i: (i, 0))
    full = lambda a: pl.BlockSpec(a.shape, lambda i: (0,) * a.ndim)
    return pl.pallas_call(
        functools.partial(_outproj_body, steps_a=steps_a),
        grid=(n // tm,),
        in_specs=[*_two_group_rows(tm, WIDTH_A, steps_a), *_two_group_rows(tm, WIDTH_B, steps_a),
                  *_two_group_rows(tm, D_MODEL, steps_a), full(w_out), full(lnw), full(lnb), full(wr_hi),
                  full(wr_lo), full(br)],
        out_specs=[row(D_MODEL), pl.BlockSpec((N_EXPERTS, tm), lambda i: (0, i))],
        out_shape=[jax.ShapeDtypeStruct((n, D_MODEL), F32), jax.ShapeDtypeStruct((N_EXPERTS, n), F32)],
        compiler_params=_cparams("parallel"),
        name="outproj",
    )(mix_a[0], mix_b[0], mix_a[1], mix_b[1], x_a, x_b, w_out, lnw, lnb, wr_hi, wr_lo, br)


def _route_body(lg_ref, lpos_ref, gate_ref, tc_ref, base_ref, cnt_ref, cnt_scr, *, tr):
    @pl.when(pl.program_id(0) == 0)
    def _():
        cnt_scr[...] = jnp.zeros_like(cnt_scr)

    l = lg_ref[...]
    rows = lax.broadcasted_iota(I32, (N_EXPERTS, tr), 0)
    vals, hots = [], []
    for j in range(TOP_K):
        m = jnp.max(l, axis=0, keepdims=True)
        idx = jnp.min(jnp.where(l == m, rows, N_EXPERTS), axis=0, keepdims=True)
        hot = rows == idx
        vals.append(m)
        hots.append(hot)
        l = jnp.where(hot, NEG_INF, l)
    es = [jnp.exp(v - vals[0]) for v in vals]
    inv = 1.0 / (es[0] + es[1] + es[2] + es[3])
    for j in range(TOP_K):
        gate_ref[j:j + 1, :] = es[j] * inv
    chosen = jnp.zeros((N_EXPERTS, tr), F32)
    for hot in hots:
        chosen = chosen + jnp.where(hot, 1.0, 0.0)
    r = lax.broadcasted_iota(I32, (tr, tr), 0)
    c = lax.broadcasted_iota(I32, (tr, tr), 1)
    before = jnp.where(r < c, 1.0, 0.0).astype(BF16)
    prior = _dot(chosen.astype(BF16), before)
    units = jnp.ceil(jnp.sum(chosen, axis=1, keepdims=True) * (1.0 / ROW_ALIGN))
    er = lax.broadcasted_iota(I32, (N_EXPERTS, N_EXPERTS), 0)
    ec = lax.broadcasted_iota(I32, (N_EXPERTS, N_EXPERTS), 1)
    earlier = jnp.where(ec < er, 1.0, 0.0).astype(BF16)
    run_len = jnp.broadcast_to(units, (N_EXPERTS, 128)) * ROW_ALIGN
    run_start = _dot(earlier, jnp.broadcast_to(units, (N_EXPERTS, 128)).astype(BF16)) * ROW_ALIGN
    where_in_tile = prior + run_start[:, 0:1]
    for j in range(TOP_K):
        lpos_ref[j:j + 1, :] = jnp.sum(jnp.where(hots[j], where_in_tile, 0.0), axis=0,
                                       keepdims=True).astype(I32)
    tc_ref[0] = run_len.astype(I32)
    base_ref[0] = cnt_scr[...].astype(I32)
    cnt_scr[...] = cnt_scr[...] + run_len
    cnt_ref[...] = cnt_scr[...].astype(I32)


def _route(logits_t):
    n = logits_t.shape[1]
    tr = MOE_TILE
    tiles = n // tr
    tok = lambda rows: pl.BlockSpec((rows, tr), lambda i: (0, i))
    per_tile = pl.BlockSpec((1, N_EXPERTS, 128), lambda i: (i, 0, 0))
    return pl.pallas_call(
        functools.partial(_route_body, tr=tr),
        grid=(tiles,),
        in_specs=[tok(N_EXPERTS)],
        out_specs=[tok(TOP_K), tok(TOP_K), per_tile, per_tile, pl.BlockSpec((N_EXPERTS, 128), lambda i: (0, 0))],
        out_shape=[jax.ShapeDtypeStruct((TOP_K, n), I32), jax.ShapeDtypeStruct((TOP_K, n), F32),
                   jax.ShapeDtypeStruct((tiles, N_EXPERTS, 128), I32),
                   jax.ShapeDtypeStruct((tiles, N_EXPERTS, 128), I32),
                   jax.ShapeDtypeStruct((N_EXPERTS, 128), I32)],
        scratch_shapes=[pltpu.VMEM((N_EXPERTS, 128), F32)],
        compiler_params=_cparams("arbitrary"),
        name="route",
    )(logits_t)


def _for_each_run_piece(tc_ref, g0_ref, t, fn):
    def per_expert(e, local):
        c = tc_ref[t, e]
        glob = g0_ref[t, e]
        lo = local
        for p in RUN_PIECES:
            take = (c & p) != 0

            @pl.when(take)
            def _():
                fn(pl.multiple_of(lo, ROW_ALIGN), pl.multiple_of(glob, ROW_ALIGN), p)

            step = jnp.where(take, p, 0)
            lo = lo + step
            glob = glob + step
        return local + c

    lax.fori_loop(0, N_EXPERTS, per_expert, 0)


def _dispatch_body(tc_ref, g0_ref, seg_ref, lpos_ref, x_ref, buf_ref, stage, zero_scr, zsem, sems, *, n_blocks):
    i = pl.program_id(0)
    slot = i % 2

    def run_copy(slot_, lo, glob, rows):
        return pltpu.make_async_copy(stage.at[slot_, pl.ds(lo, rows), :], buf_ref.at[pl.ds(glob, rows), :],
                                     sems.at[slot_])

    @pl.when(i == 0)
    def _():
        zero_scr[...] = jnp.zeros_like(zero_scr)
        n_used = seg_ref[1, N_EXPERTS - 1] // EXPERT_ROWS

        def block_copy(start):
            return pltpu.make_async_copy(
                zero_scr, buf_ref.at[pl.ds(pl.multiple_of(start, EXPERT_ROWS), EXPERT_ROWS), :], zsem)

        def clears(action):
            for e in range(N_EXPERTS):
                @pl.when(seg_ref[1, e] > seg_ref[0, e])
                def _():
                    action(block_copy(seg_ref[1, e] - EXPERT_ROWS))

            def tail(blk, _):
                action(block_copy(blk * EXPERT_ROWS))
                return 0
            lax.fori_loop(n_used, n_blocks, tail, 0)

        clears(lambda cp: cp.start())
        clears(lambda cp: cp.wait())

    r = lax.broadcasted_iota(I32, (MOE_SLOTS, MOE_TILE), 0)
    onehot = jnp.zeros((MOE_SLOTS, MOE_TILE), F32)
    for j in range(TOP_K):
        onehot = jnp.where(r == lpos_ref[j:j + 1, :], 1.0, onehot)
    stage[slot] = _dot(onehot.astype(BF16), x_ref[...].astype(BF16)).astype(BF16)

    @pl.when(i > 0)
    def _():
        _for_each_run_piece(tc_ref, g0_ref, i - 1,
                            lambda lo, glob, rows: run_copy(1 - slot, lo, glob, rows).wait())

    _for_each_run_piece(tc_ref, g0_ref, i, lambda lo, glob, rows: run_copy(slot, lo, glob, rows).start())

    @pl.when(i == pl.num_programs(0) - 1)
    def _():
        _for_each_run_piece(tc_ref, g0_ref, i, lambda lo, glob, rows: run_copy(slot, lo, glob, rows).wait())


def _dispatch(x1, lpos_t, tc, g0, seg, n_rows):
    n = x1.shape[0]
    return pl.pallas_call(
        functools.partial(_dispatch_body, n_blocks=n_rows // EXPERT_ROWS),
        grid_spec=pltpu.PrefetchScalarGridSpec(
            num_scalar_prefetch=3,
            grid=(n // MOE_TILE,),
            in_specs=[pl.BlockSpec((TOP_K, MOE_TILE), lambda i, *_: (0, i)),
                      pl.BlockSpec((MOE_TILE, D_MODEL), lambda i, *_: (i, 0))],
            out_specs=pl.BlockSpec(memory_space=pl.ANY),
            scratch_shapes=[pltpu.VMEM((2, MOE_SLOTS, D_MODEL), BF16),
                            pltpu.VMEM((EXPERT_ROWS, D_MODEL), BF16),
                            pltpu.SemaphoreType.DMA(()), pltpu.SemaphoreType.DMA((2,))],
        ),
        out_shape=jax.ShapeDtypeStruct((n_rows, D_MODEL), BF16),
        compiler_params=_cparams("arbitrary"),
        name="dispatch",
    )(tc, g0, seg, lpos_t, x1)


def _experts_body(be_ref, nu_ref, x_ref, wgu_ref, bgu_ref, wd_ref, bd_ref, o_ref, wgu_bf, wd_bf):
    i = pl.program_id(0)

    @pl.when(jnp.logical_or(i == 0, be_ref[i] != be_ref[jnp.maximum(i - 1, 0)]))
    def _():
        for r in range(0, D_MODEL, 256):
            wgu_bf[r:r + 256, :] = wgu_ref[0, r:r + 256, :].astype(BF16)
            wd_bf[r:r + 256, :] = wd_ref[0, r:r + 256, :].astype(BF16)

    @pl.when(i < nu_ref[0])
    def _():
        gu = _dot(x_ref[...], wgu_bf[...]) + bgu_ref[0]
        gate = jnp.minimum(gu[:, :D_EXPERT], SWIGLU_LIMIT)
        up = jnp.clip(gu[:, D_EXPERT:], -SWIGLU_LIMIT, SWIGLU_LIMIT)
        hdn = (up + 1.0) * gate * jax.nn.sigmoid(SWIGLU_ALPHA * gate)
        o_ref[...] = (_dot(hdn.astype(BF16), wd_bf[...]) + bd_ref[0]).astype(BF16)

    @pl.when(i >= nu_ref[0])
    def _():
        o_ref[...] = jnp.zeros_like(o_ref)


def _experts(buf, block_e, n_used, w_gu, b_gu, w_down, b_down):
    n_blocks = buf.shape[0] // EXPERT_ROWS
    rows = pl.BlockSpec((EXPERT_ROWS, D_MODEL), lambda i, be, nu: (jnp.minimum(i, nu[0] - 1), 0))
    per_e = lambda a: pl.BlockSpec((1,) + a.shape[1:], lambda i, be, nu: (be[i],) + (0,) * (a.ndim - 1))
    return pl.pallas_call(
        _experts_body,
        grid_spec=pltpu.PrefetchScalarGridSpec(
            num_scalar_prefetch=2,
            grid=(n_blocks,),
            in_specs=[rows, per_e(w_gu), per_e(b_gu), per_e(w_down), per_e(b_down)],
            out_specs=pl.BlockSpec((EXPERT_ROWS, D_MODEL), lambda i, be, nu: (i, 0)),
            scratch_shapes=[pltpu.VMEM((D_MODEL, 2 * D_EXPERT), BF16), pltpu.VMEM((D_EXPERT, D_MODEL), BF16)],
        ),
        out_shape=jax.ShapeDtypeStruct(buf.shape, BF16),
        compiler_params=_cparams("arbitrary"),
        name="experts",
    )(block_e, n_used, buf, w_gu, b_gu, w_down, b_down)


def _final_body(tc_ref, g0_ref, lpos_ref, gate_ref, x1_ref, pa_ref, pb_ref, lnw_ref, lnb_ref, wg_ref, bg_ref, wp_ref,
                eo_ref, ya_ref, yb_ref, stage, sems, *, steps_a):
    i = pl.program_id(0)
    slot = i % 2

    def run_copy(slot_, lo, glob, rows):
        return pltpu.make_async_copy(eo_ref.at[pl.ds(glob, rows), :], stage.at[slot_, pl.ds(lo, rows), :],
                                     sems.at[slot_])

    def fetch(tile, slot_):
        stage[slot_] = jnp.zeros(stage.shape[1:], stage.dtype)
        _for_each_run_piece(tc_ref, g0_ref, tile, lambda lo, glob, rows: run_copy(slot_, lo, glob, rows).start())

    @pl.when(i == 0)
    def _():
        fetch(0, 0)

    @pl.when(i + 1 < pl.num_programs(0))
    def _():
        fetch(i + 1, 1 - slot)

    _for_each_run_piece(tc_ref, g0_ref, i, lambda lo, glob, rows: run_copy(slot, lo, glob, rows).wait())

    r = lax.broadcasted_iota(I32, (MOE_TILE, MOE_SLOTS), 1)
    g = jnp.zeros((MOE_TILE, MOE_SLOTS), F32)
    for j in range(TOP_K):
        g = jnp.where(r == lpos_ref[:, j:j + 1], gate_ref[:, j:j + 1], g)
    moe = _dot(g.astype(BF16), stage[slot])
    x2 = _layer_norm(DN_ALPHA * x1_ref[...] + moe, lnw_ref[...], lnb_ref[...])
    ple_gate = jax.nn.sigmoid(_dot(x2.astype(BF16), wg_ref[...]) + bg_ref[...])
    p = jnp.where(i < steps_a, pa_ref[...], pb_ref[...])
    y = x2 + ple_gate * _dot(p.astype(BF16), wp_ref[...])

    @pl.when(i < steps_a)
    def _():
        ya_ref[...] = y

    @pl.when(i >= steps_a)
    def _():
        yb_ref[...] = y


def _final(tc, g0, lpos, gate, x1, p_a, p_b, lnw, lnb, w_gate, b_gate, w_ple, expert_out):
    n = x1.shape[0]
    steps_a = p_a.shape[0] // MOE_TILE
    row = lambda width: pl.BlockSpec((MOE_TILE, width), lambda i, *_: (i, 0))
    full = lambda a: pl.BlockSpec(a.shape, lambda i, *_: (0,) * a.ndim)
    return pl.pallas_call(
        functools.partial(_final_body, steps_a=steps_a),
        grid_spec=pltpu.PrefetchScalarGridSpec(
            num_scalar_prefetch=2,
            grid=(n // MOE_TILE,),
            in_specs=[row(TOP_K), row(TOP_K), row(D_MODEL), *_two_group_rows(MOE_TILE, D_PLE, steps_a), full(lnw),
                      full(lnb), full(w_gate), full(b_gate), full(w_ple), pl.BlockSpec(memory_space=pl.ANY)],
            out_specs=list(_two_group_rows(MOE_TILE, D_MODEL, steps_a)),
            scratch_shapes=[pltpu.VMEM((2, MOE_SLOTS, D_MODEL), BF16), pltpu.SemaphoreType.DMA((2,))],
        ),
        out_shape=[jax.ShapeDtypeStruct((p_a.shape[0], D_MODEL), F32),
                   jax.ShapeDtypeStruct((p_b.shape[0], D_MODEL), F32)],
        compiler_params=_cparams("arbitrary"),
        name="combine_final",
    )(tc, g0, lpos, gate, x1, p_a, p_b, lnw, lnb, w_gate, b_gate, w_ple, expert_out)


def _round_up(a, m):
    return -(-a // m) * m


def _token_mixers(x, hist_k, hist_v, hist_kidx, s0, pos0, wts):
    b, t, _ = x.shape
    n = b * t
    x2d = x.reshape(n, D_MODEL)

    qa, ka, va, ka_bf, va_bf, qi, ki, ki_bf, wi, hb = _inproj(x2d, wts["w_in"], wts["idx_lnw"], wts["idx_lnb"])

    k_new = ka_bf.reshape(b, t, N_KV_A * HEAD_DIM_A)
    v_new = va_bf.reshape(b, t, N_KV_A * HEAD_DIM_A)
    ki_new = ki_bf.reshape(b, t, IDX_DIM)
    q_ops = (qa.reshape(N_HEADS_A, b, t, HEAD_DIM_A), qi.reshape(N_IDX_HEADS, b, t, IDX_DIM),
             wi.reshape(b, t, N_IDX_HEADS))
    past = 0 if hist_k is None else hist_k.shape[1]
    s_real = past + t
    topk = min(TOPK_MAX, s_real // 4)
    if past:
        k_all = jnp.concatenate([hist_k.reshape(b, past, -1).astype(BF16), k_new], axis=1)
        v_all = jnp.concatenate([hist_v.reshape(b, past, -1).astype(BF16), v_new], axis=1)
        ki_all = jnp.concatenate([hist_kidx.astype(BF16), ki_new], axis=1)
    else:
        k_all, v_all, ki_all = k_new, v_new, ki_new
    pad = ((0, 0), (0, _round_up(s_real, KEY_BLOCK) - s_real), (0, 0))
    keys = tuple(jnp.pad(a, pad) for a in (ki_all, k_all, v_all))
    oa = _dsa(*q_ops, keys, s_real=s_real, pos0=pos0, topk=topk)

    ob, s_fin = _hgrn(hb.reshape(b, t, 4 * WIDTH_B), s0, wts["lb"], wts["hgrn_nw"])
    return (oa, ob.reshape(n, WIDTH_B), ka.reshape(b, t, N_KV_A, HEAD_DIM_A), va.reshape(b, t, N_KV_A, HEAD_DIM_A),
            ki.reshape(b, t, IDX_DIM), s_fin)


def _channel_mixer(mix_a, mix_b, x_a, x_b, p_a, p_b, wts):
    shape_a, shape_b = x_a.shape, x_b.shape
    x_a, x_b = x_a.reshape(-1, D_MODEL), x_b.reshape(-1, D_MODEL)
    n = x_a.shape[0] + x_b.shape[0]
    x1, logits_t = _outproj(mix_a, mix_b, x_a, x_b, wts["w_out"], wts["ln1_w"], wts["ln1_b"],
                            wts["wr_hi"], wts["wr_lo"], wts["b_router"])

    tiles = n // MOE_TILE
    lpos_t, gate_t, tile_runs, tile_base, counts = _route(logits_t)
    tc, counts = tile_runs[:, :, 0], counts[:, 0]
    padded = (counts + EXPERT_ROWS - 1) // EXPERT_ROWS * EXPERT_ROWS
    seg_end = jnp.cumsum(padded)
    seg_start = seg_end - padded
    g0 = (seg_start[None, :] + tile_base[:, :, 0]).astype(I32)
    n_blocks = -(-(n * TOP_K + N_EXPERTS * tiles * (ROW_ALIGN - 1) + N_EXPERTS * (EXPERT_ROWS - 1))
                 // EXPERT_ROWS)
    block_row0 = jnp.arange(n_blocks, dtype=I32) * EXPERT_ROWS
    block_e = jnp.minimum(jnp.sum((seg_end[None, :] <= block_row0[:, None]).astype(I32), axis=1), N_EXPERTS - 1)
    n_used = (seg_end[-1:] // EXPERT_ROWS).astype(I32)
    seg = jnp.stack([seg_start, seg_end]).astype(I32)
    buf = _dispatch(x1, lpos_t, tc, g0, seg, n_blocks * EXPERT_ROWS)
    expert_out = _experts(buf, block_e, n_used, wts["w_gu"], wts["b_gu"], wts["w_down"], wts["b_down"])
    y_a, y_b = _final(tc, g0, lpos_t.T, gate_t.T, x1, p_a.reshape(-1, D_PLE), p_b.reshape(-1, D_PLE),
                      wts["ln2_w"], wts["ln2_b"], wts["w_ple_gate"], wts["b_ple_gate"], wts["w_ple"], expert_out)
    return y_a.reshape(shape_a), y_b.reshape(shape_b)


def _prep_weights(w_in, w_out, idx_k_norm_w, idx_k_norm_b, lb, hgrn_norm_w, ln1_w, ln1_b, w_router, b_router,
                  w_gu, b_gu, w_down, b_down, ln2_w, ln2_b, w_ple, w_ple_gate, b_ple_gate):
    n_a = COL_KW + IDX_DIM + N_IDX_HEADS
    w_pad = jnp.concatenate(
        [w_in[:, :n_a], jnp.zeros((D_MODEL, COL_HB - n_a), w_in.dtype), w_in[:, n_a:]], axis=1).astype(BF16)
    wr_t = w_router.T
    wr_hi = wr_t.astype(BF16)
    row = lambda a: a.reshape(1, -1)
    return dict(
        w_in=w_pad, idx_lnw=row(idx_k_norm_w), idx_lnb=row(idx_k_norm_b), lb=row(lb), hgrn_nw=row(hgrn_norm_w),
        w_out=w_out.astype(BF16), ln1_w=row(ln1_w), ln1_b=row(ln1_b),
        wr_hi=wr_hi, wr_lo=(wr_t - wr_hi.astype(F32)).astype(BF16), b_router=b_router.reshape(N_EXPERTS, 1),
        w_gu=w_gu, b_gu=b_gu.reshape(N_EXPERTS, 1, 2 * D_EXPERT),
        w_down=w_down, b_down=b_down.reshape(N_EXPERTS, 1, D_MODEL),
        ln2_w=row(ln2_w), ln2_b=row(ln2_b), w_ple=w_ple.astype(BF16), w_ple_gate=w_ple_gate.astype(BF16),
        b_ple_gate=row(b_ple_gate))


def kernel(x_prompt, x_sample, cache_k, cache_v, cache_kidx, state_hgrn, p_prompt, p_sample, w_in, w_out,
           idx_k_norm_w, idx_k_norm_b, hgrn_lb_logits, hgrn_norm_w, ln1_w, ln1_b, w_router, b_router, w_gu, b_gu,
           w_down, b_down, ln2_w, ln2_b, w_ple, w_ple_gate, b_ple_gate):
    lb_all = jnp.cumsum(jax.nn.softmax(hgrn_lb_logits.astype(F32), axis=0), axis=0)
    xp, xs = x_prompt, x_sample
    outs = [[] for _ in range(8)]
    for i in range(DEPTH):
        wts = _prep_weights(w_in[i], w_out[i], idx_k_norm_w[i], idx_k_norm_b[i], lb_all[i], hgrn_norm_w[i],
                            ln1_w[i], ln1_b[i], w_router[i], b_router[i], w_gu[i], b_gu[i], w_down[i], b_down[i],
                            ln2_w[i], ln2_b[i], w_ple[i], w_ple_gate[i], b_ple_gate[i])
        s0p = jnp.zeros((xp.shape[0], N_HEADS_B, HEAD_DIM_B, HEAD_DIM_B), F32)
        mix_p = _token_mixers(xp, None, None, None, s0p, 0, wts)
        mix_s = _token_mixers(xs, cache_k[i], cache_v[i], cache_kidx[i], state_hgrn[i], cache_k.shape[2], wts)
        xp, xs = _channel_mixer(mix_p, mix_s, xp, xs, p_prompt[i], p_sample[i], wts)
        for lst, val in zip(outs, mix_p[2:] + mix_s[2:]):
            lst.append(val)
    return (xp, xs) + tuple(jnp.stack(l) for l in outs)
```

```python
import functools
import itertools

import jax
import jax.numpy as jnp
from jax import lax
from jax.experimental import pallas as pl
from jax.experimental.pallas import tpu as pltpu

F32 = jnp.float32
BF16 = jnp.bfloat16
I32 = jnp.int32

D_MODEL = 1024
CHUNK = 64
CHUNK_SHIFT = 6
WIDTH_A = 512
HEAD_DIM_A = 64
N_HEADS_A = 8
N_KV_A = 2
N_IDX_HEADS = 4
IDX_DIM = 64
IDX_SCALE = IDX_DIM ** -0.5 * N_IDX_HEADS ** -0.5
TOPK_MAX = 256
WIDTH_B = 512
HEAD_DIM_B = 128
N_HEADS_B = 4
N_EXPERTS = 32
TOP_K = 4
D_EXPERT = 1024
SWIGLU_LIMIT = 7.0
SWIGLU_ALPHA = 1.702
D_PLE = 256
LN_EPS = 1e-5
RMS_EPS = 1e-6
DEPTH = 1
DN_ALPHA = (2 * DEPTH) ** 0.25

COL_QA, COL_KA, COL_VA, COL_QI, COL_KW, COL_HB, COL_END = 0, 512, 640, 768, 1024, 1152, 3200

VMEM_LIMIT = 56 * 1024 * 1024
INPROJ_ROWS = 1024
KEY_BLOCK = 256
BLOCKS_PER_TRIP = 8
V_ROWS = 80
LOG2_E = 1.4426950408889634
HGRN_SUB = 16
EXPERT_ROWS = 1024
MOE_TILE = 512
ROW_ALIGN = 16
RUN_PIECES = (512, 256, 128, 64, 32, 16)
MOE_SLOTS = -(-(TOP_K * MOE_TILE + N_EXPERTS * (ROW_ALIGN - 1)) // 256) * 256
NEG_INF = float("-inf")
INT_MIN = -(2 ** 31)


def _cparams(*sem):
    return pltpu.CompilerParams(dimension_semantics=sem, vmem_limit_bytes=VMEM_LIMIT)


def _dot(a, b):
    return jnp.dot(a, b, preferred_element_type=F32)


def _dot_nt(a, b):
    return lax.dot_general(a, b, (((1,), (1,)), ((), ())), preferred_element_type=F32)


def _dot_tn(a, b):
    return lax.dot_general(a, b, (((0,), (0,)), ((), ())), preferred_element_type=F32)


def _layer_norm(z, w, b):
    mu = jnp.mean(z, axis=-1, keepdims=True)
    d = z - mu
    var = jnp.mean(d * d, axis=-1, keepdims=True)
    return d * lax.rsqrt(var + LN_EPS) * w + b


def _inproj_body(x_ref, w_ref, lnw_ref, lnb_ref, qa_ref, ka_ref, va_ref, kab_ref, vab_ref, qi_ref, ki_ref, kib_ref,
                 wi_ref, hb_ref):
    xb = x_ref[...].astype(BF16)

    def mm(c0, c1):
        return _dot(xb, w_ref[:, c0:c1])

    qa = (mm(COL_QA, COL_KA) * (HEAD_DIM_A ** -0.5 * LOG2_E)).astype(BF16)
    for h in range(N_HEADS_A):
        qa_ref[h] = qa[:, h * HEAD_DIM_A:(h + 1) * HEAD_DIM_A]
    for src, out3_ref, bf_ref in ((mm(COL_KA, COL_VA), ka_ref, kab_ref), (mm(COL_VA, COL_QI), va_ref, vab_ref)):
        for g in range(N_KV_A):
            out3_ref[:, g, :] = src[:, g * HEAD_DIM_A:(g + 1) * HEAD_DIM_A]
        bf_ref[...] = src.astype(BF16)
    qi = mm(COL_QI, COL_KW).astype(BF16)
    for h in range(N_IDX_HEADS):
        qi_ref[h] = qi[:, h * IDX_DIM:(h + 1) * IDX_DIM]
    kw = mm(COL_KW, COL_HB)
    ki = _layer_norm(kw[:, :IDX_DIM], lnw_ref[...], lnb_ref[...])
    ki_ref[...] = ki
    kib_ref[...] = ki.astype(BF16)
    wi_ref[...] = kw[:, IDX_DIM:IDX_DIM + N_IDX_HEADS]
    hb_ref[...] = mm(COL_HB, COL_END)


def _inproj(x2d, w_pad, lnw, lnb):
    n = x2d.shape[0]
    tm = min(INPROJ_ROWS, n)
    row = lambda width: pl.BlockSpec((tm, width), lambda i: (i, 0))
    full = lambda a: pl.BlockSpec(a.shape, lambda i: (0,) * a.ndim)
    heads = lambda nh, d: pl.BlockSpec((nh, tm, d), lambda i: (0, i, 0))
    kv = N_KV_A * HEAD_DIM_A
    kv3 = pl.BlockSpec((tm, N_KV_A, HEAD_DIM_A), lambda i: (i, 0, 0))
    sds = jax.ShapeDtypeStruct
    return pl.pallas_call(
        _inproj_body,
        grid=(n // tm,),
        in_specs=[row(D_MODEL), full(w_pad), full(lnw), full(lnb)],
        out_specs=[heads(N_HEADS_A, HEAD_DIM_A), kv3, kv3, row(kv), row(kv), heads(N_IDX_HEADS, IDX_DIM),
                   row(IDX_DIM), row(IDX_DIM), row(N_IDX_HEADS), row(4 * WIDTH_B)],
        out_shape=[sds((N_HEADS_A, n, HEAD_DIM_A), BF16), sds((n, N_KV_A, HEAD_DIM_A), F32),
                   sds((n, N_KV_A, HEAD_DIM_A), F32), sds((n, kv), BF16), sds((n, kv), BF16),
                   sds((N_IDX_HEADS, n, IDX_DIM), BF16), sds((n, IDX_DIM), F32), sds((n, IDX_DIM), BF16),
                   sds((n, N_IDX_HEADS), F32), sds((n, 4 * WIDTH_B), F32)],
        compiler_params=_cparams("parallel"),
        name="inproj",
    )(x2d, w_pad, lnw, lnb)


def _order_bits_to_f32(u):
    key = u ^ INT_MIN
    bits = key ^ ((key >> 31) & 0x7FFFFFFF)
    f = lax.bitcast_convert_type(bits, F32)
    return jnp.where(u >= 0, jnp.where(u <= 0x007FFFFF, NEG_INF, f), f)


def _dsa_body(qa_ref, qi_ref, wi_ref, kidx_ref, k_ref, vt_ref, o_ref, sc_scr, bias_scr, term_scr, s_scr, acc_scr,
              *, s_pad, s_real, q_rows, pair, pos0, topk):
    kb_rows = KEY_BLOCK
    heads_per_kv = N_HEADS_A // N_KV_A
    qb = pair * q_rows
    cdim = pair * HEAD_DIM_A
    v_rows = pair * V_ROWS
    j = pl.program_id(1)
    q_lo = pos0 + j * q_rows
    lane = lax.broadcasted_iota(I32, (1, qb), 1)
    q_chunk = (q_lo + (lane & (q_rows - 1))) >> CHUNK_SHIFT
    k_lim = (((q_lo + q_rows - 1) >> CHUNK_SHIFT) + 1) * CHUNK
    nkb = jnp.minimum(s_pad // kb_rows, (k_lim + kb_rows - 1) // kb_rows)

    def rows_of(kb):
        return pl.ds(pl.multiple_of(kb * kb_rows, kb_rows), kb_rows)

    def fold8(a):
        return a.reshape(kb_rows // 8, 8, qb)

    def over_blocks(body, init):
        def group(first, width, carry):
            for u in range(width):
                carry = body(first + u, carry)
            return carry
        carry = lax.fori_loop(0, nkb // BLOCKS_PER_TRIP,
                              lambda i, c: group(i * BLOCKS_PER_TRIP, BLOCKS_PER_TRIP, c), init)
        done = (nkb // BLOCKS_PER_TRIP) * BLOCKS_PER_TRIP
        width = BLOCKS_PER_TRIP // 2
        while width:
            carry = lax.cond((nkb & width) != 0, functools.partial(group, done, width), lambda c: c, carry)
            done = done + (nkb & width)
            width //= 2
        return carry

    def score_blk(kb, _):
        rows = rows_of(kb)
        kidx = kidx_ref[0, rows, :]
        for h in range(N_IDX_HEADS):
            term_scr[h] = jnp.maximum(_dot_nt(kidx, qi_ref[h, 0]), 0.0) * wi_ref[0, h:h + 1, :]
        total = (term_scr[0] + term_scr[1]) + (term_scr[2] + term_scr[3])
        spos = kb * kb_rows + lax.broadcasted_iota(I32, (kb_rows, qb), 0)
        sc = jnp.where((spos >> CHUNK_SHIFT) <= q_chunk, total * IDX_SCALE + 0.0, NEG_INF)
        if s_real < s_pad:
            sc = jnp.where(spos < s_real, sc, NEG_INF)
        sc_scr[rows, :] = sc
        return 0

    over_blocks(score_blk, 0)

    def count(thr, strict):
        def body(kb, acc):
            blk = sc_scr[rows_of(kb), :]
            hit = (blk > thr) if strict else (blk >= thr)
            return acc + jnp.sum(fold8(jnp.where(hit, 1, 0).astype(I32)), axis=0)
        acc = over_blocks(body, jnp.zeros((8, qb), I32))
        return jnp.sum(acc, axis=0, keepdims=True)

    def bit_step(i, carry):
        prefix, cnt = carry
        cand = prefix | lax.shift_left(jnp.int32(1), 31 - i)
        c = count(_order_bits_to_f32(cand), False)
        take = c >= topk
        return jnp.where(take, cand, prefix), jnp.where(take, c, cnt)

    prefix, cnt_ge = lax.fori_loop(0, 32, bit_step,
                                   (jnp.zeros((1, qb), I32), jnp.full((1, qb), nkb * kb_rows, I32)))
    tau = _order_bits_to_f32(prefix)
    finite_tau = tau > NEG_INF
    tau_floor = jnp.maximum(tau, jnp.finfo(F32).min)
    tie_lanes = jnp.where(finite_tau, jnp.where(cnt_ge > topk, 1, 0), 0)
    has_ties = jnp.max(tie_lanes) > 0

    @pl.when(jnp.logical_not(has_ties))
    def _():
        def body(kb, _):
            blk = sc_scr[rows_of(kb), :]
            bias_scr[rows_of(kb), :] = jnp.where(blk >= tau_floor, 0.0, NEG_INF)
            return 0
        over_blocks(body, 0)

    @pl.when(has_ties)
    def _():
        need = jnp.where(finite_tau, (topk - count(tau, True)).astype(F32), 0.0)
        r = lax.broadcasted_iota(I32, (kb_rows, kb_rows), 0)
        c = lax.broadcasted_iota(I32, (kb_rows, kb_rows), 1)
        tril = jnp.where(r >= c, 1.0, 0.0).astype(BF16)

        def body(kb, seen):
            blk = sc_scr[rows_of(kb), :]
            eq = blk == tau
            rank = _dot(tril, jnp.where(eq, 1.0, 0.0).astype(BF16)) + seen
            tie_bias = jnp.where(eq, jnp.where(rank <= need, 0.0, NEG_INF), NEG_INF)
            bias_scr[rows_of(kb), :] = jnp.where(blk > tau, 0.0, tie_bias)
            return rank[kb_rows - 1:kb_rows, :]
        lax.fori_loop(0, nkb, body, jnp.zeros((1, qb), F32))

    def pass1(kb, m8):
        rows = rows_of(kb)
        bias = bias_scr[rows, :]
        out = []
        for h in range(N_HEADS_A):
            g = h // heads_per_kv
            s = _dot_nt(k_ref[0, rows, g * cdim:(g + 1) * cdim], qa_ref[h, 0]) + bias
            s_scr[h, rows, :] = s
            out.append(jnp.maximum(m8[h], jnp.max(fold8(s), axis=0)))
        return tuple(out)

    m8 = over_blocks(pass1, tuple(jnp.full((8, qb), NEG_INF, F32) for _ in range(N_HEADS_A)))
    m = [jnp.max(x, axis=0, keepdims=True) for x in m8]

    acc_scr[...] = jnp.zeros_like(acc_scr)

    def pass2(kb, _):
        rows = rows_of(kb)
        for h in range(N_HEADS_A):
            g = h // heads_per_kv
            p = jnp.exp2(s_scr[h, rows, :] - m[h]).astype(BF16)
            acc_scr[h] += _dot(vt_ref[0, g * v_rows:(g + 1) * v_rows, rows], p)
        return 0

    over_blocks(pass2, 0)
    for h in range(N_HEADS_A):
        a = acc_scr[h]
        num, den = a[:HEAD_DIM_A], a[HEAD_DIM_A:HEAD_DIM_A + 1]
        for mb in range(1, pair):
            mine = lane >= mb * q_rows
            r0 = mb * V_ROWS
            num = jnp.where(mine, a[r0:r0 + HEAD_DIM_A], num)
            den = jnp.where(mine, a[r0 + HEAD_DIM_A:r0 + HEAD_DIM_A + 1], den)
        o_ref[0, h * HEAD_DIM_A:(h + 1) * HEAD_DIM_A, :] = (num * (1.0 / den)).astype(BF16)


def _dsa_pairing(b, t):
    return 2 if (2 * t <= 128 and b % 2 == 0) else 1


def _dsa(qa_h, qi_h, wi, keys, *, s_real, pos0, topk):
    _, b, t, _ = qa_h.shape
    s_pad = keys[0].shape[1]
    q_rows = min(256, t)
    pair = _dsa_pairing(b, t)
    bp = b // pair
    lanes, cdim, v_rows = pair * q_rows, pair * HEAD_DIM_A, pair * V_ROWS

    def block_diag(q):
        if pair == 1:
            return q
        nh, d = q.shape[0], q.shape[3]
        eye = jnp.eye(pair, dtype=q.dtype)
        q6 = q.reshape(nh, bp, pair, t, 1, d) * eye.reshape(1, 1, pair, 1, pair, 1)
        return q6.reshape(nh, bp, pair * t, pair * d)

    wi_op = wi.reshape(bp, pair, t, N_IDX_HEADS).transpose(0, 3, 1, 2).reshape(bp, N_IDX_HEADS, pair * t)
    ki_all, k_all, v_all = keys
    ki_op = ki_all.reshape(bp, pair, s_pad, IDX_DIM).transpose(0, 2, 1, 3).reshape(bp, s_pad, cdim)
    k_op = (k_all.reshape(bp, pair, s_pad, N_KV_A, HEAD_DIM_A).transpose(0, 2, 3, 1, 4)
            .reshape(bp, s_pad, N_KV_A * cdim))
    vt = v_all.reshape(b, s_pad, N_KV_A, HEAD_DIM_A).transpose(0, 2, 3, 1)
    vt = jnp.concatenate([vt, jnp.ones((b, N_KV_A, V_ROWS - HEAD_DIM_A, s_pad), vt.dtype)], axis=2)
    vt_op = (vt.reshape(bp, pair, N_KV_A, V_ROWS, s_pad).transpose(0, 2, 1, 3, 4)
             .reshape(bp, N_KV_A * v_rows, s_pad))

    body = functools.partial(_dsa_body, s_pad=s_pad, s_real=s_real, q_rows=q_rows, pair=pair, pos0=pos0,
                             topk=topk)
    o_t = pl.pallas_call(
        body,
        grid=(bp, t // q_rows),
        in_specs=[
            pl.BlockSpec((N_HEADS_A, 1, lanes, cdim), lambda i, j: (0, i, j, 0)),
            pl.BlockSpec((N_IDX_HEADS, 1, lanes, cdim), lambda i, j: (0, i, j, 0)),
            pl.BlockSpec((1, N_IDX_HEADS, lanes), lambda i, j: (i, 0, j)),
            pl.BlockSpec((1, s_pad, cdim), lambda i, j: (i, 0, 0)),
            pl.BlockSpec((1, s_pad, N_KV_A * cdim), lambda i, j: (i, 0, 0)),
            pl.BlockSpec((1, N_KV_A * v_rows, s_pad), lambda i, j: (i, 0, 0)),
        ],
        out_specs=pl.BlockSpec((1, WIDTH_A, lanes), lambda i, j: (i, 0, j)),
        out_shape=jax.ShapeDtypeStruct((bp, WIDTH_A, pair * t), BF16),
        scratch_shapes=[pltpu.VMEM((s_pad, lanes), F32), pltpu.VMEM((s_pad, lanes), F32),
                        pltpu.VMEM((N_IDX_HEADS, KEY_BLOCK, lanes), F32),
                        pltpu.VMEM((N_HEADS_A, s_pad, lanes), F32), pltpu.VMEM((N_HEADS_A, v_rows, lanes), F32)],
        compiler_params=_cparams("parallel", "parallel"),
        name="dsa",
    )(block_diag(qa_h), block_diag(qi_h), wi_op, ki_op, k_op, vt_op)
    return o_t.reshape(bp, WIDTH_A, pair, t).transpose(0, 2, 3, 1).reshape(b * t, WIDTH_A)


def _split3(a):
    hi = a.astype(BF16)
    r1 = a - hi.astype(F32)
    mid = r1.astype(BF16)
    lo = (r1 - mid.astype(F32)).astype(BF16)
    return hi, mid, lo


def _hgrn_body(hb_ref, s0_ref, lb_ref, nw_ref, ob_ref, sfin_ref, state_scr, *, tb, nb):
    t = pl.program_id(1)

    @pl.when(t == 0)
    def _():
        for bi, h in itertools.product(range(nb), range(N_HEADS_B)):
            state_scr[bi, h] = s0_ref[bi, h].T

    lb = lb_ref[...]
    r = lax.broadcasted_iota(I32, (CHUNK, CHUNK), 0)
    c = lax.broadcasted_iota(I32, (CHUNK, CHUNK), 1)
    causal = r >= c
    tril = jnp.where(causal, 1.0, 0.0).astype(BF16)

    for ci, bi in itertools.product(range(tb // CHUNK), range(nb)):
        rows = slice(ci * CHUNK, (ci + 1) * CHUNK)
        f = lb + (1.0 - lb) * jax.nn.sigmoid(hb_ref[bi, rows, WIDTH_B:2 * WIDTH_B])
        parts = _split3(jnp.log(f))
        bcum = _dot(tril, parts[0]) + _dot(tril, parts[1]) + _dot(tril, parts[2])
        for h in range(N_HEADS_B):
            lanes = slice(h * HEAD_DIM_B, (h + 1) * HEAD_DIM_B)
            q = hb_ref[bi, rows, lanes]
            k = 1.0 - f[:, lanes]
            v = hb_ref[bi, rows, 2 * WIDTH_B + h * HEAD_DIM_B:2 * WIDTH_B + (h + 1) * HEAD_DIM_B]
            gate = hb_ref[bi, rows, 3 * WIDTH_B + h * HEAD_DIM_B:3 * WIDTH_B + (h + 1) * HEAD_DIM_B]
            bh = bcum[:, lanes]
            b_last = bh[CHUNK - 1:CHUNK, :]
            vb = v.astype(BF16)
            state_t = state_scr[bi, h]
            o_inter = _dot_nt((q * jnp.exp(bh)).astype(BF16), state_t.astype(BF16))
            a_rows = []
            for i in range(CHUNK // HGRN_SUB):
                lo, hi = i * HGRN_SUB, (i + 1) * HGRN_SUB
                ref = bh[lo - 1:lo, :] if i else jnp.zeros((1, HEAD_DIM_B), F32)
                qs = (q[lo:hi] * jnp.exp(bh[lo:hi] - ref)).astype(BF16)
                ks = (k * jnp.exp(ref - bh)).astype(BF16)
                a_rows.append(_dot_nt(qs, ks))
            a = jnp.where(causal, jnp.concatenate(a_rows, axis=0), 0.0)
            o = o_inter + _dot(a.astype(BF16), vb)
            kdec = (k * jnp.exp(b_last - bh)).astype(BF16)
            state_scr[bi, h] = jnp.exp(b_last) * state_t + _dot_tn(vb, kdec)
            ms = jnp.mean(o * o, axis=-1, keepdims=True)
            y = o * lax.rsqrt(ms + RMS_EPS) * nw_ref[...] * (gate * jax.nn.sigmoid(gate))
            ob_ref[bi, rows, lanes] = y.astype(BF16)

    @pl.when(t == pl.num_programs(1) - 1)
    def _():
        for bi, h in itertools.product(range(nb), range(N_HEADS_B)):
            sfin_ref[bi, h] = state_scr[bi, h].T


def _hgrn(hb, s0, lb, nw):
    b, t, _ = hb.shape
    tb = min(256, t)
    nb = next(m for m in (4, 2, 1) if b % m == 0)
    return pl.pallas_call(
        functools.partial(_hgrn_body, tb=tb, nb=nb),
        grid=(b // nb, t // tb),
        in_specs=[
            pl.BlockSpec((nb, tb, 4 * WIDTH_B), lambda i, j: (i, j, 0)),
            pl.BlockSpec((nb, N_HEADS_B, HEAD_DIM_B, HEAD_DIM_B), lambda i, j: (i, 0, 0, 0)),
            pl.BlockSpec((1, WIDTH_B), lambda i, j: (0, 0)),
            pl.BlockSpec((1, HEAD_DIM_B), lambda i, j: (0, 0)),
        ],
        out_specs=[
            pl.BlockSpec((nb, tb, WIDTH_B), lambda i, j: (i, j, 0)),
            pl.BlockSpec((nb, N_HEADS_B, HEAD_DIM_B, HEAD_DIM_B), lambda i, j: (i, 0, 0, 0)),
        ],
        out_shape=[jax.ShapeDtypeStruct((b, t, WIDTH_B), BF16),
                   jax.ShapeDtypeStruct((b, N_HEADS_B, HEAD_DIM_B, HEAD_DIM_B), F32)],
        scratch_shapes=[pltpu.VMEM((nb, N_HEADS_B, HEAD_DIM_B, HEAD_DIM_B), F32)],
        compiler_params=_cparams("parallel", "arbitrary"),
        name="hgrn",
    )(hb, s0, lb, nw)


def _outproj_body(oaa_ref, oab_ref, oba_ref, obb_ref, xa_ref, xb_ref, w_ref, lnw_ref, lnb_ref, wrh_ref, wrl_ref, br_ref,
                  x1_ref, lg_ref, *, steps_a):
    first = pl.program_id(0) < steps_a
    pick = lambda a_ref, b_ref: jnp.where(first, a_ref[...], b_ref[...])
    x = pick(xa_ref, xb_ref)
    y = _dot(pick(oaa_ref, oab_ref), w_ref[:WIDTH_A, :]) + _dot(pick(oba_ref, obb_ref), w_ref[WIDTH_A:, :])
    x1 = _layer_norm(DN_ALPHA * x + y, lnw_ref[...], lnb_ref[...])
    x1_ref[...] = x1
    hi = x1.astype(BF16)
    lo = (x1 - hi.astype(F32)).astype(BF16)
    lg_ref[...] = (_dot_nt(wrh_ref[...], hi) + _dot_nt(wrh_ref[...], lo) + _dot_nt(wrl_ref[...], hi)
                   + br_ref[...])


def _two_group_rows(tm, width, steps_a):
    return (pl.BlockSpec((tm, width), lambda i, *_: (jnp.minimum(i, steps_a - 1), 0)),
            pl.BlockSpec((tm, width), lambda i, *_: (jnp.maximum(i - steps_a, 0), 0)))


def _outproj(mix_a, mix_b, x_a, x_b, w_out, lnw, lnb, wr_hi, wr_lo, br):
    tm = MOE_TILE
    assert x_a.shape[0] % tm == 0 and x_b.shape[0] % tm == 0
    n = x_a.shape[0] + x_b.shape[0]
    steps_a = x_a.shape[0] // tm
    row = lambda width: pl.BlockSpec((tm, width), lambda i: (i, 0))
    full = lambda a: pl.BlockSpec(a.shape, lambda i: (0,) * a.ndim)
    return pl.pallas_call(
        functools.partial(_outproj_body, steps_a=steps_a),
        grid=(n // tm,),
        in_specs=[*_two_group_rows(tm, WIDTH_A, steps_a), *_two_group_rows(tm, WIDTH_B, steps_a),
                  *_two_group_rows(tm, D_MODEL, steps_a), full(w_out), full(lnw), full(lnb), full(wr_hi),
                  full(wr_lo), full(br)],
        out_specs=[row(D_MODEL), pl.BlockSpec((N_EXPERTS, tm), lambda i: (0, i))],
        out_shape=[jax.ShapeDtypeStruct((n, D_MODEL), F32), jax.ShapeDtypeStruct((N_EXPERTS, n), F32)],
        compiler_params=_cparams("parallel"),
        name="outproj",
    )(mix_a[0], mix_b[0], mix_a[1], mix_b[1], x_a, x_b, w_out, lnw, lnb, wr_hi, wr_lo, br)


def _route_body(lg_ref, lpos_ref, gate_ref, tc_ref, base_ref, cnt_ref, cnt_scr, *, tr):
    @pl.when(pl.program_id(0) == 0)
    def _():
        cnt_scr[...] = jnp.zeros_like(cnt_scr)

    l = lg_ref[...]
    rows = lax.broadcasted_iota(I32, (N_EXPERTS, tr), 0)
    vals, hots = [], []
    for j in range(TOP_K):
        m = jnp.max(l, axis=0, keepdims=True)
        idx = jnp.min(jnp.where(l == m, rows, N_EXPERTS), axis=0, keepdims=True)
        hot = rows == idx
        vals.append(m)
        hots.append(hot)
        l = jnp.where(hot, NEG_INF, l)
    es = [jnp.exp(v - vals[0]) for v in vals]
    inv = 1.0 / (es[0] + es[1] + es[2] + es[3])
    for j in range(TOP_K):
        gate_ref[j:j + 1, :] = es[j] * inv
    chosen = jnp.zeros((N_EXPERTS, tr), F32)
    for hot in hots:
        chosen = chosen + jnp.where(hot, 1.0, 0.0)
    r = lax.broadcasted_iota(I32, (tr, tr), 0)
    c = lax.broadcasted_iota(I32, (tr, tr), 1)
    before = jnp.where(r < c, 1.0, 0.0).astype(BF16)
    prior = _dot(chosen.astype(BF16), before)
    units = jnp.ceil(jnp.sum(chosen, axis=1, keepdims=True) * (1.0 / ROW_ALIGN))
    er = lax.broadcasted_iota(I32, (N_EXPERTS, N_EXPERTS), 0)
    ec = lax.broadcasted_iota(I32, (N_EXPERTS, N_EXPERTS), 1)
    earlier = jnp.where(ec < er, 1.0, 0.0).astype(BF16)
    run_len = jnp.broadcast_to(units, (N_EXPERTS, 128)) * ROW_ALIGN
    run_start = _dot(earlier, jnp.broadcast_to(units, (N_EXPERTS, 128)).astype(BF16)) * ROW_ALIGN
    where_in_tile = prior + run_start[:, 0:1]
    for j in range(TOP_K):
        lpos_ref[j:j + 1, :] = jnp.sum(jnp.where(hots[j], where_in_tile, 0.0), axis=0,
                                       keepdims=True).astype(I32)
    tc_ref[0] = run_len.astype(I32)
    base_ref[0] = cnt_scr[...].astype(I32)
    cnt_scr[...] = cnt_scr[...] + run_len
    cnt_ref[...] = cnt_scr[...].astype(I32)


def _route(logits_t):
    n = logits_t.shape[1]
    tr = MOE_TILE
    tiles = n // tr
    tok = lambda rows: pl.BlockSpec((rows, tr), lambda i: (0, i))
    per_tile = pl.BlockSpec((1, N_EXPERTS, 128), lambda i: (i, 0, 0))
    return pl.pallas_call(
        functools.partial(_route_body, tr=tr),
        grid=(tiles,),
        in_specs=[tok(N_EXPERTS)],
        out_specs=[tok(TOP_K), tok(TOP_K), per_tile, per_tile, pl.BlockSpec((N_EXPERTS, 128), lambda i: (0, 0))],
        out_shape=[jax.ShapeDtypeStruct((TOP_K, n), I32), jax.ShapeDtypeStruct((TOP_K, n), F32),
                   jax.ShapeDtypeStruct((tiles, N_EXPERTS, 128), I32),
                   jax.ShapeDtypeStruct((tiles, N_EXPERTS, 128), I32),
                   jax.ShapeDtypeStruct((N_EXPERTS, 128), I32)],
        scratch_shapes=[pltpu.VMEM((N_EXPERTS, 128), F32)],
        compiler_params=_cparams("arbitrary"),
        name="route",
    )(logits_t)


def _for_each_run_piece(tc_ref, g0_ref, t, fn):
    def per_expert(e, local):
        c = tc_ref[t, e]
        glob = g0_ref[t, e]
        lo = local
        for p in RUN_PIECES:
            take = (c & p) != 0

            @pl.when(take)
            def _():
                fn(pl.multiple_of(lo, ROW_ALIGN), pl.multiple_of(glob, ROW_ALIGN), p)

            step = jnp.where(take, p, 0)
            lo = lo + step
            glob = glob + step
        return local + c

    lax.fori_loop(0, N_EXPERTS, per_expert, 0)


def _dispatch_body(tc_ref, g0_ref, seg_ref, lpos_ref, x_ref, buf_ref, stage, zero_scr, zsem, sems, *, n_blocks):
    i = pl.program_id(0)
    slot = i % 2

    def run_copy(slot_, lo, glob, rows):
        return pltpu.make_async_copy(stage.at[slot_, pl.ds(lo, rows), :], buf_ref.at[pl.ds(glob, rows), :],
                                     sems.at[slot_])

    @pl.when(i == 0)
    def _():
        zero_scr[...] = jnp.zeros_like(zero_scr)
        n_used = seg_ref[1, N_EXPERTS - 1] // EXPERT_ROWS

        def block_copy(start):
            return pltpu.make_async_copy(
                zero_scr, buf_ref.at[pl.ds(pl.multiple_of(start, EXPERT_ROWS), EXPERT_ROWS), :], zsem)

        def clears(action):
            for e in range(N_EXPERTS):
                @pl.when(seg_ref[1, e] > seg_ref[0, e])
                def _():
                    action(block_copy(seg_ref[1, e] - EXPERT_ROWS))

            def tail(blk, _):
                action(block_copy(blk * EXPERT_ROWS))
                return 0
            lax.fori_loop(n_used, n_blocks, tail, 0)

        clears(lambda cp: cp.start())
        clears(lambda cp: cp.wait())

    r = lax.broadcasted_iota(I32, (MOE_SLOTS, MOE_TILE), 0)
    onehot = jnp.zeros((MOE_SLOTS, MOE_TILE), F32)
    for j in range(TOP_K):
        onehot = jnp.where(r == lpos_ref[j:j + 1, :], 1.0, onehot)
    stage[slot] = _dot(onehot.astype(BF16), x_ref[...].astype(BF16)).astype(BF16)

    @pl.when(i > 0)
    def _():
        _for_each_run_piece(tc_ref, g0_ref, i - 1,
                            lambda lo, glob, rows: run_copy(1 - slot, lo, glob, rows).wait())

    _for_each_run_piece(tc_ref, g0_ref, i, lambda lo, glob, rows: run_copy(slot, lo, glob, rows).start())

    @pl.when(i == pl.num_programs(0) - 1)
    def _():
        _for_each_run_piece(tc_ref, g0_ref, i, lambda lo, glob, rows: run_copy(slot, lo, glob, rows).wait())


def _dispatch(x1, lpos_t, tc, g0, seg, n_rows):
    n = x1.shape[0]
    return pl.pallas_call(
        functools.partial(_dispatch_body, n_blocks=n_rows // EXPERT_ROWS),
        grid_spec=pltpu.PrefetchScalarGridSpec(
            num_scalar_prefetch=3,
            grid=(n // MOE_TILE,),
            in_specs=[pl.BlockSpec((TOP_K, MOE_TILE), lambda i, *_: (0, i)),
                      pl.BlockSpec((MOE_TILE, D_MODEL), lambda i, *_: (i, 0))],
            out_specs=pl.BlockSpec(memory_space=pl.ANY),
            scratch_shapes=[pltpu.VMEM((2, MOE_SLOTS, D_MODEL), BF16),
                            pltpu.VMEM((EXPERT_ROWS, D_MODEL), BF16),
                            pltpu.SemaphoreType.DMA(()), pltpu.SemaphoreType.DMA((2,))],
        ),
        out_shape=jax.ShapeDtypeStruct((n_rows, D_MODEL), BF16),
        compiler_params=_cparams("arbitrary"),
        name="dispatch",
    )(tc, g0, seg, lpos_t, x1)


def _experts_body(be_ref, nu_ref, x_ref, wgu_ref, bgu_ref, wd_ref, bd_ref, o_ref, wgu_bf, wd_bf):
    i = pl.program_id(0)

    @pl.when(jnp.logical_or(i == 0, be_ref[i] != be_ref[jnp.maximum(i - 1, 0)]))
    def _():
        for r in range(0, D_MODEL, 256):
            wgu_bf[r:r + 256, :] = wgu_ref[0, r:r + 256, :].astype(BF16)
            wd_bf[r:r + 256, :] = wd_ref[0, r:r + 256, :].astype(BF16)

    @pl.when(i < nu_ref[0])
    def _():
        gu = _dot(x_ref[...], wgu_bf[...]) + bgu_ref[0]
        gate = jnp.minimum(gu[:, :D_EXPERT], SWIGLU_LIMIT)
        up = jnp.clip(gu[:, D_EXPERT:], -SWIGLU_LIMIT, SWIGLU_LIMIT)
        hdn = (up + 1.0) * gate * jax.nn.sigmoid(SWIGLU_ALPHA * gate)
        o_ref[...] = (_dot(hdn.astype(BF16), wd_bf[...]) + bd_ref[0]).astype(BF16)

    @pl.when(i >= nu_ref[0])
    def _():
        o_ref[...] = jnp.zeros_like(o_ref)


def _experts(buf, block_e, n_used, w_gu, b_gu, w_down, b_down):
    n_blocks = buf.shape[0] // EXPERT_ROWS
    rows = pl.BlockSpec((EXPERT_ROWS, D_MODEL), lambda i, be, nu: (jnp.minimum(i, nu[0] - 1), 0))
    per_e = lambda a: pl.BlockSpec((1,) + a.shape[1:], lambda i, be, nu: (be[i],) + (0,) * (a.ndim - 1))
    return pl.pallas_call(
        _experts_body,
        grid_spec=pltpu.PrefetchScalarGridSpec(
            num_scalar_prefetch=2,
            grid=(n_blocks,),
            in_specs=[rows, per_e(w_gu), per_e(b_gu), per_e(w_down), per_e(b_down)],
            out_specs=pl.BlockSpec((EXPERT_ROWS, D_MODEL), lambda i, be, nu: (i, 0)),
            scratch_shapes=[pltpu.VMEM((D_MODEL, 2 * D_EXPERT), BF16), pltpu.VMEM((D_EXPERT, D_MODEL), BF16)],
        ),
        out_shape=jax.ShapeDtypeStruct(buf.shape, BF16),
        compiler_params=_cparams("arbitrary"),
        name="experts",
    )(block_e, n_used, buf, w_gu, b_gu, w_down, b_down)


def _final_body(tc_ref, g0_ref, lpos_ref, gate_ref, x1_ref, pa_ref, pb_ref, lnw_ref, lnb_ref, wg_ref, bg_ref, wp_ref,
                eo_ref, ya_ref, yb_ref, stage, sems, *, steps_a):
    i = pl.program_id(0)
    slot = i % 2

    def run_copy(slot_, lo, glob, rows):
        return pltpu.make_async_copy(eo_ref.at[pl.ds(glob, rows), :], stage.at[slot_, pl.ds(lo, rows), :],
                                     sems.at[slot_])

    def fetch(tile, slot_):
        stage[slot_] = jnp.zeros(stage.shape[1:], stage.dtype)
        _for_each_run_piece(tc_ref, g0_ref, tile, lambda lo, glob, rows: run_copy(slot_, lo, glob, rows).start())

    @pl.when(i == 0)
    def _():
        fetch(0, 0)

    @pl.when(i + 1 < pl.num_programs(0))
    def _():
        fetch(i + 1, 1 - slot)

    _for_each_run_piece(tc_ref, g0_ref, i, lambda lo, glob, rows: run_copy(slot, lo, glob, rows).wait())

    r = lax.broadcasted_iota(I32, (MOE_TILE, MOE_SLOTS), 1)
    g = jnp.zeros((MOE_TILE, MOE_SLOTS), F32)
    for j in range(TOP_K):
        g = jnp.where(r == lpos_ref[:, j:j + 1], gate_ref[:, j:j + 1], g)
    moe = _dot(g.astype(BF16), stage[slot])
    x2 = _layer_norm(DN_ALPHA * x1_ref[...] + moe, lnw_ref[...], lnb_ref[...])
    ple_gate = jax.nn.sigmoid(_dot(x2.astype(BF16), wg_ref[...]) + bg_ref[...])
    p = jnp.where(i < steps_a, pa_ref[...], pb_ref[...])
    y = x2 + ple_gate * _dot(p.astype(BF16), wp_ref[...])

    @pl.when(i < steps_a)
    def _():
        ya_ref[...] = y

    @pl.when(i >= steps_a)
    def _():
        yb_ref[...] = y


def _final(tc, g0, lpos, gate, x1, p_a, p_b, lnw, lnb, w_gate, b_gate, w_ple, expert_out):
    n = x1.shape[0]
    steps_a = p_a.shape[0] // MOE_TILE
    row = lambda width: pl.BlockSpec((MOE_TILE, width), lambda i, *_: (i, 0))
    full = lambda a: pl.BlockSpec(a.shape, lambda i, *_: (0,) * a.ndim)
    return pl.pallas_call(
        functools.partial(_final_body, steps_a=steps_a),
        grid_spec=pltpu.PrefetchScalarGridSpec(
            num_scalar_prefetch=2,
            grid=(n // MOE_TILE,),
            in_specs=[row(TOP_K), row(TOP_K), row(D_MODEL), *_two_group_rows(MOE_TILE, D_PLE, steps_a), full(lnw),
                      full(lnb), full(w_gate), full(b_gate), full(w_ple), pl.BlockSpec(memory_space=pl.ANY)],
            out_specs=list(_two_group_rows(MOE_TILE, D_MODEL, steps_a)),
            scratch_shapes=[pltpu.VMEM((2, MOE_SLOTS, D_MODEL), BF16), pltpu.SemaphoreType.DMA((2,))],
        ),
        out_shape=[jax.ShapeDtypeStruct((p_a.shape[0], D_MODEL), F32),
                   jax.ShapeDtypeStruct((p_b.shape[0], D_MODEL), F32)],
        compiler_params=_cparams("arbitrary"),
        name="combine_final",
    )(tc, g0, lpos, gate, x1, p_a, p_b, lnw, lnb, w_gate, b_gate, w_ple, expert_out)


def _round_up(a, m):
    return -(-a // m) * m


def _token_mixers(x, hist_k, hist_v, hist_kidx, s0, pos0, wts):
    b, t, _ = x.shape
    n = b * t
    x2d = x.reshape(n, D_MODEL)

    qa, ka, va, ka_bf, va_bf, qi, ki, ki_bf, wi, hb = _inproj(x2d, wts["w_in"], wts["idx_lnw"], wts["idx_lnb"])

    k_new = ka_bf.reshape(b, t, N_KV_A * HEAD_DIM_A)
    v_new = va_bf.reshape(b, t, N_KV_A * HEAD_DIM_A)
    ki_new = ki_bf.reshape(b, t, IDX_DIM)
    q_ops = (qa.reshape(N_HEADS_A, b, t, HEAD_DIM_A), qi.reshape(N_IDX_HEADS, b, t, IDX_DIM),
             wi.reshape(b, t, N_IDX_HEADS))
    past = 0 if hist_k is None else hist_k.shape[1]
    s_real = past + t
    topk = min(TOPK_MAX, s_real // 4)
    if past:
        k_all = jnp.concatenate([hist_k.reshape(b, past, -1).astype(BF16), k_new], axis=1)
        v_all = jnp.concatenate([hist_v.reshape(b, past, -1).astype(BF16), v_new], axis=1)
        ki_all = jnp.concatenate([hist_kidx.astype(BF16), ki_new], axis=1)
    else:
        k_all, v_all, ki_all = k_new, v_new, ki_new
    pad = ((0, 0), (0, _round_up(s_real, KEY_BLOCK) - s_real), (0, 0))
    keys = tuple(jnp.pad(a, pad) for a in (ki_all, k_all, v_all))
    oa = _dsa(*q_ops, keys, s_real=s_real, pos0=pos0, topk=topk)

    ob, s_fin = _hgrn(hb.reshape(b, t, 4 * WIDTH_B), s0, wts["lb"], wts["hgrn_nw"])
    return (oa, ob.reshape(n, WIDTH_B), ka.reshape(b, t, N_KV_A, HEAD_DIM_A), va.reshape(b, t, N_KV_A, HEAD_DIM_A),
            ki.reshape(b, t, IDX_DIM), s_fin)


def _channel_mixer(mix_a, mix_b, x_a, x_b, p_a, p_b, wts):
    shape_a, shape_b = x_a.shape, x_b.shape
    x_a, x_b = x_a.reshape(-1, D_MODEL), x_b.reshape(-1, D_MODEL)
    n = x_a.shape[0] + x_b.shape[0]
    x1, logits_t = _outproj(mix_a, mix_b, x_a, x_b, wts["w_out"], wts["ln1_w"], wts["ln1_b"],
                            wts["wr_hi"], wts["wr_lo"], wts["b_router"])

    tiles = n // MOE_TILE
    lpos_t, gate_t, tile_runs, tile_base, counts = _route(logits_t)
    tc, counts = tile_runs[:, :, 0], counts[:, 0]
    padded = (counts + EXPERT_ROWS - 1) // EXPERT_ROWS * EXPERT_ROWS
    seg_end = jnp.cumsum(padded)
    seg_start = seg_end - padded
    g0 = (seg_start[None, :] + tile_base[:, :, 0]).astype(I32)
    n_blocks = -(-(n * TOP_K + N_EXPERTS * tiles * (ROW_ALIGN - 1) + N_EXPERTS * (EXPERT_ROWS - 1))
                 // EXPERT_ROWS)
    block_row0 = jnp.arange(n_blocks, dtype=I32) * EXPERT_ROWS
    block_e = jnp.minimum(jnp.sum((seg_end[None, :] <= block_row0[:, None]).astype(I32), axis=1), N_EXPERTS - 1)
    n_used = (seg_end[-1:] // EXPERT_ROWS).astype(I32)
    seg = jnp.stack([seg_start, seg_end]).astype(I32)
    buf = _dispatch(x1, lpos_t, tc, g0, seg, n_blocks * EXPERT_ROWS)
    expert_out = _experts(buf, block_e, n_used, wts["w_gu"], wts["b_gu"], wts["w_down"], wts["b_down"])
    y_a, y_b = _final(tc, g0, lpos_t.T, gate_t.T, x1, p_a.reshape(-1, D_PLE), p_b.reshape(-1, D_PLE),
                      wts["ln2_w"], wts["ln2_b"], wts["w_ple_gate"], wts["b_ple_gate"], wts["w_ple"], expert_out)
    return y_a.reshape(shape_a), y_b.reshape(shape_b)


def _prep_weights(w_in, w_out, idx_k_norm_w, idx_k_norm_b, lb, hgrn_norm_w, ln1_w, ln1_b, w_router, b_router,
                  w_gu, b_gu, w_down, b_down, ln2_w, ln2_b, w_ple, w_ple_gate, b_ple_gate):
    n_a = COL_KW + IDX_DIM + N_IDX_HEADS
    w_pad = jnp.concatenate(
        [w_in[:, :n_a], jnp.zeros((D_MODEL, COL_HB - n_a), w_in.dtype), w_in[:, n_a:]], axis=1).astype(BF16)
    wr_t = w_router.T
    wr_hi = wr_t.astype(BF16)
    row = lambda a: a.reshape(1, -1)
    return dict(
        w_in=w_pad, idx_lnw=row(idx_k_norm_w), idx_lnb=row(idx_k_norm_b), lb=row(lb), hgrn_nw=row(hgrn_norm_w),
        w_out=w_out.astype(BF16), ln1_w=row(ln1_w), ln1_b=row(ln1_b),
        wr_hi=wr_hi, wr_lo=(wr_t - wr_hi.astype(F32)).astype(BF16), b_router=b_router.reshape(N_EXPERTS, 1),
        w_gu=w_gu, b_gu=b_gu.reshape(N_EXPERTS, 1, 2 * D_EXPERT),
        w_down=w_down, b_down=b_down.reshape(N_EXPERTS, 1, D_MODEL),
        ln2_w=row(ln2_w), ln2_b=row(ln2_b), w_ple=w_ple.astype(BF16), w_ple_gate=w_ple_gate.astype(BF16),
        b_ple_gate=row(b_ple_gate))


def kernel(x_prompt, x_sample, cache_k, cache_v, cache_kidx, state_hgrn, p_prompt, p_sample, w_in, w_out,
           idx_k_norm_w, idx_k_norm_b, hgrn_lb_logits, hgrn_norm_w, ln1_w, ln1_b, w_router, b_router, w_gu, b_gu,
           w_down, b_down, ln2_w, ln2_b, w_ple, w_ple_gate, b_ple_gate):
    lb_all = jnp.cumsum(jax.nn.softmax(hgrn_lb_logits.astype(F32), axis=0), axis=0)
    xp, xs = x_prompt, x_sample
    outs = [[] for _ in range(8)]
    for i in range(DEPTH):
        wts = _prep_weights(w_in[i], w_out[i], idx_k_norm_w[i], idx_k_norm_b[i], lb_all[i], hgrn_norm_w[i],
                            ln1_w[i], ln1_b[i], w_router[i], b_router[i], w_gu[i], b_gu[i], w_down[i], b_down[i],
                            ln2_w[i], ln2_b[i], w_ple[i], w_ple_gate[i], b_ple_gate[i])
        s0p = jnp.zeros((xp.shape[0], N_HEADS_B, HEAD_DIM_B, HEAD_DIM_B), F32)
        mix_p = _token_mixers(xp, None, None, None, s0p, 0, wts)
        mix_s = _token_mixers(xs, cache_k[i], cache_v[i], cache_kidx[i], state_hgrn[i], cache_k.shape[2], wts)
        xp, xs = _channel_mixer(mix_p, mix_s, xp, xs, p_prompt[i], p_sample[i], wts)
        for lst, val in zip(outs, mix_p[2:] + mix_s[2:]):
            lst.append(val)
    return (xp, xs) + tuple(jnp.stack(l) for l in outs)
```

```python
import functools
import itertools

import jax
import jax.numpy as jnp
from jax import lax
from jax.experimental import pallas as pl
from jax.experimental.pallas import tpu as pltpu

F32 = jnp.float32
BF16 = jnp.bfloat16
I32 = jnp.int32

D_MODEL = 1024
CHUNK = 64
CHUNK_SHIFT = 6
WIDTH_A = 512
HEAD_DIM_A = 64
N_HEADS_A = 8
N_KV_A = 2
N_IDX_HEADS = 4
IDX_DIM = 64
IDX_SCALE = IDX_DIM ** -0.5 * N_IDX_HEADS ** -0.5
TOPK_MAX = 256
WIDTH_B = 512
HEAD_DIM_B = 128
N_HEADS_B = 4
N_EXPERTS = 32
TOP_K = 4
D_EXPERT = 1024
SWIGLU_LIMIT = 7.0
SWIGLU_ALPHA = 1.702
D_PLE = 256
LN_EPS = 1e-5
RMS_EPS = 1e-6
DEPTH = 1
DN_ALPHA = (2 * DEPTH) ** 0.25

COL_QA, COL_KA, COL_VA, COL_QI, COL_KW, COL_HB, COL_END = 0, 512, 640, 768, 1024, 1152, 3200

VMEM_LIMIT = 56 * 1024 * 1024
INPROJ_ROWS = 1024
KEY_BLOCK = 256
BLOCKS_PER_TRIP = 8
V_ROWS = 80
LOG2_E = 1.4426950408889634
HGRN_SUB = 16
EXPERT_ROWS = 1024
MOE_TILE = 512
ROW_ALIGN = 16
RUN_PIECES = (512, 256, 128, 64, 32, 16)
MOE_SLOTS = -(-(TOP_K * MOE_TILE + N_EXPERTS * (ROW_ALIGN - 1)) // 256) * 256
NEG_INF = float("-inf")
INT_MIN = -(2 ** 31)


def _cparams(*sem):
    return pltpu.CompilerParams(dimension_semantics=sem, vmem_limit_bytes=VMEM_LIMIT)


def _dot(a, b):
    return jnp.dot(a, b, preferred_element_type=F32)


def _dot_nt(a, b):
    return lax.dot_general(a, b, (((1,), (1,)), ((), ())), preferred_element_type=F32)


def _dot_tn(a, b):
    return lax.dot_general(a, b, (((0,), (0,)), ((), ())), preferred_element_type=F32)


def _layer_norm(z, w, b):
    mu = jnp.mean(z, axis=-1, keepdims=True)
    d = z - mu
    var = jnp.mean(d * d, axis=-1, keepdims=True)
    return d * lax.rsqrt(var + LN_EPS) * w + b


def _inproj_body(x_ref, w_ref, lnw_ref, lnb_ref, qa_ref, ka_ref, va_ref, kab_ref, vab_ref, qi_ref, ki_ref, kib_ref,
                 wi_ref, hb_ref):
    xb = x_ref[...].astype(BF16)

    def mm(c0, c1):
        return _dot(xb, w_ref[:, c0:c1])

    qa = (mm(COL_QA, COL_KA) * (HEAD_DIM_A ** -0.5 * LOG2_E)).astype(BF16)
    for h in range(N_HEADS_A):
        qa_ref[h] = qa[:, h * HEAD_DIM_A:(h + 1) * HEAD_DIM_A]
    for src, out3_ref, bf_ref in ((mm(COL_KA, COL_VA), ka_ref, kab_ref), (mm(COL_VA, COL_QI), va_ref, vab_ref)):
        for g in range(N_KV_A):
            out3_ref[:, g, :] = src[:, g * HEAD_DIM_A:(g + 1) * HEAD_DIM_A]
        bf_ref[...] = src.astype(BF16)
    qi = mm(COL_QI, COL_KW).astype(BF16)
    for h in range(N_IDX_HEADS):
        qi_ref[h] = qi[:, h * IDX_DIM:(h + 1) * IDX_DIM]
    kw = mm(COL_KW, COL_HB)
    ki = _layer_norm(kw[:, :IDX_DIM], lnw_ref[...], lnb_ref[...])
    ki_ref[...] = ki
    kib_ref[...] = ki.astype(BF16)
    wi_ref[...] = kw[:, IDX_DIM:IDX_DIM + N_IDX_HEADS]
    hb_ref[...] = mm(COL_HB, COL_END)


def _inproj(x2d, w_pad, lnw, lnb):
    n = x2d.shape[0]
    tm = min(INPROJ_ROWS, n)
    row = lambda width: pl.BlockSpec((tm, width), lambda i: (i, 0))
    full = lambda a: pl.BlockSpec(a.shape, lambda i: (0,) * a.ndim)
    heads = lambda nh, d: pl.BlockSpec((nh, tm, d), lambda i: (0, i, 0))
    kv = N_KV_A * HEAD_DIM_A
    kv3 = pl.BlockSpec((tm, N_KV_A, HEAD_DIM_A), lambda i: (i, 0, 0))
    sds = jax.ShapeDtypeStruct
    return pl.pallas_call(
        _inproj_body,
        grid=(n // tm,),
        in_specs=[row(D_MODEL), full(w_pad), full(lnw), full(lnb)],
        out_specs=[heads(N_HEADS_A, HEAD_DIM_A), kv3, kv3, row(kv), row(kv), heads(N_IDX_HEADS, IDX_DIM),
                   row(IDX_DIM), row(IDX_DIM), row(N_IDX_HEADS), row(4 * WIDTH_B)],
        out_shape=[sds((N_HEADS_A, n, HEAD_DIM_A), BF16), sds((n, N_KV_A, HEAD_DIM_A), F32),
                   sds((n, N_KV_A, HEAD_DIM_A), F32), sds((n, kv), BF16), sds((n, kv), BF16),
                   sds((N_IDX_HEADS, n, IDX_DIM), BF16), sds((n, IDX_DIM), F32), sds((n, IDX_DIM), BF16),
                   sds((n, N_IDX_HEADS), F32), sds((n, 4 * WIDTH_B), F32)],
        compiler_params=_cparams("parallel"),
        name="inproj",
    )(x2d, w_pad, lnw, lnb)


def _order_bits_to_f32(u):
    key = u ^ INT_MIN
    bits = key ^ ((key >> 31) & 0x7FFFFFFF)
    f = lax.bitcast_convert_type(bits, F32)
    return jnp.where(u >= 0, jnp.where(u <= 0x007FFFFF, NEG_INF, f), f)


def _dsa_body(qa_ref, qi_ref, wi_ref, kidx_ref, k_ref, vt_ref, o_ref, sc_scr, bias_scr, term_scr, s_scr, acc_scr,
              *, s_pad, s_real, q_rows, pair, pos0, topk):
    kb_rows = KEY_BLOCK
    heads_per_kv = N_HEADS_A // N_KV_A
    qb = pair * q_rows
    cdim = pair * HEAD_DIM_A
    v_rows = pair * V_ROWS
    j = pl.program_id(1)
    q_lo = pos0 + j * q_rows
    lane = lax.broadcasted_iota(I32, (1, qb), 1)
    q_chunk = (q_lo + (lane & (q_rows - 1))) >> CHUNK_SHIFT
    k_lim = (((q_lo + q_rows - 1) >> CHUNK_SHIFT) + 1) * CHUNK
    nkb = jnp.minimum(s_pad // kb_rows, (k_lim + kb_rows - 1) // kb_rows)

    def rows_of(kb):
        return pl.ds(pl.multiple_of(kb * kb_rows, kb_rows), kb_rows)

    def fold8(a):
        return a.reshape(kb_rows // 8, 8, qb)

    def over_blocks(body, init):
        def group(first, width, carry):
            for u in range(width):
                carry = body(first + u, carry)
            return carry
        carry = lax.fori_loop(0, nkb // BLOCKS_PER_TRIP,
                              lambda i, c: group(i * BLOCKS_PER_TRIP, BLOCKS_PER_TRIP, c), init)
        done = (nkb // BLOCKS_PER_TRIP) * BLOCKS_PER_TRIP
        width = BLOCKS_PER_TRIP // 2
        while width:
            carry = lax.cond((nkb & width) != 0, functools.partial(group, done, width), lambda c: c, carry)
            done = done + (nkb & width)
            width //= 2
        return carry

    def score_blk(kb, _):
        rows = rows_of(kb)
        kidx = kidx_ref[0, rows, :]
        for h in range(N_IDX_HEADS):
            term_scr[h] = jnp.maximum(_dot_nt(kidx, qi_ref[h, 0]), 0.0) * wi_ref[0, h:h + 1, :]
        total = (term_scr[0] + term_scr[1]) + (term_scr[2] + term_scr[3])
        spos = kb * kb_rows + lax.broadcasted_iota(I32, (kb_rows, qb), 0)
        sc = jnp.where((spos >> CHUNK_SHIFT) <= q_chunk, total * IDX_SCALE + 0.0, NEG_INF)
        if s_real < s_pad:
            sc = jnp.where(spos < s_real, sc, NEG_INF)
        sc_scr[rows, :] = sc
        return 0

    over_blocks(score_blk, 0)

    def count(thr, strict):
        def body(kb, acc):
            blk = sc_scr[rows_of(kb), :]
            hit = (blk > thr) if strict else (blk >= thr)
            return acc + jnp.sum(fold8(jnp.where(hit, 1, 0).astype(I32)), axis=0)
        acc = over_blocks(body, jnp.zeros((8, qb), I32))
        return jnp.sum(acc, axis=0, keepdims=True)

    def bit_step(i, carry):
        prefix, cnt = carry
        cand = prefix | lax.shift_left(jnp.int32(1), 31 - i)
        c = count(_order_bits_to_f32(cand), False)
        take = c >= topk
        return jnp.where(take, cand, prefix), jnp.where(take, c, cnt)

    prefix, cnt_ge = lax.fori_loop(0, 32, bit_step,
                                   (jnp.zeros((1, qb), I32), jnp.full((1, qb), nkb * kb_rows, I32)))
    tau = _order_bits_to_f32(prefix)
    finite_tau = tau > NEG_INF
    tau_floor = jnp.maximum(tau, jnp.finfo(F32).min)
    tie_lanes = jnp.where(finite_tau, jnp.where(cnt_ge > topk, 1, 0), 0)
    has_ties = jnp.max(tie_lanes) > 0

    @pl.when(jnp.logical_not(has_ties))
    def _():
        def body(kb, _):
            blk = sc_scr[rows_of(kb), :]
            bias_scr[rows_of(kb), :] = jnp.where(blk >= tau_floor, 0.0, NEG_INF)
            return 0
        over_blocks(body, 0)

    @pl.when(has_ties)
    def _():
        need = jnp.where(finite_tau, (topk - count(tau, True)).astype(F32), 0.0)
        r = lax.broadcasted_iota(I32, (kb_rows, kb_rows), 0)
        c = lax.broadcasted_iota(I32, (kb_rows, kb_rows), 1)
        tril = jnp.where(r >= c, 1.0, 0.0).astype(BF16)

        def body(kb, seen):
            blk = sc_scr[rows_of(kb), :]
            eq = blk == tau
            rank = _dot(tril, jnp.where(eq, 1.0, 0.0).astype(BF16)) + seen
            tie_bias = jnp.where(eq, jnp.where(rank <= need, 0.0, NEG_INF), NEG_INF)
            bias_scr[rows_of(kb), :] = jnp.where(blk > tau, 0.0, tie_bias)
            return rank[kb_rows - 1:kb_rows, :]
        lax.fori_loop(0, nkb, body, jnp.zeros((1, qb), F32))

    def pass1(kb, m8):
        rows = rows_of(kb)
        bias = bias_scr[rows, :]
        out = []
        for h in range(N_HEADS_A):
            g = h // heads_per_kv
            s = _dot_nt(k_ref[0, rows, g * cdim:(g + 1) * cdim], qa_ref[h, 0]) + bias
            s_scr[h, rows, :] = s
            out.append(jnp.maximum(m8[h], jnp.max(fold8(s), axis=0)))
        return tuple(out)

    m8 = over_blocks(pass1, tuple(jnp.full((8, qb), NEG_INF, F32) for _ in range(N_HEADS_A)))
    m = [jnp.max(x, axis=0, keepdims=True) for x in m8]

    acc_scr[...] = jnp.zeros_like(acc_scr)

    def pass2(kb, _):
        rows = rows_of(kb)
        for h in range(N_HEADS_A):
            g = h // heads_per_kv
            p = jnp.exp2(s_scr[h, rows, :] - m[h]).astype(BF16)
            acc_scr[h] += _dot(vt_ref[0, g * v_rows:(g + 1) * v_rows, rows], p)
        return 0

    over_blocks(pass2, 0)
    for h in range(N_HEADS_A):
        a = acc_scr[h]
        num, den = a[:HEAD_DIM_A], a[HEAD_DIM_A:HEAD_DIM_A + 1]
        for mb in range(1, pair):
            mine = lane >= mb * q_rows
            r0 = mb * V_ROWS
            num = jnp.where(mine, a[r0:r0 + HEAD_DIM_A], num)
            den = jnp.where(mine, a[r0 + HEAD_DIM_A:r0 + HEAD_DIM_A + 1], den)
        o_ref[0, h * HEAD_DIM_A:(h + 1) * HEAD_DIM_A, :] = (num * (1.0 / den)).astype(BF16)


def _dsa_pairing(b, t):
    return 2 if (2 * t <= 128 and b % 2 == 0) else 1


def _dsa(qa_h, qi_h, wi, keys, *, s_real, pos0, topk):
    _, b, t, _ = qa_h.shape
    s_pad = keys[0].shape[1]
    q_rows = min(256, t)
    pair = _dsa_pairing(b, t)
    bp = b // pair
    lanes, cdim, v_rows = pair * q_rows, pair * HEAD_DIM_A, pair * V_ROWS

    def block_diag(q):
        if pair == 1:
            return q
        nh, d = q.shape[0], q.shape[3]
        eye = jnp.eye(pair, dtype=q.dtype)
        q6 = q.reshape(nh, bp, pair, t, 1, d) * eye.reshape(1, 1, pair, 1, pair, 1)
        return q6.reshape(nh, bp, pair * t, pair * d)

    wi_op = wi.reshape(bp, pair, t, N_IDX_HEADS).transpose(0, 3, 1, 2).reshape(bp, N_IDX_HEADS, pair * t)
    ki_all, k_all, v_all = keys
    ki_op = ki_all.reshape(bp, pair, s_pad, IDX_DIM).transpose(0, 2, 1, 3).reshape(bp, s_pad, cdim)
    k_op = (k_all.reshape(bp, pair, s_pad, N_KV_A, HEAD_DIM_A).transpose(0, 2, 3, 1, 4)
            .reshape(bp, s_pad, N_KV_A * cdim))
    vt = v_all.reshape(b, s_pad, N_KV_A, HEAD_DIM_A).transpose(0, 2, 3, 1)
    vt = jnp.concatenate([vt, jnp.ones((b, N_KV_A, V_ROWS - HEAD_DIM_A, s_pad), vt.dtype)], axis=2)
    vt_op = (vt.reshape(bp, pair, N_KV_A, V_ROWS, s_pad).transpose(0, 2, 1, 3, 4)
             .reshape(bp, N_KV_A * v_rows, s_pad))

    body = functools.partial(_dsa_body, s_pad=s_pad, s_real=s_real, q_rows=q_rows, pair=pair, pos0=pos0,
                             topk=topk)
    o_t = pl.pallas_call(
        body,
        grid=(bp, t // q_rows),
        in_specs=[
            pl.BlockSpec((N_HEADS_A, 1, lanes, cdim), lambda i, j: (0, i, j, 0)),
            pl.BlockSpec((N_IDX_HEADS, 1, lanes, cdim), lambda i, j: (0, i, j, 0)),
            pl.BlockSpec((1, N_IDX_HEADS, lanes), lambda i, j: (i, 0, j)),
            pl.BlockSpec((1, s_pad, cdim), lambda i, j: (i, 0, 0)),
            pl.BlockSpec((1, s_pad, N_KV_A * cdim), lambda i, j: (i, 0, 0)),
            pl.BlockSpec((1, N_KV_A * v_rows, s_pad), lambda i, j: (i, 0, 0)),
        ],
        out_specs=pl.BlockSpec((1, WIDTH_A, lanes), lambda i, j: (i, 0, j)),
        out_shape=jax.ShapeDtypeStruct((bp, WIDTH_A, pair * t), BF16),
        scratch_shapes=[pltpu.VMEM((s_pad, lanes), F32), pltpu.VMEM((s_pad, lanes), F32),
                        pltpu.VMEM((N_IDX_HEADS, KEY_BLOCK, lanes), F32),
                        pltpu.VMEM((N_HEADS_A, s_pad, lanes), F32), pltpu.VMEM((N_HEADS_A, v_rows, lanes), F32)],
        compiler_params=_cparams("parallel", "parallel"),
        name="dsa",
    )(block_diag(qa_h), block_diag(qi_h), wi_op, ki_op, k_op, vt_op)
    return o_t.reshape(bp, WIDTH_A, pair, t).transpose(0, 2, 3, 1).reshape(b * t, WIDTH_A)


def _split3(a):
    hi = a.astype(BF16)
    r1 = a - hi.astype(F32)
    mid = r1.astype(BF16)
    lo = (r1 - mid.astype(F32)).astype(BF16)
    return hi, mid, lo


def _hgrn_body(hb_ref, s0_ref, lb_ref, nw_ref, ob_ref, sfin_ref, state_scr, *, tb, nb):
    t = pl.program_id(1)

    @pl.when(t == 0)
    def _():
        for bi, h in itertools.product(range(nb), range(N_HEADS_B)):
            state_scr[bi, h] = s0_ref[bi, h].T

    lb = lb_ref[...]
    r = lax.broadcasted_iota(I32, (CHUNK, CHUNK), 0)
    c = lax.broadcasted_iota(I32, (CHUNK, CHUNK), 1)
    causal = r >= c
    tril = jnp.where(causal, 1.0, 0.0).astype(BF16)

    for ci, bi in itertools.product(range(tb // CHUNK), range(nb)):
        rows = slice(ci * CHUNK, (ci + 1) * CHUNK)
        f = lb + (1.0 - lb) * jax.nn.sigmoid(hb_ref[bi, rows, WIDTH_B:2 * WIDTH_B])
        parts = _split3(jnp.log(f))
        bcum = _dot(tril, parts[0]) + _dot(tril, parts[1]) + _dot(tril, parts[2])
        for h in range(N_HEADS_B):
            lanes = slice(h * HEAD_DIM_B, (h + 1) * HEAD_DIM_B)
            q = hb_ref[bi, rows, lanes]
            k = 1.0 - f[:, lanes]
            v = hb_ref[bi, rows, 2 * WIDTH_B + h * HEAD_DIM_B:2 * WIDTH_B + (h + 1) * HEAD_DIM_B]
            gate = hb_ref[bi, rows, 3 * WIDTH_B + h * HEAD_DIM_B:3 * WIDTH_B + (h + 1) * HEAD_DIM_B]
            bh = bcum[:, lanes]
            b_last = bh[CHUNK - 1:CHUNK, :]
            vb = v.astype(BF16)
            state_t = state_scr[bi, h]
            o_inter = _dot_nt((q * jnp.exp(bh)).astype(BF16), state_t.astype(BF16))
            a_rows = []
            for i in range(CHUNK // HGRN_SUB):
                lo, hi = i * HGRN_SUB, (i + 1) * HGRN_SUB
                ref = bh[lo - 1:lo, :] if i else jnp.zeros((1, HEAD_DIM_B), F32)
                qs = (q[lo:hi] * jnp.exp(bh[lo:hi] - ref)).astype(BF16)
                ks = (k * jnp.exp(ref - bh)).astype(BF16)
                a_rows.append(_dot_nt(qs, ks))
            a = jnp.where(causal, jnp.concatenate(a_rows, axis=0), 0.0)
            o = o_inter + _dot(a.astype(BF16), vb)
            kdec = (k * jnp.exp(b_last - bh)).astype(BF16)
            state_scr[bi, h] = jnp.exp(b_last) * state_t + _dot_tn(vb, kdec)
            ms = jnp.mean(o * o, axis=-1, keepdims=True)
            y = o * lax.rsqrt(ms + RMS_EPS) * nw_ref[...] * (gate * jax.nn.sigmoid(gate))
            ob_ref[bi, rows, lanes] = y.astype(BF16)

    @pl.when(t == pl.num_programs(1) - 1)
    def _():
        for bi, h in itertools.product(range(nb), range(N_HEADS_B)):
            sfin_ref[bi, h] = state_scr[bi, h].T


def _hgrn(hb, s0, lb, nw):
    b, t, _ = hb.shape
    tb = min(256, t)
    nb = next(m for m in (4, 2, 1) if b % m == 0)
    return pl.pallas_call(
        functools.partial(_hgrn_body, tb=tb, nb=nb),
        grid=(b // nb, t // tb),
        in_specs=[
            pl.BlockSpec((nb, tb, 4 * WIDTH_B), lambda i, j: (i, j, 0)),
            pl.BlockSpec((nb, N_HEADS_B, HEAD_DIM_B, HEAD_DIM_B), lambda i, j: (i, 0, 0, 0)),
            pl.BlockSpec((1, WIDTH_B), lambda i, j: (0, 0)),
            pl.BlockSpec((1, HEAD_DIM_B), lambda i, j: (0, 0)),
        ],
        out_specs=[
            pl.BlockSpec((nb, tb, WIDTH_B), lambda i, j: (i, j, 0)),
            pl.BlockSpec((nb, N_HEADS_B, HEAD_DIM_B, HEAD_DIM_B), lambda i, j: (i, 0, 0, 0)),
        ],
        out_shape=[jax.ShapeDtypeStruct((b, t, WIDTH_B), BF16),
                   jax.ShapeDtypeStruct((b, N_HEADS_B, HEAD_DIM_B, HEAD_DIM_B), F32)],
        scratch_shapes=[pltpu.VMEM((nb, N_HEADS_B, HEAD_DIM_B, HEAD_DIM_B), F32)],
        compiler_params=_cparams("parallel", "arbitrary"),
        name="hgrn",
    )(hb, s0, lb, nw)


def _outproj_body(oaa_ref, oab_ref, oba_ref, obb_ref, xa_ref, xb_ref, w_ref, lnw_ref, lnb_ref, wrh_ref, wrl_ref, br_ref,
                  x1_ref, lg_ref, *, steps_a):
    first = pl.program_id(0) < steps_a
    pick = lambda a_ref, b_ref: jnp.where(first, a_ref[...], b_ref[...])
    x = pick(xa_ref, xb_ref)
    y = _dot(pick(oaa_ref, oab_ref), w_ref[:WIDTH_A, :]) + _dot(pick(oba_ref, obb_ref), w_ref[WIDTH_A:, :])
    x1 = _layer_norm(DN_ALPHA * x + y, lnw_ref[...], lnb_ref[...])
    x1_ref[...] = x1
    hi = x1.astype(BF16)
    lo = (x1 - hi.astype(F32)).astype(BF16)
    lg_ref[...] = (_dot_nt(wrh_ref[...], hi) + _dot_nt(wrh_ref[...], lo) + _dot_nt(wrl_ref[...], hi)
                   + br_ref[...])


def _two_group_rows(tm, width, steps_a):
    return (pl.BlockSpec((tm, width), lambda i, *_: (jnp.minimum(i, steps_a - 1), 0)),
            pl.BlockSpec((tm, width), lambda i, *_: (jnp.maximum(i - steps_a, 0), 0)))


def _outproj(mix_a, mix_b, x_a, x_b, w_out, lnw, lnb, wr_hi, wr_lo, br):
    tm = MOE_TILE
    assert x_a.shape[0] % tm == 0 and x_b.shape[0] % tm == 0
    n = x_a.shape[0] + x_b.shape[0]
    steps_a = x_a.shape[0] // tm
    row = lambda width: pl.BlockSpec((tm, width), lambda i: (i, 0))
    full = lambda a: pl.BlockSpec(a.shape, lambda i: (0,) * a.ndim)
    return pl.pallas_call(
        functools.partial(_outproj_body, steps_a=steps_a),
        grid=(n // tm,),
        in_specs=[*_two_group_rows(tm, WIDTH_A, steps_a), *_two_group_rows(tm, WIDTH_B, steps_a),
                  *_two_group_rows(tm, D_MODEL, steps_a), full(w_out), full(lnw), full(lnb), full(wr_hi),
                  full(wr_lo), full(br)],
        out_specs=[row(D_MODEL), pl.BlockSpec((N_EXPERTS, tm), lambda i: (0, i))],
        out_shape=[jax.ShapeDtypeStruct((n, D_MODEL), F32), jax.ShapeDtypeStruct((N_EXPERTS, n), F32)],
        compiler_params=_cparams("parallel"),
        name="outproj",
    )(mix_a[0], mix_b[0], mix_a[1], mix_b[1], x_a, x_b, w_out, lnw, lnb, wr_hi, wr_lo, br)


def _route_body(lg_ref, lpos_ref, gate_ref, tc_ref, base_ref, cnt_ref, cnt_scr, *, tr):
    @pl.when(pl.program_id(0) == 0)
    def _():
        cnt_scr[...] = jnp.zeros_like(cnt_scr)

    l = lg_ref[...]
    rows = lax.broadcasted_iota(I32, (N_EXPERTS, tr), 0)
    vals, hots = [], []
    for j in range(TOP_K):
        m = jnp.max(l, axis=0, keepdims=True)
        idx = jnp.min(jnp.where(l == m, rows, N_EXPERTS), axis=0, keepdims=True)
        hot = rows == idx
        vals.append(m)
        hots.append(hot)
        l = jnp.where(hot, NEG_INF, l)
    es = [jnp.exp(v - vals[0]) for v in vals]
    inv = 1.0 / (es[0] + es[1] + es[2] + es[3])
    for j in range(TOP_K):
        gate_ref[j:j + 1, :] = es[j] * inv
    chosen = jnp.zeros((N_EXPERTS, tr), F32)
    for hot in hots:
        chosen = chosen + jnp.where(hot, 1.0, 0.0)
    r = lax.broadcasted_iota(I32, (tr, tr), 0)
    c = lax.broadcasted_iota(I32, (tr, tr), 1)
    before = jnp.where(r < c, 1.0, 0.0).astype(BF16)
    prior = _dot(chosen.astype(BF16), before)
    units = jnp.ceil(jnp.sum(chosen, axis=1, keepdims=True) * (1.0 / ROW_ALIGN))
    er = lax.broadcasted_iota(I32, (N_EXPERTS, N_EXPERTS), 0)
    ec = lax.broadcasted_iota(I32, (N_EXPERTS, N_EXPERTS), 1)
    earlier = jnp.where(ec < er, 1.0, 0.0).astype(BF16)
    run_len = jnp.broadcast_to(units, (N_EXPERTS, 128)) * ROW_ALIGN
    run_start = _dot(earlier, jnp.broadcast_to(units, (N_EXPERTS, 128)).astype(BF16)) * ROW_ALIGN
    where_in_tile = prior + run_start[:, 0:1]
    for j in range(TOP_K):
        lpos_ref[j:j + 1, :] = jnp.sum(jnp.where(hots[j], where_in_tile, 0.0), axis=0,
                                       keepdims=True).astype(I32)
    tc_ref[0] = run_len.astype(I32)
    base_ref[0] = cnt_scr[...].astype(I32)
    cnt_scr[...] = cnt_scr[...] + run_len
    cnt_ref[...] = cnt_scr[...].astype(I32)


def _route(logits_t):
    n = logits_t.shape[1]
    tr = MOE_TILE
    tiles = n // tr
    tok = lambda rows: pl.BlockSpec((rows, tr), lambda i: (0, i))
    per_tile = pl.BlockSpec((1, N_EXPERTS, 128), lambda i: (i, 0, 0))
    return pl.pallas_call(
        functools.partial(_route_body, tr=tr),
        grid=(tiles,),
        in_specs=[tok(N_EXPERTS)],
        out_specs=[tok(TOP_K), tok(TOP_K), per_tile, per_tile, pl.BlockSpec((N_EXPERTS, 128), lambda i: (0, 0))],
        out_shape=[jax.ShapeDtypeStruct((TOP_K, n), I32), jax.ShapeDtypeStruct((TOP_K, n), F32),
                   jax.ShapeDtypeStruct((tiles, N_EXPERTS, 128), I32),
                   jax.ShapeDtypeStruct((tiles, N_EXPERTS, 128), I32),
                   jax.ShapeDtypeStruct((N_EXPERTS, 128), I32)],
        scratch_shapes=[pltpu.VMEM((N_EXPERTS, 128), F32)],
        compiler_params=_cparams("arbitrary"),
        name="route",
    )(logits_t)


def _for_each_run_piece(tc_ref, g0_ref, t, fn):
    def per_expert(e, local):
        c = tc_ref[t, e]
        glob = g0_ref[t, e]
        lo = local
        for p in RUN_PIECES:
            take = (c & p) != 0

            @pl.when(take)
            def _():
                fn(pl.multiple_of(lo, ROW_ALIGN), pl.multiple_of(glob, ROW_ALIGN), p)

            step = jnp.where(take, p, 0)
            lo = lo + step
            glob = glob + step
        return local + c

    lax.fori_loop(0, N_EXPERTS, per_expert, 0)


def _dispatch_body(tc_ref, g0_ref, seg_ref, lpos_ref, x_ref, buf_ref, stage, zero_scr, zsem, sems, *, n_blocks):
    i = pl.program_id(0)
    slot = i % 2

    def run_copy(slot_, lo, glob, rows):
        return pltpu.make_async_copy(stage.at[slot_, pl.ds(lo, rows), :], buf_ref.at[pl.ds(glob, rows), :],
                                     sems.at[slot_])

    @pl.when(i == 0)
    def _():
        zero_scr[...] = jnp.zeros_like(zero_scr)
        n_used = seg_ref[1, N_EXPERTS - 1] // EXPERT_ROWS

        def block_copy(start):
            return pltpu.make_async_copy(
                zero_scr, buf_ref.at[pl.ds(pl.multiple_of(start, EXPERT_ROWS), EXPERT_ROWS), :], zsem)

        def clears(action):
            for e in range(N_EXPERTS):
                @pl.when(seg_ref[1, e] > seg_ref[0, e])
                def _():
                    action(block_copy(seg_ref[1, e] - EXPERT_ROWS))

            def tail(blk, _):
                action(block_copy(blk * EXPERT_ROWS))
                return 0
            lax.fori_loop(n_used, n_blocks, tail, 0)

        clears(lambda cp: cp.start())
        clears(lambda cp: cp.wait())

    r = lax.broadcasted_iota(I32, (MOE_SLOTS, MOE_TILE), 0)
    onehot = jnp.zeros((MOE_SLOTS, MOE_TILE), F32)
    for j in range(TOP_K):
        onehot = jnp.where(r == lpos_ref[j:j + 1, :], 1.0, onehot)
    stage[slot] = _dot(onehot.astype(BF16), x_ref[...].astype(BF16)).astype(BF16)

    @pl.when(i > 0)
    def _():
        _for_each_run_piece(tc_ref, g0_ref, i - 1,
                            lambda lo, glob, rows: run_copy(1 - slot, lo, glob, rows).wait())

    _for_each_run_piece(tc_ref, g0_ref, i, lambda lo, glob, rows: run_copy(slot, lo, glob, rows).start())

    @pl.when(i == pl.num_programs(0) - 1)
    def _():
        _for_each_run_piece(tc_ref, g0_ref, i, lambda lo, glob, rows: run_copy(slot, lo, glob, rows).wait())


def _dispatch(x1, lpos_t, tc, g0, seg, n_rows):
    n = x1.shape[0]
    return pl.pallas_call(
        functools.partial(_dispatch_body, n_blocks=n_rows // EXPERT_ROWS),
        grid_spec=pltpu.PrefetchScalarGridSpec(
            num_scalar_prefetch=3,
            grid=(n // MOE_TILE,),
            in_specs=[pl.BlockSpec((TOP_K, MOE_TILE), lambda i, *_: (0, i)),
                      pl.BlockSpec((MOE_TILE, D_MODEL), lambda i, *_: (i, 0))],
            out_specs=pl.BlockSpec(memory_space=pl.ANY),
            scratch_shapes=[pltpu.VMEM((2, MOE_SLOTS, D_MODEL), BF16),
                            pltpu.VMEM((EXPERT_ROWS, D_MODEL), BF16),
                            pltpu.SemaphoreType.DMA(()), pltpu.SemaphoreType.DMA((2,))],
        ),
        out_shape=jax.ShapeDtypeStruct((n_rows, D_MODEL), BF16),
        compiler_params=_cparams("arbitrary"),
        name="dispatch",
    )(tc, g0, seg, lpos_t, x1)


def _experts_body(be_ref, nu_ref, vr_ref, x_ref, wgu_ref, bgu_ref, wd_ref, bd_ref, o_ref, wgu_bf, wd_bf):
    i = pl.program_id(0)
    half = EXPERT_ROWS // 2
    valid = vr_ref[i]

    @pl.when(jnp.logical_or(i == 0, be_ref[i] != be_ref[jnp.maximum(i - 1, 0)]))
    def _():
        for r in range(0, D_MODEL, 256):
            wgu_bf[r:r + 256, :] = wgu_ref[0, r:r + 256, :].astype(BF16)
            wd_bf[r:r + 256, :] = wd_ref[0, r:r + 256, :].astype(BF16)

    def ffn(rows):
        gu = _dot(x_ref[rows, :], wgu_bf[...]) + bgu_ref[0]
        gate = jnp.minimum(gu[:, :D_EXPERT], SWIGLU_LIMIT)
        up = jnp.clip(gu[:, D_EXPERT:], -SWIGLU_LIMIT, SWIGLU_LIMIT)
        hdn = (up + 1.0) * gate * jax.nn.sigmoid(SWIGLU_ALPHA * gate)
        o_ref[rows, :] = (_dot(hdn.astype(BF16), wd_bf[...]) + bd_ref[0]).astype(BF16)

    @pl.when(valid > half)
    def _():
        ffn(slice(0, EXPERT_ROWS))

    @pl.when(jnp.logical_and(valid > 0, valid <= half))
    def _():
        ffn(slice(0, half))
        o_ref[half:, :] = jnp.zeros((EXPERT_ROWS - half, D_MODEL), BF16)

    @pl.when(valid == 0)
    def _():
        o_ref[...] = jnp.zeros_like(o_ref)


def _experts(buf, block_e, n_used, block_valid, w_gu, b_gu, w_down, b_down):
    n_blocks = buf.shape[0] // EXPERT_ROWS
    rows = pl.BlockSpec((EXPERT_ROWS, D_MODEL), lambda i, be, nu, vr: (jnp.minimum(i, nu[0] - 1), 0))
    per_e = lambda a: pl.BlockSpec((1,) + a.shape[1:], lambda i, be, nu, vr: (be[i],) + (0,) * (a.ndim - 1))
    return pl.pallas_call(
        _experts_body,
        grid_spec=pltpu.PrefetchScalarGridSpec(
            num_scalar_prefetch=3,
            grid=(n_blocks,),
            in_specs=[rows, per_e(w_gu), per_e(b_gu), per_e(w_down), per_e(b_down)],
            out_specs=pl.BlockSpec((EXPERT_ROWS, D_MODEL), lambda i, be, nu, vr: (i, 0)),
            scratch_shapes=[pltpu.VMEM((D_MODEL, 2 * D_EXPERT), BF16), pltpu.VMEM((D_EXPERT, D_MODEL), BF16)],
        ),
        out_shape=jax.ShapeDtypeStruct(buf.shape, BF16),
        compiler_params=_cparams("arbitrary"),
        name="experts",
    )(block_e, n_used, block_valid, buf, w_gu, b_gu, w_down, b_down)


def _final_body(tc_ref, g0_ref, lpos_ref, gate_ref, x1_ref, pa_ref, pb_ref, lnw_ref, lnb_ref, wg_ref, bg_ref, wp_ref,
                eo_ref, ya_ref, yb_ref, stage, sems, *, steps_a):
    i = pl.program_id(0)
    slot = i % 2

    def run_copy(slot_, lo, glob, rows):
        return pltpu.make_async_copy(eo_ref.at[pl.ds(glob, rows), :], stage.at[slot_, pl.ds(lo, rows), :],
                                     sems.at[slot_])

    def fetch(tile, slot_):
        stage[slot_] = jnp.zeros(stage.shape[1:], stage.dtype)
        _for_each_run_piece(tc_ref, g0_ref, tile, lambda lo, glob, rows: run_copy(slot_, lo, glob, rows).start())

    @pl.when(i == 0)
    def _():
        fetch(0, 0)

    @pl.when(i + 1 < pl.num_programs(0))
    def _():
        fetch(i + 1, 1 - slot)

    _for_each_run_piece(tc_ref, g0_ref, i, lambda lo, glob, rows: run_copy(slot, lo, glob, rows).wait())

    r = lax.broadcasted_iota(I32, (MOE_TILE, MOE_SLOTS), 1)
    g = jnp.zeros((MOE_TILE, MOE_SLOTS), F32)
    for j in range(TOP_K):
        g = jnp.where(r == lpos_ref[:, j:j + 1], gate_ref[:, j:j + 1], g)
    moe = _dot(g.astype(BF16), stage[slot])
    x2 = _layer_norm(DN_ALPHA * x1_ref[...] + moe, lnw_ref[...], lnb_ref[...])
    ple_gate = jax.nn.sigmoid(_dot(x2.astype(BF16), wg_ref[...]) + bg_ref[...])
    p = jnp.where(i < steps_a, pa_ref[...], pb_ref[...])
    y = x2 + ple_gate * _dot(p.astype(BF16), wp_ref[...])

    @pl.when(i < steps_a)
    def _():
        ya_ref[...] = y

    @pl.when(i >= steps_a)
    def _():
        yb_ref[...] = y


def _final(tc, g0, lpos, gate, x1, p_a, p_b, lnw, lnb, w_gate, b_gate, w_ple, expert_out):
    n = x1.shape[0]
    steps_a = p_a.shape[0] // MOE_TILE
    row = lambda width: pl.BlockSpec((MOE_TILE, width), lambda i, *_: (i, 0))
    full = lambda a: pl.BlockSpec(a.shape, lambda i, *_: (0,) * a.ndim)
    return pl.pallas_call(
        functools.partial(_final_body, steps_a=steps_a),
        grid_spec=pltpu.PrefetchScalarGridSpec(
            num_scalar_prefetch=2,
            grid=(n // MOE_TILE,),
            in_specs=[row(TOP_K), row(TOP_K), row(D_MODEL), *_two_group_rows(MOE_TILE, D_PLE, steps_a), full(lnw),
                      full(lnb), full(w_gate), full(b_gate), full(w_ple), pl.BlockSpec(memory_space=pl.ANY)],
            out_specs=list(_two_group_rows(MOE_TILE, D_MODEL, steps_a)),
            scratch_shapes=[pltpu.VMEM((2, MOE_SLOTS, D_MODEL), BF16), pltpu.SemaphoreType.DMA((2,))],
        ),
        out_shape=[jax.ShapeDtypeStruct((p_a.shape[0], D_MODEL), F32),
                   jax.ShapeDtypeStruct((p_b.shape[0], D_MODEL), F32)],
        compiler_params=_cparams("arbitrary"),
        name="combine_final",
    )(tc, g0, lpos, gate, x1, p_a, p_b, lnw, lnb, w_gate, b_gate, w_ple, expert_out)


def _round_up(a, m):
    return -(-a // m) * m


def _token_mixers(x, hist_k, hist_v, hist_kidx, s0, pos0, wts):
    b, t, _ = x.shape
    n = b * t
    x2d = x.reshape(n, D_MODEL)

    qa, ka, va, ka_bf, va_bf, qi, ki, ki_bf, wi, hb = _inproj(x2d, wts["w_in"], wts["idx_lnw"], wts["idx_lnb"])

    k_new = ka_bf.reshape(b, t, N_KV_A * HEAD_DIM_A)
    v_new = va_bf.reshape(b, t, N_KV_A * HEAD_DIM_A)
    ki_new = ki_bf.reshape(b, t, IDX_DIM)
    q_ops = (qa.reshape(N_HEADS_A, b, t, HEAD_DIM_A), qi.reshape(N_IDX_HEADS, b, t, IDX_DIM),
             wi.reshape(b, t, N_IDX_HEADS))
    past = 0 if hist_k is None else hist_k.shape[1]
    s_real = past + t
    topk = min(TOPK_MAX, s_real // 4)
    if past:
        k_all = jnp.concatenate([hist_k.reshape(b, past, -1).astype(BF16), k_new], axis=1)
        v_all = jnp.concatenate([hist_v.reshape(b, past, -1).astype(BF16), v_new], axis=1)
        ki_all = jnp.concatenate([hist_kidx.astype(BF16), ki_new], axis=1)
    else:
        k_all, v_all, ki_all = k_new, v_new, ki_new
    pad = ((0, 0), (0, _round_up(s_real, KEY_BLOCK) - s_real), (0, 0))
    keys = tuple(jnp.pad(a, pad) for a in (ki_all, k_all, v_all))
    oa = _dsa(*q_ops, keys, s_real=s_real, pos0=pos0, topk=topk)

    ob, s_fin = _hgrn(hb.reshape(b, t, 4 * WIDTH_B), s0, wts["lb"], wts["hgrn_nw"])
    return (oa, ob.reshape(n, WIDTH_B), ka.reshape(b, t, N_KV_A, HEAD_DIM_A), va.reshape(b, t, N_KV_A, HEAD_DIM_A),
            ki.reshape(b, t, IDX_DIM), s_fin)


def _channel_mixer(mix_a, mix_b, x_a, x_b, p_a, p_b, wts):
    shape_a, shape_b = x_a.shape, x_b.shape
    x_a, x_b = x_a.reshape(-1, D_MODEL), x_b.reshape(-1, D_MODEL)
    n = x_a.shape[0] + x_b.shape[0]
    x1, logits_t = _outproj(mix_a, mix_b, x_a, x_b, wts["w_out"], wts["ln1_w"], wts["ln1_b"],
                            wts["wr_hi"], wts["wr_lo"], wts["b_router"])

    tiles = n // MOE_TILE
    lpos_t, gate_t, tile_runs, tile_base, counts = _route(logits_t)
    tc, counts = tile_runs[:, :, 0], counts[:, 0]
    padded = (counts + EXPERT_ROWS - 1) // EXPERT_ROWS * EXPERT_ROWS
    seg_end = jnp.cumsum(padded)
    seg_start = seg_end - padded
    g0 = (seg_start[None, :] + tile_base[:, :, 0]).astype(I32)
    n_blocks = -(-(n * TOP_K + N_EXPERTS * tiles * (ROW_ALIGN - 1) + N_EXPERTS * (EXPERT_ROWS - 1))
                 // EXPERT_ROWS)
    block_row0 = jnp.arange(n_blocks, dtype=I32) * EXPERT_ROWS
    block_e = jnp.minimum(jnp.sum((seg_end[None, :] <= block_row0[:, None]).astype(I32), axis=1), N_EXPERTS - 1)
    n_used = (seg_end[-1:] // EXPERT_ROWS).astype(I32)
    seg = jnp.stack([seg_start, seg_end]).astype(I32)
    buf = _dispatch(x1, lpos_t, tc, g0, seg, n_blocks * EXPERT_ROWS)
    block_valid = jnp.clip(seg_start[block_e] + counts[block_e] - block_row0, 0, EXPERT_ROWS).astype(I32)
    expert_out = _experts(buf, block_e, n_used, block_valid, wts["w_gu"], wts["b_gu"], wts["w_down"],
                          wts["b_down"])
    y_a, y_b = _final(tc, g0, lpos_t.T, gate_t.T, x1, p_a.reshape(-1, D_PLE), p_b.reshape(-1, D_PLE),
                      wts["ln2_w"], wts["ln2_b"], wts["w_ple_gate"], wts["b_ple_gate"], wts["w_ple"], expert_out)
    return y_a.reshape(shape_a), y_b.reshape(shape_b)


def _prep_weights(w_in, w_out, idx_k_norm_w, idx_k_norm_b, lb, hgrn_norm_w, ln1_w, ln1_b, w_router, b_router,
                  w_gu, b_gu, w_down, b_down, ln2_w, ln2_b, w_ple, w_ple_gate, b_ple_gate):
    n_a = COL_KW + IDX_DIM + N_IDX_HEADS
    w_pad = jnp.concatenate(
        [w_in[:, :n_a], jnp.zeros((D_MODEL, COL_HB - n_a), w_in.dtype), w_in[:, n_a:]], axis=1).astype(BF16)
    wr_t = w_router.T
    wr_hi = wr_t.astype(BF16)
    row = lambda a: a.reshape(1, -1)
    return dict(
        w_in=w_pad, idx_lnw=row(idx_k_norm_w), idx_lnb=row(idx_k_norm_b), lb=row(lb), hgrn_nw=row(hgrn_norm_w),
        w_out=w_out.astype(BF16), ln1_w=row(ln1_w), ln1_b=row(ln1_b),
        wr_hi=wr_hi, wr_lo=(wr_t - wr_hi.astype(F32)).astype(BF16), b_router=b_router.reshape(N_EXPERTS, 1),
        w_gu=w_gu, b_gu=b_gu.reshape(N_EXPERTS, 1, 2 * D_EXPERT),
        w_down=w_down, b_down=b_down.reshape(N_EXPERTS, 1, D_MODEL),
        ln2_w=row(ln2_w), ln2_b=row(ln2_b), w_ple=w_ple.astype(BF16), w_ple_gate=w_ple_gate.astype(BF16),
        b_ple_gate=row(b_ple_gate))


def kernel(x_prompt, x_sample, cache_k, cache_v, cache_kidx, state_hgrn, p_prompt, p_sample, w_in, w_out,
           idx_k_norm_w, idx_k_norm_b, hgrn_lb_logits, hgrn_norm_w, ln1_w, ln1_b, w_router, b_router, w_gu, b_gu,
           w_down, b_down, ln2_w, ln2_b, w_ple, w_ple_gate, b_ple_gate):
    lb_all = jnp.cumsum(jax.nn.softmax(hgrn_lb_logits.astype(F32), axis=0), axis=0)
    xp, xs = x_prompt, x_sample
    outs = [[] for _ in range(8)]
    for i in range(DEPTH):
        wts = _prep_weights(w_in[i], w_out[i], idx_k_norm_w[i], idx_k_norm_b[i], lb_all[i], hgrn_norm_w[i],
                            ln1_w[i], ln1_b[i], w_router[i], b_router[i], w_gu[i], b_gu[i], w_down[i], b_down[i],
                            ln2_w[i], ln2_b[i], w_ple[i], w_ple_gate[i], b_ple_gate[i])
        s0p = jnp.zeros((xp.shape[0], N_HEADS_B, HEAD_DIM_B, HEAD_DIM_B), F32)
        mix_p = _token_mixers(xp, None, None, None, s0p, 0, wts)
        mix_s = _token_mixers(xs, cache_k[i], cache_v[i], cache_kidx[i], state_hgrn[i], cache_k.shape[2], wts)
        xp, xs = _channel_mixer(mix_p, mix_s, xp, xs, p_prompt[i], p_sample[i], wts)
        for lst, val in zip(outs, mix_p[2:] + mix_s[2:]):
            lst.append(val)
    return (xp, xs) + tuple(jnp.stack(l) for l in outs)
```
